```python
import functools
import jax, jax.numpy as jnp
from jax import lax
import numpy as np

D_MODEL = 2048
BATCH = 32
SEQ = 256
DEPTH = 4
DEC_BATCH = 4
DEC_SEQ = 1024
PAST_LEN = 512

GRID_W = 64
N_MIXERS = 2
N_MLA = (DEPTH + 1) // 2
N_RWKV = DEPTH // 2
D_FF = 5632
N_MOD = 9
RMS_EPS = 1e-6
MLA_HEADS = 16
Q_LORA = 512
KV_LORA = 512
QK_NOPE = 128
QK_ROPE = 64
V_DIM = 128
ROPE_FREQS = QK_ROPE // 4
ROPE_THETA = 10000.0
Q_BLOCK = 128
ATTN_SCALE = (QK_NOPE + QK_ROPE) ** -0.5
RWKV_HEAD = 64
RWKV_HEADS = D_MODEL // RWKV_HEAD
DECAY_LORA = 96
A_LORA = 96
G_LORA = 256
GN_EPS = 64e-5

kernel_name = "hybrid_mla_rwkv7_prefix_diffusion_step"


def _rmsnorm(x, w):
    xf = x.astype(jnp.float32)
    y = xf * lax.rsqrt(jnp.mean(xf * xf, axis=-1, keepdims=True) + RMS_EPS)
    return (y * w.astype(jnp.float32)).astype(x.dtype)


def _modulate(x, g, shift, scale):
    return _rmsnorm(x, g) * (1 + scale) + shift


def _swiglu(h, w_in, w_out):
    gate, up = jnp.split(h @ w_in, 2, axis=-1)
    return (jax.nn.silu(gate) * up) @ w_out


def _axial_angles(n_tokens):
    rows = n_tokens // GRID_W
    row = jnp.repeat(jnp.arange(rows, dtype=jnp.float32), GRID_W)
    col = jnp.tile(jnp.arange(GRID_W, dtype=jnp.float32), rows)
    inv = ROPE_THETA ** (-jnp.arange(ROPE_FREQS, dtype=jnp.float32) / ROPE_FREQS)
    ang = jnp.stack([row[:, None] * inv, col[:, None] * inv], axis=1)[:, :, None, :]
    return jnp.cos(ang), jnp.sin(ang)


def _axial_rope(x, cos, sin):
    shp = x.shape
    xr = x.reshape(shp[:-1] + (2, 2, ROPE_FREQS))
    rot = jnp.stack([-xr[..., 1, :], xr[..., 0, :]], axis=-2)
    out = xr * cos.astype(x.dtype) + rot * sin.astype(x.dtype)
    return out.reshape(shp)


def _attend(q_nope, q_rope, k_nope, k_rope, v):
    B, S, H, _ = q_nope.shape
    blk = min(Q_BLOCK, S)
    nb = S // blk
    qn = q_nope.reshape(B, nb, blk, H, QK_NOPE).swapaxes(0, 1)
    qr = q_rope.reshape(B, nb, blk, H, QK_ROPE).swapaxes(0, 1)

    def one(args):
        qn_b, qr_b = args
        s = (jnp.einsum('bqhd,bkhd->bhqk', qn_b, k_nope, preferred_element_type=jnp.float32)
             + jnp.einsum('bqhr,bkr->bhqk', qr_b, k_rope, preferred_element_type=jnp.float32)) * ATTN_SCALE
        pr = jax.nn.softmax(s, axis=-1).astype(v.dtype)
        return jnp.einsum('bhqk,bkhd->bqhd', pr, v)

    o = lax.map(one, (qn, qr))
    return o.swapaxes(0, 1).reshape(B, S, H, V_DIM)


def _mla_q(h, p, j):
    B, S, _ = h.shape
    cq = _rmsnorm(h @ p['mla_w_dq'][j], p['mla_q_norm'][j])
    q = (cq @ p['mla_w_uq'][j]).reshape(B, S, MLA_HEADS, QK_NOPE + QK_ROPE)
    return q[..., :QK_NOPE], q[..., QK_NOPE:]


def _mla_kv(h, p, j):
    kv = h @ p['mla_w_dkv'][j]
    return _rmsnorm(kv[..., :KV_LORA], p['mla_kv_norm'][j]), kv[..., KV_LORA:]


def _mla_expand(ckv, p, j):
    B, T, _ = ckv.shape
    kv = (ckv @ p['mla_w_ukv'][j]).reshape(B, T, MLA_HEADS, QK_NOPE + V_DIM)
    return kv[..., :QK_NOPE], kv[..., QK_NOPE:]


def _mla_out(o, p, j):
    B, S = o.shape[:2]
    return o.reshape(B, S, MLA_HEADS * V_DIM) @ p['mla_w_o'][j]


def _mla_context(h, p, j):
    qn, qr = _mla_q(h, p, j)
    ckv, kr = _mla_kv(h, p, j)
    kn, v = _mla_expand(ckv, p, j)
    return _mla_out(_attend(qn, qr, kn, kr, v), p, j), (ckv, kr)


def _mla_latent(h, ctx_ckv, ctx_krope, cos, sin, p, j):
    qn, qr = _mla_q(h, p, j)
    qr = _axial_rope(qr, cos[:, None], sin[:, None])
    ckv, kr = _mla_kv(h, p, j)
    kr = _axial_rope(kr, cos, sin)
    ckv = jnp.concatenate([ctx_ckv.astype(ckv.dtype), ckv], axis=1)
    kr = jnp.concatenate([ctx_krope.astype(kr.dtype), kr], axis=1)
    kn, v = _mla_expand(ckv, p, j)
    return _mla_out(_attend(qn, qr, kn, kr, v), p, j), None


def _wkv_scan(r, w, k, v, a, b, s0, reverse):
    def step(S, inp):
        r_t, w_t, k_t, v_t, a_t, b_t = inp
        sa = jnp.einsum('bhij,bhj->bhi', S, a_t)
        S = S * w_t[:, :, None, :] + sa[..., None] * b_t[:, :, None, :] + v_t[..., None] * k_t[:, :, None, :]
        return S, jnp.einsum('bhij,bhj->bhi', S, r_t)

    xs = tuple(jnp.moveaxis(t, 1, 0) for t in (r, w, k, v, a, b))
    s_final, ys = lax.scan(step, s0.astype(jnp.float32), xs, reverse=reverse)
    return jnp.moveaxis(ys, 0, 1), s_final


def _rwkv_mix(h, s0_fwd, s0_bwd, p, j):
    B, T, D = h.shape
    H, N = RWKV_HEADS, RWKV_HEAD
    f32 = jnp.float32
    hp = jnp.pad(h, ((0, 0), (1, 1), (0, 0)))
    xx = 0.5 * (hp[:, :-2] + hp[:, 2:]) - h
    mu = p['rwkv_mu'][j]
    xr, xk, xv, xg = (h + xx * mu[i] for i in range(4))
    r = (xr @ p['rwkv_w_r'][j]).astype(f32).reshape(B, T, H, N)
    k = (xk @ p['rwkv_w_k'][j]).astype(f32).reshape(B, T, H, N)
    v = (xv @ p['rwkv_w_v'][j]).astype(f32).reshape(B, T, H, N)
    g = jax.nn.sigmoid(xg @ p['rwkv_g1'][j]) @ p['rwkv_g2'][j]
    kk = k * p['rwkv_k_k'][j].astype(f32).reshape(H, N)
    kk = kk / jnp.maximum(jnp.linalg.norm(kk, axis=-1, keepdims=True), 1e-12)
    k_a = p['rwkv_k_a'][j].astype(f32).reshape(H, N)
    r_k = p['rwkv_r_k'][j].astype(f32)
    y = jnp.zeros((B, T, H, N), f32)
    bonus = jnp.zeros((B, T, H, N), f32)
    finals = []
    for d, s0 in enumerate((s0_fwd, s0_bwd)):
        xw = h + xx * p['rwkv_mu_dir'][j, d, 0]
        xa = h + xx * p['rwkv_mu_dir'][j, d, 1]
        w_log = -jax.nn.softplus(-(p['rwkv_w0'][j, d] + jnp.tanh(xw @ p['rwkv_w1'][j, d]) @ p['rwkv_w2'][j, d])) - 0.5
        decay = jnp.exp(-jnp.exp(w_log.astype(f32))).reshape(B, T, H, N)
        a = jax.nn.sigmoid(p['rwkv_a0'][j, d] + (xa @ p['rwkv_a1'][j, d]) @ p['rwkv_a2'][j, d]).astype(f32).reshape(B, T, H, N)
        kd = k * (1 + (a - 1) * k_a)
        yd, s_final = _wkv_scan(r, decay, kd, v, -kk, kk * a, s0, reverse=(d == 1))
        y = y + yd
        bonus = bonus + jnp.sum(r * kd * r_k, axis=-1, keepdims=True) * v
        finals.append(s_final)
    mean = jnp.mean(y, axis=-1, keepdims=True)
    var = jnp.mean(jnp.square(y - mean), axis=-1, keepdims=True)
    yn = ((y - mean) * lax.rsqrt(var + GN_EPS)).reshape(B, T, D)
    yn = yn * p['rwkv_ln_w'][j].astype(f32) + p['rwkv_ln_b'][j].astype(f32)
    o = (yn + bonus.reshape(B, T, D)).astype(h.dtype)
    return (o * g) @ p['rwkv_w_o'][j], (finals[0], finals[1])


def _layer(x, cond, l, p, mix):
    m = (jax.nn.silu(cond) @ p['w_ada'][l] + p['b_ada'][l]).reshape(cond.shape[0], 1, N_MOD, D_MODEL)
    h = _modulate(x, p['norm_sub'][l, 0], m[:, :, 0], m[:, :, 1])
    x = x + 0.5 * m[:, :, 2] * _swiglu(h, p['w_ffn_in'][l, 0], p['w_ffn_out'][l, 0])
    h = _modulate(x, p['norm_sub'][l, 1], m[:, :, 3], m[:, :, 4])
    out, extra = mix(h)
    x = x + m[:, :, 5] * out
    h = _modulate(x, p['norm_sub'][l, 2], m[:, :, 6], m[:, :, 7])
    x = x + 0.5 * m[:, :, 8] * _swiglu(h, p['w_ffn_in'][l, 1], p['w_ffn_out'][l, 1])
    return x, extra


def _context_pass(x, c_ctx, p):
    cond = c_ctx[None, :]
    zero = jnp.zeros((x.shape[0], RWKV_HEADS, RWKV_HEAD, RWKV_HEAD), jnp.float32)
    ckv_l, kr_l, sf_l, sb_l = [], [], [], []
    for l in range(DEPTH):
        j = l // N_MIXERS
        if l % N_MIXERS == 0:
            mix = functools.partial(_mla_context, p=p, j=j)
        else:
            mix = functools.partial(_rwkv_mix, s0_fwd=zero, s0_bwd=zero, p=p, j=j)
        x, extra = _layer(x, cond, l, p, mix)
        if l % N_MIXERS == 0:
            ckv_l.append(extra[0])
            kr_l.append(extra[1])
        else:
            sf_l.append(extra[0])
            sb_l.append(extra[1])
    y = _rmsnorm(x, p['norm_final'])
    return y, jnp.stack(ckv_l, axis=1), jnp.stack(kr_l, axis=1), jnp.stack(sf_l, axis=1), jnp.stack(sb_l, axis=1)


def _latent_pass(x, c, cache_ckv, cache_krope, state_fwd, state_bwd, p):
    cos, sin = _axial_angles(x.shape[1])
    for l in range(DEPTH):
        j = l // N_MIXERS
        if l % N_MIXERS == 0:
            mix = functools.partial(_mla_latent, ctx_ckv=cache_ckv[:, j], ctx_krope=cache_krope[:, j],
                                    cos=cos, sin=sin, p=p, j=j)
        else:
            mix = functools.partial(_rwkv_mix, s0_fwd=state_fwd[:, j], s0_bwd=state_bwd[:, j], p=p, j=j)
        x, _ = _layer(x, c, l, p, mix)
    return _rmsnorm(x, p['norm_final'])


def setup_inputs(seed: int = 0) -> dict:
    key = jax.random.key(seed)
    ks = iter(jax.random.split(key, 48))

    def nrm(shape, scale=1.0):
        return jax.random.normal(next(ks), shape, jnp.float32) * scale

    def uni(shape, lo, hi):
        return jax.random.uniform(next(ks), shape, jnp.float32, lo, hi)

    D, H, RH, N = D_MODEL, MLA_HEADS, RWKV_HEADS, RWKV_HEAD
    return {
        'x_prompt': nrm((BATCH, SEQ, D)),
        'x_sample': nrm((DEC_BATCH, DEC_SEQ, D)),
        'cache_ckv': nrm((DEC_BATCH, N_MLA, PAST_LEN, KV_LORA)),
        'cache_krope': nrm((DEC_BATCH, N_MLA, PAST_LEN, QK_ROPE)),
        'state_wkv_fwd': nrm((DEC_BATCH, N_RWKV, RH, N, N), 0.5),
        'state_wkv_bwd': nrm((DEC_BATCH, N_RWKV, RH, N, N), 0.5),
        'c': nrm((DEC_BATCH, D)),
        'c_ctx': nrm((D,)),
        'w_ada': nrm((DEPTH, D, N_MOD * D), 0.5 * D ** -0.5),
        'b_ada': nrm((DEPTH, N_MOD * D), 0.02),
        'norm_sub': 1 + nrm((DEPTH, 3, D), 0.02),
        'norm_final': 1 + nrm((D,), 0.02),
        'w_ffn_in': nrm((DEPTH, 2, D, 2 * D_FF), D ** -0.5),
        'w_ffn_out': nrm((DEPTH, 2, D_FF, D), D_FF ** -0.5),
        'mla_w_dq': nrm((N_MLA, D, Q_LORA), D ** -0.5),
        'mla_q_norm': 1 + nrm((N_MLA, Q_LORA), 0.02),
        'mla_w_uq': nrm((N_MLA, Q_LORA, H * (QK_NOPE + QK_ROPE)), Q_LORA ** -0.5),
        'mla_w_dkv': nrm((N_MLA, D, KV_LORA + QK_ROPE), D ** -0.5),
        'mla_kv_norm': 1 + nrm((N_MLA, KV_LORA), 0.02),
        'mla_w_ukv': nrm((N_MLA, KV_LORA, H * (QK_NOPE + V_DIM)), KV_LORA ** -0.5),
        'mla_w_o': nrm((N_MLA, H * V_DIM, D), (H * V_DIM) ** -0.5),
        'rwkv_mu': uni((N_RWKV, 4, D), 0.0, 1.0),
        'rwkv_mu_dir': uni((N_RWKV, 2, 2, D), 0.0, 1.0),
        'rwkv_w_r': nrm((N_RWKV, D, D), D ** -0.5),
        'rwkv_w_k': nrm((N_RWKV, D, D), D ** -0.5),
        'rwkv_w_v': nrm((N_RWKV, D, D), D ** -0.5),
        'rwkv_w0': uni((N_RWKV, 2, D), -5.0, 1.0),
        'rwkv_w1': nrm((N_RWKV, 2, D, DECAY_LORA), D ** -0.5),
        'rwkv_w2': nrm((N_RWKV, 2, DECAY_LORA, D), 0.5 * DECAY_LORA ** -0.5),
        'rwkv_a0': nrm((N_RWKV, 2, D), 0.5),
        'rwkv_a1': nrm((N_RWKV, 2, D, A_LORA), D ** -0.5),
        'rwkv_a2': nrm((N_RWKV, 2, A_LORA, D), 0.5 * A_LORA ** -0.5),
        'rwkv_g1': nrm((N_RWKV, D, G_LORA), D ** -0.5),
        'rwkv_g2': nrm((N_RWKV, G_LORA, D), G_LORA ** -0.5),
        'rwkv_k_k': 0.85 + nrm((N_RWKV, D), 0.05),
        'rwkv_k_a': 1 + nrm((N_RWKV, D), 0.05),
        'rwkv_r_k': nrm((N_RWKV, RH, N), 0.1),
        'rwkv_ln_w': 1 + nrm((N_RWKV, D), 0.02),
        'rwkv_ln_b': nrm((N_RWKV, D), 0.02),
        'rwkv_w_o': nrm((N_RWKV, D, D), D ** -0.5),
    }


def reference(x_prompt, x_sample, cache_ckv, cache_krope, state_wkv_fwd, state_wkv_bwd, c, c_ctx,
              w_ada, b_ada, norm_sub, norm_final, w_ffn_in, w_ffn_out,
              mla_w_dq, mla_q_norm, mla_w_uq, mla_w_dkv, mla_kv_norm, mla_w_ukv, mla_w_o,
              rwkv_mu, rwkv_mu_dir, rwkv_w_r, rwkv_w_k, rwkv_w_v, rwkv_w0, rwkv_w1, rwkv_w2,
              rwkv_a0, rwkv_a1, rwkv_a2, rwkv_g1, rwkv_g2, rwkv_k_k, rwkv_k_a, rwkv_r_k,
              rwkv_ln_w, rwkv_ln_b, rwkv_w_o):
    p = dict(w_ada=w_ada, b_ada=b_ada, norm_sub=norm_sub, norm_final=norm_final,
             w_ffn_in=w_ffn_in, w_ffn_out=w_ffn_out,
             mla_w_dq=mla_w_dq, mla_q_norm=mla_q_norm, mla_w_uq=mla_w_uq, mla_w_dkv=mla_w_dkv,
             mla_kv_norm=mla_kv_norm, mla_w_ukv=mla_w_ukv, mla_w_o=mla_w_o,
             rwkv_mu=rwkv_mu, rwkv_mu_dir=rwkv_mu_dir, rwkv_w_r=rwkv_w_r, rwkv_w_k=rwkv_w_k,
             rwkv_w_v=rwkv_w_v, rwkv_w0=rwkv_w0, rwkv_w1=rwkv_w1, rwkv_w2=rwkv_w2,
             rwkv_a0=rwkv_a0, rwkv_a1=rwkv_a1, rwkv_a2=rwkv_a2, rwkv_g1=rwkv_g1, rwkv_g2=rwkv_g2,
             rwkv_k_k=rwkv_k_k, rwkv_k_a=rwkv_k_a, rwkv_r_k=rwkv_r_k,
             rwkv_ln_w=rwkv_ln_w, rwkv_ln_b=rwkv_ln_b, rwkv_w_o=rwkv_w_o)
    y_prompt, new_ckv, new_krope, new_sf, new_sb = _context_pass(x_prompt, c_ctx, p)
    y_sample = _latent_pass(x_sample, c, cache_ckv, cache_krope, state_wkv_fwd, state_wkv_bwd, p)
    return (y_prompt, y_sample, new_ckv, new_krope, new_sf, new_sb)
```

```python
import functools

import jax
import jax.numpy as jnp
from jax import lax
from jax.experimental import pallas as pl
from jax.experimental.pallas import tpu as pltpu

D_MODEL = 2048
DEPTH = 4
N_MIXERS = 2
D_FF = 5632
N_MOD = 9
RMS_EPS = 1e-6
MLA_HEADS = 16
Q_LORA = 512
KV_LORA = 512
QK_NOPE = 128
QK_ROPE = 64
V_DIM = 128
ROPE_FREQS = QK_ROPE // 4
ROPE_THETA = 10000.0
GRID_W = 64
ATTN_SCALE = (QK_NOPE + QK_ROPE) ** -0.5
RWKV_HEAD = 64
RWKV_HEADS = D_MODEL // RWKV_HEAD
GN_EPS = 64e-5
LORA_PAD = 128

LANES = 128
COND_ROWS = 8
VMEM_LIMIT = 56 * 1024 * 1024

TM_FFN = 512
TF_FFN = 512
TM_PROJ = 512
TM_RWKV = 256
TN_ADA = 1024
Q_TILE = 256
SCAN_TC = 32
POST_TC = 32

BF16 = jnp.bfloat16
F32 = jnp.float32


def _params(*sem):
    return pltpu.CompilerParams(dimension_semantics=sem, vmem_limit_bytes=VMEM_LIMIT)


def _sigmoid(x):
    return 1.0 / (1.0 + jnp.exp(-x))


def _modulate(x, g, shift, scale):
    ms = jnp.mean(x * x, axis=-1, keepdims=True)
    return (x * lax.rsqrt(ms + RMS_EPS) * g) * (1.0 + scale) + shift


def _rms(x, w):
    ms = jnp.mean(x * x, axis=-1, keepdims=True)
    return x * lax.rsqrt(ms + RMS_EPS) * w


def _dot(a, b):
    return jnp.dot(a, b, preferred_element_type=F32)


def _dot_nt(a, b):
    return lax.dot_general(a, b, (((1,), (1,)), ((), ())), preferred_element_type=F32)


class _Geom:
    def __init__(self, b_ctx, s_ctx, b_lat, s_lat):
        self.b_ctx, self.s_ctx, self.b_lat, self.s_lat = b_ctx, s_ctx, b_lat, s_lat
        self.n_ctx = b_ctx * s_ctx
        self.n_lat = b_lat * s_lat
        self.n_tok = self.n_ctx + self.n_lat

    def cond_row(self, i, tm):
        start = i * tm
        return jnp.where(start < self.n_ctx, 0, 1 + (start - self.n_ctx) // self.s_lat)


def _ada_kernel(c_ref, w_ref, b_ref, o_ref):
    c = c_ref[...]
    s = (c * _sigmoid(c)).astype(BF16)
    o_ref[...] = _dot(s, w_ref[...].astype(BF16)) + b_ref[...]


def _ada(cond, w_ada, b_ada):
    n = N_MOD * D_MODEL
    return pl.pallas_call(
        _ada_kernel,
        grid=(DEPTH, n // TN_ADA),
        in_specs=[
            pl.BlockSpec((COND_ROWS, D_MODEL), lambda l, j: (0, 0)),
            pl.BlockSpec((None, D_MODEL, TN_ADA), lambda l, j: (l, 0, j)),
            pl.BlockSpec((None, 1, TN_ADA), lambda l, j: (l, 0, j)),
        ],
        out_specs=pl.BlockSpec((None, COND_ROWS, TN_ADA), lambda l, j: (l, 0, j)),
        out_shape=jax.ShapeDtypeStruct((DEPTH, COND_ROWS, n), F32),
        compiler_params=_params("parallel", "parallel"),
        name="ada",
    )(cond, w_ada, b_ada.reshape(DEPTH, 1, n))


def _ffn_kernel(x_ref, m_ref, g_ref, wg_ref, wu_ref, wo_ref, nf_ref, o_ref, h_ref, acc_ref, *, sub, final):
    f = pl.program_id(1)

    @pl.when(f == 0)
    def _():
        h = _modulate(x_ref[...], g_ref[...], m_ref[3 * sub:3 * sub + 1, :], m_ref[3 * sub + 1:3 * sub + 2, :])
        h_ref[...] = h.astype(BF16)
        acc_ref[...] = jnp.zeros_like(acc_ref)

    h = h_ref[...]
    gate = _dot(h, wg_ref[...])
    up = _dot(h, wu_ref[...])
    act = (gate * _sigmoid(gate) * up).astype(BF16)
    acc_ref[...] += _dot(act, wo_ref[...])

    @pl.when(f == pl.num_programs(1) - 1)
    def _():
        y = x_ref[...] + 0.5 * m_ref[3 * sub + 2:3 * sub + 3, :] * acc_ref[...]
        if final:
            y = _rms(y, nf_ref[...])
        o_ref[...] = y


def _ffn(geo, x, mods, g, w_in, w_out, nf, *, sub, final):
    tm, tf = TM_FFN, TF_FFN
    nf_blocks = D_FF // tf
    kern = functools.partial(_ffn_kernel, sub=sub, final=final)
    return pl.pallas_call(
        kern,
        grid=(geo.n_tok // tm, nf_blocks),
        in_specs=[
            pl.BlockSpec((tm, D_MODEL), lambda i, f: (i, 0)),
            pl.BlockSpec((None, N_MOD, D_MODEL), lambda i, f: (geo.cond_row(i, tm), 0, 0)),
            pl.BlockSpec((1, D_MODEL), lambda i, f: (0, 0)),
            pl.BlockSpec((D_MODEL, tf), lambda i, f: (0, f)),
            pl.BlockSpec((D_MODEL, tf), lambda i, f: (0, f + nf_blocks)),
            pl.BlockSpec((tf, D_MODEL), lambda i, f: (f, 0)),
            pl.BlockSpec((1, D_MODEL), lambda i, f: (0, 0)),
        ],
        out_specs=pl.BlockSpec((tm, D_MODEL), lambda i, f: (i, 0)),
        out_shape=jax.ShapeDtypeStruct((geo.n_tok, D_MODEL), F32),
        scratch_shapes=[pltpu.VMEM((tm, D_MODEL), BF16), pltpu.VMEM((tm, D_MODEL), F32)],
        compiler_params=_params("parallel", "arbitrary"),
        name="ffn",
    )(x, mods, g, w_in, w_in, w_out, nf)


def _mla_proj_kernel(x_ref, m_ref, g_ref, w_ref, qn_ref, kvn_ref, cos_ref, sin_ref,
                     cq_ref, ckv_ref, ckvb_ref, kr_ref):
    h = _modulate(x_ref[...], g_ref[...], m_ref[3:4, :], m_ref[4:5, :]).astype(BF16)
    z = _dot(h, w_ref[...])
    cq_ref[...] = _rms(z[:, :Q_LORA], qn_ref[...]).astype(BF16)
    ckv = _rms(z[:, Q_LORA:Q_LORA + KV_LORA], kvn_ref[...])
    ckv_ref[...] = ckv
    ckvb_ref[...] = ckv.astype(BF16)
    o = Q_LORA + KV_LORA
    kr_ref[...] = z[:, o:o + QK_ROPE] * cos_ref[...] + z[:, o + QK_ROPE:o + 2 * QK_ROPE] * sin_ref[...]


def _mla_proj(geo, x, mods, g, w_cat, qn, kvn, cos, sin):
    tm = TM_PROJ
    n_out = w_cat.shape[1]
    row = lambda i: (i, 0)
    fix = lambda i: (0, 0)
    return pl.pallas_call(
        _mla_proj_kernel,
        grid=(geo.n_tok // tm,),
        in_specs=[
            pl.BlockSpec((tm, D_MODEL), row),
            pl.BlockSpec((None, N_MOD, D_MODEL), lambda i: (geo.cond_row(i, tm), 0, 0)),
            pl.BlockSpec((1, D_MODEL), fix),
            pl.BlockSpec((D_MODEL, n_out), fix),
            pl.BlockSpec((1, Q_LORA), fix),
            pl.BlockSpec((1, KV_LORA), fix),
            pl.BlockSpec((tm, QK_ROPE), row),
            pl.BlockSpec((tm, QK_ROPE), row),
        ],
        out_specs=[
            pl.BlockSpec((tm, Q_LORA), row),
            pl.BlockSpec((tm, KV_LORA), row),
            pl.BlockSpec((tm, KV_LORA), row),
            pl.BlockSpec((tm, QK_ROPE), row),
        ],
        out_shape=[
            jax.ShapeDtypeStruct((geo.n_tok, Q_LORA), BF16),
            jax.ShapeDtypeStruct((geo.n_tok, KV_LORA), F32),
            jax.ShapeDtypeStruct((geo.n_tok, KV_LORA), BF16),
            jax.ShapeDtypeStruct((geo.n_tok, QK_ROPE), F32),
        ],
        compiler_params=_params("parallel"),
        name="mla_proj",
    )(x, mods, g, w_cat, qn, kvn, cos, sin)


def _attn_kernel(*refs, s_len, t_past, rope):
    if rope:
        (cq_ref, ckvb_ref, kr_ref, pckv_ref, pkr_ref, cos_ref, sin_ref,
         wq_ref, wqr_ref, wk_ref, wv_ref, o_ref, kv_all, kr_all) = refs
    else:
        cq_ref, ckvb_ref, kr_ref, wq_ref, wk_ref, wv_ref, o_ref, kv_all, kr_all = refs
    if t_past:
        kv_all[0:t_past, :] = pckv_ref[...].astype(BF16)
        kr_all[0:t_past, :] = pkr_ref[...].astype(BF16)
    kv_all[t_past:, :] = ckvb_ref[...]
    kr_all[t_past:, :] = kr_ref[...].astype(BF16)

    def head(hd, carry):
        kv = kv_all[...]
        kn = _dot(kv, wk_ref[hd]).astype(BF16)
        v = _dot(kv, wv_ref[hd]).astype(BF16)
        krb = kr_all[...]
        col = pl.multiple_of(hd * V_DIM, V_DIM)
        for qb in range(s_len // Q_TILE):
            rows = slice(qb * Q_TILE, (qb + 1) * Q_TILE)
            cq = cq_ref[rows, :]
            q = _dot(cq, wq_ref[hd])
            qn = q[:, :QK_NOPE]
            qr = q[:, QK_NOPE:]
            if rope:
                qr = qr * cos_ref[rows, :] + _dot(cq, wqr_ref[hd]) * sin_ref[rows, :]
            s = (_dot_nt(qn.astype(BF16), kn) + _dot_nt(qr.astype(BF16), krb)) * ATTN_SCALE
            p = jnp.exp(s - jnp.max(s, axis=-1, keepdims=True))
            pr = (p / jnp.sum(p, axis=-1, keepdims=True)).astype(BF16)
            o_ref[rows, pl.ds(col, V_DIM)] = _dot(pr, v).astype(BF16)
        return carry

    lax.fori_loop(0, MLA_HEADS, head, 0)


def _attn(geo, cq, ckvb, kr, wq, wqr, wk, wv, *, latent, past_ckv=None, past_kr=None, cos=None, sin=None):
    if latent:
        nb, s_len, off = geo.b_lat, geo.s_lat, geo.n_ctx // geo.s_lat
        t_past = past_ckv.shape[1]
    else:
        nb, s_len, off, t_past = geo.b_ctx, geo.s_ctx, 0, 0
    row = lambda b: (b + off, 0)
    fix3 = lambda b: (0, 0, 0)
    tok = lambda width: pl.BlockSpec((s_len, width), row)
    wspec = lambda w: pl.BlockSpec(w.shape, fix3)
    in_specs = [tok(Q_LORA), tok(KV_LORA), tok(QK_ROPE)]
    args = [cq, ckvb, kr]
    if latent:
        in_specs += [pl.BlockSpec((None, t_past, KV_LORA), lambda b: (b, 0, 0)),
                     pl.BlockSpec((None, t_past, QK_ROPE), lambda b: (b, 0, 0)),
                     tok(QK_ROPE), tok(QK_ROPE), wspec(wq), wspec(wqr)]
        args += [past_ckv, past_kr, cos, sin, wq, wqr]
    else:
        in_specs += [wspec(wq)]
        args += [wq]
    in_specs += [wspec(wk), wspec(wv)]
    args += [wk, wv]
    kern = functools.partial(_attn_kernel, s_len=s_len, t_past=t_past, rope=latent)
    return pl.pallas_call(
        kern,
        grid=(nb,),
        in_specs=in_specs,
        out_specs=pl.BlockSpec((s_len, D_MODEL), lambda b: (b, 0)),
        out_shape=jax.ShapeDtypeStruct((nb * s_len, D_MODEL), BF16),
        scratch_shapes=[pltpu.VMEM((t_past + s_len, KV_LORA), BF16),
                        pltpu.VMEM((t_past + s_len, QK_ROPE), BF16)],
        compiler_params=_params("parallel"),
        name="attn_lat" if latent else "attn_ctx",
    )(*args)


def _out_proj_kernel(*refs, gated):
    if gated:
        a_ref, g_ref, w_ref, x_ref, m_ref, o_ref = refs
        a = (a_ref[...] * g_ref[...]).astype(BF16)
    else:
        a_ref, w_ref, x_ref, m_ref, o_ref = refs
        a = a_ref[...]
    o_ref[...] = x_ref[...] + m_ref[5:6, :] * _dot(a, w_ref[...])


def _out_proj(geo, a, gmul, w, x, mods):
    tm = TM_PROJ
    row = pl.BlockSpec((tm, D_MODEL), lambda i: (i, 0))
    gated = gmul is not None
    in_specs = [row] + ([row] if gated else []) + [
        pl.BlockSpec((D_MODEL, D_MODEL), lambda i: (0, 0)),
        row,
        pl.BlockSpec((None, N_MOD, D_MODEL), lambda i: (geo.cond_row(i, tm), 0, 0)),
    ]
    args = [a] + ([gmul] if gated else []) + [w, x, mods]
    return pl.pallas_call(
        functools.partial(_out_proj_kernel, gated=gated),
        grid=(geo.n_tok // tm,),
        in_specs=in_specs,
        out_specs=row,
        out_shape=jax.ShapeDtypeStruct((geo.n_tok, D_MODEL), F32),
        compiler_params=_params("parallel"),
        name="out_proj",
    )(*args)


def _shifted(geo, x_ref, xp_ref, xn_ref, m_ref, g_ref, tm):
    i = pl.program_id(0)
    g, shift, scale = g_ref[...], m_ref[3:4, :], m_ref[4:5, :]
    h = _modulate(x_ref[...], g, shift, scale)
    start = i * tm
    seq = jnp.where(start < geo.n_ctx, geo.s_ctx, geo.s_lat)
    rel = jnp.where(start < geo.n_ctx, start, start - geo.n_ctx)
    has_prev = (rel % seq) != 0
    has_next = ((rel + tm) % seq) != 0
    hp = _modulate(xp_ref[...], g, shift, scale)[7:8, :]
    hn = _modulate(xn_ref[...], g, shift, scale)[0:1, :]
    hp = jnp.where(has_prev, hp, 0.0)
    hn = jnp.where(has_next, hn, 0.0)
    r = lax.broadcasted_iota(jnp.int32, h.shape, 0)
    down = jnp.where(r == 0, hp, pltpu.roll(h, 1, 0))
    up = jnp.where(r == tm - 1, hn, pltpu.roll(h, tm - 1, 0))
    return h, 0.5 * (down + up) - h


def _halo_specs(geo, tm):
    nb8 = geo.n_tok // 8
    return [
        pl.BlockSpec((tm, D_MODEL), lambda i, *_: (i, 0)),
        pl.BlockSpec((8, D_MODEL), lambda i, *_: (jnp.maximum(i * (tm // 8) - 1, 0), 0)),
        pl.BlockSpec((8, D_MODEL), lambda i, *_: (jnp.minimum((i + 1) * (tm // 8), nb8 - 1), 0)),
        pl.BlockSpec((None, N_MOD, D_MODEL), lambda i, *_: (geo.cond_row(i, tm), 0, 0)),
        pl.BlockSpec((1, D_MODEL), lambda i, *_: (0, 0)),
    ]


def _rkv_kernel(x_ref, xp_ref, xn_ref, m_ref, g_ref, mu_ref, w_ref, o_ref, h_ref, xx_ref, *, geo, tm):
    @pl.when(pl.program_id(1) == 0)
    def _():
        h, xx = _shifted(geo, x_ref, xp_ref, xn_ref, m_ref, g_ref, tm)
        h_ref[...] = h
        xx_ref[...] = xx

    xm = (h_ref[...] + xx_ref[...] * mu_ref[...]).astype(BF16)
    o_ref[...] = _dot(xm, w_ref[...])


def _rkv(geo, x, mods, g, mu3, w3):
    tm = TM_RWKV
    return pl.pallas_call(
        functools.partial(_rkv_kernel, geo=geo, tm=tm),
        grid=(geo.n_tok // tm, 3),
        in_specs=_halo_specs(geo, tm) + [
            pl.BlockSpec((None, 1, D_MODEL), lambda i, p: (p, 0, 0)),
            pl.BlockSpec((None, D_MODEL, D_MODEL), lambda i, p: (p, 0, 0)),
        ],
        out_specs=pl.BlockSpec((None, tm, D_MODEL), lambda i, p: (p, i, 0)),
        out_shape=jax.ShapeDtypeStruct((3, geo.n_tok, D_MODEL), F32),
        scratch_shapes=[pltpu.VMEM((tm, D_MODEL), F32), pltpu.VMEM((tm, D_MODEL), F32)],
        compiler_params=_params("parallel", "arbitrary"),
        name="rwkv_rkv",
    )(x, x, x, mods, g, mu3, w3)


def _lora_kernel(x_ref, xp_ref, xn_ref, m_ref, g_ref, mug_ref, mud_ref, g1_ref, g2_ref,
                 wa1_ref, wa2_ref, w0_ref, a0_ref, gate_ref, dec_ref, asig_ref, *, geo, tm):
    h, xx = _shifted(geo, x_ref, xp_ref, xn_ref, m_ref, g_ref, tm)
    mix = lambda mu: (h + xx * mu).astype(BF16)
    zg = _dot(mix(mug_ref[...]), g1_ref[...])
    gate_ref[...] = _dot(_sigmoid(zg).astype(BF16), g2_ref[...])
    for d in range(2):
        zw = _dot(mix(mud_ref[2 * d]), wa1_ref[d])
        wl = w0_ref[d] + _dot(jnp.tanh(zw).astype(BF16), wa2_ref[d])
        u = -wl
        softplus = jnp.maximum(u, 0.0) + jnp.log1p(jnp.exp(-jnp.abs(u)))
        dec_ref[d] = jnp.exp(-jnp.exp(-softplus - 0.5))
        za = _dot(mix(mud_ref[2 * d + 1]), wa1_ref[2 + d])
        asig_ref[d] = _sigmoid(a0_ref[d] + _dot(za.astype(BF16), wa2_ref[2 + d]))


def _lora(geo, x, mods, g, mu_g, mu_dir, g1, g2, wa1, wa2, w0, a0):
    tm = TM_RWKV
    full = lambda a: pl.BlockSpec(a.shape, lambda i: (0,) * a.ndim)
    tok2 = pl.BlockSpec((2, tm, D_MODEL), lambda i: (0, i, 0))
    return pl.pallas_call(
        functools.partial(_lora_kernel, geo=geo, tm=tm),
        grid=(geo.n_tok // tm,),
        in_specs=_halo_specs(geo, tm) + [full(a) for a in (mu_g, mu_dir, g1, g2, wa1, wa2, w0, a0)],
        out_specs=[pl.BlockSpec((tm, D_MODEL), lambda i: (i, 0)), tok2, tok2],
        out_shape=[jax.ShapeDtypeStruct((geo.n_tok, D_MODEL), F32),
                   jax.ShapeDtypeStruct((2, geo.n_tok, D_MODEL), F32),
                   jax.ShapeDtypeStruct((2, geo.n_tok, D_MODEL), F32)],
        compiler_params=_params("parallel"),
        name="rwkv_lora",
    )(x, x, x, mods, g, mu_g, mu_dir, g1, g2, wa1, wa2, w0, a0)


def _scan_kernel(*refs, tc, zero_init):
    if zero_init:
        r_ref, k_ref, v_ref, dec_ref, asig_ref, kk_ref, ka_ref, rk_ref, y_ref, bon_ref, sf_ref, s_ref = refs
    else:
        (r_ref, k_ref, v_ref, dec_ref, asig_ref, kk_ref, ka_ref, rk_ref, s0_ref,
         y_ref, bon_ref, sf_ref, s_ref) = refs
    d = pl.program_id(1)
    c = pl.program_id(2)
    n = RWKV_HEAD

    @pl.when(c == 0)
    def _():
        if zero_init:
            s_ref[...] = jnp.zeros_like(s_ref)
        else:
            s_ref[...] = s0_ref[...]

    kk_c, ka_c, rk_c = kk_ref[...], ka_ref[...], rk_ref[...]

    def step(tt, carry):
        t = jnp.where(d == 0, tt, tc - 1 - tt)
        r, k, v, w, a = r_ref[t], k_ref[t], v_ref[t], dec_ref[t], asig_ref[t]
        kk = k * kk_c
        nrm = jnp.sqrt(jnp.sum(kk * kk, axis=0, keepdims=True))
        kk = kk / jnp.maximum(nrm, 1e-12)
        a_in = -kk
        b_in = kk * a
        kd = k * (1.0 + (a - 1.0) * ka_c)
        bon_ref[t] = jnp.sum(r * kd * rk_c, axis=0, keepdims=True) * v

        def row(i, carry2):
            s_old = s_ref[i]
            sa = jnp.sum(s_old * a_in, axis=0, keepdims=True)
            s_new = s_old * w + sa * b_in + v_ref[t, pl.ds(i, 1), :] * kd
            s_ref[i] = s_new
            y_ref[t, pl.ds(i, 1), :] = jnp.sum(s_new * r, axis=0, keepdims=True)
            return carry2

        lax.fori_loop(0, n, row, 0, unroll=4)
        return carry

    lax.fori_loop(0, tc, step, 0)

    @pl.when(c == pl.num_programs(2) - 1)
    def _():
        sf_ref[...] = s_ref[...]


def _scan(r, k, v, dec, asig, kk_c, ka_c, rk_c, s0):
    t_len, n, p = r.shape
    groups = p // LANES
    tc = SCAN_TC
    nc = t_len // tc
    zero_init = s0 is None
    tchunk = lambda d, c: c + d * (nc - 1 - 2 * c)
    shared = pl.BlockSpec((tc, n, LANES), lambda g, d, c: (tchunk(d, c), 0, g))
    perdir = pl.BlockSpec((None, tc, n, LANES), lambda g, d, c: (d, tchunk(d, c), 0, g))
    const = pl.BlockSpec((n, LANES), lambda g, d, c: (0, 0))
    in_specs = [shared, shared, shared, perdir, perdir, const, const, const]
    args = [r, k, v, dec, asig, kk_c, ka_c, rk_c]
    if not zero_init:
        in_specs.append(pl.BlockSpec((None, n, n, LANES), lambda g, d, c: (d, 0, 0, g)))
        args.append(s0)
    return pl.pallas_call(
        functools.partial(_scan_kernel, tc=tc, zero_init=zero_init),
        grid=(groups, 2, nc),
        in_specs=in_specs,
        out_specs=[perdir, perdir, pl.BlockSpec((None, n, n, LANES), lambda g, d, c: (d, 0, 0, g))],
        out_shape=[jax.ShapeDtypeStruct((2, t_len, n, p), F32),
                   jax.ShapeDtypeStruct((2, t_len, n, p), F32),
                   jax.ShapeDtypeStruct((2, n, n, p), F32)],
        scratch_shapes=[pltpu.VMEM((n, n, LANES), F32)],
        compiler_params=_params("parallel", "parallel", "arbitrary"),
        name="wkv_scan",
    )(*args)


def _post_kernel(y_ref, bon_ref, lw_ref, lb_ref, o_ref):
    y = y_ref[0] + y_ref[1]
    mean = jnp.mean(y, axis=1, keepdims=True)
    yc = y - mean
    var = jnp.mean(yc * yc, axis=1, keepdims=True)
    o_ref[...] = yc * lax.rsqrt(var + GN_EPS) * lw_ref[...] + lb_ref[...] + bon_ref[0] + bon_ref[1]


def _post(y, bon, lw_c, lb_c):
    _, t_len, n, p = y.shape
    tc = POST_TC
    both = pl.BlockSpec((2, tc, n, LANES), lambda t, g: (0, t, 0, g))
    const = pl.BlockSpec((n, LANES), lambda t, g: (0, 0))
    return pl.pallas_call(
        _post_kernel,
        grid=(t_len // tc, p // LANES),
        in_specs=[both, both, const, const],
        out_specs=pl.BlockSpec((tc, n, LANES), lambda t, g: (t, 0, g)),
        out_shape=jax.ShapeDtypeStruct((t_len, n, p), F32),
        compiler_params=_params("parallel", "parallel"),
        name="wkv_post",
    )(y, bon, lw_c, lb_c)


def _to_scan(geo, a):
    lead = a.shape[:-2]
    nl = len(lead)
    perm = tuple(range(nl)) + (nl + 1, nl + 3, nl + 0, nl + 2)
    ctx = a[..., :geo.n_ctx, :].reshape(lead + (geo.b_ctx, geo.s_ctx, RWKV_HEADS, RWKV_HEAD))
    lat = a[..., geo.n_ctx:, :].reshape(lead + (geo.b_lat, geo.s_lat, RWKV_HEADS, RWKV_HEAD))
    ctx = ctx.transpose(perm).reshape(lead + (geo.s_ctx, RWKV_HEAD, geo.b_ctx * RWKV_HEADS))
    lat = lat.transpose(perm).reshape(lead + (geo.s_lat, RWKV_HEAD, geo.b_lat * RWKV_HEADS))
    return ctx, lat


def _from_scan(geo, ctx, lat):
    ctx = ctx.reshape(geo.s_ctx, RWKV_HEAD, geo.b_ctx, RWKV_HEADS).transpose(2, 0, 3, 1)
    lat = lat.reshape(geo.s_lat, RWKV_HEAD, geo.b_lat, RWKV_HEADS).transpose(2, 0, 3, 1)
    return jnp.concatenate([ctx.reshape(geo.n_ctx, D_MODEL), lat.reshape(geo.n_lat, D_MODEL)], axis=0)


def _head_tile(vec):
    t = vec.reshape(RWKV_HEADS, RWKV_HEAD).T
    return jnp.tile(t, (1, LANES // RWKV_HEADS))


def _state_to_scan(s):
    b, h, n, _ = s.shape
    return s.transpose(2, 3, 0, 1).reshape(n, n, b * h)


def _state_from_scan(s, b):
    n = s.shape[0]
    return s.reshape(n, n, b, RWKV_HEADS).transpose(2, 3, 0, 1)


def _rope_tables(geo):
    rows = geo.s_lat // GRID_W
    row = jnp.repeat(jnp.arange(rows, dtype=F32), GRID_W)
    col = jnp.tile(jnp.arange(GRID_W, dtype=F32), rows)
    inv = ROPE_THETA ** (-jnp.arange(ROPE_FREQS, dtype=F32) / ROPE_FREQS)
    ang = jnp.stack([row[:, None] * inv, col[:, None] * inv], axis=1)
    ang = jnp.broadcast_to(ang[:, :, None, :], (geo.s_lat, 2, 2, ROPE_FREQS)).reshape(geo.s_lat, QK_ROPE)
    cos = jnp.concatenate([jnp.ones((geo.n_ctx, QK_ROPE), F32), jnp.tile(jnp.cos(ang), (geo.b_lat, 1))], axis=0)
    sin = jnp.concatenate([jnp.zeros((geo.n_ctx, QK_ROPE), F32), jnp.tile(jnp.sin(ang), (geo.b_lat, 1))], axis=0)
    return cos, sin


def _rot_cols(w):
    w4 = w.reshape(w.shape[:-1] + (2, 2, ROPE_FREQS))
    return jnp.stack([-w4[..., 1, :], w4[..., 0, :]], axis=-2).reshape(w.shape)


def _mla_layer(geo, x, mods, g, j, cos, sin, cache_ckv, cache_krope, p):
    w_dkv = p['mla_w_dkv'][j]
    w_cat = jnp.concatenate([p['mla_w_dq'][j], w_dkv, _rot_cols(w_dkv[:, KV_LORA:])], axis=1).astype(BF16)
    cq, ckv, ckvb, kr = _mla_proj(geo, x, mods, g, w_cat, p['mla_q_norm'][j][None], p['mla_kv_norm'][j][None],
                                  cos, sin)
    w_uq = p['mla_w_uq'][j].reshape(Q_LORA, MLA_HEADS, QK_NOPE + QK_ROPE).transpose(1, 0, 2)
    wq = w_uq.astype(BF16)
    wqr = _rot_cols(w_uq[..., QK_NOPE:]).astype(BF16)
    w_ukv = p['mla_w_ukv'][j].reshape(KV_LORA, MLA_HEADS, QK_NOPE + V_DIM).transpose(1, 0, 2)
    wk = w_ukv[..., :QK_NOPE].astype(BF16)
    wv = w_ukv[..., QK_NOPE:].astype(BF16)
    o_ctx = _attn(geo, cq, ckvb, kr, wq, wqr, wk, wv, latent=False)
    o_lat = _attn(geo, cq, ckvb, kr, wq, wqr, wk, wv, latent=True,
                  past_ckv=cache_ckv[:, j], past_kr=cache_krope[:, j], cos=cos, sin=sin)
    o = jnp.concatenate([o_ctx, o_lat], axis=0)
    x = _out_proj(geo, o, None, p['mla_w_o'][j].astype(BF16), x, mods)
    new_ckv = ckv[:geo.n_ctx].reshape(geo.b_ctx, geo.s_ctx, KV_LORA)
    new_kr = kr[:geo.n_ctx].reshape(geo.b_ctx, geo.s_ctx, QK_ROPE)
    return x, new_ckv, new_kr


def _rwkv_layer(geo, x, mods, g, j, s0_fwd, s0_bwd, p):
    mu = p['rwkv_mu'][j]
    mu_dir = p['rwkv_mu_dir'][j].reshape(4, 1, D_MODEL)
    w3 = jnp.stack([p['rwkv_w_r'][j], p['rwkv_w_k'][j], p['rwkv_w_v'][j]]).astype(BF16)
    rkv = _rkv(geo, x, mods, g, mu[:3, None, :], w3)
    pad1 = lambda w: jnp.pad(w, ((0, 0), (0, 0), (0, LORA_PAD - w.shape[-1])))
    pad2 = lambda w: jnp.pad(w, ((0, 0), (0, LORA_PAD - w.shape[-2]), (0, 0)))
    wa1 = jnp.concatenate([pad1(p['rwkv_w1'][j]), pad1(p['rwkv_a1'][j])]).astype(BF16)
    wa2 = jnp.concatenate([pad2(p['rwkv_w2'][j]), pad2(p['rwkv_a2'][j])]).astype(BF16)
    gate, dec, asig = _lora(geo, x, mods, g, mu[3:4], mu_dir, p['rwkv_g1'][j].astype(BF16),
                            p['rwkv_g2'][j].astype(BF16), wa1, wa2,
                            p['rwkv_w0'][j][:, None, :], p['rwkv_a0'][j][:, None, :])
    rkv_c, rkv_l = _to_scan(geo, rkv)
    dec_c, dec_l = _to_scan(geo, dec)
    as_c, as_l = _to_scan(geo, asig)
    kk_c = _head_tile(p['rwkv_k_k'][j])
    ka_c = _head_tile(p['rwkv_k_a'][j])
    rk_c = _head_tile(p['rwkv_r_k'][j].reshape(D_MODEL))
    lw_c = _head_tile(p['rwkv_ln_w'][j])
    lb_c = _head_tile(p['rwkv_ln_b'][j])
    y_c, bon_c, sf_c = _scan(rkv_c[0], rkv_c[1], rkv_c[2], dec_c, as_c, kk_c, ka_c, rk_c, None)
    s0 = jnp.stack([_state_to_scan(s0_fwd), _state_to_scan(s0_bwd)])
    y_l, bon_l, _ = _scan(rkv_l[0], rkv_l[1], rkv_l[2], dec_l, as_l, kk_c, ka_c, rk_c, s0)
    z = _from_scan(geo, _post(y_c, bon_c, lw_c, lb_c), _post(y_l, bon_l, lw_c, lb_c))
    x = _out_proj(geo, z, gate, p['rwkv_w_o'][j].astype(BF16), x, mods)
    return x, _state_from_scan(sf_c[0], geo.b_ctx), _state_from_scan(sf_c[1], geo.b_ctx)


def kernel(x_prompt, x_sample, cache_ckv, cache_krope, state_wkv_fwd, state_wkv_bwd, c, c_ctx, w_ada, b_ada, norm_sub, norm_final, w_ffn_in, w_ffn_out, mla_w_dq, mla_q_norm, mla_w_uq, mla_w_dkv, mla_kv_norm, mla_w_ukv, mla_w_o, rwkv_mu, rwkv_mu_dir, rwkv_w_r, rwkv_w_k, rwkv_w_v, rwkv_w0, rwkv_w1, rwkv_w2, rwkv_a0, rwkv_a1, rwkv_a2, rwkv_g1, rwkv_g2, rwkv_k_k, rwkv_k_a, rwkv_r_k, rwkv_ln_w, rwkv_ln_b, rwkv_w_o):
    p = dict(mla_w_dq=mla_w_dq, mla_q_norm=mla_q_norm, mla_w_uq=mla_w_uq, mla_w_dkv=mla_w_dkv,
             mla_kv_norm=mla_kv_norm, mla_w_ukv=mla_w_ukv, mla_w_o=mla_w_o,
             rwkv_mu=rwkv_mu, rwkv_mu_dir=rwkv_mu_dir, rwkv_w_r=rwkv_w_r, rwkv_w_k=rwkv_w_k,
             rwkv_w_v=rwkv_w_v, rwkv_w0=rwkv_w0, rwkv_w1=rwkv_w1, rwkv_w2=rwkv_w2,
             rwkv_a0=rwkv_a0, rwkv_a1=rwkv_a1, rwkv_a2=rwkv_a2, rwkv_g1=rwkv_g1, rwkv_g2=rwkv_g2,
             rwkv_k_k=rwkv_k_k, rwkv_k_a=rwkv_k_a, rwkv_r_k=rwkv_r_k,
             rwkv_ln_w=rwkv_ln_w, rwkv_ln_b=rwkv_ln_b, rwkv_w_o=rwkv_w_o)
    b_ctx, s_ctx, _ = x_prompt.shape
    b_lat, s_lat, _ = x_sample.shape
    geo = _Geom(b_ctx, s_ctx, b_lat, s_lat)
    assert geo.n_ctx % s_lat == 0 and s_ctx % TM_RWKV == 0 and s_lat % TM_FFN == 0
    assert LANES % RWKV_HEADS == 0 and b_lat * RWKV_HEADS == LANES and (b_ctx * RWKV_HEADS) % LANES == 0

    x = jnp.concatenate([x_prompt.reshape(geo.n_ctx, D_MODEL), x_sample.reshape(geo.n_lat, D_MODEL)], axis=0)
    cond = jnp.concatenate([c_ctx[None], c, jnp.zeros((COND_ROWS - 1 - b_lat, D_MODEL), F32)], axis=0)
    mods_all = _ada(cond, w_ada, b_ada).reshape(DEPTH, COND_ROWS, N_MOD, D_MODEL)
    cos, sin = _rope_tables(geo)
    nf = norm_final[None]

    ckv_l, kr_l, sf_l, sb_l = [], [], [], []
    for l in range(DEPTH):
        j = l // N_MIXERS
        mods = mods_all[l]
        w_in = w_ffn_in[l].astype(BF16)
        w_out = w_ffn_out[l].astype(BF16)
        x = _ffn(geo, x, mods, norm_sub[l, 0][None], w_in[0], w_out[0], nf, sub=0, final=False)
        if l % N_MIXERS == 0:
            x, ckv, kr = _mla_layer(geo, x, mods, norm_sub[l, 1][None], j, cos, sin, cache_ckv, cache_krope, p)
            ckv_l.append(ckv)
            kr_l.append(kr)
        else:
            x, sf, sb = _rwkv_layer(geo, x, mods, norm_sub[l, 1][None], j,
                                    state_wkv_fwd[:, j], state_wkv_bwd[:, j], p)
            sf_l.append(sf)
            sb_l.append(sb)
        x = _ffn(geo, x, mods, norm_sub[l, 2][None], w_in[1], w_out[1], nf, sub=2, final=(l == DEPTH - 1))

    y_prompt = x[:geo.n_ctx].reshape(b_ctx, s_ctx, D_MODEL)
    y_sample = x[geo.n_ctx:].reshape(b_lat, s_lat, D_MODEL)
    return (y_prompt, y_sample, jnp.stack(ckv_l, axis=1), jnp.stack(kr_l, axis=1),
            jnp.stack(sf_l, axis=1), jnp.stack(sb_l, axis=1))
```

```python
import functools

import jax
import jax.numpy as jnp
from jax import lax
from jax.experimental import pallas as pl
from jax.experimental.pallas import tpu as pltpu

D_MODEL = 2048
DEPTH = 4
N_MIXERS = 2
D_FF = 5632
N_MOD = 9
RMS_EPS = 1e-6
MLA_HEADS = 16
Q_LORA = 512
KV_LORA = 512
QK_NOPE = 128
QK_ROPE = 64
V_DIM = 128
ROPE_FREQS = QK_ROPE // 4
ROPE_THETA = 10000.0
GRID_W = 64
ATTN_SCALE = (QK_NOPE + QK_ROPE) ** -0.5
RWKV_HEAD = 64
RWKV_HEADS = D_MODEL // RWKV_HEAD
GN_EPS = 64e-5
LORA_PAD = 128

COND_ROWS = 8
VMEM_LIMIT = 56 * 1024 * 1024

TM_FFN = 512
TF_FFN = 512
TM_PROJ = 512
TM_RWKV = 256
TN_ADA = 1024
Q_TILE = 256
CHUNK = 64
PAIR = 2 * RWKV_HEAD
N_PAIRS = D_MODEL // PAIR
PAIR_GROUP = 8
TB_WKV = 256
TM_OUT = 256

BF16 = jnp.bfloat16
F32 = jnp.float32


def _params(*sem):
    return pltpu.CompilerParams(dimension_semantics=sem, vmem_limit_bytes=VMEM_LIMIT)


def _sigmoid(x):
    return 1.0 / (1.0 + jnp.exp(-x))


def _modulate(x, g, shift, scale):
    ms = jnp.mean(x * x, axis=-1, keepdims=True)
    return (x * lax.rsqrt(ms + RMS_EPS) * g) * (1.0 + scale) + shift


def _rms(x, w):
    ms = jnp.mean(x * x, axis=-1, keepdims=True)
    return x * lax.rsqrt(ms + RMS_EPS) * w


def _dot(a, b):
    return jnp.dot(a, b, preferred_element_type=F32)


def _dot_nt(a, b):
    return lax.dot_general(a, b, (((1,), (1,)), ((), ())), preferred_element_type=F32)


class _Geom:
    def __init__(self, b_ctx, s_ctx, b_lat, s_lat):
        self.b_ctx, self.s_ctx, self.b_lat, self.s_lat = b_ctx, s_ctx, b_lat, s_lat
        self.n_ctx = b_ctx * s_ctx
        self.n_lat = b_lat * s_lat
        self.n_tok = self.n_ctx + self.n_lat

    def cond_row(self, i, tm):
        start = i * tm
        return jnp.where(start < self.n_ctx, 0, 1 + (start - self.n_ctx) // self.s_lat)


def _ada_kernel(c_ref, w_ref, b_ref, o_ref):
    c = c_ref[...]
    s = (c * _sigmoid(c)).astype(BF16)
    o_ref[...] = _dot(s, w_ref[...].astype(BF16)) + b_ref[...]


def _ada(cond, w_ada, b_ada):
    n = N_MOD * D_MODEL
    return pl.pallas_call(
        _ada_kernel,
        grid=(DEPTH, n // TN_ADA),
        in_specs=[
            pl.BlockSpec((COND_ROWS, D_MODEL), lambda l, j: (0, 0)),
            pl.BlockSpec((None, D_MODEL, TN_ADA), lambda l, j: (l, 0, j)),
            pl.BlockSpec((None, 1, TN_ADA), lambda l, j: (l, 0, j)),
        ],
        out_specs=pl.BlockSpec((None, COND_ROWS, TN_ADA), lambda l, j: (l, 0, j)),
        out_shape=jax.ShapeDtypeStruct((DEPTH, COND_ROWS, n), F32),
        compiler_params=_params("parallel", "parallel"),
        name="ada",
    )(cond, w_ada, b_ada.reshape(DEPTH, 1, n))


def _ffn_kernel(x_ref, m_ref, g_ref, wg_ref, wu_ref, wo_ref, nf_ref, o_ref, h_ref, acc_ref, *, sub, final):
    f = pl.program_id(1)

    @pl.when(f == 0)
    def _():
        h = _modulate(x_ref[...], g_ref[...], m_ref[3 * sub:3 * sub + 1, :], m_ref[3 * sub + 1:3 * sub + 2, :])
        h_ref[...] = h.astype(BF16)
        acc_ref[...] = jnp.zeros_like(acc_ref)

    h = h_ref[...]
    gate = _dot(h, wg_ref[...])
    up = _dot(h, wu_ref[...])
    act = (gate * _sigmoid(gate) * up).astype(BF16)
    acc_ref[...] += _dot(act, wo_ref[...])

    @pl.when(f == pl.num_programs(1) - 1)
    def _():
        y = x_ref[...] + 0.5 * m_ref[3 * sub + 2:3 * sub + 3, :] * acc_ref[...]
        if final:
            y = _rms(y, nf_ref[...])
        o_ref[...] = y


def _ffn(geo, x, mods, g, w_in, w_out, nf, *, layer, half, sub, final):
    tm, tf = TM_FFN, TF_FFN
    nf_blocks = D_FF // tf
    kern = functools.partial(_ffn_kernel, sub=sub, final=final)
    return pl.pallas_call(
        kern,
        grid=(geo.n_tok // tm, nf_blocks),
        in_specs=[
            pl.BlockSpec((tm, D_MODEL), lambda i, f: (i, 0)),
            pl.BlockSpec((None, N_MOD, D_MODEL), lambda i, f: (geo.cond_row(i, tm), 0, 0)),
            pl.BlockSpec((1, D_MODEL), lambda i, f: (0, 0)),
            pl.BlockSpec((None, None, D_MODEL, tf), lambda i, f: (layer, half, 0, f)),
            pl.BlockSpec((None, None, D_MODEL, tf), lambda i, f: (layer, half, 0, f + nf_blocks)),
            pl.BlockSpec((None, None, tf, D_MODEL), lambda i, f: (layer, half, f, 0)),
            pl.BlockSpec((1, D_MODEL), lambda i, f: (0, 0)),
        ],
        out_specs=pl.BlockSpec((tm, D_MODEL), lambda i, f: (i, 0)),
        out_shape=jax.ShapeDtypeStruct((geo.n_tok, D_MODEL), F32),
        scratch_shapes=[pltpu.VMEM((tm, D_MODEL), BF16), pltpu.VMEM((tm, D_MODEL), F32)],
        compiler_params=_params("parallel", "arbitrary"),
        name="ffn",
    )(x, mods, g, w_in, w_in, w_out, nf)


def _mla_proj_kernel(x_ref, m_ref, g_ref, w_ref, qn_ref, kvn_ref, cos_ref, sin_ref,
                     cq_ref, ckv_ref, ckvb_ref, kr_ref):
    h = _modulate(x_ref[...], g_ref[...], m_ref[3:4, :], m_ref[4:5, :]).astype(BF16)
    z = _dot(h, w_ref[...])
    cq_ref[...] = _rms(z[:, :Q_LORA], qn_ref[...]).astype(BF16)
    ckv = _rms(z[:, Q_LORA:Q_LORA + KV_LORA], kvn_ref[...])
    ckv_ref[...] = ckv
    ckvb_ref[...] = ckv.astype(BF16)
    o = Q_LORA + KV_LORA
    kr_ref[...] = z[:, o:o + QK_ROPE] * cos_ref[...] + z[:, o + QK_ROPE:o + 2 * QK_ROPE] * sin_ref[...]


def _mla_proj(geo, x, mods, g, w_cat, qn, kvn, cos, sin):
    tm = TM_PROJ
    n_out = w_cat.shape[1]
    row = lambda i: (i, 0)
    fix = lambda i: (0, 0)
    return pl.pallas_call(
        _mla_proj_kernel,
        grid=(geo.n_tok // tm,),
        in_specs=[
            pl.BlockSpec((tm, D_MODEL), row),
            pl.BlockSpec((None, N_MOD, D_MODEL), lambda i: (geo.cond_row(i, tm), 0, 0)),
            pl.BlockSpec((1, D_MODEL), fix),
            pl.BlockSpec((D_MODEL, n_out), fix),
            pl.BlockSpec((1, Q_LORA), fix),
            pl.BlockSpec((1, KV_LORA), fix),
            pl.BlockSpec((tm, QK_ROPE), row),
            pl.BlockSpec((tm, QK_ROPE), row),
        ],
        out_specs=[
            pl.BlockSpec((tm, Q_LORA), row),
            pl.BlockSpec((tm, KV_LORA), row),
            pl.BlockSpec((tm, KV_LORA), row),
            pl.BlockSpec((tm, QK_ROPE), row),
        ],
        out_shape=[
            jax.ShapeDtypeStruct((geo.n_tok, Q_LORA), BF16),
            jax.ShapeDtypeStruct((geo.n_tok, KV_LORA), F32),
            jax.ShapeDtypeStruct((geo.n_tok, KV_LORA), BF16),
            jax.ShapeDtypeStruct((geo.n_tok, QK_ROPE), F32),
        ],
        compiler_params=_params("parallel"),
        name="mla_proj",
    )(x, mods, g, w_cat, qn, kvn, cos, sin)


def _attn_kernel(*refs, s_len, t_past, rope):
    if rope:
        (cq_ref, ckvb_ref, kr_ref, pckv_ref, pkr_ref, cos_ref, sin_ref,
         wq_ref, wqr_ref, wk_ref, wv_ref, o_ref, kv_all, kr_all) = refs
    else:
        cq_ref, ckvb_ref, kr_ref, wq_ref, wk_ref, wv_ref, o_ref, kv_all, kr_all = refs
    if t_past:
        kv_all[0:t_past, :] = pckv_ref[...].astype(BF16)
        kr_all[0:t_past, :] = pkr_ref[...].astype(BF16)
    kv_all[t_past:, :] = ckvb_ref[...]
    kr_all[t_past:, :] = kr_ref[...].astype(BF16)

    def head(hd, carry):
        kv = kv_all[...]
        kn = _dot(kv, wk_ref[hd]).astype(BF16)
        v = _dot(kv, wv_ref[hd]).astype(BF16)
        krb = kr_all[...]
        col = pl.multiple_of(hd * V_DIM, V_DIM)
        for qb in range(s_len // Q_TILE):
            rows = slice(qb * Q_TILE, (qb + 1) * Q_TILE)
            cq = cq_ref[rows, :]
            q = _dot(cq, wq_ref[hd])
            qn = q[:, :QK_NOPE]
            qr = q[:, QK_NOPE:]
            if rope:
                qr = qr * cos_ref[rows, :] + _dot(cq, wqr_ref[hd]) * sin_ref[rows, :]
            s = (_dot_nt(qn.astype(BF16), kn) + _dot_nt(qr.astype(BF16), krb)) * ATTN_SCALE
            p = jnp.exp(s - jnp.max(s, axis=-1, keepdims=True))
            pr = (p / jnp.sum(p, axis=-1, keepdims=True)).astype(BF16)
            o_ref[rows, pl.ds(col, V_DIM)] = _dot(pr, v).astype(BF16)
        return carry

    lax.fori_loop(0, MLA_HEADS, head, 0)


def _attn(geo, cq, ckvb, kr, wq, wqr, wk, wv, *, latent, past_ckv=None, past_kr=None, cos=None, sin=None):
    if latent:
        nb, s_len, off = geo.b_lat, geo.s_lat, geo.n_ctx // geo.s_lat
        t_past = past_ckv.shape[1]
    else:
        nb, s_len, off, t_past = geo.b_ctx, geo.s_ctx, 0, 0
    row = lambda b: (b + off, 0)
    fix3 = lambda b: (0, 0, 0)
    tok = lambda width: pl.BlockSpec((s_len, width), row)
    wspec = lambda w: pl.BlockSpec(w.shape, fix3)
    in_specs = [tok(Q_LORA), tok(KV_LORA), tok(QK_ROPE)]
    args = [cq, ckvb, kr]
    if latent:
        in_specs += [pl.BlockSpec((None, t_past, KV_LORA), lambda b: (b, 0, 0)),
                     pl.BlockSpec((None, t_past, QK_ROPE), lambda b: (b, 0, 0)),
                     tok(QK_ROPE), tok(QK_ROPE), wspec(wq), wspec(wqr)]
        args += [past_ckv, past_kr, cos, sin, wq, wqr]
    else:
        in_specs += [wspec(wq)]
        args += [wq]
    in_specs += [wspec(wk), wspec(wv)]
    args += [wk, wv]
    kern = functools.partial(_attn_kernel, s_len=s_len, t_past=t_past, rope=latent)
    return pl.pallas_call(
        kern,
        grid=(nb,),
        in_specs=in_specs,
        out_specs=pl.BlockSpec((s_len, D_MODEL), lambda b: (b, 0)),
        out_shape=jax.ShapeDtypeStruct((nb * s_len, D_MODEL), BF16),
        scratch_shapes=[pltpu.VMEM((t_past + s_len, KV_LORA), BF16),
                        pltpu.VMEM((t_past + s_len, QK_ROPE), BF16)],
        compiler_params=_params("parallel"),
        name="attn_lat" if latent else "attn_ctx",
    )(*args)


def _out_proj_kernel(a_ref, w_ref, x_ref, m_ref, o_ref):
    o_ref[...] = x_ref[...] + m_ref[5:6, :] * _dot(a_ref[...], w_ref[...])


def _out_proj(geo, a, w, x, mods):
    tm = TM_PROJ
    row = pl.BlockSpec((tm, D_MODEL), lambda i: (i, 0))
    return pl.pallas_call(
        _out_proj_kernel,
        grid=(geo.n_tok // tm,),
        in_specs=[row, pl.BlockSpec((D_MODEL, D_MODEL), lambda i: (0, 0)), row,
                  pl.BlockSpec((None, N_MOD, D_MODEL), lambda i: (geo.cond_row(i, tm), 0, 0))],
        out_specs=row,
        out_shape=jax.ShapeDtypeStruct((geo.n_tok, D_MODEL), F32),
        compiler_params=_params("parallel"),
        name="out_proj",
    )(a, w, x, mods)


def _shifted(geo, x_ref, xp_ref, xn_ref, m_ref, g_ref, tm):
    i = pl.program_id(0)
    g, shift, scale = g_ref[...], m_ref[3:4, :], m_ref[4:5, :]
    h = _modulate(x_ref[...], g, shift, scale)
    start = i * tm
    seq = jnp.where(start < geo.n_ctx, geo.s_ctx, geo.s_lat)
    rel = jnp.where(start < geo.n_ctx, start, start - geo.n_ctx)
    has_prev = (rel % seq) != 0
    has_next = ((rel + tm) % seq) != 0
    hp = _modulate(xp_ref[...], g, shift, scale)[7:8, :]
    hn = _modulate(xn_ref[...], g, shift, scale)[0:1, :]
    hp = jnp.where(has_prev, hp, 0.0)
    hn = jnp.where(has_next, hn, 0.0)
    r = lax.broadcasted_iota(jnp.int32, h.shape, 0)
    down = jnp.where(r == 0, hp, pltpu.roll(h, 1, 0))
    up = jnp.where(r == tm - 1, hn, pltpu.roll(h, tm - 1, 0))
    return h, 0.5 * (down + up) - h


def _halo_specs(geo, tm):
    nb8 = geo.n_tok // 8
    return [
        pl.BlockSpec((tm, D_MODEL), lambda i, *_: (i, 0)),
        pl.BlockSpec((8, D_MODEL), lambda i, *_: (jnp.maximum(i * (tm // 8) - 1, 0), 0)),
        pl.BlockSpec((8, D_MODEL), lambda i, *_: (jnp.minimum((i + 1) * (tm // 8), nb8 - 1), 0)),
        pl.BlockSpec((None, N_MOD, D_MODEL), lambda i, *_: (geo.cond_row(i, tm), 0, 0)),
        pl.BlockSpec((1, D_MODEL), lambda i, *_: (0, 0)),
    ]


def _rkv_kernel(x_ref, xp_ref, xn_ref, m_ref, g_ref, mu_ref, w_ref, o_ref, h_ref, xx_ref, *, geo, tm):
    @pl.when(pl.program_id(1) == 0)
    def _():
        h, xx = _shifted(geo, x_ref, xp_ref, xn_ref, m_ref, g_ref, tm)
        h_ref[...] = h
        xx_ref[...] = xx

    xm = (h_ref[...] + xx_ref[...] * mu_ref[...]).astype(BF16)
    o_ref[...] = _dot(xm, w_ref[...])


def _rkv(geo, x, mods, g, mu3, w3):
    tm = TM_RWKV
    return pl.pallas_call(
        functools.partial(_rkv_kernel, geo=geo, tm=tm),
        grid=(geo.n_tok // tm, 3),
        in_specs=_halo_specs(geo, tm) + [
            pl.BlockSpec((None, 1, D_MODEL), lambda i, p: (p, 0, 0)),
            pl.BlockSpec((None, D_MODEL, D_MODEL), lambda i, p: (p, 0, 0)),
        ],
        out_specs=pl.BlockSpec((None, tm, D_MODEL), lambda i, p: (p, i, 0)),
        out_shape=jax.ShapeDtypeStruct((3, geo.n_tok, D_MODEL), F32),
        scratch_shapes=[pltpu.VMEM((tm, D_MODEL), F32), pltpu.VMEM((tm, D_MODEL), F32)],
        compiler_params=_params("parallel", "arbitrary"),
        name="rwkv_rkv",
    )(x, x, x, mods, g, mu3, w3)


def _lora_kernel(x_ref, xp_ref, xn_ref, m_ref, g_ref, mug_ref, mud_ref, g1_ref, g2_ref,
                 wa1_ref, wa2_ref, w0_ref, a0_ref, gate_ref, dec_ref, asig_ref, *, geo, tm):
    h, xx = _shifted(geo, x_ref, xp_ref, xn_ref, m_ref, g_ref, tm)
    mix = lambda mu: (h + xx * mu).astype(BF16)
    zg = _dot(mix(mug_ref[...]), g1_ref[...])
    gate_ref[...] = _dot(_sigmoid(zg).astype(BF16), g2_ref[...])
    for d in range(2):
        zw = _dot(mix(mud_ref[2 * d]), wa1_ref[d])
        wl = w0_ref[d] + _dot(jnp.tanh(zw).astype(BF16), wa2_ref[d])
        u = -wl
        softplus = jnp.maximum(u, 0.0) + jnp.log1p(jnp.exp(-jnp.abs(u)))
        dec_ref[d] = jnp.exp(-jnp.exp(-softplus - 0.5))
        za = _dot(mix(mud_ref[2 * d + 1]), wa1_ref[2 + d])
        asig_ref[d] = _sigmoid(a0_ref[d] + _dot(za.astype(BF16), wa2_ref[2 + d]))


def _lora(geo, x, mods, g, mu_g, mu_dir, g1, g2, wa1, wa2, w0, a0):
    tm = TM_RWKV
    full = lambda a: pl.BlockSpec(a.shape, lambda i: (0,) * a.ndim)
    tok2 = pl.BlockSpec((2, tm, D_MODEL), lambda i: (0, i, 0))
    return pl.pallas_call(
        functools.partial(_lora_kernel, geo=geo, tm=tm),
        grid=(geo.n_tok // tm,),
        in_specs=_halo_specs(geo, tm) + [full(a) for a in (mu_g, mu_dir, g1, g2, wa1, wa2, w0, a0)],
        out_specs=[pl.BlockSpec((tm, D_MODEL), lambda i: (i, 0)), tok2, tok2],
        out_shape=[jax.ShapeDtypeStruct((geo.n_tok, D_MODEL), F32),
                   jax.ShapeDtypeStruct((2, geo.n_tok, D_MODEL), F32),
                   jax.ShapeDtypeStruct((2, geo.n_tok, D_MODEL), F32)],
        compiler_params=_params("parallel"),
        name="rwkv_lora",
    )(x, x, x, mods, g, mu_g, mu_dir, g1, g2, wa1, wa2, w0, a0)


def _split2(x):
    hi = x.astype(BF16)
    return hi, (x - hi.astype(F32)).astype(BF16)


def _head_sums(x, ones_blk):
    hi, lo = _split2(x)
    return _dot(hi, ones_blk) + _dot(lo, ones_blk)


def _pair_ones():
    r = lax.broadcasted_iota(jnp.int32, (PAIR, PAIR), 0)
    c = lax.broadcasted_iota(jnp.int32, (PAIR, PAIR), 1)
    same = (r < RWKV_HEAD) == (c < RWKV_HEAD)
    return same, jnp.where(same, 1.0, 0.0).astype(BF16)


def _wkv_kernel(r_ref, k_ref, v_ref, dec_ref, asig_ref, kk_ref, ka_ref, rk_ref, s0_ref,
                y_ref, bon_ref, sf_ref, s_ref, *, geo, tb):
    d = pl.program_id(0)
    i = pl.program_id(1)
    nblk = pl.num_programs(1)
    blk = i + d * (nblk - 1 - 2 * i)
    start = blk * tb
    seq = jnp.where(start < geo.n_ctx, geo.s_ctx, geo.s_lat)
    rel = jnp.where(start < geo.n_ctx, start, start - geo.n_ctx)
    at_lo = (rel % seq) == 0
    at_hi = ((rel + tb) % seq) == 0
    first = jnp.where(d == 0, at_lo, at_hi)
    last = jnp.where(d == 0, at_hi, at_lo)

    @pl.when(first)
    def _():
        s_ref[...] = s0_ref[...]

    nch = tb // CHUNK
    sgn = 1 - 2 * d
    fwd = (d == 0).astype(F32)
    row = lax.broadcasted_iota(jnp.int32, (CHUNK, PAIR), 0)
    lane = lax.broadcasted_iota(jnp.int32, (CHUNK, PAIR), 1)
    sidx = lane & (RWKV_HEAD - 1)
    delta = (row - sidx) * sgn
    strict = delta > 0
    incl = delta >= 0
    eye = jnp.where(row == sidx, 1.0, 0.0)
    off_masks = []
    m = 1
    while m < CHUNK:
        off_masks.append(strict & ((row // (2 * m)) == (sidx // (2 * m))) & ((row // m) != (sidx // m)))
        m *= 2
    head0 = lane < RWKV_HEAD
    tr = lax.broadcasted_iota(jnp.int32, (CHUNK, CHUNK), 0)
    ts = lax.broadcasted_iota(jnp.int32, (CHUNK, CHUNK), 1)
    tri = jnp.where((tr - ts) * sgn >= 0, 1.0, 0.0).astype(BF16)
    same_head, ones_blk = _pair_ones()

    def expand(x):
        return jnp.concatenate([jnp.where(head0, x, 0.0), jnp.where(head0, 0.0, x)], axis=0)

    def chunk_body(ci, carry):
        cc = ci * sgn + d * (nch - 1)
        rows = pl.ds(pl.multiple_of(cc * CHUNK, CHUNK), CHUNK)

        def group_body(pg, carry2):
            ps = [pg * PAIR_GROUP + q for q in range(PAIR_GROUP)]
            cols = [pl.ds(pl.multiple_of(p * PAIR, PAIR), PAIR) for p in ps]
            each = lambda f, *ls: [f(*xs) for xs in zip(*ls)]
            r = [r_ref[rows, c] for c in cols]
            k = [k_ref[rows, c] for c in cols]
            v = [v_ref[rows, c] for c in cols]
            w = [dec_ref[rows, c] for c in cols]
            a = [asig_ref[rows, c] for c in cols]
            kk = [x * kk_ref[:, c] for x, c in zip(k, cols)]
            n2 = [_head_sums(x * x, ones_blk) for x in kk]
            kk = each(lambda x, n: x / jnp.maximum(jnp.sqrt(n), 1e-12), kk, n2)
            b_in = each(lambda x, y: x * y, kk, a)
            kd = [x * (1.0 + (y - 1.0) * ka_ref[:, c]) for x, y, c in zip(k, a, cols)]
            rkd = [_head_sums(x * y * rk_ref[:, c], ones_blk) for x, y, c in zip(r, kd, cols)]
            for c, x, y in zip(cols, rkd, v):
                bon_ref[rows, c] = x * y

            logw = [jnp.log(x) for x in w]
            l1 = [x.astype(BF16) for x in logw]
            e1 = each(lambda x, y: x - y.astype(F32), logw, l1)
            l2 = [x.astype(BF16) for x in e1]
            l3 = each(lambda x, y: (x - y.astype(F32)).astype(BF16), e1, l2)
            cum = each(lambda x, y, z: _dot(tri, x) + _dot(tri, y) + _dot(tri, z), l1, l2, l3)
            tot = [fwd * x[CHUNK - 1:CHUNK, :] + (1.0 - fwd) * x[0:1, :] for x in cum]
            c_inv = [jnp.exp(-x) for x in cum]
            at = each(lambda x, y, z: -x * jnp.exp(y - z), kk, cum, logw)
            rt = each(lambda x, y: x * jnp.exp(y), r, cum)
            c_end = each(lambda x, y: jnp.exp(x - y), tot, cum)

            lhs = each(lambda x, y: jnp.concatenate([x, y], axis=0).astype(BF16), at, rt)
            rhs = each(lambda x, y, z: jnp.concatenate([expand(x * z), expand(y * z)], axis=0).astype(BF16),
                       b_in, kd, c_inv)
            g = each(_dot_nt, lhs, rhs)
            ab = [jnp.where(strict, x[:CHUNK, :PAIR], 0.0) for x in g]
            ak = [jnp.where(strict, x[:CHUNK, PAIR:], 0.0) for x in g]
            rbk = [jnp.concatenate([jnp.where(incl, x[CHUNK:, :PAIR], 0.0),
                                    jnp.where(incl, x[CHUNK:, PAIR:], 0.0)], axis=1).astype(BF16) for x in g]

            t_inv = [eye + jnp.where(off_masks[0], x, 0.0) for x in ab]
            for off in off_masks[1:]:
                lx = each(lambda x, t: _dot(jnp.where(off, x, 0.0).astype(BF16), expand(t).astype(BF16)), ab, t_inv)
                t_inv = each(lambda t, x: t + _dot(t.astype(BF16), expand(x).astype(BF16)), t_inv, lx)

            s2 = [s_ref[p] for p in ps]
            ars = each(lambda x, s: _dot_nt(x, s.astype(BF16)), lhs, s2)
            vexp = [expand(x).astype(BF16) for x in v]
            rhs_u = each(lambda x, y, z: x[:CHUNK] + _dot(y.astype(BF16), z), ars, ak, vexp)
            u = each(lambda t, x: _dot(t.astype(BF16), expand(x).astype(BF16)), t_inv, rhs_u)
            uv = each(lambda x, y: jnp.concatenate([expand(x).astype(BF16), y], axis=0), u, vexp)
            y_out = each(lambda x, y, z: x[CHUNK:] + _dot(y, z), ars, rbk, uv)
            for c, x in zip(cols, y_out):
                y_ref[rows, c] = x
            uv_t = each(lambda x, y: jnp.concatenate([x, y], axis=0).T.astype(BF16), u, v)
            bk = each(lambda x, y, z: jnp.concatenate([x * z, y * z], axis=0).astype(BF16), b_in, kd, c_end)
            upd = each(_dot, uv_t, bk)
            for p, s, x, t in zip(ps, s2, upd, tot):
                s_ref[p] = s * jnp.exp(t) + jnp.where(same_head, x, 0.0)
            return carry2

        lax.fori_loop(0, N_PAIRS // PAIR_GROUP, group_body, 0)
        return carry

    lax.fori_loop(0, nch, chunk_body, 0)

    @pl.when(last)
    def _():
        sf_ref[...] = s_ref[...]


def _wkv(geo, rkv, dec, asig, kk_c, ka_c, rk_c, s0):
    tb = TB_WKV
    nblk = geo.n_tok // tb
    blk = lambda d, i: i + d * (nblk - 1 - 2 * i)
    tok = lambda which: pl.BlockSpec((None, tb, D_MODEL), lambda d, i: (which, blk(d, i), 0))
    perdir = pl.BlockSpec((None, tb, D_MODEL), lambda d, i: (d, blk(d, i), 0))
    const = pl.BlockSpec((1, D_MODEL), lambda d, i: (0, 0))
    state = (None, None, N_PAIRS, PAIR, PAIR)
    return pl.pallas_call(
        functools.partial(_wkv_kernel, geo=geo, tb=tb),
        grid=(2, nblk),
        in_specs=[tok(0), tok(1), tok(2), perdir, perdir, const, const, const,
                  pl.BlockSpec(state, lambda d, i: (d, geo.cond_row(blk(d, i), tb), 0, 0, 0))],
        out_specs=[perdir, perdir,
                   pl.BlockSpec(state, lambda d, i: (d, jnp.minimum(blk(d, i) * tb // geo.s_ctx, geo.b_ctx), 0, 0, 0))],
        out_shape=[jax.ShapeDtypeStruct((2, geo.n_tok, D_MODEL), F32),
                   jax.ShapeDtypeStruct((2, geo.n_tok, D_MODEL), F32),
                   jax.ShapeDtypeStruct((2, geo.b_ctx + 1, N_PAIRS, PAIR, PAIR), F32)],
        scratch_shapes=[pltpu.VMEM((N_PAIRS, PAIR, PAIR), F32)],
        compiler_params=_params("arbitrary", "arbitrary"),
        name="wkv_chunked",
    )(rkv, rkv, rkv, dec, asig, kk_c, ka_c, rk_c, s0)


def _rwkv_out_kernel(y_ref, bon_ref, gate_ref, lw_ref, lb_ref, w_ref, x_ref, m_ref, o_ref, z_ref):
    _, ones_blk = _pair_ones()
    inv_n = 1.0 / RWKV_HEAD
    for s in range(N_PAIRS):
        cols = slice(s * PAIR, (s + 1) * PAIR)
        y = y_ref[0, :, cols] + y_ref[1, :, cols]
        yc = y - _head_sums(y, ones_blk) * inv_n
        var = _head_sums(yc * yc, ones_blk) * inv_n
        z = yc * lax.rsqrt(var + GN_EPS) * lw_ref[:, cols] + lb_ref[:, cols] + bon_ref[0, :, cols] + bon_ref[1, :, cols]
        z_ref[:, cols] = (z * gate_ref[:, cols]).astype(BF16)
    o_ref[...] = x_ref[...] + m_ref[5:6, :] * _dot(z_ref[...], w_ref[...])


def _rwkv_out(geo, y, bon, gate, lw, lb, w, x, mods):
    tm = TM_OUT
    row = pl.BlockSpec((tm, D_MODEL), lambda i: (i, 0))
    both = pl.BlockSpec((2, tm, D_MODEL), lambda i: (0, i, 0))
    const = pl.BlockSpec((1, D_MODEL), lambda i: (0, 0))
    return pl.pallas_call(
        _rwkv_out_kernel,
        grid=(geo.n_tok // tm,),
        in_specs=[both, both, row, const, const, pl.BlockSpec((D_MODEL, D_MODEL), lambda i: (0, 0)), row,
                  pl.BlockSpec((None, N_MOD, D_MODEL), lambda i: (geo.cond_row(i, tm), 0, 0))],
        out_specs=row,
        out_shape=jax.ShapeDtypeStruct((geo.n_tok, D_MODEL), F32),
        scratch_shapes=[pltpu.VMEM((tm, D_MODEL), BF16)],
        compiler_params=_params("parallel"),
        name="rwkv_out",
    )(y, bon, gate, lw, lb, w, x, mods)


def _state_to_pairs(s):
    b = s.shape[0]
    s5 = s.reshape(b, N_PAIRS, 2, RWKV_HEAD, RWKV_HEAD)
    z = jnp.zeros_like(s5[:, :, 0])
    top = jnp.concatenate([s5[:, :, 0], z], axis=-1)
    bot = jnp.concatenate([z, s5[:, :, 1]], axis=-1)
    return jnp.concatenate([top, bot], axis=-2)


def _state_from_pairs(sp):
    b = sp.shape[0]
    h0 = sp[:, :, :RWKV_HEAD, :RWKV_HEAD]
    h1 = sp[:, :, RWKV_HEAD:, RWKV_HEAD:]
    return jnp.stack([h0, h1], axis=2).reshape(b, RWKV_HEADS, RWKV_HEAD, RWKV_HEAD)


def _rope_tables(geo):
    rows = geo.s_lat // GRID_W
    row = jnp.repeat(jnp.arange(rows, dtype=F32), GRID_W)
    col = jnp.tile(jnp.arange(GRID_W, dtype=F32), rows)
    inv = ROPE_THETA ** (-jnp.arange(ROPE_FREQS, dtype=F32) / ROPE_FREQS)
    ang = jnp.stack([row[:, None] * inv, col[:, None] * inv], axis=1)
    ang = jnp.broadcast_to(ang[:, :, None, :], (geo.s_lat, 2, 2, ROPE_FREQS)).reshape(geo.s_lat, QK_ROPE)
    cos = jnp.concatenate([jnp.ones((geo.n_ctx, QK_ROPE), F32), jnp.tile(jnp.cos(ang), (geo.b_lat, 1))], axis=0)
    sin = jnp.concatenate([jnp.zeros((geo.n_ctx, QK_ROPE), F32), jnp.tile(jnp.sin(ang), (geo.b_lat, 1))], axis=0)
    return cos, sin


def _rot_cols(w):
    w4 = w.reshape(w.shape[:-1] + (2, 2, ROPE_FREQS))
    return jnp.stack([-w4[..., 1, :], w4[..., 0, :]], axis=-2).reshape(w.shape)


def _mla_layer(geo, x, mods, g, j, cos, sin, cache_ckv, cache_krope, p):
    w_dkv = p['mla_w_dkv'][j]
    w_cat = jnp.concatenate([p['mla_w_dq'][j], w_dkv, _rot_cols(w_dkv[:, KV_LORA:])], axis=1).astype(BF16)
    cq, ckv, ckvb, kr = _mla_proj(geo, x, mods, g, w_cat, p['mla_q_norm'][j][None], p['mla_kv_norm'][j][None],
                                  cos, sin)
    w_uq = p['mla_w_uq'][j].reshape(Q_LORA, MLA_HEADS, QK_NOPE + QK_ROPE).transpose(1, 0, 2)
    wq = w_uq.astype(BF16)
    wqr = _rot_cols(w_uq[..., QK_NOPE:]).astype(BF16)
    w_ukv = p['mla_w_ukv'][j].reshape(KV_LORA, MLA_HEADS, QK_NOPE + V_DIM).transpose(1, 0, 2)
    wk = w_ukv[..., :QK_NOPE].astype(BF16)
    wv = w_ukv[..., QK_NOPE:].astype(BF16)
    o_ctx = _attn(geo, cq, ckvb, kr, wq, wqr, wk, wv, latent=False)
    o_lat = _attn(geo, cq, ckvb, kr, wq, wqr, wk, wv, latent=True,
                  past_ckv=cache_ckv[:, j], past_kr=cache_krope[:, j], cos=cos, sin=sin)
    o = jnp.concatenate([o_ctx, o_lat], axis=0)
    x = _out_proj(geo, o, p['mla_w_o'][j].astype(BF16), x, mods)
    new_ckv = ckv[:geo.n_ctx].reshape(geo.b_ctx, geo.s_ctx, KV_LORA)
    new_kr = kr[:geo.n_ctx].reshape(geo.b_ctx, geo.s_ctx, QK_ROPE)
    return x, new_ckv, new_kr


def _rwkv_layer(geo, x, mods, g, j, s0_fwd, s0_bwd, p):
    mu = p['rwkv_mu'][j]
    mu_dir = p['rwkv_mu_dir'][j].reshape(4, 1, D_MODEL)
    w3 = jnp.stack([p['rwkv_w_r'][j], p['rwkv_w_k'][j], p['rwkv_w_v'][j]]).astype(BF16)
    rkv = _rkv(geo, x, mods, g, mu[:3, None, :], w3)
    pad1 = lambda w: jnp.pad(w, ((0, 0), (0, 0), (0, LORA_PAD - w.shape[-1])))
    pad2 = lambda w: jnp.pad(w, ((0, 0), (0, LORA_PAD - w.shape[-2]), (0, 0)))
    wa1 = jnp.concatenate([pad1(p['rwkv_w1'][j]), pad1(p['rwkv_a1'][j])]).astype(BF16)
    wa2 = jnp.concatenate([pad2(p['rwkv_w2'][j]), pad2(p['rwkv_a2'][j])]).astype(BF16)
    gate, dec, asig = _lora(geo, x, mods, g, mu[3:4], mu_dir, p['rwkv_g1'][j].astype(BF16),
                            p['rwkv_g2'][j].astype(BF16), wa1, wa2,
                            p['rwkv_w0'][j][:, None, :], p['rwkv_a0'][j][:, None, :])
    zero = jnp.zeros((2, 1, N_PAIRS, PAIR, PAIR), F32)
    s0 = jnp.concatenate([zero, jnp.stack([_state_to_pairs(s0_fwd), _state_to_pairs(s0_bwd)])], axis=1)
    y, bon, sf = _wkv(geo, rkv, dec, asig, p['rwkv_k_k'][j][None], p['rwkv_k_a'][j][None],
                      p['rwkv_r_k'][j].reshape(1, D_MODEL), s0)
    x = _rwkv_out(geo, y, bon, gate, p['rwkv_ln_w'][j][None], p['rwkv_ln_b'][j][None],
                  p['rwkv_w_o'][j].astype(BF16), x, mods)
    return x, _state_from_pairs(sf[0, :geo.b_ctx]), _state_from_pairs(sf[1, :geo.b_ctx])


def kernel(x_prompt, x_sample, cache_ckv, cache_krope, state_wkv_fwd, state_wkv_bwd, c, c_ctx, w_ada, b_ada, norm_sub, norm_final, w_ffn_in, w_ffn_out, mla_w_dq, mla_q_norm, mla_w_uq, mla_w_dkv, mla_kv_norm, mla_w_ukv, mla_w_o, rwkv_mu, rwkv_mu_dir, rwkv_w_r, rwkv_w_k, rwkv_w_v, rwkv_w0, rwkv_w1, rwkv_w2, rwkv_a0, rwkv_a1, rwkv_a2, rwkv_g1, rwkv_g2, rwkv_k_k, rwkv_k_a, rwkv_r_k, rwkv_ln_w, rwkv_ln_b, rwkv_w_o):
    p = dict(mla_w_dq=mla_w_dq, mla_q_norm=mla_q_norm, mla_w_uq=mla_w_uq, mla_w_dkv=mla_w_dkv,
             mla_kv_norm=mla_kv_norm, mla_w_ukv=mla_w_ukv, mla_w_o=mla_w_o,
             rwkv_mu=rwkv_mu, rwkv_mu_dir=rwkv_mu_dir, rwkv_w_r=rwkv_w_r, rwkv_w_k=rwkv_w_k,
             rwkv_w_v=rwkv_w_v, rwkv_w0=rwkv_w0, rwkv_w1=rwkv_w1, rwkv_w2=rwkv_w2,
             rwkv_a0=rwkv_a0, rwkv_a1=rwkv_a1, rwkv_a2=rwkv_a2, rwkv_g1=rwkv_g1, rwkv_g2=rwkv_g2,
             rwkv_k_k=rwkv_k_k, rwkv_k_a=rwkv_k_a, rwkv_r_k=rwkv_r_k,
             rwkv_ln_w=rwkv_ln_w, rwkv_ln_b=rwkv_ln_b, rwkv_w_o=rwkv_w_o)
    b_ctx, s_ctx, _ = x_prompt.shape
    b_lat, s_lat, _ = x_sample.shape
    geo = _Geom(b_ctx, s_ctx, b_lat, s_lat)
    assert geo.n_ctx % s_lat == 0 and s_ctx % TM_RWKV == 0 and s_lat % TM_FFN == 0
    assert s_ctx == TB_WKV and s_lat % TB_WKV == 0

    x = jnp.concatenate([x_prompt.reshape(geo.n_ctx, D_MODEL), x_sample.reshape(geo.n_lat, D_MODEL)], axis=0)
    cond = jnp.concatenate([c_ctx[None], c, jnp.zeros((COND_ROWS - 1 - b_lat, D_MODEL), F32)], axis=0)
    mods_all = _ada(cond, w_ada, b_ada).reshape(DEPTH, COND_ROWS, N_MOD, D_MODEL)
    cos, sin = _rope_tables(geo)
    nf = norm_final[None]
    w_in = w_ffn_in.astype(BF16)
    w_out = w_ffn_out.astype(BF16)

    ckv_l, kr_l, sf_l, sb_l = [], [], [], []
    for l in range(DEPTH):
        j = l // N_MIXERS
        mods = mods_all[l]
        x = _ffn(geo, x, mods, norm_sub[l, 0][None], w_in, w_out, nf, layer=l, half=0, sub=0, final=False)
        if l % N_MIXERS == 0:
            x, ckv, kr = _mla_layer(geo, x, mods, norm_sub[l, 1][None], j, cos, sin, cache_ckv, cache_krope, p)
            ckv_l.append(ckv)
            kr_l.append(kr)
        else:
            x, sf, sb = _rwkv_layer(geo, x, mods, norm_sub[l, 1][None], j,
                                    state_wkv_fwd[:, j], state_wkv_bwd[:, j], p)
            sf_l.append(sf)
            sb_l.append(sb)
        x = _ffn(geo, x, mods, norm_sub[l, 2][None], w_in, w_out, nf, layer=l, half=1, sub=2,
                 final=(l == DEPTH - 1))

    y_prompt = x[:geo.n_ctx].reshape(b_ctx, s_ctx, D_MODEL)
    y_sample = x[geo.n_ctx:].reshape(b_lat, s_lat, D_MODEL)
    return (y_prompt, y_sample, jnp.stack(ckv_l, axis=1), jnp.stack(kr_l, axis=1),
            jnp.stack(sf_l, axis=1), jnp.stack(sb_l, axis=1))
```

```python
import functools

import jax
import jax.numpy as jnp
from jax import lax
from jax.experimental import pallas as pl
from jax.experimental.pallas import tpu as pltpu

D_MODEL = 2048
DEPTH = 4
N_MIXERS = 2
D_FF = 5632
N_MOD = 9
RMS_EPS = 1e-6
MLA_HEADS = 16
Q_LORA = 512
KV_LORA = 512
QK_NOPE = 128
QK_ROPE = 64
V_DIM = 128
ROPE_FREQS = QK_ROPE // 4
ROPE_THETA = 10000.0
GRID_W = 64
ATTN_SCALE = (QK_NOPE + QK_ROPE) ** -0.5
RWKV_HEAD = 64
RWKV_HEADS = D_MODEL // RWKV_HEAD
GN_EPS = 64e-5
LORA_PAD = 128

COND_ROWS = 8
VMEM_LIMIT = 56 * 1024 * 1024

TM_FFN = 512
TF_FFN = 512
TM_PROJ = 512
TM_RWKV = 256
TN_ADA = 1024
Q_TILE = 256
CHUNK = 64
PAIR = 2 * RWKV_HEAD
N_PAIRS = D_MODEL // PAIR
PAIR_GROUP = 16
TB_WKV = 256
TM_OUT = 256

BF16 = jnp.bfloat16
F32 = jnp.float32


def _params(*sem):
    return pltpu.CompilerParams(dimension_semantics=sem, vmem_limit_bytes=VMEM_LIMIT)


def _sigmoid(x):
    return 1.0 / (1.0 + jnp.exp(-x))


def _modulate(x, g, shift, scale):
    ms = jnp.mean(x * x, axis=-1, keepdims=True)
    return (x * lax.rsqrt(ms + RMS_EPS) * g) * (1.0 + scale) + shift


def _rms(x, w):
    ms = jnp.mean(x * x, axis=-1, keepdims=True)
    return x * lax.rsqrt(ms + RMS_EPS) * w


def _dot(a, b):
    return jnp.dot(a, b, preferred_element_type=F32)


def _dot_nt(a, b):
    return lax.dot_general(a, b, (((1,), (1,)), ((), ())), preferred_element_type=F32)


class _Geom:
    def __init__(self, b_ctx, s_ctx, b_lat, s_lat):
        self.b_ctx, self.s_ctx, self.b_lat, self.s_lat = b_ctx, s_ctx, b_lat, s_lat
        self.n_ctx = b_ctx * s_ctx
        self.n_lat = b_lat * s_lat
        self.n_tok = self.n_ctx + self.n_lat

    def cond_row(self, i, tm):
        start = i * tm
        return jnp.where(start < self.n_ctx, 0, 1 + (start - self.n_ctx) // self.s_lat)


def _ada_kernel(c_ref, w_ref, b_ref, o_ref):
    c = c_ref[...]
    s = (c * _sigmoid(c)).astype(BF16)
    o_ref[...] = _dot(s, w_ref[...].astype(BF16)) + b_ref[...]


def _ada(cond, w_ada, b_ada):
    n = N_MOD * D_MODEL
    return pl.pallas_call(
        _ada_kernel,
        grid=(DEPTH, n // TN_ADA),
        in_specs=[
            pl.BlockSpec((COND_ROWS, D_MODEL), lambda l, j: (0, 0)),
            pl.BlockSpec((None, D_MODEL, TN_ADA), lambda l, j: (l, 0, j)),
            pl.BlockSpec((None, 1, TN_ADA), lambda l, j: (l, 0, j)),
        ],
        out_specs=pl.BlockSpec((None, COND_ROWS, TN_ADA), lambda l, j: (l, 0, j)),
        out_shape=jax.ShapeDtypeStruct((DEPTH, COND_ROWS, n), F32),
        compiler_params=_params("parallel", "parallel"),
        name="ada",
    )(cond, w_ada, b_ada.reshape(DEPTH, 1, n))


def _ffn_kernel(x_ref, m_ref, g_ref, wg_ref, wu_ref, wo_ref, nf_ref, o_ref, h_ref, acc_ref, *, sub, final):
    f = pl.program_id(1)

    @pl.when(f == 0)
    def _():
        h = _modulate(x_ref[...], g_ref[...], m_ref[3 * sub:3 * sub + 1, :], m_ref[3 * sub + 1:3 * sub + 2, :])
        h_ref[...] = h.astype(BF16)
        acc_ref[...] = jnp.zeros_like(acc_ref)

    h = h_ref[...]
    gate = _dot(h, wg_ref[...])
    up = _dot(h, wu_ref[...])
    act = (gate * _sigmoid(gate) * up).astype(BF16)
    acc_ref[...] += _dot(act, wo_ref[...])

    @pl.when(f == pl.num_programs(1) - 1)
    def _():
        y = x_ref[...] + 0.5 * m_ref[3 * sub + 2:3 * sub + 3, :] * acc_ref[...]
        if final:
            y = _rms(y, nf_ref[...])
        o_ref[...] = y


def _ffn(geo, x, mods, g, w_in, w_out, nf, *, layer, half, sub, final):
    tm, tf = TM_FFN, TF_FFN
    nf_blocks = D_FF // tf
    kern = functools.partial(_ffn_kernel, sub=sub, final=final)
    return pl.pallas_call(
        kern,
        grid=(geo.n_tok // tm, nf_blocks),
        in_specs=[
            pl.BlockSpec((tm, D_MODEL), lambda i, f: (i, 0)),
            pl.BlockSpec((None, N_MOD, D_MODEL), lambda i, f: (geo.cond_row(i, tm), 0, 0)),
            pl.BlockSpec((1, D_MODEL), lambda i, f: (0, 0)),
            pl.BlockSpec((None, None, D_MODEL, tf), lambda i, f: (layer, half, 0, f)),
            pl.BlockSpec((None, None, D_MODEL, tf), lambda i, f: (layer, half, 0, f + nf_blocks)),
            pl.BlockSpec((None, None, tf, D_MODEL), lambda i, f: (layer, half, f, 0)),
            pl.BlockSpec((1, D_MODEL), lambda i, f: (0, 0)),
        ],
        out_specs=pl.BlockSpec((tm, D_MODEL), lambda i, f: (i, 0)),
        out_shape=jax.ShapeDtypeStruct((geo.n_tok, D_MODEL), F32),
        scratch_shapes=[pltpu.VMEM((tm, D_MODEL), BF16), pltpu.VMEM((tm, D_MODEL), F32)],
        compiler_params=_params("parallel", "arbitrary"),
        name="ffn",
    )(x, mods, g, w_in, w_in, w_out, nf)


def _mla_proj_kernel(x_ref, m_ref, g_ref, w_ref, qn_ref, kvn_ref, cos_ref, sin_ref,
                     cq_ref, ckv_ref, ckvb_ref, kr_ref):
    h = _modulate(x_ref[...], g_ref[...], m_ref[3:4, :], m_ref[4:5, :]).astype(BF16)
    z = _dot(h, w_ref[...])
    cq_ref[...] = _rms(z[:, :Q_LORA], qn_ref[...]).astype(BF16)
    ckv = _rms(z[:, Q_LORA:Q_LORA + KV_LORA], kvn_ref[...])
    ckv_ref[...] = ckv
    ckvb_ref[...] = ckv.astype(BF16)
    o = Q_LORA + KV_LORA
    kr_ref[...] = z[:, o:o + QK_ROPE] * cos_ref[...] + z[:, o + QK_ROPE:o + 2 * QK_ROPE] * sin_ref[...]


def _mla_proj(geo, x, mods, g, w_cat, qn, kvn, cos, sin):
    tm = TM_PROJ
    n_out = w_cat.shape[1]
    row = lambda i: (i, 0)
    fix = lambda i: (0, 0)
    return pl.pallas_call(
        _mla_proj_kernel,
        grid=(geo.n_tok // tm,),
        in_specs=[
            pl.BlockSpec((tm, D_MODEL), row),
            pl.BlockSpec((None, N_MOD, D_MODEL), lambda i: (geo.cond_row(i, tm), 0, 0)),
            pl.BlockSpec((1, D_MODEL), fix),
            pl.BlockSpec((D_MODEL, n_out), fix),
            pl.BlockSpec((1, Q_LORA), fix),
            pl.BlockSpec((1, KV_LORA), fix),
            pl.BlockSpec((tm, QK_ROPE), row),
            pl.BlockSpec((tm, QK_ROPE), row),
        ],
        out_specs=[
            pl.BlockSpec((tm, Q_LORA), row),
            pl.BlockSpec((tm, KV_LORA), row),
            pl.BlockSpec((tm, KV_LORA), row),
            pl.BlockSpec((tm, QK_ROPE), row),
        ],
        out_shape=[
            jax.ShapeDtypeStruct((geo.n_tok, Q_LORA), BF16),
            jax.ShapeDtypeStruct((geo.n_tok, KV_LORA), F32),
            jax.ShapeDtypeStruct((geo.n_tok, KV_LORA), BF16),
            jax.ShapeDtypeStruct((geo.n_tok, QK_ROPE), F32),
        ],
        compiler_params=_params("parallel"),
        name="mla_proj",
    )(x, mods, g, w_cat, qn, kvn, cos, sin)


def _attn_kernel(*refs, s_len, t_past, rope):
    if rope:
        (cq_ref, ckvb_ref, kr_ref, pckv_ref, pkr_ref, cos_ref, sin_ref,
         wq_ref, wqr_ref, wk_ref, wv_ref, o_ref, kv_all, kr_all) = refs
    else:
        cq_ref, ckvb_ref, kr_ref, wq_ref, wk_ref, wv_ref, o_ref, kv_all, kr_all = refs
    if t_past:
        kv_all[0:t_past, :] = pckv_ref[...].astype(BF16)
        kr_all[0:t_past, :] = pkr_ref[...].astype(BF16)
    kv_all[t_past:, :] = ckvb_ref[...]
    kr_all[t_past:, :] = kr_ref[...].astype(BF16)

    def head(hd, carry):
        kv = kv_all[...]
        kn = _dot(kv, wk_ref[hd]).astype(BF16)
        v = _dot(kv, wv_ref[hd]).astype(BF16)
        krb = kr_all[...]
        col = pl.multiple_of(hd * V_DIM, V_DIM)
        for qb in range(s_len // Q_TILE):
            rows = slice(qb * Q_TILE, (qb + 1) * Q_TILE)
            cq = cq_ref[rows, :]
            q = _dot(cq, wq_ref[hd])
            qn = q[:, :QK_NOPE]
            qr = q[:, QK_NOPE:]
            if rope:
                qr = qr * cos_ref[rows, :] + _dot(cq, wqr_ref[hd]) * sin_ref[rows, :]
            s = (_dot_nt(qn.astype(BF16), kn) + _dot_nt(qr.astype(BF16), krb)) * ATTN_SCALE
            p = jnp.exp(s - jnp.max(s, axis=-1, keepdims=True))
            pr = (p / jnp.sum(p, axis=-1, keepdims=True)).astype(BF16)
            o_ref[rows, pl.ds(col, V_DIM)] = _dot(pr, v).astype(BF16)
        return carry

    lax.fori_loop(0, MLA_HEADS, head, 0)


def _attn(geo, cq, ckvb, kr, wq, wqr, wk, wv, *, latent, past_ckv=None, past_kr=None, cos=None, sin=None):
    if latent:
        nb, s_len, off = geo.b_lat, geo.s_lat, geo.n_ctx // geo.s_lat
        t_past = past_ckv.shape[1]
    else:
        nb, s_len, off, t_past = geo.b_ctx, geo.s_ctx, 0, 0
    row = lambda b: (b + off, 0)
    fix3 = lambda b: (0, 0, 0)
    tok = lambda width: pl.BlockSpec((s_len, width), row)
    wspec = lambda w: pl.BlockSpec(w.shape, fix3)
    in_specs = [tok(Q_LORA), tok(KV_LORA), tok(QK_ROPE)]
    args = [cq, ckvb, kr]
    if latent:
        in_specs += [pl.BlockSpec((None, t_past, KV_LORA), lambda b: (b, 0, 0)),
                     pl.BlockSpec((None, t_past, QK_ROPE), lambda b: (b, 0, 0)),
                     tok(QK_ROPE), tok(QK_ROPE), wspec(wq), wspec(wqr)]
        args += [past_ckv, past_kr, cos, sin, wq, wqr]
    else:
        in_specs += [wspec(wq)]
        args += [wq]
    in_specs += [wspec(wk), wspec(wv)]
    args += [wk, wv]
    kern = functools.partial(_attn_kernel, s_len=s_len, t_past=t_past, rope=latent)
    return pl.pallas_call(
        kern,
        grid=(nb,),
        in_specs=in_specs,
        out_specs=pl.BlockSpec((s_len, D_MODEL), lambda b: (b, 0)),
        out_shape=jax.ShapeDtypeStruct((nb * s_len, D_MODEL), BF16),
        scratch_shapes=[pltpu.VMEM((t_past + s_len, KV_LORA), BF16),
                        pltpu.VMEM((t_past + s_len, QK_ROPE), BF16)],
        compiler_params=_params("parallel"),
        name="attn_lat" if latent else "attn_ctx",
    )(*args)


def _out_proj_kernel(a_ref, w_ref, x_ref, m_ref, o_ref):
    o_ref[...] = x_ref[...] + m_ref[5:6, :] * _dot(a_ref[...], w_ref[...])


def _out_proj(geo, a, w, x, mods):
    tm = TM_PROJ
    row = pl.BlockSpec((tm, D_MODEL), lambda i: (i, 0))
    return pl.pallas_call(
        _out_proj_kernel,
        grid=(geo.n_tok // tm,),
        in_specs=[row, pl.BlockSpec((D_MODEL, D_MODEL), lambda i: (0, 0)), row,
                  pl.BlockSpec((None, N_MOD, D_MODEL), lambda i: (geo.cond_row(i, tm), 0, 0))],
        out_specs=row,
        out_shape=jax.ShapeDtypeStruct((geo.n_tok, D_MODEL), F32),
        compiler_params=_params("parallel"),
        name="out_proj",
    )(a, w, x, mods)


def _shifted(geo, x_ref, xp_ref, xn_ref, m_ref, g_ref, tm):
    i = pl.program_id(0)
    g, shift, scale = g_ref[...], m_ref[3:4, :], m_ref[4:5, :]
    h = _modulate(x_ref[...], g, shift, scale)
    start = i * tm
    seq = jnp.where(start < geo.n_ctx, geo.s_ctx, geo.s_lat)
    rel = jnp.where(start < geo.n_ctx, start, start - geo.n_ctx)
    has_prev = (rel % seq) != 0
    has_next = ((rel + tm) % seq) != 0
    hp = _modulate(xp_ref[...], g, shift, scale)[7:8, :]
    hn = _modulate(xn_ref[...], g, shift, scale)[0:1, :]
    hp = jnp.where(has_prev, hp, 0.0)
    hn = jnp.where(has_next, hn, 0.0)
    r = lax.broadcasted_iota(jnp.int32, h.shape, 0)
    down = jnp.where(r == 0, hp, pltpu.roll(h, 1, 0))
    up = jnp.where(r == tm - 1, hn, pltpu.roll(h, tm - 1, 0))
    return h, 0.5 * (down + up) - h


def _halo_specs(geo, tm):
    nb8 = geo.n_tok // 8
    return [
        pl.BlockSpec((tm, D_MODEL), lambda i, *_: (i, 0)),
        pl.BlockSpec((8, D_MODEL), lambda i, *_: (jnp.maximum(i * (tm // 8) - 1, 0), 0)),
        pl.BlockSpec((8, D_MODEL), lambda i, *_: (jnp.minimum((i + 1) * (tm // 8), nb8 - 1), 0)),
        pl.BlockSpec((None, N_MOD, D_MODEL), lambda i, *_: (geo.cond_row(i, tm), 0, 0)),
        pl.BlockSpec((1, D_MODEL), lambda i, *_: (0, 0)),
    ]


def _rkv_kernel(x_ref, xp_ref, xn_ref, m_ref, g_ref, mu_ref, w_ref, o_ref, h_ref, xx_ref, *, geo, tm):
    @pl.when(pl.program_id(1) == 0)
    def _():
        h, xx = _shifted(geo, x_ref, xp_ref, xn_ref, m_ref, g_ref, tm)
        h_ref[...] = h
        xx_ref[...] = xx

    xm = (h_ref[...] + xx_ref[...] * mu_ref[...]).astype(BF16)
    o_ref[...] = _dot(xm, w_ref[pl.program_id(1)])


def _rkv(geo, x, mods, g, mu3, w3):
    tm = TM_RWKV
    return pl.pallas_call(
        functools.partial(_rkv_kernel, geo=geo, tm=tm),
        grid=(geo.n_tok // tm, 3),
        in_specs=_halo_specs(geo, tm) + [
            pl.BlockSpec((None, 1, D_MODEL), lambda i, p: (p, 0, 0)),
            pl.BlockSpec((3, D_MODEL, D_MODEL), lambda i, p: (0, 0, 0), pipeline_mode=pl.Buffered(1)),
        ],
        out_specs=pl.BlockSpec((None, tm, D_MODEL), lambda i, p: (p, i, 0)),
        out_shape=jax.ShapeDtypeStruct((3, geo.n_tok, D_MODEL), F32),
        scratch_shapes=[pltpu.VMEM((tm, D_MODEL), F32), pltpu.VMEM((tm, D_MODEL), F32)],
        compiler_params=_params("parallel", "arbitrary"),
        name="rwkv_rkv",
    )(x, x, x, mods, g, mu3, w3)


def _lora_kernel(x_ref, xp_ref, xn_ref, m_ref, g_ref, mug_ref, mud_ref, g1_ref, g2_ref,
                 wa1_ref, wa2_ref, w0_ref, a0_ref, gate_ref, dec_ref, asig_ref, *, geo, tm):
    h, xx = _shifted(geo, x_ref, xp_ref, xn_ref, m_ref, g_ref, tm)
    mix = lambda mu: (h + xx * mu).astype(BF16)
    zg = _dot(mix(mug_ref[...]), g1_ref[...])
    gate_ref[...] = _dot(_sigmoid(zg).astype(BF16), g2_ref[...])
    for d in range(2):
        zw = _dot(mix(mud_ref[2 * d]), wa1_ref[d])
        wl = w0_ref[d] + _dot(jnp.tanh(zw).astype(BF16), wa2_ref[d])
        u = -wl
        softplus = jnp.maximum(u, 0.0) + jnp.log1p(jnp.exp(-jnp.abs(u)))
        dec_ref[d] = jnp.exp(-jnp.exp(-softplus - 0.5))
        za = _dot(mix(mud_ref[2 * d + 1]), wa1_ref[2 + d])
        asig_ref[d] = _sigmoid(a0_ref[d] + _dot(za.astype(BF16), wa2_ref[2 + d]))


def _lora(geo, x, mods, g, mu_g, mu_dir, g1, g2, wa1, wa2, w0, a0):
    tm = TM_RWKV
    full = lambda a: pl.BlockSpec(a.shape, lambda i: (0,) * a.ndim)
    tok2 = pl.BlockSpec((2, tm, D_MODEL), lambda i: (0, i, 0))
    return pl.pallas_call(
        functools.partial(_lora_kernel, geo=geo, tm=tm),
        grid=(geo.n_tok // tm,),
        in_specs=_halo_specs(geo, tm) + [full(a) for a in (mu_g, mu_dir, g1, g2, wa1, wa2, w0, a0)],
        out_specs=[pl.BlockSpec((tm, D_MODEL), lambda i: (i, 0)), tok2, tok2],
        out_shape=[jax.ShapeDtypeStruct((geo.n_tok, D_MODEL), F32),
                   jax.ShapeDtypeStruct((2, geo.n_tok, D_MODEL), F32),
                   jax.ShapeDtypeStruct((2, geo.n_tok, D_MODEL), F32)],
        compiler_params=_params("parallel"),
        name="rwkv_lora",
    )(x, x, x, mods, g, mu_g, mu_dir, g1, g2, wa1, wa2, w0, a0)


def _split2(x):
    hi = x.astype(BF16)
    return hi, (x - hi.astype(F32)).astype(BF16)


def _head_sums(x, ones_blk):
    hi, lo = _split2(x)
    return _dot(hi, ones_blk) + _dot(lo, ones_blk)


def _pair_ones():
    r = lax.broadcasted_iota(jnp.int32, (PAIR, PAIR), 0)
    c = lax.broadcasted_iota(jnp.int32, (PAIR, PAIR), 1)
    same = (r < RWKV_HEAD) == (c < RWKV_HEAD)
    return same, jnp.where(same, 1.0, 0.0).astype(BF16)


def _wkv_kernel(r_ref, k_ref, v_ref, dec_ref, asig_ref, kk_ref, ka_ref, rk_ref, s0_ref,
                y_ref, bon_ref, sf_ref, s_ref, *, geo, tb):
    d = pl.program_id(0)
    i = pl.program_id(1)
    nblk = pl.num_programs(1)
    blk = i + d * (nblk - 1 - 2 * i)
    start = blk * tb
    seq = jnp.where(start < geo.n_ctx, geo.s_ctx, geo.s_lat)
    rel = jnp.where(start < geo.n_ctx, start, start - geo.n_ctx)
    at_lo = (rel % seq) == 0
    at_hi = ((rel + tb) % seq) == 0
    first = jnp.where(d == 0, at_lo, at_hi)
    last = jnp.where(d == 0, at_hi, at_lo)

    @pl.when(first)
    def _():
        s_ref[...] = s0_ref[...]

    nch = tb // CHUNK
    sgn = 1 - 2 * d
    fwd = (d == 0).astype(F32)
    row = lax.broadcasted_iota(jnp.int32, (CHUNK, PAIR), 0)
    lane = lax.broadcasted_iota(jnp.int32, (CHUNK, PAIR), 1)
    sidx = lane & (RWKV_HEAD - 1)
    delta = (row - sidx) * sgn
    strict = delta > 0
    incl = delta >= 0
    eye = jnp.where(row == sidx, 1.0, 0.0)
    off_masks = []
    m = 1
    while m < CHUNK:
        off_masks.append(strict & ((row // (2 * m)) == (sidx // (2 * m))) & ((row // m) != (sidx // m)))
        m *= 2
    head0 = lane < RWKV_HEAD
    tr = lax.broadcasted_iota(jnp.int32, (CHUNK, 3 * CHUNK), 0)
    ts = lax.broadcasted_iota(jnp.int32, (CHUNK, 3 * CHUNK), 1) & (CHUNK - 1)
    tri3 = jnp.where((tr - ts) * sgn >= 0, 1.0, 0.0).astype(BF16)
    same_head, ones_blk = _pair_ones()

    def expand(x):
        return jnp.concatenate([jnp.where(head0, x, 0.0), jnp.where(head0, 0.0, x)], axis=0)

    def chunk_body(ci, carry):
        cc = ci * sgn + d * (nch - 1)
        rows = pl.ds(pl.multiple_of(cc * CHUNK, CHUNK), CHUNK)

        def group_body(pg, carry2):
            ps = [pg * PAIR_GROUP + q for q in range(PAIR_GROUP)]
            cols = [pl.ds(pl.multiple_of(p * PAIR, PAIR), PAIR) for p in ps]
            each = lambda f, *ls: [f(*xs) for xs in zip(*ls)]
            r = [r_ref[rows, c] for c in cols]
            k = [k_ref[rows, c] for c in cols]
            v = [v_ref[rows, c] for c in cols]
            w = [dec_ref[rows, c] for c in cols]
            a = [asig_ref[rows, c] for c in cols]
            kk = [x * kk_ref[:, c] for x, c in zip(k, cols)]
            kd = [x * (1.0 + (y - 1.0) * ka_ref[:, c]) for x, y, c in zip(k, a, cols)]
            sums = [_head_sums(jnp.concatenate([x * x, y * z * rk_ref[:, c]], axis=0), ones_blk)
                    for x, y, z, c in zip(kk, r, kd, cols)]
            kk = each(lambda x, s: x / jnp.maximum(jnp.sqrt(s[:CHUNK]), 1e-12), kk, sums)
            b_in = each(lambda x, y: x * y, kk, a)
            for c, s, y in zip(cols, sums, v):
                bon_ref[rows, c] = s[CHUNK:] * y

            logw = [jnp.log(x) for x in w]
            l1 = [x.astype(BF16) for x in logw]
            e1 = each(lambda x, y: x - y.astype(F32), logw, l1)
            l2 = [x.astype(BF16) for x in e1]
            l3 = each(lambda x, y: (x - y.astype(F32)).astype(BF16), e1, l2)
            cum = each(lambda x, y, z: _dot(tri3, jnp.concatenate([x, y, z], axis=0)), l1, l2, l3)
            tot = [fwd * x[CHUNK - 1:CHUNK, :] + (1.0 - fwd) * x[0:1, :] for x in cum]
            c_inv = [jnp.exp(-x) for x in cum]
            at = each(lambda x, y, z: -x * jnp.exp(y - z), kk, cum, logw)
            rt = each(lambda x, y: x * jnp.exp(y), r, cum)
            c_end = each(lambda x, y: jnp.exp(x - y), tot, cum)

            lhs = each(lambda x, y: jnp.concatenate([x, y], axis=0).astype(BF16), at, rt)
            rhs = each(lambda x, y, z: jnp.concatenate([expand(x * z), expand(y * z)], axis=0).astype(BF16),
                       b_in, kd, c_inv)
            g = each(_dot_nt, lhs, rhs)
            ab = [jnp.where(strict, x[:CHUNK, :PAIR], 0.0) for x in g]
            ak = [jnp.where(strict, x[:CHUNK, PAIR:], 0.0) for x in g]
            rbk = [jnp.concatenate([jnp.where(incl, x[CHUNK:, :PAIR], 0.0),
                                    jnp.where(incl, x[CHUNK:, PAIR:], 0.0)], axis=1).astype(BF16) for x in g]

            t_inv = [eye + jnp.where(off_masks[0], x, 0.0) for x in ab]
            for off in off_masks[1:]:
                lx = each(lambda x, t: _dot(jnp.where(off, x, 0.0).astype(BF16), expand(t).astype(BF16)), ab, t_inv)
                t_inv = each(lambda t, x: t + _dot(t.astype(BF16), expand(x).astype(BF16)), t_inv, lx)

            s2 = [s_ref[p] for p in ps]
            ars = each(lambda x, s: _dot_nt(x, s.astype(BF16)), lhs, s2)
            vexp = [expand(x).astype(BF16) for x in v]
            rhs_u = each(lambda x, y, z: x[:CHUNK] + _dot(y.astype(BF16), z), ars, ak, vexp)
            u = each(lambda t, x: _dot(t.astype(BF16), expand(x).astype(BF16)), t_inv, rhs_u)
            uv = each(lambda x, y: jnp.concatenate([expand(x).astype(BF16), y], axis=0), u, vexp)
            y_out = each(lambda x, y, z: x[CHUNK:] + _dot(y, z), ars, rbk, uv)
            for c, x in zip(cols, y_out):
                y_ref[rows, c] = x
            uv_t = each(lambda x, y: jnp.concatenate([x, y], axis=0).T.astype(BF16), u, v)
            bk = each(lambda x, y, z: jnp.concatenate([x * z, y * z], axis=0).astype(BF16), b_in, kd, c_end)
            upd = each(_dot, uv_t, bk)
            for p, s, x, t in zip(ps, s2, upd, tot):
                s_ref[p] = s * jnp.exp(t) + jnp.where(same_head, x, 0.0)
            return carry2

        lax.fori_loop(0, N_PAIRS // PAIR_GROUP, group_body, 0)
        return carry

    lax.fori_loop(0, nch, chunk_body, 0)

    @pl.when(last)
    def _():
        sf_ref[...] = s_ref[...]


def _wkv(geo, rkv, dec, asig, kk_c, ka_c, rk_c, s0):
    tb = TB_WKV
    nblk = geo.n_tok // tb
    blk = lambda d, i: i + d * (nblk - 1 - 2 * i)
    tok = lambda which: pl.BlockSpec((None, tb, D_MODEL), lambda d, i: (which, blk(d, i), 0))
    perdir = pl.BlockSpec((None, tb, D_MODEL), lambda d, i: (d, blk(d, i), 0))
    const = pl.BlockSpec((1, D_MODEL), lambda d, i: (0, 0))
    state = (None, None, N_PAIRS, PAIR, PAIR)
    return pl.pallas_call(
        functools.partial(_wkv_kernel, geo=geo, tb=tb),
        grid=(2, nblk),
        in_specs=[tok(0), tok(1), tok(2), perdir, perdir, const, const, const,
                  pl.BlockSpec(state, lambda d, i: (d, geo.cond_row(blk(d, i), tb), 0, 0, 0))],
        out_specs=[perdir, perdir,
                   pl.BlockSpec(state, lambda d, i: (d, jnp.minimum(blk(d, i) * tb // geo.s_ctx, geo.b_ctx), 0, 0, 0))],
        out_shape=[jax.ShapeDtypeStruct((2, geo.n_tok, D_MODEL), F32),
                   jax.ShapeDtypeStruct((2, geo.n_tok, D_MODEL), F32),
                   jax.ShapeDtypeStruct((2, geo.b_ctx + 1, N_PAIRS, PAIR, PAIR), F32)],
        scratch_shapes=[pltpu.VMEM((N_PAIRS, PAIR, PAIR), F32)],
        compiler_params=_params("arbitrary", "arbitrary"),
        name="wkv_chunked",
    )(rkv, rkv, rkv, dec, asig, kk_c, ka_c, rk_c, s0)


def _rwkv_out_kernel(y_ref, bon_ref, gate_ref, lw_ref, lb_ref, w_ref, x_ref, m_ref, o_ref, z_ref):
    _, ones_blk = _pair_ones()
    inv_n = 1.0 / RWKV_HEAD
    for s in range(N_PAIRS):
        cols = slice(s * PAIR, (s + 1) * PAIR)
        y = y_ref[0, :, cols] + y_ref[1, :, cols]
        yc = y - _head_sums(y, ones_blk) * inv_n
        var = _head_sums(yc * yc, ones_blk) * inv_n
        z = yc * lax.rsqrt(var + GN_EPS) * lw_ref[:, cols] + lb_ref[:, cols] + bon_ref[0, :, cols] + bon_ref[1, :, cols]
        z_ref[:, cols] = (z * gate_ref[:, cols]).astype(BF16)
    o_ref[...] = x_ref[...] + m_ref[5:6, :] * _dot(z_ref[...], w_ref[...])


def _rwkv_out(geo, y, bon, gate, lw, lb, w, x, mods):
    tm = TM_OUT
    row = pl.BlockSpec((tm, D_MODEL), lambda i: (i, 0))
    both = pl.BlockSpec((2, tm, D_MODEL), lambda i: (0, i, 0))
    const = pl.BlockSpec((1, D_MODEL), lambda i: (0, 0))
    return pl.pallas_call(
        _rwkv_out_kernel,
        grid=(geo.n_tok // tm,),
        in_specs=[both, both, row, const, const, pl.BlockSpec((D_MODEL, D_MODEL), lambda i: (0, 0)), row,
                  pl.BlockSpec((None, N_MOD, D_MODEL), lambda i: (geo.cond_row(i, tm), 0, 0))],
        out_specs=row,
        out_shape=jax.ShapeDtypeStruct((geo.n_tok, D_MODEL), F32),
        scratch_shapes=[pltpu.VMEM((tm, D_MODEL), BF16)],
        compiler_params=_params("parallel"),
        name="rwkv_out",
    )(y, bon, gate, lw, lb, w, x, mods)


def _state_to_pairs(s):
    b = s.shape[0]
    s5 = s.reshape(b, N_PAIRS, 2, RWKV_HEAD, RWKV_HEAD)
    z = jnp.zeros_like(s5[:, :, 0])
    top = jnp.concatenate([s5[:, :, 0], z], axis=-1)
    bot = jnp.concatenate([z, s5[:, :, 1]], axis=-1)
    return jnp.concatenate([top, bot], axis=-2)


def _state_from_pairs(sp):
    b = sp.shape[0]
    h0 = sp[:, :, :RWKV_HEAD, :RWKV_HEAD]
    h1 = sp[:, :, RWKV_HEAD:, RWKV_HEAD:]
    return jnp.stack([h0, h1], axis=2).reshape(b, RWKV_HEADS, RWKV_HEAD, RWKV_HEAD)


def _rope_tables(geo):
    rows = geo.s_lat // GRID_W
    row = jnp.repeat(jnp.arange(rows, dtype=F32), GRID_W)
    col = jnp.tile(jnp.arange(GRID_W, dtype=F32), rows)
    inv = ROPE_THETA ** (-jnp.arange(ROPE_FREQS, dtype=F32) / ROPE_FREQS)
    ang = jnp.stack([row[:, None] * inv, col[:, None] * inv], axis=1)
    ang = jnp.broadcast_to(ang[:, :, None, :], (geo.s_lat, 2, 2, ROPE_FREQS)).reshape(geo.s_lat, QK_ROPE)
    cos = jnp.concatenate([jnp.ones((geo.n_ctx, QK_ROPE), F32), jnp.tile(jnp.cos(ang), (geo.b_lat, 1))], axis=0)
    sin = jnp.concatenate([jnp.zeros((geo.n_ctx, QK_ROPE), F32), jnp.tile(jnp.sin(ang), (geo.b_lat, 1))], axis=0)
    return cos, sin


def _rot_cols(w):
    w4 = w.reshape(w.shape[:-1] + (2, 2, ROPE_FREQS))
    return jnp.stack([-w4[..., 1, :], w4[..., 0, :]], axis=-2).reshape(w.shape)


def _mla_layer(geo, x, mods, g, j, cos, sin, cache_ckv, cache_krope, p):
    w_dkv = p['mla_w_dkv'][j]
    w_cat = jnp.concatenate([p['mla_w_dq'][j], w_dkv, _rot_cols(w_dkv[:, KV_LORA:])], axis=1).astype(BF16)
    cq, ckv, ckvb, kr = _mla_proj(geo, x, mods, g, w_cat, p['mla_q_norm'][j][None], p['mla_kv_norm'][j][None],
                                  cos, sin)
    w_uq = p['mla_w_uq'][j].reshape(Q_LORA, MLA_HEADS, QK_NOPE + QK_ROPE).transpose(1, 0, 2)
    wq = w_uq.astype(BF16)
    wqr = _rot_cols(w_uq[..., QK_NOPE:]).astype(BF16)
    w_ukv = p['mla_w_ukv'][j].reshape(KV_LORA, MLA_HEADS, QK_NOPE + V_DIM).transpose(1, 0, 2)
    wk = w_ukv[..., :QK_NOPE].astype(BF16)
    wv = w_ukv[..., QK_NOPE:].astype(BF16)
    o_ctx = _attn(geo, cq, ckvb, kr, wq, wqr, wk, wv, latent=False)
    o_lat = _attn(geo, cq, ckvb, kr, wq, wqr, wk, wv, latent=True,
                  past_ckv=cache_ckv[:, j], past_kr=cache_krope[:, j], cos=cos, sin=sin)
    o = jnp.concatenate([o_ctx, o_lat], axis=0)
    x = _out_proj(geo, o, p['mla_w_o'][j].astype(BF16), x, mods)
    new_ckv = ckv[:geo.n_ctx].reshape(geo.b_ctx, geo.s_ctx, KV_LORA)
    new_kr = kr[:geo.n_ctx].reshape(geo.b_ctx, geo.s_ctx, QK_ROPE)
    return x, new_ckv, new_kr


def _rwkv_layer(geo, x, mods, g, j, s0_fwd, s0_bwd, p):
    mu = p['rwkv_mu'][j]
    mu_dir = p['rwkv_mu_dir'][j].reshape(4, 1, D_MODEL)
    w3 = jnp.stack([p['rwkv_w_r'][j], p['rwkv_w_k'][j], p['rwkv_w_v'][j]]).astype(BF16)
    rkv = _rkv(geo, x, mods, g, mu[:3, None, :], w3)
    pad1 = lambda w: jnp.pad(w, ((0, 0), (0, 0), (0, LORA_PAD - w.shape[-1])))
    pad2 = lambda w: jnp.pad(w, ((0, 0), (0, LORA_PAD - w.shape[-2]), (0, 0)))
    wa1 = jnp.concatenate([pad1(p['rwkv_w1'][j]), pad1(p['rwkv_a1'][j])]).astype(BF16)
    wa2 = jnp.concatenate([pad2(p['rwkv_w2'][j]), pad2(p['rwkv_a2'][j])]).astype(BF16)
    gate, dec, asig = _lora(geo, x, mods, g, mu[3:4], mu_dir, p['rwkv_g1'][j].astype(BF16),
                            p['rwkv_g2'][j].astype(BF16), wa1, wa2,
                            p['rwkv_w0'][j][:, None, :], p['rwkv_a0'][j][:, None, :])
    zero = jnp.zeros((2, 1, N_PAIRS, PAIR, PAIR), F32)
    s0 = jnp.concatenate([zero, jnp.stack([_state_to_pairs(s0_fwd), _state_to_pairs(s0_bwd)])], axis=1)
    y, bon, sf = _wkv(geo, rkv, dec, asig, p['rwkv_k_k'][j][None], p['rwkv_k_a'][j][None],
                      p['rwkv_r_k'][j].reshape(1, D_MODEL), s0)
    x = _rwkv_out(geo, y, bon, gate, p['rwkv_ln_w'][j][None], p['rwkv_ln_b'][j][None],
                  p['rwkv_w_o'][j].astype(BF16), x, mods)
    return x, _state_from_pairs(sf[0, :geo.b_ctx]), _state_from_pairs(sf[1, :geo.b_ctx])


def kernel(x_prompt, x_sample, cache_ckv, cache_krope, state_wkv_fwd, state_wkv_bwd, c, c_ctx, w_ada, b_ada, norm_sub, norm_final, w_ffn_in, w_ffn_out, mla_w_dq, mla_q_norm, mla_w_uq, mla_w_dkv, mla_kv_norm, mla_w_ukv, mla_w_o, rwkv_mu, rwkv_mu_dir, rwkv_w_r, rwkv_w_k, rwkv_w_v, rwkv_w0, rwkv_w1, rwkv_w2, rwkv_a0, rwkv_a1, rwkv_a2, rwkv_g1, rwkv_g2, rwkv_k_k, rwkv_k_a, rwkv_r_k, rwkv_ln_w, rwkv_ln_b, rwkv_w_o):
    p = dict(mla_w_dq=mla_w_dq, mla_q_norm=mla_q_norm, mla_w_uq=mla_w_uq, mla_w_dkv=mla_w_dkv,
             mla_kv_norm=mla_kv_norm, mla_w_ukv=mla_w_ukv, mla_w_o=mla_w_o,
             rwkv_mu=rwkv_mu, rwkv_mu_dir=rwkv_mu_dir, rwkv_w_r=rwkv_w_r, rwkv_w_k=rwkv_w_k,
             rwkv_w_v=rwkv_w_v, rwkv_w0=rwkv_w0, rwkv_w1=rwkv_w1, rwkv_w2=rwkv_w2,
             rwkv_a0=rwkv_a0, rwkv_a1=rwkv_a1, rwkv_a2=rwkv_a2, rwkv_g1=rwkv_g1, rwkv_g2=rwkv_g2,
             rwkv_k_k=rwkv_k_k, rwkv_k_a=rwkv_k_a, rwkv_r_k=rwkv_r_k,
             rwkv_ln_w=rwkv_ln_w, rwkv_ln_b=rwkv_ln_b, rwkv_w_o=rwkv_w_o)
    b_ctx, s_ctx, _ = x_prompt.shape
    b_lat, s_lat, _ = x_sample.shape
    geo = _Geom(b_ctx, s_ctx, b_lat, s_lat)
    assert geo.n_ctx % s_lat == 0 and s_ctx % TM_RWKV == 0 and s_lat % TM_FFN == 0
    assert s_ctx == TB_WKV and s_lat % TB_WKV == 0

    x = jnp.concatenate([x_prompt.reshape(geo.n_ctx, D_MODEL), x_sample.reshape(geo.n_lat, D_MODEL)], axis=0)
    cond = jnp.concatenate([c_ctx[None], c, jnp.zeros((COND_ROWS - 1 - b_lat, D_MODEL), F32)], axis=0)
    mods_all = _ada(cond, w_ada, b_ada).reshape(DEPTH, COND_ROWS, N_MOD, D_MODEL)
    cos, sin = _rope_tables(geo)
    nf = norm_final[None]
    w_in = w_ffn_in.astype(BF16)
    w_out = w_ffn_out.astype(BF16)

    ckv_l, kr_l, sf_l, sb_l = [], [], [], []
    for l in range(DEPTH):
        j = l // N_MIXERS
        mods = mods_all[l]
        x = _ffn(geo, x, mods, norm_sub[l, 0][None], w_in, w_out, nf, layer=l, half=0, sub=0, final=False)
        if l % N_MIXERS == 0:
            x, ckv, kr = _mla_layer(geo, x, mods, norm_sub[l, 1][None], j, cos, sin, cache_ckv, cache_krope, p)
            ckv_l.append(ckv)
            kr_l.append(kr)
        else:
            x, sf, sb = _rwkv_layer(geo, x, mods, norm_sub[l, 1][None], j,
                                    state_wkv_fwd[:, j], state_wkv_bwd[:, j], p)
            sf_l.append(sf)
            sb_l.append(sb)
        x = _ffn(geo, x, mods, norm_sub[l, 2][None], w_in, w_out, nf, layer=l, half=1, sub=2,
                 final=(l == DEPTH - 1))

    y_prompt = x[:geo.n_ctx].reshape(b_ctx, s_ctx, D_MODEL)
    y_sample = x[geo.n_ctx:].reshape(b_lat, s_lat, D_MODEL)
    return (y_prompt, y_sample, jnp.stack(ckv_l, axis=1), jnp.stack(kr_l, axis=1),
            jnp.stack(sf_l, axis=1), jnp.stack(sb_l, axis=1))
```

```python
import functools

import jax
import jax.numpy as jnp
from jax import lax
from jax.experimental import pallas as pl
from jax.experimental.pallas import tpu as pltpu

D_MODEL = 2048
DEPTH = 4
N_MIXERS = 2
D_FF = 5632
N_MOD = 9
RMS_EPS = 1e-6
MLA_HEADS = 16
Q_LORA = 512
KV_LORA = 512
QK_NOPE = 128
QK_ROPE = 64
V_DIM = 128
ROPE_FREQS = QK_ROPE // 4
ROPE_THETA = 10000.0
GRID_W = 64
ATTN_SCALE = (QK_NOPE + QK_ROPE) ** -0.5
RWKV_HEAD = 64
RWKV_HEADS = D_MODEL // RWKV_HEAD
GN_EPS = 64e-5
LOG_DECAY_SCALE = 0.6065306597126334
LORA_PAD = 128

COND_ROWS = 8
VMEM_LIMIT = 56 * 1024 * 1024

TM_FFN = 512
TF_FFN = 512
TM_PROJ = 512
TM_RWKV = 256
TN_ADA = 1024
Q_TILE = 256
ATTN_LOCKSTEP = 4
CHUNK = 64
PAIR = 2 * RWKV_HEAD
N_PAIRS = D_MODEL // PAIR
PAIR_GROUP = 16
TB_WKV = 256
TM_OUT = 256

BF16 = jnp.bfloat16
F32 = jnp.float32


def _params(*sem):
    return pltpu.CompilerParams(dimension_semantics=sem, vmem_limit_bytes=VMEM_LIMIT)


def _sigmoid(x):
    return 1.0 / (1.0 + jnp.exp(-x))


def _modulate(x, g, shift, scale):
    ms = jnp.mean(x * x, axis=-1, keepdims=True)
    return (x * lax.rsqrt(ms + RMS_EPS) * g) * (1.0 + scale) + shift


def _rms(x, w):
    ms = jnp.mean(x * x, axis=-1, keepdims=True)
    return x * lax.rsqrt(ms + RMS_EPS) * w


def _dot(a, b):
    return jnp.dot(a, b, preferred_element_type=F32)


def _dot_nt(a, b):
    return lax.dot_general(a, b, (((1,), (1,)), ((), ())), preferred_element_type=F32)


class _Geom:
    def __init__(self, b_ctx, s_ctx, b_lat, s_lat):
        self.b_ctx, self.s_ctx, self.b_lat, self.s_lat = b_ctx, s_ctx, b_lat, s_lat
        self.n_ctx = b_ctx * s_ctx
        self.n_lat = b_lat * s_lat
        self.n_tok = self.n_ctx + self.n_lat

    def cond_row(self, i, tm):
        start = i * tm
        return jnp.where(start < self.n_ctx, 0, 1 + (start - self.n_ctx) // self.s_lat)


def _ada_kernel(c_ref, w_ref, b_ref, o_ref):
    c = c_ref[...]
    s = (c * _sigmoid(c)).astype(BF16)
    o_ref[...] = _dot(s, w_ref[...].astype(BF16)) + b_ref[...]


def _ada(cond, w_ada, b_ada):
    n = N_MOD * D_MODEL
    return pl.pallas_call(
        _ada_kernel,
        grid=(DEPTH, n // TN_ADA),
        in_specs=[
            pl.BlockSpec((COND_ROWS, D_MODEL), lambda l, j: (0, 0)),
            pl.BlockSpec((None, D_MODEL, TN_ADA), lambda l, j: (l, 0, j)),
            pl.BlockSpec((None, 1, TN_ADA), lambda l, j: (l, 0, j)),
        ],
        out_specs=pl.BlockSpec((None, COND_ROWS, TN_ADA), lambda l, j: (l, 0, j)),
        out_shape=jax.ShapeDtypeStruct((DEPTH, COND_ROWS, n), F32),
        compiler_params=_params("parallel", "parallel"),
        name="ada",
    )(cond, w_ada, b_ada.reshape(DEPTH, 1, n))


def _ffn_kernel(x_ref, m_ref, g_ref, wg_ref, wu_ref, wo_ref, nf_ref, o_ref, h_ref, acc_ref, *, sub, final):
    f = pl.program_id(1)

    @pl.when(f == 0)
    def _():
        h = _modulate(x_ref[...], g_ref[...], m_ref[3 * sub:3 * sub + 1, :], m_ref[3 * sub + 1:3 * sub + 2, :])
        h_ref[...] = h.astype(BF16)
        acc_ref[...] = jnp.zeros_like(acc_ref)

    h = h_ref[...]
    gate = _dot(h, wg_ref[...])
    up = _dot(h, wu_ref[...])
    act = (gate * _sigmoid(gate) * up).astype(BF16)
    acc_ref[...] += _dot(act, wo_ref[...])

    @pl.when(f == pl.num_programs(1) - 1)
    def _():
        y = x_ref[...] + 0.5 * m_ref[3 * sub + 2:3 * sub + 3, :] * acc_ref[...]
        if final:
            y = _rms(y, nf_ref[...])
        o_ref[...] = y


def _ffn(geo, x, mods, g, w_in, w_out, nf, *, layer, half, sub, final):
    tm, tf = TM_FFN, TF_FFN
    nf_blocks = D_FF // tf
    kern = functools.partial(_ffn_kernel, sub=sub, final=final)
    return pl.pallas_call(
        kern,
        grid=(geo.n_tok // tm, nf_blocks),
        in_specs=[
            pl.BlockSpec((tm, D_MODEL), lambda i, f: (i, 0)),
            pl.BlockSpec((None, N_MOD, D_MODEL), lambda i, f: (geo.cond_row(i, tm), 0, 0)),
            pl.BlockSpec((1, D_MODEL), lambda i, f: (0, 0)),
            pl.BlockSpec((None, None, D_MODEL, tf), lambda i, f: (layer, half, 0, f)),
            pl.BlockSpec((None, None, D_MODEL, tf), lambda i, f: (layer, half, 0, f + nf_blocks)),
            pl.BlockSpec((None, None, tf, D_MODEL), lambda i, f: (layer, half, f, 0)),
            pl.BlockSpec((1, D_MODEL), lambda i, f: (0, 0)),
        ],
        out_specs=pl.BlockSpec((tm, D_MODEL), lambda i, f: (i, 0)),
        out_shape=jax.ShapeDtypeStruct((geo.n_tok, D_MODEL), F32),
        scratch_shapes=[pltpu.VMEM((tm, D_MODEL), BF16), pltpu.VMEM((tm, D_MODEL), F32)],
        compiler_params=_params("parallel", "arbitrary"),
        name="ffn",
    )(x, mods, g, w_in, w_in, w_out, nf)


def _mla_proj_kernel(x_ref, m_ref, g_ref, w_ref, qn_ref, kvn_ref, cos_ref, sin_ref,
                     cq_ref, ckv_ref, ckvb_ref, kr_ref):
    h = _modulate(x_ref[...], g_ref[...], m_ref[3:4, :], m_ref[4:5, :]).astype(BF16)
    z = _dot(h, w_ref[...])
    cq_ref[...] = _rms(z[:, :Q_LORA], qn_ref[...]).astype(BF16)
    ckv = _rms(z[:, Q_LORA:Q_LORA + KV_LORA], kvn_ref[...])
    ckv_ref[...] = ckv
    ckvb_ref[...] = ckv.astype(BF16)
    o = Q_LORA + KV_LORA
    kr_ref[...] = z[:, o:o + QK_ROPE] * cos_ref[...] + z[:, o + QK_ROPE:o + 2 * QK_ROPE] * sin_ref[...]


def _mla_proj(geo, x, mods, g, w_cat, qn, kvn, cos, sin):
    tm = TM_PROJ
    n_out = w_cat.shape[1]
    row = lambda i: (i, 0)
    fix = lambda i: (0, 0)
    return pl.pallas_call(
        _mla_proj_kernel,
        grid=(geo.n_tok // tm,),
        in_specs=[
            pl.BlockSpec((tm, D_MODEL), row),
            pl.BlockSpec((None, N_MOD, D_MODEL), lambda i: (geo.cond_row(i, tm), 0, 0)),
            pl.BlockSpec((1, D_MODEL), fix),
            pl.BlockSpec((D_MODEL, n_out), fix),
            pl.BlockSpec((1, Q_LORA), fix),
            pl.BlockSpec((1, KV_LORA), fix),
            pl.BlockSpec((tm, QK_ROPE), row),
            pl.BlockSpec((tm, QK_ROPE), row),
        ],
        out_specs=[
            pl.BlockSpec((tm, Q_LORA), row),
            pl.BlockSpec((tm, KV_LORA), row),
            pl.BlockSpec((tm, KV_LORA), row),
            pl.BlockSpec((tm, QK_ROPE), row),
        ],
        out_shape=[
            jax.ShapeDtypeStruct((geo.n_tok, Q_LORA), BF16),
            jax.ShapeDtypeStruct((geo.n_tok, KV_LORA), F32),
            jax.ShapeDtypeStruct((geo.n_tok, KV_LORA), BF16),
            jax.ShapeDtypeStruct((geo.n_tok, QK_ROPE), F32),
        ],
        compiler_params=_params("parallel"),
        name="mla_proj",
    )(x, mods, g, w_cat, qn, kvn, cos, sin)


def _attn_kernel(*refs, s_len, t_past, rope):
    if rope:
        (cq_ref, ckvb_ref, kr_ref, pckv_ref, pkr_ref, cos_ref, sin_ref,
         wq_ref, wqr_ref, wkv_ref, o_ref, kv_all, kr_all) = refs
    else:
        cq_ref, ckvb_ref, kr_ref, wq_ref, wkv_ref, o_ref, kv_all, kr_all = refs
    if t_past:
        kv_all[0:t_past, :] = pckv_ref[...].astype(BF16)
        kr_all[0:t_past, :] = pkr_ref[...].astype(BF16)
    kv_all[t_past:, :] = ckvb_ref[...]
    kr_all[t_past:, :] = kr_ref[...].astype(BF16)

    n_q = s_len // Q_TILE
    hg = max(1, ATTN_LOCKSTEP // n_q)
    row_slices = [slice(qb * Q_TILE, (qb + 1) * Q_TILE) for qb in range(n_q)]

    def head_group(g, carry):
        heads = [g * hg + i for i in range(hg)]
        kv = kv_all[...]
        krb = kr_all[...]
        kvp = [_dot(kv, wkv_ref[hd]) for hd in heads]
        kn = [x[:, :QK_NOPE].astype(BF16) for x in kvp]
        v = [x[:, QK_NOPE:].astype(BF16) for x in kvp]
        items = [(i, qb) for i in range(hg) for qb in range(n_q)]
        cq = [cq_ref[rows, :] for rows in row_slices]
        q = [_dot(cq[qb], wq_ref[heads[i]]) for i, qb in items]
        qr = [x[:, QK_NOPE:] for x in q]
        if rope:
            rot = [_dot(cq[qb], wqr_ref[heads[i]]) for i, qb in items]
            qr = [x * cos_ref[row_slices[qb], :] + y * sin_ref[row_slices[qb], :]
                  for x, y, (i, qb) in zip(qr, rot, items)]
        s = [(_dot_nt(x[:, :QK_NOPE].astype(BF16), kn[i]) + _dot_nt(y.astype(BF16), krb)) * ATTN_SCALE
             for x, y, (i, qb) in zip(q, qr, items)]
        p = [jnp.exp(x - jnp.max(x, axis=-1, keepdims=True)) for x in s]
        pr = [(x * (1.0 / jnp.sum(x, axis=-1, keepdims=True))).astype(BF16) for x in p]
        o = [_dot(x, v[i]).astype(BF16) for x, (i, qb) in zip(pr, items)]
        for x, (i, qb) in zip(o, items):
            o_ref[row_slices[qb], pl.ds(pl.multiple_of(heads[i] * V_DIM, V_DIM), V_DIM)] = x
        return carry

    lax.fori_loop(0, MLA_HEADS // hg, head_group, 0)


def _attn(geo, cq, ckvb, kr, wq, wqr, wkv, *, latent, past_ckv=None, past_kr=None, cos=None, sin=None):
    if latent:
        nb, s_len, off = geo.b_lat, geo.s_lat, geo.n_ctx // geo.s_lat
        t_past = past_ckv.shape[1]
    else:
        nb, s_len, off, t_past = geo.b_ctx, geo.s_ctx, 0, 0
    row = lambda b: (b + off, 0)
    fix3 = lambda b: (0, 0, 0)
    tok = lambda width: pl.BlockSpec((s_len, width), row)
    wspec = lambda w: pl.BlockSpec(w.shape, fix3)
    in_specs = [tok(Q_LORA), tok(KV_LORA), tok(QK_ROPE)]
    args = [cq, ckvb, kr]
    if latent:
        in_specs += [pl.BlockSpec((None, t_past, KV_LORA), lambda b: (b, 0, 0)),
                     pl.BlockSpec((None, t_past, QK_ROPE), lambda b: (b, 0, 0)),
                     tok(QK_ROPE), tok(QK_ROPE), wspec(wq), wspec(wqr)]
        args += [past_ckv, past_kr, cos, sin, wq, wqr]
    else:
        in_specs += [wspec(wq)]
        args += [wq]
    in_specs += [wspec(wkv)]
    args += [wkv]
    kern = functools.partial(_attn_kernel, s_len=s_len, t_past=t_past, rope=latent)
    return pl.pallas_call(
        kern,
        grid=(nb,),
        in_specs=in_specs,
        out_specs=pl.BlockSpec((s_len, D_MODEL), lambda b: (b, 0)),
        out_shape=jax.ShapeDtypeStruct((nb * s_len, D_MODEL), BF16),
        scratch_shapes=[pltpu.VMEM((t_past + s_len, KV_LORA), BF16),
                        pltpu.VMEM((t_past + s_len, QK_ROPE), BF16)],
        compiler_params=_params("parallel"),
        name="attn_lat" if latent else "attn_ctx",
    )(*args)


def _out_proj_kernel(a_ref, w_ref, x_ref, m_ref, o_ref):
    o_ref[...] = x_ref[...] + m_ref[5:6, :] * _dot(a_ref[...], w_ref[...])


def _out_proj(geo, a, w, x, mods):
    tm = TM_PROJ
    row = pl.BlockSpec((tm, D_MODEL), lambda i: (i, 0))
    return pl.pallas_call(
        _out_proj_kernel,
        grid=(geo.n_tok // tm,),
        in_specs=[row, pl.BlockSpec((D_MODEL, D_MODEL), lambda i: (0, 0)), row,
                  pl.BlockSpec((None, N_MOD, D_MODEL), lambda i: (geo.cond_row(i, tm), 0, 0))],
        out_specs=row,
        out_shape=jax.ShapeDtypeStruct((geo.n_tok, D_MODEL), F32),
        compiler_params=_params("parallel"),
        name="out_proj",
    )(a, w, x, mods)


def _shifted(geo, x_ref, xp_ref, xn_ref, m_ref, g_ref, tm):
    i = pl.program_id(0)
    g, shift, scale = g_ref[...], m_ref[3:4, :], m_ref[4:5, :]
    h = _modulate(x_ref[...], g, shift, scale)
    start = i * tm
    seq = jnp.where(start < geo.n_ctx, geo.s_ctx, geo.s_lat)
    rel = jnp.where(start < geo.n_ctx, start, start - geo.n_ctx)
    has_prev = (rel % seq) != 0
    has_next = ((rel + tm) % seq) != 0
    hp = _modulate(xp_ref[...], g, shift, scale)[7:8, :]
    hn = _modulate(xn_ref[...], g, shift, scale)[0:1, :]
    hp = jnp.where(has_prev, hp, 0.0)
    hn = jnp.where(has_next, hn, 0.0)
    r = lax.broadcasted_iota(jnp.int32, h.shape, 0)
    down = jnp.where(r == 0, hp, pltpu.roll(h, 1, 0))
    up = jnp.where(r == tm - 1, hn, pltpu.roll(h, tm - 1, 0))
    return h, 0.5 * (down + up) - h


def _halo_specs(geo, tm):
    nb8 = geo.n_tok // 8
    return [
        pl.BlockSpec((tm, D_MODEL), lambda i, *_: (i, 0)),
        pl.BlockSpec((8, D_MODEL), lambda i, *_: (jnp.maximum(i * (tm // 8) - 1, 0), 0)),
        pl.BlockSpec((8, D_MODEL), lambda i, *_: (jnp.minimum((i + 1) * (tm // 8), nb8 - 1), 0)),
        pl.BlockSpec((None, N_MOD, D_MODEL), lambda i, *_: (geo.cond_row(i, tm), 0, 0)),
        pl.BlockSpec((1, D_MODEL), lambda i, *_: (0, 0)),
    ]


def _rkv_kernel(x_ref, xp_ref, xn_ref, m_ref, g_ref, mu_ref, w_ref, o_ref, h_ref, xx_ref, *, geo, tm):
    @pl.when(pl.program_id(1) == 0)
    def _():
        h, xx = _shifted(geo, x_ref, xp_ref, xn_ref, m_ref, g_ref, tm)
        h_ref[...] = h
        xx_ref[...] = xx

    xm = (h_ref[...] + xx_ref[...] * mu_ref[...]).astype(BF16)
    o_ref[...] = _dot(xm, w_ref[pl.program_id(1)])


def _rkv(geo, x, mods, g, mu3, w3):
    tm = TM_RWKV
    return pl.pallas_call(
        functools.partial(_rkv_kernel, geo=geo, tm=tm),
        grid=(geo.n_tok // tm, 3),
        in_specs=_halo_specs(geo, tm) + [
            pl.BlockSpec((None, 1, D_MODEL), lambda i, p: (p, 0, 0)),
            pl.BlockSpec((3, D_MODEL, D_MODEL), lambda i, p: (0, 0, 0), pipeline_mode=pl.Buffered(1)),
        ],
        out_specs=pl.BlockSpec((None, tm, D_MODEL), lambda i, p: (p, i, 0)),
        out_shape=jax.ShapeDtypeStruct((3, geo.n_tok, D_MODEL), F32),
        scratch_shapes=[pltpu.VMEM((tm, D_MODEL), F32), pltpu.VMEM((tm, D_MODEL), F32)],
        compiler_params=_params("parallel", "arbitrary"),
        name="rwkv_rkv",
    )(x, x, x, mods, g, mu3, w3)


def _lora_kernel(x_ref, xp_ref, xn_ref, m_ref, g_ref, mug_ref, mud_ref, g1_ref, g2_ref,
                 wa1_ref, wa2_ref, w0_ref, a0_ref, gate_ref, dec_ref, asig_ref, *, geo, tm):
    h, xx = _shifted(geo, x_ref, xp_ref, xn_ref, m_ref, g_ref, tm)
    mix = lambda mu: (h + xx * mu).astype(BF16)
    zg = _dot(mix(mug_ref[...]), g1_ref[...])
    gate_ref[...] = _dot(_sigmoid(zg).astype(BF16), g2_ref[...])
    for d in range(2):
        zw = _dot(mix(mud_ref[2 * d]), wa1_ref[d])
        wl = w0_ref[d] + _dot(jnp.tanh(zw).astype(BF16), wa2_ref[d])
        dec_ref[d] = -LOG_DECAY_SCALE * _sigmoid(wl)
        za = _dot(mix(mud_ref[2 * d + 1]), wa1_ref[2 + d])
        asig_ref[d] = _sigmoid(a0_ref[d] + _dot(za.astype(BF16), wa2_ref[2 + d]))


def _lora(geo, x, mods, g, mu_g, mu_dir, g1, g2, wa1, wa2, w0, a0):
    tm = TM_RWKV
    full = lambda a: pl.BlockSpec(a.shape, lambda i: (0,) * a.ndim)
    tok2 = pl.BlockSpec((2, tm, D_MODEL), lambda i: (0, i, 0))
    return pl.pallas_call(
        functools.partial(_lora_kernel, geo=geo, tm=tm),
        grid=(geo.n_tok // tm,),
        in_specs=_halo_specs(geo, tm) + [full(a) for a in (mu_g, mu_dir, g1, g2, wa1, wa2, w0, a0)],
        out_specs=[pl.BlockSpec((tm, D_MODEL), lambda i: (i, 0)), tok2, tok2],
        out_shape=[jax.ShapeDtypeStruct((geo.n_tok, D_MODEL), F32),
                   jax.ShapeDtypeStruct((2, geo.n_tok, D_MODEL), F32),
                   jax.ShapeDtypeStruct((2, geo.n_tok, D_MODEL), F32)],
        compiler_params=_params("parallel"),
        name="rwkv_lora",
    )(x, x, x, mods, g, mu_g, mu_dir, g1, g2, wa1, wa2, w0, a0)


def _split2(x):
    hi = x.astype(BF16)
    return hi, (x - hi.astype(F32)).astype(BF16)


def _head_sums(x, ones_blk):
    hi, lo = _split2(x)
    return _dot(hi, ones_blk) + _dot(lo, ones_blk)


def _pair_ones():
    r = lax.broadcasted_iota(jnp.int32, (PAIR, PAIR), 0)
    c = lax.broadcasted_iota(jnp.int32, (PAIR, PAIR), 1)
    same = (r < RWKV_HEAD) == (c < RWKV_HEAD)
    return same, jnp.where(same, 1.0, 0.0).astype(BF16)


def _wkv_kernel(r_ref, k_ref, v_ref, dec_ref, asig_ref, kk_ref, ka_ref, rk_ref, s0_ref,
                y_ref, bon_ref, sf_ref, s_ref, *, geo, tb):
    d = pl.program_id(0)
    i = pl.program_id(1)
    nblk = pl.num_programs(1)
    blk = i + d * (nblk - 1 - 2 * i)
    start = blk * tb
    seq = jnp.where(start < geo.n_ctx, geo.s_ctx, geo.s_lat)
    rel = jnp.where(start < geo.n_ctx, start, start - geo.n_ctx)
    at_lo = (rel % seq) == 0
    at_hi = ((rel + tb) % seq) == 0
    first = jnp.where(d == 0, at_lo, at_hi)
    last = jnp.where(d == 0, at_hi, at_lo)

    @pl.when(first)
    def _():
        s_ref[...] = s0_ref[...]

    nch = tb // CHUNK
    sgn = 1 - 2 * d
    fwd = (d == 0).astype(F32)
    row = lax.broadcasted_iota(jnp.int32, (CHUNK, PAIR), 0)
    lane = lax.broadcasted_iota(jnp.int32, (CHUNK, PAIR), 1)
    sidx = lane & (RWKV_HEAD - 1)
    delta = (row - sidx) * sgn
    strict = delta > 0
    incl = delta >= 0
    eye = jnp.where(row == sidx, 1.0, 0.0)
    off_masks = []
    m = 1
    while m < CHUNK:
        off_masks.append(strict & ((row // (2 * m)) == (sidx // (2 * m))) & ((row // m) != (sidx // m)))
        m *= 2
    head0 = lane < RWKV_HEAD
    tr = lax.broadcasted_iota(jnp.int32, (CHUNK, 3 * CHUNK), 0)
    ts = lax.broadcasted_iota(jnp.int32, (CHUNK, 3 * CHUNK), 1) & (CHUNK - 1)
    tri3 = jnp.where((tr - ts) * sgn >= 0, 1.0, 0.0).astype(BF16)
    same_head, ones_blk = _pair_ones()

    def expand(x):
        return jnp.concatenate([jnp.where(head0, x, 0.0), jnp.where(head0, 0.0, x)], axis=0)

    def chunk_body(ci, carry):
        cc = ci * sgn + d * (nch - 1)
        rows = pl.ds(pl.multiple_of(cc * CHUNK, CHUNK), CHUNK)

        def group_body(pg, carry2):
            ps = [pg * PAIR_GROUP + q for q in range(PAIR_GROUP)]
            cols = [pl.ds(pl.multiple_of(p * PAIR, PAIR), PAIR) for p in ps]
            each = lambda f, *ls: [f(*xs) for xs in zip(*ls)]
            r = [r_ref[rows, c] for c in cols]
            k = [k_ref[rows, c] for c in cols]
            v = [v_ref[rows, c] for c in cols]
            logw = [dec_ref[rows, c] for c in cols]
            a = [asig_ref[rows, c] for c in cols]
            kk = [x * kk_ref[:, c] for x, c in zip(k, cols)]
            kd = [x * (1.0 + (y - 1.0) * ka_ref[:, c]) for x, y, c in zip(k, a, cols)]
            sums = [_head_sums(jnp.concatenate([x * x, y * z * rk_ref[:, c]], axis=0), ones_blk)
                    for x, y, z, c in zip(kk, r, kd, cols)]
            kk = each(lambda x, s: x / jnp.maximum(jnp.sqrt(s[:CHUNK]), 1e-12), kk, sums)
            b_in = each(lambda x, y: x * y, kk, a)
            for c, s, y in zip(cols, sums, v):
                bon_ref[rows, c] = s[CHUNK:] * y

            l1 =[x.astype(BF16) for x in logw]
            e1 = each(lambda x, y: x - y.astype(F32), logw, l1)
            l2 = [x.astype(BF16) for x in e1]
            l3 = each(lambda x, y: (x - y.astype(F32)).astype(BF16), e1, l2)
            cum = each(lambda x, y, z: _dot(tri3, jnp.concatenate([x, y, z], axis=0)), l1, l2, l3)
            tot = [fwd * x[CHUNK - 1:CHUNK, :] + (1.0 - fwd) * x[0:1, :] for x in cum]
            c_inv = [jnp.exp(-x) for x in cum]
            at = each(lambda x, y, z: -x * jnp.exp(y - z), kk, cum, logw)
            rt = each(lambda x, y: x * jnp.exp(y), r, cum)
            c_end = each(lambda x, y: jnp.exp(x - y), tot, cum)

            lhs = each(lambda x, y: jnp.concatenate([x, y], axis=0).astype(BF16), at, rt)
            rhs = each(lambda x, y, z: jnp.concatenate([expand(x * z), expand(y * z)], axis=0).astype(BF16),
                       b_in, kd, c_inv)
            g = each(_dot_nt, lhs, rhs)
            ab = [jnp.where(strict, x[:CHUNK, :PAIR], 0.0) for x in g]
            ak = [jnp.where(strict, x[:CHUNK, PAIR:], 0.0) for x in g]
            rbk = [jnp.concatenate([jnp.where(incl, x[CHUNK:, :PAIR], 0.0),
                                    jnp.where(incl, x[CHUNK:, PAIR:], 0.0)], axis=1).astype(BF16) for x in g]

            t_inv = [eye + jnp.where(off_masks[0], x, 0.0) for x in ab]
            for off in off_masks[1:]:
                lx = each(lambda x, t: _dot(jnp.where(off, x, 0.0).astype(BF16), expand(t).astype(BF16)), ab, t_inv)
                t_inv = each(lambda t, x: t + _dot(t.astype(BF16), expand(x).astype(BF16)), t_inv, lx)

            s2 = [s_ref[p] for p in ps]
            ars = each(lambda x, s: _dot_nt(x, s.astype(BF16)), lhs, s2)
            vexp = [expand(x).astype(BF16) for x in v]
            rhs_u = each(lambda x, y, z: x[:CHUNK] + _dot(y.astype(BF16), z), ars, ak, vexp)
            u = each(lambda t, x: _dot(t.astype(BF16), expand(x).astype(BF16)), t_inv, rhs_u)
            uv = each(lambda x, y: jnp.concatenate([expand(x).astype(BF16), y], axis=0), u, vexp)
            y_out = each(lambda x, y, z: x[CHUNK:] + _dot(y, z), ars, rbk, uv)
            for c, x in zip(cols, y_out):
                y_ref[rows, c] = x
            uv_t = each(lambda x, y: jnp.concatenate([x, y], axis=0).T.astype(BF16), u, v)
            bk = each(lambda x, y, z: jnp.concatenate([x * z, y * z], axis=0).astype(BF16), b_in, kd, c_end)
            upd = each(_dot, uv_t, bk)
            for p, s, x, t in zip(ps, s2, upd, tot):
                s_ref[p] = s * jnp.exp(t) + jnp.where(same_head, x, 0.0)
            return carry2

        lax.fori_loop(0, N_PAIRS // PAIR_GROUP, group_body, 0)
        return carry

    lax.fori_loop(0, nch, chunk_body, 0)

    @pl.when(last)
    def _():
        sf_ref[...] = s_ref[...]


def _wkv(geo, rkv, dec, asig, kk_c, ka_c, rk_c, s0):
    tb = TB_WKV
    nblk = geo.n_tok // tb
    blk = lambda d, i: i + d * (nblk - 1 - 2 * i)
    tok = lambda which: pl.BlockSpec((None, tb, D_MODEL), lambda d, i: (which, blk(d, i), 0))
    perdir = pl.BlockSpec((None, tb, D_MODEL), lambda d, i: (d, blk(d, i), 0))
    const = pl.BlockSpec((1, D_MODEL), lambda d, i: (0, 0))
    state = (None, None, N_PAIRS, PAIR, PAIR)
    return pl.pallas_call(
        functools.partial(_wkv_kernel, geo=geo, tb=tb),
        grid=(2, nblk),
        in_specs=[tok(0), tok(1), tok(2), perdir, perdir, const, const, const,
                  pl.BlockSpec(state, lambda d, i: (d, geo.cond_row(blk(d, i), tb), 0, 0, 0))],
        out_specs=[perdir, perdir,
                   pl.BlockSpec(state, lambda d, i: (d, jnp.minimum(blk(d, i) * tb // geo.s_ctx, geo.b_ctx), 0, 0, 0))],
        out_shape=[jax.ShapeDtypeStruct((2, geo.n_tok, D_MODEL), F32),
                   jax.ShapeDtypeStruct((2, geo.n_tok, D_MODEL), F32),
                   jax.ShapeDtypeStruct((2, geo.b_ctx + 1, N_PAIRS, PAIR, PAIR), F32)],
        scratch_shapes=[pltpu.VMEM((N_PAIRS, PAIR, PAIR), F32)],
        compiler_params=_params("arbitrary", "arbitrary"),
        name="wkv_chunked",
    )(rkv, rkv, rkv, dec, asig, kk_c, ka_c, rk_c, s0)


def _rwkv_out_kernel(y_ref, bon_ref, gate_ref, lw_ref, lb_ref, w_ref, x_ref, m_ref, o_ref, z_ref):
    _, ones_blk = _pair_ones()
    inv_n = 1.0 / RWKV_HEAD
    for s in range(N_PAIRS):
        cols = slice(s * PAIR, (s + 1) * PAIR)
        y = y_ref[0, :, cols] + y_ref[1, :, cols]
        yc = y - _head_sums(y, ones_blk) * inv_n
        var = _head_sums(yc * yc, ones_blk) * inv_n
        z = yc * lax.rsqrt(var + GN_EPS) * lw_ref[:, cols] + lb_ref[:, cols] + bon_ref[0, :, cols] + bon_ref[1, :, cols]
        z_ref[:, cols] = (z * gate_ref[:, cols]).astype(BF16)
    o_ref[...] = x_ref[...] + m_ref[5:6, :] * _dot(z_ref[...], w_ref[...])


def _rwkv_out(geo, y, bon, gate, lw, lb, w, x, mods):
    tm = TM_OUT
    row = pl.BlockSpec((tm, D_MODEL), lambda i: (i, 0))
    both = pl.BlockSpec((2, tm, D_MODEL), lambda i: (0, i, 0))
    const = pl.BlockSpec((1, D_MODEL), lambda i: (0, 0))
    return pl.pallas_call(
        _rwkv_out_kernel,
        grid=(geo.n_tok // tm,),
        in_specs=[both, both, row, const, const, pl.BlockSpec((D_MODEL, D_MODEL), lambda i: (0, 0)), row,
                  pl.BlockSpec((None, N_MOD, D_MODEL), lambda i: (geo.cond_row(i, tm), 0, 0))],
        out_specs=row,
        out_shape=jax.ShapeDtypeStruct((geo.n_tok, D_MODEL), F32),
        scratch_shapes=[pltpu.VMEM((tm, D_MODEL), BF16)],
        compiler_params=_params("parallel"),
        name="rwkv_out",
    )(y, bon, gate, lw, lb, w, x, mods)


def _state_to_pairs(s):
    b = s.shape[0]
    s5 = s.reshape(b, N_PAIRS, 2, RWKV_HEAD, RWKV_HEAD)
    z = jnp.zeros_like(s5[:, :, 0])
    top = jnp.concatenate([s5[:, :, 0], z], axis=-1)
    bot = jnp.concatenate([z, s5[:, :, 1]], axis=-1)
    return jnp.concatenate([top, bot], axis=-2)


def _state_from_pairs(sp):
    b = sp.shape[0]
    h0 = sp[:, :, :RWKV_HEAD, :RWKV_HEAD]
    h1 = sp[:, :, RWKV_HEAD:, RWKV_HEAD:]
    return jnp.stack([h0, h1], axis=2).reshape(b, RWKV_HEADS, RWKV_HEAD, RWKV_HEAD)


def _rope_tables(geo):
    rows = geo.s_lat // GRID_W
    row = jnp.repeat(jnp.arange(rows, dtype=F32), GRID_W)
    col = jnp.tile(jnp.arange(GRID_W, dtype=F32), rows)
    inv = ROPE_THETA ** (-jnp.arange(ROPE_FREQS, dtype=F32) / ROPE_FREQS)
    ang = jnp.stack([row[:, None] * inv, col[:, None] * inv], axis=1)
    ang = jnp.broadcast_to(ang[:, :, None, :], (geo.s_lat, 2, 2, ROPE_FREQS)).reshape(geo.s_lat, QK_ROPE)
    cos = jnp.concatenate([jnp.ones((geo.n_ctx, QK_ROPE), F32), jnp.tile(jnp.cos(ang), (geo.b_lat, 1))], axis=0)
    sin = jnp.concatenate([jnp.zeros((geo.n_ctx, QK_ROPE), F32), jnp.tile(jnp.sin(ang), (geo.b_lat, 1))], axis=0)
    return cos, sin


def _rot_cols(w):
    w4 = w.reshape(w.shape[:-1] + (2, 2, ROPE_FREQS))
    return jnp.stack([-w4[..., 1, :], w4[..., 0, :]], axis=-2).reshape(w.shape)


def _mla_layer(geo, x, mods, g, j, cos, sin, cache_ckv, cache_krope, p):
    w_dkv = p['mla_w_dkv'][j]
    w_cat = jnp.concatenate([p['mla_w_dq'][j], w_dkv, _rot_cols(w_dkv[:, KV_LORA:])], axis=1).astype(BF16)
    cq, ckv, ckvb, kr = _mla_proj(geo, x, mods, g, w_cat, p['mla_q_norm'][j][None], p['mla_kv_norm'][j][None],
                                  cos, sin)
    w_uq = p['mla_w_uq'][j].reshape(Q_LORA, MLA_HEADS, QK_NOPE + QK_ROPE).transpose(1, 0, 2)
    wq = w_uq.astype(BF16)
    wqr = _rot_cols(w_uq[..., QK_NOPE:]).astype(BF16)
    wkv = p['mla_w_ukv'][j].reshape(KV_LORA, MLA_HEADS, QK_NOPE + V_DIM).transpose(1, 0, 2).astype(BF16)
    o_ctx = _attn(geo, cq, ckvb, kr, wq, wqr, wkv, latent=False)
    o_lat = _attn(geo, cq, ckvb, kr, wq, wqr, wkv, latent=True,
                  past_ckv=cache_ckv[:, j], past_kr=cache_krope[:, j], cos=cos, sin=sin)
    o = jnp.concatenate([o_ctx, o_lat], axis=0)
    x = _out_proj(geo, o, p['mla_w_o'][j].astype(BF16), x, mods)
    new_ckv = ckv[:geo.n_ctx].reshape(geo.b_ctx, geo.s_ctx, KV_LORA)
    new_kr = kr[:geo.n_ctx].reshape(geo.b_ctx, geo.s_ctx, QK_ROPE)
    return x, new_ckv, new_kr


def _rwkv_layer(geo, x, mods, g, j, s0_fwd, s0_bwd, p):
    mu = p['rwkv_mu'][j]
    mu_dir = p['rwkv_mu_dir'][j].reshape(4, 1, D_MODEL)
    w3 = jnp.stack([p['rwkv_w_r'][j], p['rwkv_w_k'][j], p['rwkv_w_v'][j]]).astype(BF16)
    rkv = _rkv(geo, x, mods, g, mu[:3, None, :], w3)
    pad1 = lambda w: jnp.pad(w, ((0, 0), (0, 0), (0, LORA_PAD - w.shape[-1])))
    pad2 = lambda w: jnp.pad(w, ((0, 0), (0, LORA_PAD - w.shape[-2]), (0, 0)))
    wa1 = jnp.concatenate([pad1(p['rwkv_w1'][j]), pad1(p['rwkv_a1'][j])]).astype(BF16)
    wa2 = jnp.concatenate([pad2(p['rwkv_w2'][j]), pad2(p['rwkv_a2'][j])]).astype(BF16)
    gate, dec, asig = _lora(geo, x, mods, g, mu[3:4], mu_dir, p['rwkv_g1'][j].astype(BF16),
                            p['rwkv_g2'][j].astype(BF16), wa1, wa2,
                            p['rwkv_w0'][j][:, None, :], p['rwkv_a0'][j][:, None, :])
    zero = jnp.zeros((2, 1, N_PAIRS, PAIR, PAIR), F32)
    s0 = jnp.concatenate([zero, jnp.stack([_state_to_pairs(s0_fwd), _state_to_pairs(s0_bwd)])], axis=1)
    y, bon, sf = _wkv(geo, rkv, dec, asig, p['rwkv_k_k'][j][None], p['rwkv_k_a'][j][None],
                      p['rwkv_r_k'][j].reshape(1, D_MODEL), s0)
    x = _rwkv_out(geo, y, bon, gate, p['rwkv_ln_w'][j][None], p['rwkv_ln_b'][j][None],
                  p['rwkv_w_o'][j].astype(BF16), x, mods)
    return x, _state_from_pairs(sf[0, :geo.b_ctx]), _state_from_pairs(sf[1, :geo.b_ctx])


def kernel(x_prompt, x_sample, cache_ckv, cache_krope, state_wkv_fwd, state_wkv_bwd, c, c_ctx, w_ada, b_ada, norm_sub, norm_final, w_ffn_in, w_ffn_out, mla_w_dq, mla_q_norm, mla_w_uq, mla_w_dkv, mla_kv_norm, mla_w_ukv, mla_w_o, rwkv_mu, rwkv_mu_dir, rwkv_w_r, rwkv_w_k, rwkv_w_v, rwkv_w0, rwkv_w1, rwkv_w2, rwkv_a0, rwkv_a1, rwkv_a2, rwkv_g1, rwkv_g2, rwkv_k_k, rwkv_k_a, rwkv_r_k, rwkv_ln_w, rwkv_ln_b, rwkv_w_o):
    p = dict(mla_w_dq=mla_w_dq, mla_q_norm=mla_q_norm, mla_w_uq=mla_w_uq, mla_w_dkv=mla_w_dkv,
             mla_kv_norm=mla_kv_norm, mla_w_ukv=mla_w_ukv, mla_w_o=mla_w_o,
             rwkv_mu=rwkv_mu, rwkv_mu_dir=rwkv_mu_dir, rwkv_w_r=rwkv_w_r, rwkv_w_k=rwkv_w_k,
             rwkv_w_v=rwkv_w_v, rwkv_w0=rwkv_w0, rwkv_w1=rwkv_w1, rwkv_w2=rwkv_w2,
             rwkv_a0=rwkv_a0, rwkv_a1=rwkv_a1, rwkv_a2=rwkv_a2, rwkv_g1=rwkv_g1, rwkv_g2=rwkv_g2,
             rwkv_k_k=rwkv_k_k, rwkv_k_a=rwkv_k_a, rwkv_r_k=rwkv_r_k,
             rwkv_ln_w=rwkv_ln_w, rwkv_ln_b=rwkv_ln_b, rwkv_w_o=rwkv_w_o)
    b_ctx, s_ctx, _ = x_prompt.shape
    b_lat, s_lat, _ = x_sample.shape
    geo = _Geom(b_ctx, s_ctx, b_lat, s_lat)
    assert geo.n_ctx % s_lat == 0 and s_ctx % TM_RWKV == 0 and s_lat % TM_FFN == 0
    assert s_ctx == TB_WKV and s_lat % TB_WKV == 0

    x = jnp.concatenate([x_prompt.reshape(geo.n_ctx, D_MODEL), x_sample.reshape(geo.n_lat, D_MODEL)], axis=0)
    cond = jnp.concatenate([c_ctx[None], c, jnp.zeros((COND_ROWS - 1 - b_lat, D_MODEL), F32)], axis=0)
    mods_all = _ada(cond, w_ada, b_ada).reshape(DEPTH, COND_ROWS, N_MOD, D_MODEL)
    cos, sin = _rope_tables(geo)
    nf = norm_final[None]
    w_in = w_ffn_in.astype(BF16)
    w_out = w_ffn_out.astype(BF16)

    ckv_l, kr_l, sf_l, sb_l = [], [], [], []
    for l in range(DEPTH):
        j = l // N_MIXERS
        mods = mods_all[l]
        x = _ffn(geo, x, mods, norm_sub[l, 0][None], w_in, w_out, nf, layer=l, half=0, sub=0, final=False)
        if l % N_MIXERS == 0:
            x, ckv, kr = _mla_layer(geo, x, mods, norm_sub[l, 1][None], j, cos, sin, cache_ckv, cache_krope, p)
            ckv_l.append(ckv)
            kr_l.append(kr)
        else:
            x, sf, sb = _rwkv_layer(geo, x, mods, norm_sub[l, 1][None], j,
                                    state_wkv_fwd[:, j], state_wkv_bwd[:, j], p)
            sf_l.append(sf)
            sb_l.append(sb)
        x = _ffn(geo, x, mods, norm_sub[l, 2][None], w_in, w_out, nf, layer=l, half=1, sub=2,
                 final=(l == DEPTH - 1))

    y_prompt = x[:geo.n_ctx].reshape(b_ctx, s_ctx, D_MODEL)
    y_sample = x[geo.n_ctx:].reshape(b_lat, s_lat, D_MODEL)
    return (y_prompt, y_sample, jnp.stack(ckv_l, axis=1), jnp.stack(kr_l, axis=1),
            jnp.stack(sf_l, axis=1), jnp.stack(sb_l, axis=1))
```

```python
import functools

import jax
import jax.numpy as jnp
from jax import lax
from jax.experimental import pallas as pl
from jax.experimental.pallas import tpu as pltpu

D_MODEL = 2048
DEPTH = 4
N_MIXERS = 2
D_FF = 5632
N_MOD = 9
RMS_EPS = 1e-6
MLA_HEADS = 16
Q_LORA = 512
KV_LORA = 512
QK_NOPE = 128
QK_ROPE = 64
V_DIM = 128
ROPE_FREQS = QK_ROPE // 4
ROPE_THETA = 10000.0
GRID_W = 64
ATTN_SCALE = (QK_NOPE + QK_ROPE) ** -0.5
RWKV_HEAD = 64
RWKV_HEADS = D_MODEL // RWKV_HEAD
GN_EPS = 64e-5
LOG_DECAY_SCALE = 0.6065306597126334
LORA_PAD = 128

COND_ROWS = 8
BF16_ROWS = 16
VMEM_LIMIT = 56 * 1024 * 1024

TM_FFN = 512
TF_FFN = 512
FFN_SPLIT = 2
TM_PROJ = 512
TM_RWKV = 256
TN_ADA = 1024
Q_TILE = 256
ATTN_LOCKSTEP = 4
CHUNK = 64
PAIR = 2 * RWKV_HEAD
N_PAIRS = D_MODEL // PAIR
PAIR_GROUP = 16
TB_WKV = 256
TM_OUT = 256

BF16 = jnp.bfloat16
F32 = jnp.float32


def _params(*sem):
    return pltpu.CompilerParams(dimension_semantics=sem, vmem_limit_bytes=VMEM_LIMIT)


def _sigmoid(x):
    return 1.0 / (1.0 + jnp.exp(-x))


def _modulate(x, g, shift, scale):
    ms = jnp.mean(x * x, axis=-1, keepdims=True)
    return (x * lax.rsqrt(ms + RMS_EPS) * g) * (1.0 + scale) + shift


def _rms(x, w):
    ms = jnp.mean(x * x, axis=-1, keepdims=True)
    return x * lax.rsqrt(ms + RMS_EPS) * w


def _dot(a, b):
    return jnp.dot(a, b, preferred_element_type=F32)


def _dot_nt(a, b):
    return lax.dot_general(a, b, (((1,), (1,)), ((), ())), preferred_element_type=F32)


class _Geom:
    def __init__(self, b_ctx, s_ctx, b_lat, s_lat):
        self.b_ctx, self.s_ctx, self.b_lat, self.s_lat = b_ctx, s_ctx, b_lat, s_lat
        self.n_ctx = b_ctx * s_ctx
        self.n_lat = b_lat * s_lat
        self.n_tok = self.n_ctx + self.n_lat

    def cond_row(self, i, tm):
        start = i * tm
        return jnp.where(start < self.n_ctx, 0, 1 + (start - self.n_ctx) // self.s_lat)


def _ada_kernel(c_ref, w_ref, b_ref, o_ref):
    c = c_ref[...]
    s = (c * _sigmoid(c)).astype(BF16)
    o_ref[...] = _dot(s, w_ref[...].astype(BF16)) + b_ref[...]


def _ada(cond, w_ada, b_ada):
    n = N_MOD * D_MODEL
    return pl.pallas_call(
        _ada_kernel,
        grid=(DEPTH, n // TN_ADA),
        in_specs=[
            pl.BlockSpec((COND_ROWS, D_MODEL), lambda l, j: (0, 0)),
            pl.BlockSpec((None, D_MODEL, TN_ADA), lambda l, j: (l, 0, j)),
            pl.BlockSpec((None, 1, TN_ADA), lambda l, j: (l, 0, j)),
        ],
        out_specs=pl.BlockSpec((None, COND_ROWS, TN_ADA), lambda l, j: (l, 0, j)),
        out_shape=jax.ShapeDtypeStruct((DEPTH, COND_ROWS, n), F32),
        compiler_params=_params("parallel", "parallel"),
        name="ada",
    )(cond, w_ada, b_ada.reshape(DEPTH, 1, n))


def _ffn_kernel(x_ref, xn_ref, m_ref, mn_ref, g_ref, wg_ref, wu_ref, wo_ref, nf_ref, o_ref, h_ref, acc_ref,
                *, sub, final, rows_per_step):
    i = pl.program_id(0)
    f = pl.program_id(1)
    slot = i % 2

    def modulated(xr, mr, rows):
        return _modulate(xr[rows, :], g_ref[...], mr[3 * sub:3 * sub + 1, :], mr[3 * sub + 1:3 * sub + 2, :]).astype(BF16)

    @pl.when((f == 0) & (i == 0))
    def _():
        h_ref[0] = modulated(x_ref, m_ref, slice(None))

    h = h_ref[slot]
    width = wg_ref.shape[1] // FFN_SPLIT
    acts = []
    for s in range(FFN_SPLIT):
        cols = slice(s * width, (s + 1) * width)
        gate = _dot(h, wg_ref[:, cols])
        up = _dot(h, wu_ref[:, cols])
        acts.append((gate * _sigmoid(gate) * up).astype(BF16))
    acc = jnp.where(f == 0, 0.0, acc_ref[...])
    for s, act in enumerate(acts):
        acc = acc + _dot(act, wo_ref[s * width:(s + 1) * width, :])
    acc_ref[...] = acc

    start = jnp.minimum(f * rows_per_step, x_ref.shape[0] - rows_per_step)
    rows = pl.ds(pl.multiple_of(start, BF16_ROWS), rows_per_step)
    h_ref[1 - slot, rows, :] = modulated(xn_ref, mn_ref, rows)

    @pl.when(f == pl.num_programs(1) - 1)
    def _():
        y = x_ref[...] + 0.5 * m_ref[3 * sub + 2:3 * sub + 3, :] * acc_ref[...]
        if final:
            y = _rms(y, nf_ref[...])
        o_ref[...] = y


def _ffn(geo, x, mods, g, w_in, w_out, nf, *, layer, half, sub, final):
    tm, tf = TM_FFN, TF_FFN
    nf_blocks = D_FF // tf
    n_blocks = geo.n_tok // tm
    rows_per_step = -(-tm // (nf_blocks * BF16_ROWS)) * BF16_ROWS
    assert rows_per_step * nf_blocks >= tm and (tm - rows_per_step) % BF16_ROWS == 0
    kern = functools.partial(_ffn_kernel, sub=sub, final=final, rows_per_step=rows_per_step)
    nxt = lambda i: jnp.minimum(i + 1, n_blocks - 1)
    return pl.pallas_call(
        kern,
        grid=(n_blocks, nf_blocks),
        in_specs=[
            pl.BlockSpec((tm, D_MODEL), lambda i, f: (i, 0)),
            pl.BlockSpec((tm, D_MODEL), lambda i, f: (nxt(i), 0)),
            pl.BlockSpec((None, N_MOD, D_MODEL), lambda i, f: (geo.cond_row(i, tm), 0, 0)),
            pl.BlockSpec((None, N_MOD, D_MODEL), lambda i, f: (geo.cond_row(nxt(i), tm), 0, 0)),
            pl.BlockSpec((1, D_MODEL), lambda i, f: (0, 0)),
            pl.BlockSpec((None, None, D_MODEL, tf), lambda i, f: (layer, half, 0, f)),
            pl.BlockSpec((None, None, D_MODEL, tf), lambda i, f: (layer, half, 0, f + nf_blocks)),
            pl.BlockSpec((None, None, tf, D_MODEL), lambda i, f: (layer, half, f, 0)),
            pl.BlockSpec((1, D_MODEL), lambda i, f: (0, 0)),
        ],
        out_specs=pl.BlockSpec((tm, D_MODEL), lambda i, f: (i, 0)),
        out_shape=jax.ShapeDtypeStruct((geo.n_tok, D_MODEL), F32),
        scratch_shapes=[pltpu.VMEM((2, tm, D_MODEL), BF16), pltpu.VMEM((tm, D_MODEL), F32)],
        compiler_params=_params("arbitrary", "arbitrary"),
        name="ffn",
    )(x, x, mods, mods, g, w_in, w_in, w_out, nf)


def _mla_proj_kernel(x_ref, m_ref, g_ref, w_ref, qn_ref, kvn_ref, cos_ref, sin_ref,
                     cq_ref, ckv_ref, ckvb_ref, kr_ref):
    h = _modulate(x_ref[...], g_ref[...], m_ref[3:4, :], m_ref[4:5, :]).astype(BF16)
    z = _dot(h, w_ref[...])
    cq_ref[...] = _rms(z[:, :Q_LORA], qn_ref[...]).astype(BF16)
    ckv = _rms(z[:, Q_LORA:Q_LORA + KV_LORA], kvn_ref[...])
    ckv_ref[...] = ckv
    ckvb_ref[...] = ckv.astype(BF16)
    o = Q_LORA + KV_LORA
    kr_ref[...] = z[:, o:o + QK_ROPE] * cos_ref[...] + z[:, o + QK_ROPE:o + 2 * QK_ROPE] * sin_ref[...]


def _mla_proj(geo, x, mods, g, w_cat, qn, kvn, cos, sin):
    tm = TM_PROJ
    n_out = w_cat.shape[1]
    row = lambda i: (i, 0)
    fix = lambda i: (0, 0)
    return pl.pallas_call(
        _mla_proj_kernel,
        grid=(geo.n_tok // tm,),
        in_specs=[
            pl.BlockSpec((tm, D_MODEL), row),
            pl.BlockSpec((None, N_MOD, D_MODEL), lambda i: (geo.cond_row(i, tm), 0, 0)),
            pl.BlockSpec((1, D_MODEL), fix),
            pl.BlockSpec((D_MODEL, n_out), fix),
            pl.BlockSpec((1, Q_LORA), fix),
            pl.BlockSpec((1, KV_LORA), fix),
            pl.BlockSpec((tm, QK_ROPE), row),
            pl.BlockSpec((tm, QK_ROPE), row),
        ],
        out_specs=[
            pl.BlockSpec((tm, Q_LORA), row),
            pl.BlockSpec((tm, KV_LORA), row),
            pl.BlockSpec((tm, KV_LORA), row),
            pl.BlockSpec((tm, QK_ROPE), row),
        ],
        out_shape=[
            jax.ShapeDtypeStruct((geo.n_tok, Q_LORA), BF16),
            jax.ShapeDtypeStruct((geo.n_tok, KV_LORA), F32),
            jax.ShapeDtypeStruct((geo.n_tok, KV_LORA), BF16),
            jax.ShapeDtypeStruct((geo.n_tok, QK_ROPE), F32),
        ],
        compiler_params=_params("parallel"),
        name="mla_proj",
    )(x, mods, g, w_cat, qn, kvn, cos, sin)


def _attn_kernel(*refs, s_len, t_past, rope):
    if rope:
        (cq_ref, ckvb_ref, kr_ref, pckv_ref, pkr_ref, cos_ref, sin_ref,
         wq_ref, wqr_ref, wkv_ref, o_ref, kv_all, kr_all) = refs
    else:
        cq_ref, ckvb_ref, kr_ref, wq_ref, wkv_ref, o_ref, kv_all, kr_all = refs
    if t_past:
        kv_all[0:t_past, :] = pckv_ref[...].astype(BF16)
        kr_all[0:t_past, :] = pkr_ref[...].astype(BF16)
    kv_all[t_past:, :] = ckvb_ref[...]
    kr_all[t_past:, :] = kr_ref[...].astype(BF16)

    n_q = s_len // Q_TILE
    hg = max(1, ATTN_LOCKSTEP // n_q)
    row_slices = [slice(qb * Q_TILE, (qb + 1) * Q_TILE) for qb in range(n_q)]

    def head_group(g, carry):
        heads = [g * hg + i for i in range(hg)]
        kv = kv_all[...]
        krb = kr_all[...]
        kvp = [_dot(kv, wkv_ref[hd]) for hd in heads]
        kn = [x[:, :QK_NOPE].astype(BF16) for x in kvp]
        v = [x[:, QK_NOPE:].astype(BF16) for x in kvp]
        items = [(i, qb) for i in range(hg) for qb in range(n_q)]
        cq = [cq_ref[rows, :] for rows in row_slices]
        q = [_dot(cq[qb], wq_ref[heads[i]]) for i, qb in items]
        qr = [x[:, QK_NOPE:] for x in q]
        if rope:
            rot = [_dot(cq[qb], wqr_ref[heads[i]]) for i, qb in items]
            qr = [x * cos_ref[row_slices[qb], :] + y * sin_ref[row_slices[qb], :]
                  for x, y, (i, qb) in zip(qr, rot, items)]
        s = [(_dot_nt(x[:, :QK_NOPE].astype(BF16), kn[i]) + _dot_nt(y.astype(BF16), krb)) * ATTN_SCALE
             for x, y, (i, qb) in zip(q, qr, items)]
        p = [jnp.exp(x - jnp.max(x, axis=-1, keepdims=True)) for x in s]
        pr = [(x * (1.0 / jnp.sum(x, axis=-1, keepdims=True))).astype(BF16) for x in p]
        o = [_dot(x, v[i]).astype(BF16) for x, (i, qb) in zip(pr, items)]
        for x, (i, qb) in zip(o, items):
            o_ref[row_slices[qb], pl.ds(pl.multiple_of(heads[i] * V_DIM, V_DIM), V_DIM)] = x
        return carry

    lax.fori_loop(0, MLA_HEADS // hg, head_group, 0)


def _attn(geo, cq, ckvb, kr, wq, wqr, wkv, *, latent, past_ckv=None, past_kr=None, cos=None, sin=None):
    if latent:
        nb, s_len, off = geo.b_lat, geo.s_lat, geo.n_ctx // geo.s_lat
        t_past = past_ckv.shape[1]
    else:
        nb, s_len, off, t_past = geo.b_ctx, geo.s_ctx, 0, 0
    row = lambda b: (b + off, 0)
    fix3 = lambda b: (0, 0, 0)
    tok = lambda width: pl.BlockSpec((s_len, width), row)
    wspec = lambda w: pl.BlockSpec(w.shape, fix3)
    in_specs = [tok(Q_LORA), tok(KV_LORA), tok(QK_ROPE)]
    args = [cq, ckvb, kr]
    if latent:
        in_specs += [pl.BlockSpec((None, t_past, KV_LORA), lambda b: (b, 0, 0)),
                     pl.BlockSpec((None, t_past, QK_ROPE), lambda b: (b, 0, 0)),
                     tok(QK_ROPE), tok(QK_ROPE), wspec(wq), wspec(wqr)]
        args += [past_ckv, past_kr, cos, sin, wq, wqr]
    else:
        in_specs += [wspec(wq)]
        args += [wq]
    in_specs += [wspec(wkv)]
    args += [wkv]
    kern = functools.partial(_attn_kernel, s_len=s_len, t_past=t_past, rope=latent)
    return pl.pallas_call(
        kern,
        grid=(nb,),
        in_specs=in_specs,
        out_specs=pl.BlockSpec((s_len, D_MODEL), lambda b: (b, 0)),
        out_shape=jax.ShapeDtypeStruct((nb * s_len, D_MODEL), BF16),
        scratch_shapes=[pltpu.VMEM((t_past + s_len, KV_LORA), BF16),
                        pltpu.VMEM((t_past + s_len, QK_ROPE), BF16)],
        compiler_params=_params("parallel"),
        name="attn_lat" if latent else "attn_ctx",
    )(*args)


def _out_proj_kernel(a_ref, w_ref, x_ref, m_ref, o_ref):
    o_ref[...] = x_ref[...] + m_ref[5:6, :] * _dot(a_ref[...], w_ref[...])


def _out_proj(geo, a, w, x, mods):
    tm = TM_PROJ
    row = pl.BlockSpec((tm, D_MODEL), lambda i: (i, 0))
    return pl.pallas_call(
        _out_proj_kernel,
        grid=(geo.n_tok // tm,),
        in_specs=[row, pl.BlockSpec((D_MODEL, D_MODEL), lambda i: (0, 0)), row,
                  pl.BlockSpec((None, N_MOD, D_MODEL), lambda i: (geo.cond_row(i, tm), 0, 0))],
        out_specs=row,
        out_shape=jax.ShapeDtypeStruct((geo.n_tok, D_MODEL), F32),
        compiler_params=_params("parallel"),
        name="out_proj",
    )(a, w, x, mods)


def _shifted(geo, x_ref, xp_ref, xn_ref, m_ref, g_ref, tm):
    i = pl.program_id(0)
    g, shift, scale = g_ref[...], m_ref[3:4, :], m_ref[4:5, :]
    h = _modulate(x_ref[...], g, shift, scale)
    start = i * tm
    seq = jnp.where(start < geo.n_ctx, geo.s_ctx, geo.s_lat)
    rel = jnp.where(start < geo.n_ctx, start, start - geo.n_ctx)
    has_prev = (rel % seq) != 0
    has_next = ((rel + tm) % seq) != 0
    hp = _modulate(xp_ref[...], g, shift, scale)[7:8, :]
    hn = _modulate(xn_ref[...], g, shift, scale)[0:1, :]
    hp = jnp.where(has_prev, hp, 0.0)
    hn = jnp.where(has_next, hn, 0.0)
    r = lax.broadcasted_iota(jnp.int32, h.shape, 0)
    down = jnp.where(r == 0, hp, pltpu.roll(h, 1, 0))
    up = jnp.where(r == tm - 1, hn, pltpu.roll(h, tm - 1, 0))
    return h, 0.5 * (down + up) - h


def _halo_specs(geo, tm):
    nb8 = geo.n_tok // 8
    return [
        pl.BlockSpec((tm, D_MODEL), lambda i, *_: (i, 0)),
        pl.BlockSpec((8, D_MODEL), lambda i, *_: (jnp.maximum(i * (tm // 8) - 1, 0), 0)),
        pl.BlockSpec((8, D_MODEL), lambda i, *_: (jnp.minimum((i + 1) * (tm // 8), nb8 - 1), 0)),
        pl.BlockSpec((None, N_MOD, D_MODEL), lambda i, *_: (geo.cond_row(i, tm), 0, 0)),
        pl.BlockSpec((1, D_MODEL), lambda i, *_: (0, 0)),
    ]


def _rkv_kernel(x_ref, xp_ref, xn_ref, m_ref, g_ref, mu_ref, w_ref, o_ref, h_ref, xx_ref, *, geo, tm):
    @pl.when(pl.program_id(1) == 0)
    def _():
        h, xx = _shifted(geo, x_ref, xp_ref, xn_ref, m_ref, g_ref, tm)
        h_ref[...] = h
        xx_ref[...] = xx

    xm = (h_ref[...] + xx_ref[...] * mu_ref[...]).astype(BF16)
    o_ref[...] = _dot(xm, w_ref[pl.program_id(1)])


def _rkv(geo, x, mods, g, mu3, w3):
    tm = TM_RWKV
    return pl.pallas_call(
        functools.partial(_rkv_kernel, geo=geo, tm=tm),
        grid=(geo.n_tok // tm, 3),
        in_specs=_halo_specs(geo, tm) + [
            pl.BlockSpec((None, 1, D_MODEL), lambda i, p: (p, 0, 0)),
            pl.BlockSpec((3, D_MODEL, D_MODEL), lambda i, p: (0, 0, 0), pipeline_mode=pl.Buffered(1)),
        ],
        out_specs=pl.BlockSpec((None, tm, D_MODEL), lambda i, p: (p, i, 0)),
        out_shape=jax.ShapeDtypeStruct((3, geo.n_tok, D_MODEL), F32),
        scratch_shapes=[pltpu.VMEM((tm, D_MODEL), F32), pltpu.VMEM((tm, D_MODEL), F32)],
        compiler_params=_params("parallel", "arbitrary"),
        name="rwkv_rkv",
    )(x, x, x, mods, g, mu3, w3)


def _lora_kernel(x_ref, xp_ref, xn_ref, m_ref, g_ref, mug_ref, mud_ref, g1_ref, g2_ref,
                 wa1_ref, wa2_ref, w0_ref, a0_ref, gate_ref, dec_ref, asig_ref, *, geo, tm):
    h, xx = _shifted(geo, x_ref, xp_ref, xn_ref, m_ref, g_ref, tm)
    mix = lambda mu: (h + xx * mu).astype(BF16)
    zg = _dot(mix(mug_ref[...]), g1_ref[...])
    gate_ref[...] = _dot(_sigmoid(zg).astype(BF16), g2_ref[...])
    for d in range(2):
        zw = _dot(mix(mud_ref[2 * d]), wa1_ref[d])
        wl = w0_ref[d] + _dot(jnp.tanh(zw).astype(BF16), wa2_ref[d])
        dec_ref[d] = -LOG_DECAY_SCALE * _sigmoid(wl)
        za = _dot(mix(mud_ref[2 * d + 1]), wa1_ref[2 + d])
        asig_ref[d] = _sigmoid(a0_ref[d] + _dot(za.astype(BF16), wa2_ref[2 + d]))


def _lora(geo, x, mods, g, mu_g, mu_dir, g1, g2, wa1, wa2, w0, a0):
    tm = TM_RWKV
    full = lambda a: pl.BlockSpec(a.shape, lambda i: (0,) * a.ndim)
    tok2 = pl.BlockSpec((2, tm, D_MODEL), lambda i: (0, i, 0))
    return pl.pallas_call(
        functools.partial(_lora_kernel, geo=geo, tm=tm),
        grid=(geo.n_tok // tm,),
        in_specs=_halo_specs(geo, tm) + [full(a) for a in (mu_g, mu_dir, g1, g2, wa1, wa2, w0, a0)],
        out_specs=[pl.BlockSpec((tm, D_MODEL), lambda i: (i, 0)), tok2, tok2],
        out_shape=[jax.ShapeDtypeStruct((geo.n_tok, D_MODEL), F32),
                   jax.ShapeDtypeStruct((2, geo.n_tok, D_MODEL), F32),
                   jax.ShapeDtypeStruct((2, geo.n_tok, D_MODEL), F32)],
        compiler_params=_params("parallel"),
        name="rwkv_lora",
    )(x, x, x, mods, g, mu_g, mu_dir, g1, g2, wa1, wa2, w0, a0)


def _split2(x):
    hi = x.astype(BF16)
    return hi, (x - hi.astype(F32)).astype(BF16)


def _head_sums(x, ones_blk):
    hi, lo = _split2(x)
    return _dot(hi, ones_blk) + _dot(lo, ones_blk)


def _pair_ones():
    r = lax.broadcasted_iota(jnp.int32, (PAIR, PAIR), 0)
    c = lax.broadcasted_iota(jnp.int32, (PAIR, PAIR), 1)
    same = (r < RWKV_HEAD) == (c < RWKV_HEAD)
    return same, jnp.where(same, 1.0, 0.0).astype(BF16)


def _wkv_kernel(r_ref, k_ref, v_ref, dec_ref, asig_ref, kk_ref, ka_ref, rk_ref, s0_ref,
                y_ref, bon_ref, sf_ref, s_ref, *, geo, tb):
    d = pl.program_id(0)
    i = pl.program_id(1)
    nblk = pl.num_programs(1)
    blk = i + d * (nblk - 1 - 2 * i)
    start = blk * tb
    seq = jnp.where(start < geo.n_ctx, geo.s_ctx, geo.s_lat)
    rel = jnp.where(start < geo.n_ctx, start, start - geo.n_ctx)
    at_lo = (rel % seq) == 0
    at_hi = ((rel + tb) % seq) == 0
    first = jnp.where(d == 0, at_lo, at_hi)
    last = jnp.where(d == 0, at_hi, at_lo)

    @pl.when(first)
    def _():
        s_ref[...] = s0_ref[...]

    nch = tb // CHUNK
    sgn = 1 - 2 * d
    fwd = (d == 0).astype(F32)
    row = lax.broadcasted_iota(jnp.int32, (CHUNK, PAIR), 0)
    lane = lax.broadcasted_iota(jnp.int32, (CHUNK, PAIR), 1)
    sidx = lane & (RWKV_HEAD - 1)
    delta = (row - sidx) * sgn
    strict = delta > 0
    incl = delta >= 0
    eye = jnp.where(row == sidx, 1.0, 0.0)
    off_masks = []
    m = 1
    while m < CHUNK:
        off_masks.append(strict & ((row // (2 * m)) == (sidx // (2 * m))) & ((row // m) != (sidx // m)))
        m *= 2
    head0 = lane < RWKV_HEAD
    tr = lax.broadcasted_iota(jnp.int32, (CHUNK, 3 * CHUNK), 0)
    ts = lax.broadcasted_iota(jnp.int32, (CHUNK, 3 * CHUNK), 1) & (CHUNK - 1)
    tri3 = jnp.where((tr - ts) * sgn >= 0, 1.0, 0.0).astype(BF16)
    same_head, ones_blk = _pair_ones()

    def expand(x):
        return jnp.concatenate([jnp.where(head0, x, 0.0), jnp.where(head0, 0.0, x)], axis=0)

    def chunk_body(ci, carry):
        cc = ci * sgn + d * (nch - 1)
        rows = pl.ds(pl.multiple_of(cc * CHUNK, CHUNK), CHUNK)

        def group_body(pg, carry2):
            ps = [pg * PAIR_GROUP + q for q in range(PAIR_GROUP)]
            cols = [pl.ds(pl.multiple_of(p * PAIR, PAIR), PAIR) for p in ps]
            each = lambda f, *ls: [f(*xs) for xs in zip(*ls)]
            r = [r_ref[rows, c] for c in cols]
            k = [k_ref[rows, c] for c in cols]
            v = [v_ref[rows, c] for c in cols]
            logw = [dec_ref[rows, c] for c in cols]
            a = [asig_ref[rows, c] for c in cols]
            kk = [x * kk_ref[:, c] for x, c in zip(k, cols)]
            kd = [x * (1.0 + (y - 1.0) * ka_ref[:, c]) for x, y, c in zip(k, a, cols)]
            sums = [_head_sums(jnp.concatenate([x * x, y * z * rk_ref[:, c]], axis=0), ones_blk)
                    for x, y, z, c in zip(kk, r, kd, cols)]
            kk = each(lambda x, s: x / jnp.maximum(jnp.sqrt(s[:CHUNK]), 1e-12), kk, sums)
            b_in = each(lambda x, y: x * y, kk, a)
            for c, s, y in zip(cols, sums, v):
                bon_ref[rows, c] = s[CHUNK:] * y

            l1 =[x.astype(BF16) for x in logw]
            e1 = each(lambda x, y: x - y.astype(F32), logw, l1)
            l2 = [x.astype(BF16) for x in e1]
            l3 = each(lambda x, y: (x - y.astype(F32)).astype(BF16), e1, l2)
            cum = each(lambda x, y, z: _dot(tri3, jnp.concatenate([x, y, z], axis=0)), l1, l2, l3)
            tot = [fwd * x[CHUNK - 1:CHUNK, :] + (1.0 - fwd) * x[0:1, :] for x in cum]
            c_inv = [jnp.exp(-x) for x in cum]
            at = each(lambda x, y, z: -x * jnp.exp(y - z), kk, cum, logw)
            rt = each(lambda x, y: x * jnp.exp(y), r, cum)
            c_end = each(lambda x, y: jnp.exp(x - y), tot, cum)

            lhs = each(lambda x, y: jnp.concatenate([x, y], axis=0).astype(BF16), at, rt)
            rhs = each(lambda x, y, z: jnp.concatenate([expand(x * z), expand(y * z)], axis=0).astype(BF16),
                       b_in, kd, c_inv)
            g = each(_dot_nt, lhs, rhs)
            ab = [jnp.where(strict, x[:CHUNK, :PAIR], 0.0) for x in g]
            ak = [jnp.where(strict, x[:CHUNK, PAIR:], 0.0) for x in g]
            rbk = [jnp.concatenate([jnp.where(incl, x[CHUNK:, :PAIR], 0.0),
                                    jnp.where(incl, x[CHUNK:, PAIR:], 0.0)], axis=1).astype(BF16) for x in g]

            t_inv = [eye + jnp.where(off_masks[0], x, 0.0) for x in ab]
            for off in off_masks[1:]:
                lx = each(lambda x, t: _dot(jnp.where(off, x, 0.0).astype(BF16), expand(t).astype(BF16)), ab, t_inv)
                t_inv = each(lambda t, x: t + _dot(t.astype(BF16), expand(x).astype(BF16)), t_inv, lx)

            s2 = [s_ref[p] for p in ps]
            ars = each(lambda x, s: _dot_nt(x, s.astype(BF16)), lhs, s2)
            vexp = [expand(x).astype(BF16) for x in v]
            rhs_u = each(lambda x, y, z: x[:CHUNK] + _dot(y.astype(BF16), z), ars, ak, vexp)
            u = each(lambda t, x: _dot(t.astype(BF16), expand(x).astype(BF16)), t_inv, rhs_u)
            uv = each(lambda x, y: jnp.concatenate([expand(x).astype(BF16), y], axis=0), u, vexp)
            y_out = each(lambda x, y, z: x[CHUNK:] + _dot(y, z), ars, rbk, uv)
            for c, x in zip(cols, y_out):
                y_ref[rows, c] = x
            uv_t = each(lambda x, y: jnp.concatenate([x, y], axis=0).T.astype(BF16), u, v)
            bk = each(lambda x, y, z: jnp.concatenate([x * z, y * z], axis=0).astype(BF16), b_in, kd, c_end)
            upd = each(_dot, uv_t, bk)
            for p, s, x, t in zip(ps, s2, upd, tot):
                s_ref[p] = s * jnp.exp(t) + jnp.where(same_head, x, 0.0)
            return carry2

        lax.fori_loop(0, N_PAIRS // PAIR_GROUP, group_body, 0)
        return carry

    lax.fori_loop(0, nch, chunk_body, 0)

    @pl.when(last)
    def _():
        sf_ref[...] = s_ref[...]


def _wkv(geo, rkv, dec, asig, kk_c, ka_c, rk_c, s0):
    tb = TB_WKV
    nblk = geo.n_tok // tb
    blk = lambda d, i: i + d * (nblk - 1 - 2 * i)
    tok = lambda which: pl.BlockSpec((None, tb, D_MODEL), lambda d, i: (which, blk(d, i), 0))
    perdir = pl.BlockSpec((None, tb, D_MODEL), lambda d, i: (d, blk(d, i), 0))
    const = pl.BlockSpec((1, D_MODEL), lambda d, i: (0, 0))
    state = (None, None, N_PAIRS, PAIR, PAIR)
    return pl.pallas_call(
        functools.partial(_wkv_kernel, geo=geo, tb=tb),
        grid=(2, nblk),
        in_specs=[tok(0), tok(1), tok(2), perdir, perdir, const, const, const,
                  pl.BlockSpec(state, lambda d, i: (d, geo.cond_row(blk(d, i), tb), 0, 0, 0))],
        out_specs=[perdir, perdir,
                   pl.BlockSpec(state, lambda d, i: (d, jnp.minimum(blk(d, i) * tb // geo.s_ctx, geo.b_ctx), 0, 0, 0))],
        out_shape=[jax.ShapeDtypeStruct((2, geo.n_tok, D_MODEL), F32),
                   jax.ShapeDtypeStruct((2, geo.n_tok, D_MODEL), F32),
                   jax.ShapeDtypeStruct((2, geo.b_ctx + 1, N_PAIRS, PAIR, PAIR), F32)],
        scratch_shapes=[pltpu.VMEM((N_PAIRS, PAIR, PAIR), F32)],
        compiler_params=_params("arbitrary", "arbitrary"),
        name="wkv_chunked",
    )(rkv, rkv, rkv, dec, asig, kk_c, ka_c, rk_c, s0)


def _rwkv_out_kernel(y_ref, bon_ref, gate_ref, lw_ref, lb_ref, w_ref, x_ref, m_ref, o_ref, z_ref):
    _, ones_blk = _pair_ones()
    inv_n = 1.0 / RWKV_HEAD
    for s in range(N_PAIRS):
        cols = slice(s * PAIR, (s + 1) * PAIR)
        y = y_ref[0, :, cols] + y_ref[1, :, cols]
        yc = y - _head_sums(y, ones_blk) * inv_n
        var = _head_sums(yc * yc, ones_blk) * inv_n
        z = yc * lax.rsqrt(var + GN_EPS) * lw_ref[:, cols] + lb_ref[:, cols] + bon_ref[0, :, cols] + bon_ref[1, :, cols]
        z_ref[:, cols] = (z * gate_ref[:, cols]).astype(BF16)
    o_ref[...] = x_ref[...] + m_ref[5:6, :] * _dot(z_ref[...], w_ref[...])


def _rwkv_out(geo, y, bon, gate, lw, lb, w, x, mods):
    tm = TM_OUT
    row = pl.BlockSpec((tm, D_MODEL), lambda i: (i, 0))
    both = pl.BlockSpec((2, tm, D_MODEL), lambda i: (0, i, 0))
    const = pl.BlockSpec((1, D_MODEL), lambda i: (0, 0))
    return pl.pallas_call(
        _rwkv_out_kernel,
        grid=(geo.n_tok // tm,),
        in_specs=[both, both, row, const, const, pl.BlockSpec((D_MODEL, D_MODEL), lambda i: (0, 0)), row,
                  pl.BlockSpec((None, N_MOD, D_MODEL), lambda i: (geo.cond_row(i, tm), 0, 0))],
        out_specs=row,
        out_shape=jax.ShapeDtypeStruct((geo.n_tok, D_MODEL), F32),
        scratch_shapes=[pltpu.VMEM((tm, D_MODEL), BF16)],
        compiler_params=_params("parallel"),
        name="rwkv_out",
    )(y, bon, gate, lw, lb, w, x, mods)


def _state_to_pairs(s):
    b = s.shape[0]
    s5 = s.reshape(b, N_PAIRS, 2, RWKV_HEAD, RWKV_HEAD)
    z = jnp.zeros_like(s5[:, :, 0])
    top = jnp.concatenate([s5[:, :, 0], z], axis=-1)
    bot = jnp.concatenate([z, s5[:, :, 1]], axis=-1)
    return jnp.concatenate([top, bot], axis=-2)


def _state_from_pairs(sp):
    b = sp.shape[0]
    h0 = sp[:, :, :RWKV_HEAD, :RWKV_HEAD]
    h1 = sp[:, :, RWKV_HEAD:, RWKV_HEAD:]
    return jnp.stack([h0, h1], axis=2).reshape(b, RWKV_HEADS, RWKV_HEAD, RWKV_HEAD)


def _rope_tables(geo):
    rows = geo.s_lat // GRID_W
    row = jnp.repeat(jnp.arange(rows, dtype=F32), GRID_W)
    col = jnp.tile(jnp.arange(GRID_W, dtype=F32), rows)
    inv = ROPE_THETA ** (-jnp.arange(ROPE_FREQS, dtype=F32) / ROPE_FREQS)
    ang = jnp.stack([row[:, None] * inv, col[:, None] * inv], axis=1)
    ang = jnp.broadcast_to(ang[:, :, None, :], (geo.s_lat, 2, 2, ROPE_FREQS)).reshape(geo.s_lat, QK_ROPE)
    cos = jnp.concatenate([jnp.ones((geo.n_ctx, QK_ROPE), F32), jnp.tile(jnp.cos(ang), (geo.b_lat, 1))], axis=0)
    sin = jnp.concatenate([jnp.zeros((geo.n_ctx, QK_ROPE), F32), jnp.tile(jnp.sin(ang), (geo.b_lat, 1))], axis=0)
    return cos, sin


def _rot_cols(w):
    w4 = w.reshape(w.shape[:-1] + (2, 2, ROPE_FREQS))
    return jnp.stack([-w4[..., 1, :], w4[..., 0, :]], axis=-2).reshape(w.shape)


def _mla_layer(geo, x, mods, g, j, cos, sin, cache_ckv, cache_krope, p):
    w_dkv = p['mla_w_dkv'][j]
    w_cat = jnp.concatenate([p['mla_w_dq'][j], w_dkv, _rot_cols(w_dkv[:, KV_LORA:])], axis=1).astype(BF16)
    cq, ckv, ckvb, kr = _mla_proj(geo, x, mods, g, w_cat, p['mla_q_norm'][j][None], p['mla_kv_norm'][j][None],
                                  cos, sin)
    w_uq = p['mla_w_uq'][j].reshape(Q_LORA, MLA_HEADS, QK_NOPE + QK_ROPE).transpose(1, 0, 2)
    wq = w_uq.astype(BF16)
    wqr = _rot_cols(w_uq[..., QK_NOPE:]).astype(BF16)
    wkv = p['mla_w_ukv'][j].reshape(KV_LORA, MLA_HEADS, QK_NOPE + V_DIM).transpose(1, 0, 2).astype(BF16)
    o_ctx = _attn(geo, cq, ckvb, kr, wq, wqr, wkv, latent=False)
    o_lat = _attn(geo, cq, ckvb, kr, wq, wqr, wkv, latent=True,
                  past_ckv=cache_ckv[:, j], past_kr=cache_krope[:, j], cos=cos, sin=sin)
    o = jnp.concatenate([o_ctx, o_lat], axis=0)
    x = _out_proj(geo, o, p['mla_w_o'][j].astype(BF16), x, mods)
    new_ckv = ckv[:geo.n_ctx].reshape(geo.b_ctx, geo.s_ctx, KV_LORA)
    new_kr = kr[:geo.n_ctx].reshape(geo.b_ctx, geo.s_ctx, QK_ROPE)
    return x, new_ckv, new_kr


def _rwkv_layer(geo, x, mods, g, j, s0_fwd, s0_bwd, p):
    mu = p['rwkv_mu'][j]
    mu_dir = p['rwkv_mu_dir'][j].reshape(4, 1, D_MODEL)
    w3 = jnp.stack([p['rwkv_w_r'][j], p['rwkv_w_k'][j], p['rwkv_w_v'][j]]).astype(BF16)
    rkv = _rkv(geo, x, mods, g, mu[:3, None, :], w3)
    pad1 = lambda w: jnp.pad(w, ((0, 0), (0, 0), (0, LORA_PAD - w.shape[-1])))
    pad2 = lambda w: jnp.pad(w, ((0, 0), (0, LORA_PAD - w.shape[-2]), (0, 0)))
    wa1 = jnp.concatenate([pad1(p['rwkv_w1'][j]), pad1(p['rwkv_a1'][j])]).astype(BF16)
    wa2 = jnp.concatenate([pad2(p['rwkv_w2'][j]), pad2(p['rwkv_a2'][j])]).astype(BF16)
    gate, dec, asig = _lora(geo, x, mods, g, mu[3:4], mu_dir, p['rwkv_g1'][j].astype(BF16),
                            p['rwkv_g2'][j].astype(BF16), wa1, wa2,
                            p['rwkv_w0'][j][:, None, :], p['rwkv_a0'][j][:, None, :])
    zero = jnp.zeros((2, 1, N_PAIRS, PAIR, PAIR), F32)
    s0 = jnp.concatenate([zero, jnp.stack([_state_to_pairs(s0_fwd), _state_to_pairs(s0_bwd)])], axis=1)
    y, bon, sf = _wkv(geo, rkv, dec, asig, p['rwkv_k_k'][j][None], p['rwkv_k_a'][j][None],
                      p['rwkv_r_k'][j].reshape(1, D_MODEL), s0)
    x = _rwkv_out(geo, y, bon, gate, p['rwkv_ln_w'][j][None], p['rwkv_ln_b'][j][None],
                  p['rwkv_w_o'][j].astype(BF16), x, mods)
    return x, _state_from_pairs(sf[0, :geo.b_ctx]), _state_from_pairs(sf[1, :geo.b_ctx])


def kernel(x_prompt, x_sample, cache_ckv, cache_krope, state_wkv_fwd, state_wkv_bwd, c, c_ctx, w_ada, b_ada, norm_sub, norm_final, w_ffn_in, w_ffn_out, mla_w_dq, mla_q_norm, mla_w_uq, mla_w_dkv, mla_kv_norm, mla_w_ukv, mla_w_o, rwkv_mu, rwkv_mu_dir, rwkv_w_r, rwkv_w_k, rwkv_w_v, rwkv_w0, rwkv_w1, rwkv_w2, rwkv_a0, rwkv_a1, rwkv_a2, rwkv_g1, rwkv_g2, rwkv_k_k, rwkv_k_a, rwkv_r_k, rwkv_ln_w, rwkv_ln_b, rwkv_w_o):
    p = dict(mla_w_dq=mla_w_dq, mla_q_norm=mla_q_norm, mla_w_uq=mla_w_uq, mla_w_dkv=mla_w_dkv,
             mla_kv_norm=mla_kv_norm, mla_w_ukv=mla_w_ukv, mla_w_o=mla_w_o,
             rwkv_mu=rwkv_mu, rwkv_mu_dir=rwkv_mu_dir, rwkv_w_r=rwkv_w_r, rwkv_w_k=rwkv_w_k,
             rwkv_w_v=rwkv_w_v, rwkv_w0=rwkv_w0, rwkv_w1=rwkv_w1, rwkv_w2=rwkv_w2,
             rwkv_a0=rwkv_a0, rwkv_a1=rwkv_a1, rwkv_a2=rwkv_a2, rwkv_g1=rwkv_g1, rwkv_g2=rwkv_g2,
             rwkv_k_k=rwkv_k_k, rwkv_k_a=rwkv_k_a, rwkv_r_k=rwkv_r_k,
             rwkv_ln_w=rwkv_ln_w, rwkv_ln_b=rwkv_ln_b, rwkv_w_o=rwkv_w_o)
    b_ctx, s_ctx, _ = x_prompt.shape
    b_lat, s_lat, _ = x_sample.shape
    geo = _Geom(b_ctx, s_ctx, b_lat, s_lat)
    assert geo.n_ctx % s_lat == 0 and s_ctx % TM_RWKV == 0 and s_lat % TM_FFN == 0
    assert s_ctx == TB_WKV and s_lat % TB_WKV == 0

    x = jnp.concatenate([x_prompt.reshape(geo.n_ctx, D_MODEL), x_sample.reshape(geo.n_lat, D_MODEL)], axis=0)
    cond = jnp.concatenate([c_ctx[None], c, jnp.zeros((COND_ROWS - 1 - b_lat, D_MODEL), F32)], axis=0)
    mods_all = _ada(cond, w_ada, b_ada).reshape(DEPTH, COND_ROWS, N_MOD, D_MODEL)
    cos, sin = _rope_tables(geo)
    nf = norm_final[None]
    w_in = w_ffn_in.astype(BF16)
    w_out = w_ffn_out.astype(BF16)

    ckv_l, kr_l, sf_l, sb_l = [], [], [], []
    for l in range(DEPTH):
        j = l // N_MIXERS
        mods = mods_all[l]
        x = _ffn(geo, x, mods, norm_sub[l, 0][None], w_in, w_out, nf, layer=l, half=0, sub=0, final=False)
        if l % N_MIXERS == 0:
            x, ckv, kr = _mla_layer(geo, x, mods, norm_sub[l, 1][None], j, cos, sin, cache_ckv, cache_krope, p)
            ckv_l.append(ckv)
            kr_l.append(kr)
        else:
            x, sf, sb = _rwkv_layer(geo, x, mods, norm_sub[l, 1][None], j,
                                    state_wkv_fwd[:, j], state_wkv_bwd[:, j], p)
            sf_l.append(sf)
            sb_l.append(sb)
        x = _ffn(geo, x, mods, norm_sub[l, 2][None], w_in, w_out, nf, layer=l, half=1, sub=2,
                 final=(l == DEPTH - 1))

    y_prompt = x[:geo.n_ctx].reshape(b_ctx, s_ctx, D_MODEL)
    y_sample = x[geo.n_ctx:].reshape(b_lat, s_lat, D_MODEL)
    return (y_prompt, y_sample, jnp.stack(ckv_l, axis=1), jnp.stack(kr_l, axis=1),
            jnp.stack(sf_l, axis=1), jnp.stack(sb_l, axis=1))
```

```python
import functools

import jax
import jax.numpy as jnp
from jax import lax
from jax.experimental import pallas as pl
from jax.experimental.pallas import tpu as pltpu

D_MODEL = 2048
DEPTH = 4
N_MIXERS = 2
D_FF = 5632
N_MOD = 9
RMS_EPS = 1e-6
MLA_HEADS = 16
Q_LORA = 512
KV_LORA = 512
QK_NOPE = 128
QK_ROPE = 64
V_DIM = 128
ROPE_FREQS = QK_ROPE // 4
ROPE_THETA = 10000.0
GRID_W = 64
ATTN_SCALE = (QK_NOPE + QK_ROPE) ** -0.5
RWKV_HEAD = 64
RWKV_HEADS = D_MODEL // RWKV_HEAD
GN_EPS = 64e-5
LOG_DECAY_SCALE = 0.6065306597126334
LORA_PAD = 128

COND_ROWS = 8
VMEM_LIMIT = 56 * 1024 * 1024

TM_FFN = 512
TF_FFN = 512
CAST_TILE = 256
TM_PROJ = 512
TM_RWKV = 256
TN_ADA = 1024
Q_TILE = 256
ATTN_LOCKSTEP = 4
CHUNK = 64
PAIR = 2 * RWKV_HEAD
N_PAIRS = D_MODEL // PAIR
PAIR_GROUP = 16
TB_WKV = 256
TM_OUT = 256

BF16 = jnp.bfloat16
F32 = jnp.float32


def _params(*sem):
    return pltpu.CompilerParams(dimension_semantics=sem, vmem_limit_bytes=VMEM_LIMIT)


def _sigmoid(x):
    return 1.0 / (1.0 + jnp.exp(-x))


def _modulate(x, g, shift, scale):
    ms = jnp.mean(x * x, axis=-1, keepdims=True)
    return (x * lax.rsqrt(ms + RMS_EPS) * g) * (1.0 + scale) + shift


def _rms(x, w):
    ms = jnp.mean(x * x, axis=-1, keepdims=True)
    return x * lax.rsqrt(ms + RMS_EPS) * w


def _dot(a, b):
    return jnp.dot(a, b, preferred_element_type=F32)


def _dot_nt(a, b):
    return lax.dot_general(a, b, (((1,), (1,)), ((), ())), preferred_element_type=F32)


class _Geom:
    def __init__(self, b_ctx, s_ctx, b_lat, s_lat):
        self.b_ctx, self.s_ctx, self.b_lat, self.s_lat = b_ctx, s_ctx, b_lat, s_lat
        self.n_ctx = b_ctx * s_ctx
        self.n_lat = b_lat * s_lat
        self.n_tok = self.n_ctx + self.n_lat

    def cond_row(self, i, tm):
        start = i * tm
        return jnp.where(start < self.n_ctx, 0, 1 + (start - self.n_ctx) // self.s_lat)


def _ada_kernel(c_ref, w_ref, b_ref, o_ref):
    c = c_ref[...]
    s = (c * _sigmoid(c)).astype(BF16)
    o_ref[...] = _dot(s, w_ref[...].astype(BF16)) + b_ref[...]


def _ada(cond, w_ada, b_ada):
    n = N_MOD * D_MODEL
    return pl.pallas_call(
        _ada_kernel,
        grid=(DEPTH, n // TN_ADA),
        in_specs=[
            pl.BlockSpec((COND_ROWS, D_MODEL), lambda l, j: (0, 0)),
            pl.BlockSpec((None, D_MODEL, TN_ADA), lambda l, j: (l, 0, j)),
            pl.BlockSpec((None, 1, TN_ADA), lambda l, j: (l, 0, j)),
        ],
        out_specs=pl.BlockSpec((None, COND_ROWS, TN_ADA), lambda l, j: (l, 0, j)),
        out_shape=jax.ShapeDtypeStruct((DEPTH, COND_ROWS, n), F32),
        compiler_params=_params("parallel", "parallel"),
        name="ada",
    )(cond, w_ada, b_ada.reshape(DEPTH, 1, n))


def _ffn_kernel(*refs, sub, final, cast_next):
    if cast_next:
        (x_ref, m_ref, g_ref, wg_ref, wu_ref, wo_ref, nf_ref, ci_ref, co_ref,
         o_ref, cib_ref, cob_ref, h_ref, acc_ref) = refs
    else:
        x_ref, m_ref, g_ref, wg_ref, wu_ref, wo_ref, nf_ref, o_ref, h_ref, acc_ref = refs
    f = pl.program_id(1)

    @pl.when(f == 0)
    def _():
        h = _modulate(x_ref[...], g_ref[...], m_ref[3 * sub:3 * sub + 1, :], m_ref[3 * sub + 1:3 * sub + 2, :])
        h_ref[...] = h.astype(BF16)
        acc_ref[...] = jnp.zeros_like(acc_ref)

    h = h_ref[...]
    gate = _dot(h, wg_ref[...])
    up = _dot(h, wu_ref[...])
    act = (gate * _sigmoid(gate) * up).astype(BF16)
    acc_ref[...] += _dot(act, wo_ref[...])
    if cast_next:
        step = pl.program_id(0) * pl.num_programs(1) + f
        n_in, n_out = cast_next

        @pl.when(step < n_in)
        def _():
            cib_ref[...] = ci_ref[...].astype(BF16)

        @pl.when((step >= n_in) & (step < n_in + n_out))
        def _():
            cob_ref[...] = co_ref[...].astype(BF16)

    @pl.when(f == pl.num_programs(1) - 1)
    def _():
        y = x_ref[...] + 0.5 * m_ref[3 * sub + 2:3 * sub + 3, :] * acc_ref[...]
        if final:
            y = _rms(y, nf_ref[...])
        o_ref[...] = y


def _ffn(geo, x, mods, g, w_in, w_out, nf, *, sub, final, nxt=None):
    tm, tf = TM_FFN, TF_FFN
    nf_blocks = D_FF // tf
    n_in, n_out = 2 * D_FF // CAST_TILE, D_FF // CAST_TILE
    step = lambda i, f: i * nf_blocks + f
    in_blk = lambda i, f: jnp.minimum(step(i, f), n_in - 1)
    out_blk = lambda i, f: jnp.clip(step(i, f) - n_in, 0, n_out - 1)
    in_specs = [
        pl.BlockSpec((tm, D_MODEL), lambda i, f: (i, 0)),
        pl.BlockSpec((None, N_MOD, D_MODEL), lambda i, f: (geo.cond_row(i, tm), 0, 0)),
        pl.BlockSpec((1, D_MODEL), lambda i, f: (0, 0)),
        pl.BlockSpec((D_MODEL, tf), lambda i, f: (0, f)),
        pl.BlockSpec((D_MODEL, tf), lambda i, f: (0, f + nf_blocks)),
        pl.BlockSpec((tf, D_MODEL), lambda i, f: (f, 0)),
        pl.BlockSpec((1, D_MODEL), lambda i, f: (0, 0)),
    ]
    out_specs = [pl.BlockSpec((tm, D_MODEL), lambda i, f: (i, 0))]
    out_shape = [jax.ShapeDtypeStruct((geo.n_tok, D_MODEL), F32)]
    args = [x, mods, g, w_in, w_in, w_out, nf]
    if nxt is not None:
        w_in_all, w_out_all, layer, half = nxt
        assert (geo.n_tok // tm) * nf_blocks >= n_in + n_out
        in_specs += [
            pl.BlockSpec((None, None, D_MODEL, CAST_TILE), lambda i, f: (layer, half, 0, in_blk(i, f))),
            pl.BlockSpec((None, None, CAST_TILE, D_MODEL), lambda i, f: (layer, half, out_blk(i, f), 0)),
        ]
        out_specs += [
            pl.BlockSpec((D_MODEL, CAST_TILE), lambda i, f: (0, in_blk(i, f))),
            pl.BlockSpec((CAST_TILE, D_MODEL), lambda i, f: (out_blk(i, f), 0)),
        ]
        out_shape += [jax.ShapeDtypeStruct((D_MODEL, 2 * D_FF), BF16), jax.ShapeDtypeStruct((D_FF, D_MODEL), BF16)]
        args += [w_in_all, w_out_all]
    return pl.pallas_call(
        functools.partial(_ffn_kernel, sub=sub, final=final, cast_next=(n_in, n_out) if nxt is not None else None),
        grid=(geo.n_tok // tm, nf_blocks),
        in_specs=in_specs,
        out_specs=out_specs,
        out_shape=out_shape,
        scratch_shapes=[pltpu.VMEM((tm, D_MODEL), BF16), pltpu.VMEM((tm, D_MODEL), F32)],
        compiler_params=_params("arbitrary", "arbitrary"),
        name="ffn",
    )(*args)


def _mla_proj_kernel(x_ref, m_ref, g_ref, w_ref, qn_ref, kvn_ref, cos_ref, sin_ref,
                     cq_ref, ckv_ref, ckvb_ref, kr_ref):
    h = _modulate(x_ref[...], g_ref[...], m_ref[3:4, :], m_ref[4:5, :]).astype(BF16)
    z = _dot(h, w_ref[...])
    cq_ref[...] = _rms(z[:, :Q_LORA], qn_ref[...]).astype(BF16)
    ckv = _rms(z[:, Q_LORA:Q_LORA + KV_LORA], kvn_ref[...])
    ckv_ref[...] = ckv
    ckvb_ref[...] = ckv.astype(BF16)
    o = Q_LORA + KV_LORA
    kr_ref[...] = z[:, o:o + QK_ROPE] * cos_ref[...] + z[:, o + QK_ROPE:o + 2 * QK_ROPE] * sin_ref[...]


def _mla_proj(geo, x, mods, g, w_cat, qn, kvn, cos, sin):
    tm = TM_PROJ
    n_out = w_cat.shape[1]
    row = lambda i: (i, 0)
    fix = lambda i: (0, 0)
    return pl.pallas_call(
        _mla_proj_kernel,
        grid=(geo.n_tok // tm,),
        in_specs=[
            pl.BlockSpec((tm, D_MODEL), row),
            pl.BlockSpec((None, N_MOD, D_MODEL), lambda i: (geo.cond_row(i, tm), 0, 0)),
            pl.BlockSpec((1, D_MODEL), fix),
            pl.BlockSpec((D_MODEL, n_out), fix),
            pl.BlockSpec((1, Q_LORA), fix),
            pl.BlockSpec((1, KV_LORA), fix),
            pl.BlockSpec((tm, QK_ROPE), row),
            pl.BlockSpec((tm, QK_ROPE), row),
        ],
        out_specs=[
            pl.BlockSpec((tm, Q_LORA), row),
            pl.BlockSpec((tm, KV_LORA), row),
            pl.BlockSpec((tm, KV_LORA), row),
            pl.BlockSpec((tm, QK_ROPE), row),
        ],
        out_shape=[
            jax.ShapeDtypeStruct((geo.n_tok, Q_LORA), BF16),
            jax.ShapeDtypeStruct((geo.n_tok, KV_LORA), F32),
            jax.ShapeDtypeStruct((geo.n_tok, KV_LORA), BF16),
            jax.ShapeDtypeStruct((geo.n_tok, QK_ROPE), F32),
        ],
        compiler_params=_params("parallel"),
        name="mla_proj",
    )(x, mods, g, w_cat, qn, kvn, cos, sin)


def _attn_kernel(*refs, s_len, t_past, rope):
    if rope:
        (cq_ref, ckvb_ref, kr_ref, pckv_ref, pkr_ref, cos_ref, sin_ref,
         wq_ref, wqr_ref, wkv_ref, o_ref, kv_all, kr_all) = refs
    else:
        cq_ref, ckvb_ref, kr_ref, wq_ref, wkv_ref, o_ref, kv_all, kr_all = refs
    if t_past:
        kv_all[0:t_past, :] = pckv_ref[...].astype(BF16)
        kr_all[0:t_past, :] = pkr_ref[...].astype(BF16)
    kv_all[t_past:, :] = ckvb_ref[...]
    kr_all[t_past:, :] = kr_ref[...].astype(BF16)

    n_q = s_len // Q_TILE
    hg = max(1, ATTN_LOCKSTEP // n_q)
    row_slices = [slice(qb * Q_TILE, (qb + 1) * Q_TILE) for qb in range(n_q)]

    def head_group(g, carry):
        heads = [g * hg + i for i in range(hg)]
        kv = kv_all[...]
        krb = kr_all[...]
        kvp = [_dot(kv, wkv_ref[hd]) for hd in heads]
        kn = [x[:, :QK_NOPE].astype(BF16) for x in kvp]
        v = [x[:, QK_NOPE:].astype(BF16) for x in kvp]
        items = [(i, qb) for i in range(hg) for qb in range(n_q)]
        cq = [cq_ref[rows, :] for rows in row_slices]
        q = [_dot(cq[qb], wq_ref[heads[i]]) for i, qb in items]
        qr = [x[:, QK_NOPE:] for x in q]
        if rope:
            rot = [_dot(cq[qb], wqr_ref[heads[i]]) for i, qb in items]
            qr = [x * cos_ref[row_slices[qb], :] + y * sin_ref[row_slices[qb], :]
                  for x, y, (i, qb) in zip(qr, rot, items)]
        s = [(_dot_nt(x[:, :QK_NOPE].astype(BF16), kn[i]) + _dot_nt(y.astype(BF16), krb)) * ATTN_SCALE
             for x, y, (i, qb) in zip(q, qr, items)]
        p = [jnp.exp(x - jnp.max(x, axis=-1, keepdims=True)) for x in s]
        pr = [(x * (1.0 / jnp.sum(x, axis=-1, keepdims=True))).astype(BF16) for x in p]
        o = [_dot(x, v[i]).astype(BF16) for x, (i, qb) in zip(pr, items)]
        for x, (i, qb) in zip(o, items):
            o_ref[row_slices[qb], pl.ds(pl.multiple_of(heads[i] * V_DIM, V_DIM), V_DIM)] = x
        return carry

    lax.fori_loop(0, MLA_HEADS // hg, head_group, 0)


def _attn(geo, cq, ckvb, kr, wq, wqr, wkv, *, latent, past_ckv=None, past_kr=None, cos=None, sin=None):
    if latent:
        nb, s_len, off = geo.b_lat, geo.s_lat, geo.n_ctx // geo.s_lat
        t_past = past_ckv.shape[1]
    else:
        nb, s_len, off, t_past = geo.b_ctx, geo.s_ctx, 0, 0
    row = lambda b: (b + off, 0)
    fix3 = lambda b: (0, 0, 0)
    tok = lambda width: pl.BlockSpec((s_len, width), row)
    wspec = lambda w: pl.BlockSpec(w.shape, fix3)
    in_specs = [tok(Q_LORA), tok(KV_LORA), tok(QK_ROPE)]
    args = [cq, ckvb, kr]
    if latent:
        in_specs += [pl.BlockSpec((None, t_past, KV_LORA), lambda b: (b, 0, 0)),
                     pl.BlockSpec((None, t_past, QK_ROPE), lambda b: (b, 0, 0)),
                     tok(QK_ROPE), tok(QK_ROPE), wspec(wq), wspec(wqr)]
        args += [past_ckv, past_kr, cos, sin, wq, wqr]
    else:
        in_specs += [wspec(wq)]
        args += [wq]
    in_specs += [wspec(wkv)]
    args += [wkv]
    kern = functools.partial(_attn_kernel, s_len=s_len, t_past=t_past, rope=latent)
    return pl.pallas_call(
        kern,
        grid=(nb,),
        in_specs=in_specs,
        out_specs=pl.BlockSpec((s_len, D_MODEL), lambda b: (b, 0)),
        out_shape=jax.ShapeDtypeStruct((nb * s_len, D_MODEL), BF16),
        scratch_shapes=[pltpu.VMEM((t_past + s_len, KV_LORA), BF16),
                        pltpu.VMEM((t_past + s_len, QK_ROPE), BF16)],
        compiler_params=_params("parallel"),
        name="attn_lat" if latent else "attn_ctx",
    )(*args)


def _out_proj_kernel(a_ref, w_ref, x_ref, m_ref, o_ref):
    o_ref[...] = x_ref[...] + m_ref[5:6, :] * _dot(a_ref[...], w_ref[...])


def _out_proj(geo, a, w, x, mods):
    tm = TM_PROJ
    row = pl.BlockSpec((tm, D_MODEL), lambda i: (i, 0))
    return pl.pallas_call(
        _out_proj_kernel,
        grid=(geo.n_tok // tm,),
        in_specs=[row, pl.BlockSpec((D_MODEL, D_MODEL), lambda i: (0, 0)), row,
                  pl.BlockSpec((None, N_MOD, D_MODEL), lambda i: (geo.cond_row(i, tm), 0, 0))],
        out_specs=row,
        out_shape=jax.ShapeDtypeStruct((geo.n_tok, D_MODEL), F32),
        compiler_params=_params("parallel"),
        name="out_proj",
    )(a, w, x, mods)


def _shifted(geo, x_ref, xp_ref, xn_ref, m_ref, g_ref, tm):
    i = pl.program_id(0)
    g, shift, scale = g_ref[...], m_ref[3:4, :], m_ref[4:5, :]
    h = _modulate(x_ref[...], g, shift, scale)
    start = i * tm
    seq = jnp.where(start < geo.n_ctx, geo.s_ctx, geo.s_lat)
    rel = jnp.where(start < geo.n_ctx, start, start - geo.n_ctx)
    has_prev = (rel % seq) != 0
    has_next = ((rel + tm) % seq) != 0
    hp = _modulate(xp_ref[...], g, shift, scale)[7:8, :]
    hn = _modulate(xn_ref[...], g, shift, scale)[0:1, :]
    hp = jnp.where(has_prev, hp, 0.0)
    hn = jnp.where(has_next, hn, 0.0)
    r = lax.broadcasted_iota(jnp.int32, h.shape, 0)
    down = jnp.where(r == 0, hp, pltpu.roll(h, 1, 0))
    up = jnp.where(r == tm - 1, hn, pltpu.roll(h, tm - 1, 0))
    return h, 0.5 * (down + up) - h


def _halo_specs(geo, tm):
    nb8 = geo.n_tok // 8
    return [
        pl.BlockSpec((tm, D_MODEL), lambda i, *_: (i, 0)),
        pl.BlockSpec((8, D_MODEL), lambda i, *_: (jnp.maximum(i * (tm // 8) - 1, 0), 0)),
        pl.BlockSpec((8, D_MODEL), lambda i, *_: (jnp.minimum((i + 1) * (tm // 8), nb8 - 1), 0)),
        pl.BlockSpec((None, N_MOD, D_MODEL), lambda i, *_: (geo.cond_row(i, tm), 0, 0)),
        pl.BlockSpec((1, D_MODEL), lambda i, *_: (0, 0)),
    ]


def _rkv_kernel(x_ref, xp_ref, xn_ref, m_ref, g_ref, mu_ref, w_ref, o_ref, h_ref, xx_ref, *, geo, tm):
    @pl.when(pl.program_id(1) == 0)
    def _():
        h, xx = _shifted(geo, x_ref, xp_ref, xn_ref, m_ref, g_ref, tm)
        h_ref[...] = h
        xx_ref[...] = xx

    xm = (h_ref[...] + xx_ref[...] * mu_ref[...]).astype(BF16)
    o_ref[...] = _dot(xm, w_ref[pl.program_id(1)])


def _rkv(geo, x, mods, g, mu3, w3):
    tm = TM_RWKV
    return pl.pallas_call(
        functools.partial(_rkv_kernel, geo=geo, tm=tm),
        grid=(geo.n_tok // tm, 3),
        in_specs=_halo_specs(geo, tm) + [
            pl.BlockSpec((None, 1, D_MODEL), lambda i, p: (p, 0, 0)),
            pl.BlockSpec((3, D_MODEL, D_MODEL), lambda i, p: (0, 0, 0), pipeline_mode=pl.Buffered(1)),
        ],
        out_specs=pl.BlockSpec((None, tm, D_MODEL), lambda i, p: (p, i, 0)),
        out_shape=jax.ShapeDtypeStruct((3, geo.n_tok, D_MODEL), F32),
        scratch_shapes=[pltpu.VMEM((tm, D_MODEL), F32), pltpu.VMEM((tm, D_MODEL), F32)],
        compiler_params=_params("parallel", "arbitrary"),
        name="rwkv_rkv",
    )(x, x, x, mods, g, mu3, w3)


def _lora_kernel(x_ref, xp_ref, xn_ref, m_ref, g_ref, mug_ref, mud_ref, g1_ref, g2_ref,
                 wa1_ref, wa2_ref, w0_ref, a0_ref, gate_ref, dec_ref, asig_ref, *, geo, tm):
    h, xx = _shifted(geo, x_ref, xp_ref, xn_ref, m_ref, g_ref, tm)
    mix = lambda mu: (h + xx * mu).astype(BF16)
    zg = _dot(mix(mug_ref[...]), g1_ref[...])
    gate_ref[...] = _dot(_sigmoid(zg).astype(BF16), g2_ref[...])
    for d in range(2):
        zw = _dot(mix(mud_ref[2 * d]), wa1_ref[d])
        wl = w0_ref[d] + _dot(jnp.tanh(zw).astype(BF16), wa2_ref[d])
        dec_ref[d] = -LOG_DECAY_SCALE * _sigmoid(wl)
        za = _dot(mix(mud_ref[2 * d + 1]), wa1_ref[2 + d])
        asig_ref[d] = _sigmoid(a0_ref[d] + _dot(za.astype(BF16), wa2_ref[2 + d]))


def _lora(geo, x, mods, g, mu_g, mu_dir, g1, g2, wa1, wa2, w0, a0):
    tm = TM_RWKV
    full = lambda a: pl.BlockSpec(a.shape, lambda i: (0,) * a.ndim)
    tok2 = pl.BlockSpec((2, tm, D_MODEL), lambda i: (0, i, 0))
    return pl.pallas_call(
        functools.partial(_lora_kernel, geo=geo, tm=tm),
        grid=(geo.n_tok // tm,),
        in_specs=_halo_specs(geo, tm) + [full(a) for a in (mu_g, mu_dir, g1, g2, wa1, wa2, w0, a0)],
        out_specs=[pl.BlockSpec((tm, D_MODEL), lambda i: (i, 0)), tok2, tok2],
        out_shape=[jax.ShapeDtypeStruct((geo.n_tok, D_MODEL), F32),
                   jax.ShapeDtypeStruct((2, geo.n_tok, D_MODEL), F32),
                   jax.ShapeDtypeStruct((2, geo.n_tok, D_MODEL), F32)],
        compiler_params=_params("parallel"),
        name="rwkv_lora",
    )(x, x, x, mods, g, mu_g, mu_dir, g1, g2, wa1, wa2, w0, a0)


def _split2(x):
    hi = x.astype(BF16)
    return hi, (x - hi.astype(F32)).astype(BF16)


def _head_sums(x, ones_blk):
    hi, lo = _split2(x)
    return _dot(hi, ones_blk) + _dot(lo, ones_blk)


def _pair_ones():
    r = lax.broadcasted_iota(jnp.int32, (PAIR, PAIR), 0)
    c = lax.broadcasted_iota(jnp.int32, (PAIR, PAIR), 1)
    same = (r < RWKV_HEAD) == (c < RWKV_HEAD)
    return same, jnp.where(same, 1.0, 0.0).astype(BF16)


def _wkv_kernel(r_ref, k_ref, v_ref, dec_ref, asig_ref, kk_ref, ka_ref, rk_ref, s0_ref,
                y_ref, bon_ref, sf_ref, s_ref, *, geo, tb):
    d = pl.program_id(0)
    i = pl.program_id(1)
    nblk = pl.num_programs(1)
    blk = i + d * (nblk - 1 - 2 * i)
    start = blk * tb
    seq = jnp.where(start < geo.n_ctx, geo.s_ctx, geo.s_lat)
    rel = jnp.where(start < geo.n_ctx, start, start - geo.n_ctx)
    at_lo = (rel % seq) == 0
    at_hi = ((rel + tb) % seq) == 0
    first = jnp.where(d == 0, at_lo, at_hi)
    last = jnp.where(d == 0, at_hi, at_lo)

    @pl.when(first)
    def _():
        s_ref[...] = s0_ref[...]

    nch = tb // CHUNK
    sgn = 1 - 2 * d
    fwd = (d == 0).astype(F32)
    row = lax.broadcasted_iota(jnp.int32, (CHUNK, PAIR), 0)
    lane = lax.broadcasted_iota(jnp.int32, (CHUNK, PAIR), 1)
    sidx = lane & (RWKV_HEAD - 1)
    delta = (row - sidx) * sgn
    strict = delta > 0
    incl = delta >= 0
    eye = jnp.where(row == sidx, 1.0, 0.0)
    off_masks = []
    m = 1
    while m < CHUNK:
        off_masks.append(strict & ((row // (2 * m)) == (sidx // (2 * m))) & ((row // m) != (sidx // m)))
        m *= 2
    head0 = lane < RWKV_HEAD
    tr = lax.broadcasted_iota(jnp.int32, (CHUNK, 3 * CHUNK), 0)
    ts = lax.broadcasted_iota(jnp.int32, (CHUNK, 3 * CHUNK), 1) & (CHUNK - 1)
    tri3 = jnp.where((tr - ts) * sgn >= 0, 1.0, 0.0).astype(BF16)
    same_head, ones_blk = _pair_ones()

    def expand(x):
        return jnp.concatenate([jnp.where(head0, x, 0.0), jnp.where(head0, 0.0, x)], axis=0)

    def chunk_body(ci, carry):
        cc = ci * sgn + d * (nch - 1)
        rows = pl.ds(pl.multiple_of(cc * CHUNK, CHUNK), CHUNK)

        def group_body(pg, carry2):
            ps = [pg * PAIR_GROUP + q for q in range(PAIR_GROUP)]
            cols = [pl.ds(pl.multiple_of(p * PAIR, PAIR), PAIR) for p in ps]
            each = lambda f, *ls: [f(*xs) for xs in zip(*ls)]
            r = [r_ref[rows, c] for c in cols]
            k = [k_ref[rows, c] for c in cols]
            v = [v_ref[rows, c] for c in cols]
            logw = [dec_ref[rows, c] for c in cols]
            a = [asig_ref[rows, c] for c in cols]
            kk = [x * kk_ref[:, c] for x, c in zip(k, cols)]
            kd = [x * (1.0 + (y - 1.0) * ka_ref[:, c]) for x, y, c in zip(k, a, cols)]
            sums = [_head_sums(jnp.concatenate([x * x, y * z * rk_ref[:, c]], axis=0), ones_blk)
                    for x, y, z, c in zip(kk, r, kd, cols)]
            kk = each(lambda x, s: x / jnp.maximum(jnp.sqrt(s[:CHUNK]), 1e-12), kk, sums)
            b_in = each(lambda x, y: x * y, kk, a)
            for c, s, y in zip(cols, sums, v):
                bon_ref[rows, c] = s[CHUNK:] * y

            l1 =[x.astype(BF16) for x in logw]
            e1 = each(lambda x, y: x - y.astype(F32), logw, l1)
            l2 = [x.astype(BF16) for x in e1]
            l3 = each(lambda x, y: (x - y.astype(F32)).astype(BF16), e1, l2)
            cum = each(lambda x, y, z: _dot(tri3, jnp.concatenate([x, y, z], axis=0)), l1, l2, l3)
            tot = [fwd * x[CHUNK - 1:CHUNK, :] + (1.0 - fwd) * x[0:1, :] for x in cum]
            c_inv = [jnp.exp(-x) for x in cum]
            at = each(lambda x, y, z: -x * jnp.exp(y - z), kk, cum, logw)
            rt = each(lambda x, y: x * jnp.exp(y), r, cum)
            c_end = each(lambda x, y: jnp.exp(x - y), tot, cum)

            lhs = each(lambda x, y: jnp.concatenate([x, y], axis=0).astype(BF16), at, rt)
            rhs = each(lambda x, y, z: jnp.concatenate([expand(x * z), expand(y * z)], axis=0).astype(BF16),
                       b_in, kd, c_inv)
            g = each(_dot_nt, lhs, rhs)
            ab = [jnp.where(strict, x[:CHUNK, :PAIR], 0.0) for x in g]
            ak = [jnp.where(strict, x[:CHUNK, PAIR:], 0.0) for x in g]
            rbk = [jnp.concatenate([jnp.where(incl, x[CHUNK:, :PAIR], 0.0),
                                    jnp.where(incl, x[CHUNK:, PAIR:], 0.0)], axis=1).astype(BF16) for x in g]

            t_inv = [eye + jnp.where(off_masks[0], x, 0.0) for x in ab]
            for off in off_masks[1:]:
                lx = each(lambda x, t: _dot(jnp.where(off, x, 0.0).astype(BF16), expand(t).astype(BF16)), ab, t_inv)
                t_inv = each(lambda t, x: t + _dot(t.astype(BF16), expand(x).astype(BF16)), t_inv, lx)

            s2 = [s_ref[p] for p in ps]
            ars = each(lambda x, s: _dot_nt(x, s.astype(BF16)), lhs, s2)
            vexp = [expand(x).astype(BF16) for x in v]
            rhs_u = each(lambda x, y, z: x[:CHUNK] + _dot(y.astype(BF16), z), ars, ak, vexp)
            u = each(lambda t, x: _dot(t.astype(BF16), expand(x).astype(BF16)), t_inv, rhs_u)
            uv = each(lambda x, y: jnp.concatenate([expand(x).astype(BF16), y], axis=0), u, vexp)
            y_out = each(lambda x, y, z: x[CHUNK:] + _dot(y, z), ars, rbk, uv)
            for c, x in zip(cols, y_out):
                y_ref[rows, c] = x
            uv_t = each(lambda x, y: jnp.concatenate([x, y], axis=0).T.astype(BF16), u, v)
            bk = each(lambda x, y, z: jnp.concatenate([x * z, y * z], axis=0).astype(BF16), b_in, kd, c_end)
            upd = each(_dot, uv_t, bk)
            for p, s, x, t in zip(ps, s2, upd, tot):
                s_ref[p] = s * jnp.exp(t) + jnp.where(same_head, x, 0.0)
            return carry2

        lax.fori_loop(0, N_PAIRS // PAIR_GROUP, group_body, 0)
        return carry

    lax.fori_loop(0, nch, chunk_body, 0)

    @pl.when(last)
    def _():
        sf_ref[...] = s_ref[...]


def _wkv(geo, rkv, dec, asig, kk_c, ka_c, rk_c, s0):
    tb = TB_WKV
    nblk = geo.n_tok // tb
    blk = lambda d, i: i + d * (nblk - 1 - 2 * i)
    tok = lambda which: pl.BlockSpec((None, tb, D_MODEL), lambda d, i: (which, blk(d, i), 0))
    perdir = pl.BlockSpec((None, tb, D_MODEL), lambda d, i: (d, blk(d, i), 0))
    const = pl.BlockSpec((1, D_MODEL), lambda d, i: (0, 0))
    state = (None, None, N_PAIRS, PAIR, PAIR)
    return pl.pallas_call(
        functools.partial(_wkv_kernel, geo=geo, tb=tb),
        grid=(2, nblk),
        in_specs=[tok(0), tok(1), tok(2), perdir, perdir, const, const, const,
                  pl.BlockSpec(state, lambda d, i: (d, geo.cond_row(blk(d, i), tb), 0, 0, 0))],
        out_specs=[perdir, perdir,
                   pl.BlockSpec(state, lambda d, i: (d, jnp.minimum(blk(d, i) * tb // geo.s_ctx, geo.b_ctx), 0, 0, 0))],
        out_shape=[jax.ShapeDtypeStruct((2, geo.n_tok, D_MODEL), F32),
                   jax.ShapeDtypeStruct((2, geo.n_tok, D_MODEL), F32),
                   jax.ShapeDtypeStruct((2, geo.b_ctx + 1, N_PAIRS, PAIR, PAIR), F32)],
        scratch_shapes=[pltpu.VMEM((N_PAIRS, PAIR, PAIR), F32)],
        compiler_params=_params("arbitrary", "arbitrary"),
        name="wkv_chunked",
    )(rkv, rkv, rkv, dec, asig, kk_c, ka_c, rk_c, s0)


def _rwkv_out_kernel(y_ref, bon_ref, gate_ref, lw_ref, lb_ref, w_ref, x_ref, m_ref, o_ref, z_ref):
    _, ones_blk = _pair_ones()
    inv_n = 1.0 / RWKV_HEAD
    for s in range(N_PAIRS):
        cols = slice(s * PAIR, (s + 1) * PAIR)
        y = y_ref[0, :, cols] + y_ref[1, :, cols]
        yc = y - _head_sums(y, ones_blk) * inv_n
        var = _head_sums(yc * yc, ones_blk) * inv_n
        z = yc * lax.rsqrt(var + GN_EPS) * lw_ref[:, cols] + lb_ref[:, cols] + bon_ref[0, :, cols] + bon_ref[1, :, cols]
        z_ref[:, cols] = (z * gate_ref[:, cols]).astype(BF16)
    o_ref[...] = x_ref[...] + m_ref[5:6, :] * _dot(z_ref[...], w_ref[...])


def _rwkv_out(geo, y, bon, gate, lw, lb, w, x, mods):
    tm = TM_OUT
    row = pl.BlockSpec((tm, D_MODEL), lambda i: (i, 0))
    both = pl.BlockSpec((2, tm, D_MODEL), lambda i: (0, i, 0))
    const = pl.BlockSpec((1, D_MODEL), lambda i: (0, 0))
    return pl.pallas_call(
        _rwkv_out_kernel,
        grid=(geo.n_tok // tm,),
        in_specs=[both, both, row, const, const, pl.BlockSpec((D_MODEL, D_MODEL), lambda i: (0, 0)), row,
                  pl.BlockSpec((None, N_MOD, D_MODEL), lambda i: (geo.cond_row(i, tm), 0, 0))],
        out_specs=row,
        out_shape=jax.ShapeDtypeStruct((geo.n_tok, D_MODEL), F32),
        scratch_shapes=[pltpu.VMEM((tm, D_MODEL), BF16)],
        compiler_params=_params("parallel"),
        name="rwkv_out",
    )(y, bon, gate, lw, lb, w, x, mods)


def _state_to_pairs(s):
    b = s.shape[0]
    s5 = s.reshape(b, N_PAIRS, 2, RWKV_HEAD, RWKV_HEAD)
    z = jnp.zeros_like(s5[:, :, 0])
    top = jnp.concatenate([s5[:, :, 0], z], axis=-1)
    bot = jnp.concatenate([z, s5[:, :, 1]], axis=-1)
    return jnp.concatenate([top, bot], axis=-2)


def _state_from_pairs(sp):
    b = sp.shape[0]
    h0 = sp[:, :, :RWKV_HEAD, :RWKV_HEAD]
    h1 = sp[:, :, RWKV_HEAD:, RWKV_HEAD:]
    return jnp.stack([h0, h1], axis=2).reshape(b, RWKV_HEADS, RWKV_HEAD, RWKV_HEAD)


def _rope_tables(geo):
    rows = geo.s_lat // GRID_W
    row = jnp.repeat(jnp.arange(rows, dtype=F32), GRID_W)
    col = jnp.tile(jnp.arange(GRID_W, dtype=F32), rows)
    inv = ROPE_THETA ** (-jnp.arange(ROPE_FREQS, dtype=F32) / ROPE_FREQS)
    ang = jnp.stack([row[:, None] * inv, col[:, None] * inv], axis=1)
    ang = jnp.broadcast_to(ang[:, :, None, :], (geo.s_lat, 2, 2, ROPE_FREQS)).reshape(geo.s_lat, QK_ROPE)
    cos = jnp.concatenate([jnp.ones((geo.n_ctx, QK_ROPE), F32), jnp.tile(jnp.cos(ang), (geo.b_lat, 1))], axis=0)
    sin = jnp.concatenate([jnp.zeros((geo.n_ctx, QK_ROPE), F32), jnp.tile(jnp.sin(ang), (geo.b_lat, 1))], axis=0)
    return cos, sin


def _rot_cols(w):
    w4 = w.reshape(w.shape[:-1] + (2, 2, ROPE_FREQS))
    return jnp.stack([-w4[..., 1, :], w4[..., 0, :]], axis=-2).reshape(w.shape)


def _mla_layer(geo, x, mods, g, j, cos, sin, cache_ckv, cache_krope, p):
    w_dkv = p['mla_w_dkv'][j]
    w_cat = jnp.concatenate([p['mla_w_dq'][j], w_dkv, _rot_cols(w_dkv[:, KV_LORA:])], axis=1).astype(BF16)
    cq, ckv, ckvb, kr = _mla_proj(geo, x, mods, g, w_cat, p['mla_q_norm'][j][None], p['mla_kv_norm'][j][None],
                                  cos, sin)
    w_uq = p['mla_w_uq'][j].reshape(Q_LORA, MLA_HEADS, QK_NOPE + QK_ROPE).transpose(1, 0, 2)
    wq = w_uq.astype(BF16)
    wqr = _rot_cols(w_uq[..., QK_NOPE:]).astype(BF16)
    wkv = p['mla_w_ukv'][j].reshape(KV_LORA, MLA_HEADS, QK_NOPE + V_DIM).transpose(1, 0, 2).astype(BF16)
    o_ctx = _attn(geo, cq, ckvb, kr, wq, wqr, wkv, latent=False)
    o_lat = _attn(geo, cq, ckvb, kr, wq, wqr, wkv, latent=True,
                  past_ckv=cache_ckv[:, j], past_kr=cache_krope[:, j], cos=cos, sin=sin)
    o = jnp.concatenate([o_ctx, o_lat], axis=0)
    x = _out_proj(geo, o, p['mla_w_o'][j].astype(BF16), x, mods)
    new_ckv = ckv[:geo.n_ctx].reshape(geo.b_ctx, geo.s_ctx, KV_LORA)
    new_kr = kr[:geo.n_ctx].reshape(geo.b_ctx, geo.s_ctx, QK_ROPE)
    return x, new_ckv, new_kr


def _rwkv_layer(geo, x, mods, g, j, s0_fwd, s0_bwd, p):
    mu = p['rwkv_mu'][j]
    mu_dir = p['rwkv_mu_dir'][j].reshape(4, 1, D_MODEL)
    w3 = jnp.stack([p['rwkv_w_r'][j], p['rwkv_w_k'][j], p['rwkv_w_v'][j]]).astype(BF16)
    rkv = _rkv(geo, x, mods, g, mu[:3, None, :], w3)
    pad1 = lambda w: jnp.pad(w, ((0, 0), (0, 0), (0, LORA_PAD - w.shape[-1])))
    pad2 = lambda w: jnp.pad(w, ((0, 0), (0, LORA_PAD - w.shape[-2]), (0, 0)))
    wa1 = jnp.concatenate([pad1(p['rwkv_w1'][j]), pad1(p['rwkv_a1'][j])]).astype(BF16)
    wa2 = jnp.concatenate([pad2(p['rwkv_w2'][j]), pad2(p['rwkv_a2'][j])]).astype(BF16)
    gate, dec, asig = _lora(geo, x, mods, g, mu[3:4], mu_dir, p['rwkv_g1'][j].astype(BF16),
                            p['rwkv_g2'][j].astype(BF16), wa1, wa2,
                            p['rwkv_w0'][j][:, None, :], p['rwkv_a0'][j][:, None, :])
    zero = jnp.zeros((2, 1, N_PAIRS, PAIR, PAIR), F32)
    s0 = jnp.concatenate([zero, jnp.stack([_state_to_pairs(s0_fwd), _state_to_pairs(s0_bwd)])], axis=1)
    y, bon, sf = _wkv(geo, rkv, dec, asig, p['rwkv_k_k'][j][None], p['rwkv_k_a'][j][None],
                      p['rwkv_r_k'][j].reshape(1, D_MODEL), s0)
    x = _rwkv_out(geo, y, bon, gate, p['rwkv_ln_w'][j][None], p['rwkv_ln_b'][j][None],
                  p['rwkv_w_o'][j].astype(BF16), x, mods)
    return x, _state_from_pairs(sf[0, :geo.b_ctx]), _state_from_pairs(sf[1, :geo.b_ctx])


def kernel(x_prompt, x_sample, cache_ckv, cache_krope, state_wkv_fwd, state_wkv_bwd, c, c_ctx, w_ada, b_ada, norm_sub, norm_final, w_ffn_in, w_ffn_out, mla_w_dq, mla_q_norm, mla_w_uq, mla_w_dkv, mla_kv_norm, mla_w_ukv, mla_w_o, rwkv_mu, rwkv_mu_dir, rwkv_w_r, rwkv_w_k, rwkv_w_v, rwkv_w0, rwkv_w1, rwkv_w2, rwkv_a0, rwkv_a1, rwkv_a2, rwkv_g1, rwkv_g2, rwkv_k_k, rwkv_k_a, rwkv_r_k, rwkv_ln_w, rwkv_ln_b, rwkv_w_o):
    p = dict(mla_w_dq=mla_w_dq, mla_q_norm=mla_q_norm, mla_w_uq=mla_w_uq, mla_w_dkv=mla_w_dkv,
             mla_kv_norm=mla_kv_norm, mla_w_ukv=mla_w_ukv, mla_w_o=mla_w_o,
             rwkv_mu=rwkv_mu, rwkv_mu_dir=rwkv_mu_dir, rwkv_w_r=rwkv_w_r, rwkv_w_k=rwkv_w_k,
             rwkv_w_v=rwkv_w_v, rwkv_w0=rwkv_w0, rwkv_w1=rwkv_w1, rwkv_w2=rwkv_w2,
             rwkv_a0=rwkv_a0, rwkv_a1=rwkv_a1, rwkv_a2=rwkv_a2, rwkv_g1=rwkv_g1, rwkv_g2=rwkv_g2,
             rwkv_k_k=rwkv_k_k, rwkv_k_a=rwkv_k_a, rwkv_r_k=rwkv_r_k,
             rwkv_ln_w=rwkv_ln_w, rwkv_ln_b=rwkv_ln_b, rwkv_w_o=rwkv_w_o)
    b_ctx, s_ctx, _ = x_prompt.shape
    b_lat, s_lat, _ = x_sample.shape
    geo = _Geom(b_ctx, s_ctx, b_lat, s_lat)
    assert geo.n_ctx % s_lat == 0 and s_ctx % TM_RWKV == 0 and s_lat % TM_FFN == 0
    assert s_ctx == TB_WKV and s_lat % TB_WKV == 0

    x = jnp.concatenate([x_prompt.reshape(geo.n_ctx, D_MODEL), x_sample.reshape(geo.n_lat, D_MODEL)], axis=0)
    cond = jnp.concatenate([c_ctx[None], c, jnp.zeros((COND_ROWS - 1 - b_lat, D_MODEL), F32)], axis=0)
    mods_all = _ada(cond, w_ada, b_ada).reshape(DEPTH, COND_ROWS, N_MOD, D_MODEL)
    cos, sin = _rope_tables(geo)
    nf = norm_final[None]
    w_in = w_ffn_in[0, 0].astype(BF16)
    w_out = w_ffn_out[0, 0].astype(BF16)

    ckv_l, kr_l, sf_l, sb_l = [], [], [], []
    for l in range(DEPTH):
        j = l // N_MIXERS
        mods = mods_all[l]
        x, w_in, w_out = _ffn(geo, x, mods, norm_sub[l, 0][None], w_in, w_out, nf, sub=0, final=False,
                              nxt=(w_ffn_in, w_ffn_out, l, 1))
        if l % N_MIXERS == 0:
            x, ckv, kr = _mla_layer(geo, x, mods, norm_sub[l, 1][None], j, cos, sin, cache_ckv, cache_krope, p)
            ckv_l.append(ckv)
            kr_l.append(kr)
        else:
            x, sf, sb = _rwkv_layer(geo, x, mods, norm_sub[l, 1][None], j,
                                    state_wkv_fwd[:, j], state_wkv_bwd[:, j], p)
            sf_l.append(sf)
            sb_l.append(sb)
        if l < DEPTH - 1:
            x, w_in, w_out = _ffn(geo, x, mods, norm_sub[l, 2][None], w_in, w_out, nf, sub=2, final=False,
                                  nxt=(w_ffn_in, w_ffn_out, l + 1, 0))
        else:
            x, = _ffn(geo, x, mods, norm_sub[l, 2][None], w_in, w_out, nf, sub=2, final=True)

    y_prompt = x[:geo.n_ctx].reshape(b_ctx, s_ctx, D_MODEL)
    y_sample = x[geo.n_ctx:].reshape(b_lat, s_lat, D_MODEL)
    return (y_prompt, y_sample, jnp.stack(ckv_l, axis=1), jnp.stack(kr_l, axis=1),
            jnp.stack(sf_l, axis=1), jnp.stack(sb_l, axis=1))
```

```python
import functools

import jax
import jax.numpy as jnp
from jax import lax
from jax.experimental import pallas as pl
from jax.experimental.pallas import tpu as pltpu

D_MODEL = 2048
DEPTH = 4
N_MIXERS = 2
D_FF = 5632
N_MOD = 9
RMS_EPS = 1e-6
MLA_HEADS = 16
Q_LORA = 512
KV_LORA = 512
QK_NOPE = 128
QK_ROPE = 64
V_DIM = 128
ROPE_FREQS = QK_ROPE // 4
ROPE_THETA = 10000.0
GRID_W = 64
ATTN_SCALE = (QK_NOPE + QK_ROPE) ** -0.5
RWKV_HEAD = 64
RWKV_HEADS = D_MODEL // RWKV_HEAD
GN_EPS = 64e-5
LOG_DECAY_SCALE = 0.6065306597126334
LORA_PAD = 128

COND_ROWS = 8
VMEM_LIMIT = 56 * 1024 * 1024

TM_FFN = 512
TF_FFN = 512
CAST_TILE = 256
TM_PROJ = 512
TM_RWKV = 256
TN_ADA = 1024
Q_TILE = 256
ATTN_LOCKSTEP = 4
CHUNK = 64
PAIR = 2 * RWKV_HEAD
N_PAIRS = D_MODEL // PAIR
PAIR_GROUP = 16
TB_WKV = 256
TM_OUT = 256

BF16 = jnp.bfloat16
F32 = jnp.float32


def _params(*sem):
    return pltpu.CompilerParams(dimension_semantics=sem, vmem_limit_bytes=VMEM_LIMIT)


def _sigmoid(x):
    return 1.0 / (1.0 + jnp.exp(-x))


def _modulate(x, g, shift, scale):
    ms = jnp.mean(x * x, axis=-1, keepdims=True)
    return (x * lax.rsqrt(ms + RMS_EPS) * g) * (1.0 + scale) + shift


def _rms(x, w):
    ms = jnp.mean(x * x, axis=-1, keepdims=True)
    return x * lax.rsqrt(ms + RMS_EPS) * w


def _dot(a, b):
    return jnp.dot(a, b, preferred_element_type=F32)


def _dot_nt(a, b):
    return lax.dot_general(a, b, (((1,), (1,)), ((), ())), preferred_element_type=F32)


class _Geom:
    def __init__(self, b_ctx, s_ctx, b_lat, s_lat):
        self.b_ctx, self.s_ctx, self.b_lat, self.s_lat = b_ctx, s_ctx, b_lat, s_lat
        self.n_ctx = b_ctx * s_ctx
        self.n_lat = b_lat * s_lat
        self.n_tok = self.n_ctx + self.n_lat

    def cond_row(self, i, tm):
        start = i * tm
        return jnp.where(start < self.n_ctx, 0, 1 + (start - self.n_ctx) // self.s_lat)


def _ada_kernel(c_ref, w_ref, b_ref, o_ref):
    c = c_ref[...]
    s = (c * _sigmoid(c)).astype(BF16)
    o_ref[...] = _dot(s, w_ref[...].astype(BF16)) + b_ref[...]


def _ada(cond, w_ada, b_ada):
    n = N_MOD * D_MODEL
    return pl.pallas_call(
        _ada_kernel,
        grid=(DEPTH, n // TN_ADA),
        in_specs=[
            pl.BlockSpec((COND_ROWS, D_MODEL), lambda l, j: (0, 0)),
            pl.BlockSpec((None, D_MODEL, TN_ADA), lambda l, j: (l, 0, j)),
            pl.BlockSpec((None, 1, TN_ADA), lambda l, j: (l, 0, j)),
        ],
        out_specs=pl.BlockSpec((None, COND_ROWS, TN_ADA), lambda l, j: (l, 0, j)),
        out_shape=jax.ShapeDtypeStruct((DEPTH, COND_ROWS, n), F32),
        compiler_params=_params("parallel", "parallel"),
        name="ada",
    )(cond, w_ada, b_ada.reshape(DEPTH, 1, n))


def _ffn_kernel(*refs, sub, final, cast_next):
    if cast_next:
        (x_ref, m_ref, g_ref, wg_ref, wu_ref, wo_ref, nf_ref, ci_ref, co_ref,
         o_ref, cib_ref, cob_ref, h_ref, acc_ref) = refs
    else:
        x_ref, m_ref, g_ref, wg_ref, wu_ref, wo_ref, nf_ref, o_ref, h_ref, acc_ref = refs
    f = pl.program_id(1)

    @pl.when(f == 0)
    def _():
        h = _modulate(x_ref[...], g_ref[...], m_ref[3 * sub:3 * sub + 1, :], m_ref[3 * sub + 1:3 * sub + 2, :])
        h_ref[...] = h.astype(BF16)
        acc_ref[...] = jnp.zeros_like(acc_ref)

    h = h_ref[...]
    gate = _dot(h, wg_ref[...])
    up = _dot(h, wu_ref[...])
    act = (gate * _sigmoid(gate) * up).astype(BF16)
    acc_ref[...] += _dot(act, wo_ref[...])
    if cast_next:
        step = pl.program_id(0) * pl.num_programs(1) + f
        n_in, n_out = cast_next

        @pl.when(step < n_in)
        def _():
            cib_ref[...] = ci_ref[...].astype(BF16)

        @pl.when((step >= n_in) & (step < n_in + n_out))
        def _():
            cob_ref[...] = co_ref[...].astype(BF16)

    @pl.when(f == pl.num_programs(1) - 1)
    def _():
        y = x_ref[...] + 0.5 * m_ref[3 * sub + 2:3 * sub + 3, :] * acc_ref[...]
        if final:
            y = _rms(y, nf_ref[...])
        o_ref[...] = y


def _ffn(geo, x, mods, g, w_in, w_out, nf, *, sub, final, nxt=None):
    tm, tf = TM_FFN, TF_FFN
    nf_blocks = D_FF // tf
    n_in, n_out = 2 * D_FF // CAST_TILE, D_FF // CAST_TILE
    step = lambda i, f: i * nf_blocks + f
    in_blk = lambda i, f: jnp.minimum(step(i, f), n_in - 1)
    out_blk = lambda i, f: jnp.clip(step(i, f) - n_in, 0, n_out - 1)
    in_specs = [
        pl.BlockSpec((tm, D_MODEL), lambda i, f: (i, 0)),
        pl.BlockSpec((None, N_MOD, D_MODEL), lambda i, f: (geo.cond_row(i, tm), 0, 0)),
        pl.BlockSpec((1, D_MODEL), lambda i, f: (0, 0)),
        pl.BlockSpec((D_MODEL, tf), lambda i, f: (0, f)),
        pl.BlockSpec((D_MODEL, tf), lambda i, f: (0, f + nf_blocks)),
        pl.BlockSpec((tf, D_MODEL), lambda i, f: (f, 0)),
        pl.BlockSpec((1, D_MODEL), lambda i, f: (0, 0)),
    ]
    out_specs = [pl.BlockSpec((tm, D_MODEL), lambda i, f: (i, 0))]
    out_shape = [jax.ShapeDtypeStruct((geo.n_tok, D_MODEL), F32)]
    args = [x, mods, g, w_in, w_in, w_out, nf]
    if nxt is not None:
        w_in_all, w_out_all, layer, half = nxt
        assert (geo.n_tok // tm) * nf_blocks >= n_in + n_out
        in_specs += [
            pl.BlockSpec((None, None, D_MODEL, CAST_TILE), lambda i, f: (layer, half, 0, in_blk(i, f))),
            pl.BlockSpec((None, None, CAST_TILE, D_MODEL), lambda i, f: (layer, half, out_blk(i, f), 0)),
        ]
        out_specs += [
            pl.BlockSpec((D_MODEL, CAST_TILE), lambda i, f: (0, in_blk(i, f))),
            pl.BlockSpec((CAST_TILE, D_MODEL), lambda i, f: (out_blk(i, f), 0)),
        ]
        out_shape += [jax.ShapeDtypeStruct((D_MODEL, 2 * D_FF), BF16), jax.ShapeDtypeStruct((D_FF, D_MODEL), BF16)]
        args += [w_in_all, w_out_all]
    return pl.pallas_call(
        functools.partial(_ffn_kernel, sub=sub, final=final, cast_next=(n_in, n_out) if nxt is not None else None),
        grid=(geo.n_tok // tm, nf_blocks),
        in_specs=in_specs,
        out_specs=out_specs,
        out_shape=out_shape,
        scratch_shapes=[pltpu.VMEM((tm, D_MODEL), BF16), pltpu.VMEM((tm, D_MODEL), F32)],
        compiler_params=_params("arbitrary", "arbitrary"),
        name="ffn",
    )(*args)


def _mla_proj_kernel(x_ref, m_ref, g_ref, w_ref, qn_ref, kvn_ref, cos_ref, sin_ref,
                     cq_ref, ckv_ref, ckvb_ref, kr_ref):
    h = _modulate(x_ref[...], g_ref[...], m_ref[3:4, :], m_ref[4:5, :]).astype(BF16)
    z = _dot(h, w_ref[...])
    cq_ref[...] = _rms(z[:, :Q_LORA], qn_ref[...]).astype(BF16)
    ckv = _rms(z[:, Q_LORA:Q_LORA + KV_LORA], kvn_ref[...])
    ckv_ref[...] = ckv
    ckvb_ref[...] = ckv.astype(BF16)
    o = Q_LORA + KV_LORA
    kr_ref[...] = z[:, o:o + QK_ROPE] * cos_ref[...] + z[:, o + QK_ROPE:o + 2 * QK_ROPE] * sin_ref[...]


def _mla_proj(geo, x, mods, g, w_cat, qn, kvn, cos, sin):
    tm = TM_PROJ
    n_out = w_cat.shape[1]
    row = lambda i: (i, 0)
    fix = lambda i: (0, 0)
    return pl.pallas_call(
        _mla_proj_kernel,
        grid=(geo.n_tok // tm,),
        in_specs=[
            pl.BlockSpec((tm, D_MODEL), row),
            pl.BlockSpec((None, N_MOD, D_MODEL), lambda i: (geo.cond_row(i, tm), 0, 0)),
            pl.BlockSpec((1, D_MODEL), fix),
            pl.BlockSpec((D_MODEL, n_out), fix),
            pl.BlockSpec((1, Q_LORA), fix),
            pl.BlockSpec((1, KV_LORA), fix),
            pl.BlockSpec((tm, QK_ROPE), row),
            pl.BlockSpec((tm, QK_ROPE), row),
        ],
        out_specs=[
            pl.BlockSpec((tm, Q_LORA), row),
            pl.BlockSpec((tm, KV_LORA), row),
            pl.BlockSpec((tm, KV_LORA), row),
            pl.BlockSpec((tm, QK_ROPE), row),
        ],
        out_shape=[
            jax.ShapeDtypeStruct((geo.n_tok, Q_LORA), BF16),
            jax.ShapeDtypeStruct((geo.n_tok, KV_LORA), F32),
            jax.ShapeDtypeStruct((geo.n_tok, KV_LORA), BF16),
            jax.ShapeDtypeStruct((geo.n_tok, QK_ROPE), F32),
        ],
        compiler_params=_params("parallel"),
        name="mla_proj",
    )(x, mods, g, w_cat, qn, kvn, cos, sin)


def _attn_kernel(*refs, s_len, t_past, rope):
    if rope:
        (cq_ref, ckvb_ref, kr_ref, pckv_ref, pkr_ref, cos_ref, sin_ref,
         wq_ref, wqr_ref, wkv_ref, _, o_ref, kv_all, kr_all) = refs
    else:
        cq_ref, ckvb_ref, kr_ref, wq_ref, wkv_ref, o_ref, kv_all, kr_all = refs
    if t_past:
        kv_all[0:t_past, :] = pckv_ref[...].astype(BF16)
        kr_all[0:t_past, :] = pkr_ref[...].astype(BF16)
    kv_all[t_past:, :] = ckvb_ref[...]
    kr_all[t_past:, :] = kr_ref[...].astype(BF16)

    n_q = s_len // Q_TILE
    hg = max(1, ATTN_LOCKSTEP // n_q)
    row_slices = [slice(qb * Q_TILE, (qb + 1) * Q_TILE) for qb in range(n_q)]

    def head_group(g, carry):
        heads = [g * hg + i for i in range(hg)]
        kv = kv_all[...]
        krb = kr_all[...]
        kvp = [_dot(kv, wkv_ref[hd]) for hd in heads]
        kn = [x[:, :QK_NOPE].astype(BF16) for x in kvp]
        v = [x[:, QK_NOPE:].astype(BF16) for x in kvp]
        items = [(i, qb) for i in range(hg) for qb in range(n_q)]
        cq = [cq_ref[rows, :] for rows in row_slices]
        q = [_dot(cq[qb], wq_ref[heads[i]]) for i, qb in items]
        qr = [x[:, QK_NOPE:] for x in q]
        if rope:
            rot = [_dot(cq[qb], wqr_ref[heads[i]]) for i, qb in items]
            qr = [x * cos_ref[row_slices[qb], :] + y * sin_ref[row_slices[qb], :]
                  for x, y, (i, qb) in zip(qr, rot, items)]
        s = [(_dot_nt(x[:, :QK_NOPE].astype(BF16), kn[i]) + _dot_nt(y.astype(BF16), krb)) * ATTN_SCALE
             for x, y, (i, qb) in zip(q, qr, items)]
        p = [jnp.exp(x - jnp.max(x, axis=-1, keepdims=True)) for x in s]
        pr = [(x * (1.0 / jnp.sum(x, axis=-1, keepdims=True))).astype(BF16) for x in p]
        o = [_dot(x, v[i]).astype(BF16) for x, (i, qb) in zip(pr, items)]
        for x, (i, qb) in zip(o, items):
            o_ref[row_slices[qb], pl.ds(pl.multiple_of(heads[i] * V_DIM, V_DIM), V_DIM)] = x
        return carry

    lax.fori_loop(0, MLA_HEADS // hg, head_group, 0)


def _attn(geo, cq, ckvb, kr, wq, wqr, wkv, *, latent, past_ckv=None, past_kr=None, cos=None, sin=None, o_ctx=None):
    if latent:
        nb, s_len, off = geo.b_lat, geo.s_lat, geo.n_ctx // geo.s_lat
        t_past = past_ckv.shape[1]
    else:
        nb, s_len, off, t_past = geo.b_ctx, geo.s_ctx, 0, 0
    row = lambda b: (b + off, 0)
    fix3 = lambda b: (0, 0, 0)
    tok = lambda width: pl.BlockSpec((s_len, width), row)
    wspec = lambda w: pl.BlockSpec(w.shape, fix3)
    in_specs = [tok(Q_LORA), tok(KV_LORA), tok(QK_ROPE)]
    args = [cq, ckvb, kr]
    if latent:
        in_specs += [pl.BlockSpec((None, t_past, KV_LORA), lambda b: (b, 0, 0)),
                     pl.BlockSpec((None, t_past, QK_ROPE), lambda b: (b, 0, 0)),
                     tok(QK_ROPE), tok(QK_ROPE), wspec(wq), wspec(wqr)]
        args += [past_ckv, past_kr, cos, sin, wq, wqr]
    else:
        in_specs += [wspec(wq)]
        args += [wq]
    in_specs += [wspec(wkv)]
    args += [wkv]
    aliases = {}
    if latent:
        in_specs += [pl.BlockSpec(memory_space=pl.ANY)]
        args += [o_ctx]
        aliases = {len(args) - 1: 0}
    kern = functools.partial(_attn_kernel, s_len=s_len, t_past=t_past, rope=latent)
    return pl.pallas_call(
        kern,
        grid=(nb,),
        in_specs=in_specs,
        out_specs=pl.BlockSpec((s_len, D_MODEL), row),
        out_shape=jax.ShapeDtypeStruct((geo.n_tok, D_MODEL), BF16),
        input_output_aliases=aliases,
        scratch_shapes=[pltpu.VMEM((t_past + s_len, KV_LORA), BF16),
                        pltpu.VMEM((t_past + s_len, QK_ROPE), BF16)],
        compiler_params=_params("parallel"),
        name="attn_lat" if latent else "attn_ctx",
    )(*args)


def _out_proj_kernel(a_ref, w_ref, x_ref, m_ref, o_ref):
    o_ref[...] = x_ref[...] + m_ref[5:6, :] * _dot(a_ref[...], w_ref[...])


def _out_proj(geo, a, w, x, mods):
    tm = TM_PROJ
    row = pl.BlockSpec((tm, D_MODEL), lambda i: (i, 0))
    return pl.pallas_call(
        _out_proj_kernel,
        grid=(geo.n_tok // tm,),
        in_specs=[row, pl.BlockSpec((D_MODEL, D_MODEL), lambda i: (0, 0)), row,
                  pl.BlockSpec((None, N_MOD, D_MODEL), lambda i: (geo.cond_row(i, tm), 0, 0))],
        out_specs=row,
        out_shape=jax.ShapeDtypeStruct((geo.n_tok, D_MODEL), F32),
        compiler_params=_params("parallel"),
        name="out_proj",
    )(a, w, x, mods)


def _shifted(geo, x_ref, xp_ref, xn_ref, m_ref, g_ref, tm):
    i = pl.program_id(0)
    g, shift, scale = g_ref[...], m_ref[3:4, :], m_ref[4:5, :]
    h = _modulate(x_ref[...], g, shift, scale)
    start = i * tm
    seq = jnp.where(start < geo.n_ctx, geo.s_ctx, geo.s_lat)
    rel = jnp.where(start < geo.n_ctx, start, start - geo.n_ctx)
    has_prev = (rel % seq) != 0
    has_next = ((rel + tm) % seq) != 0
    hp = _modulate(xp_ref[...], g, shift, scale)[7:8, :]
    hn = _modulate(xn_ref[...], g, shift, scale)[0:1, :]
    hp = jnp.where(has_prev, hp, 0.0)
    hn = jnp.where(has_next, hn, 0.0)
    r = lax.broadcasted_iota(jnp.int32, h.shape, 0)
    down = jnp.where(r == 0, hp, pltpu.roll(h, 1, 0))
    up = jnp.where(r == tm - 1, hn, pltpu.roll(h, tm - 1, 0))
    return h, 0.5 * (down + up) - h


def _halo_specs(geo, tm):
    nb8 = geo.n_tok // 8
    return [
        pl.BlockSpec((tm, D_MODEL), lambda i, *_: (i, 0)),
        pl.BlockSpec((8, D_MODEL), lambda i, *_: (jnp.maximum(i * (tm // 8) - 1, 0), 0)),
        pl.BlockSpec((8, D_MODEL), lambda i, *_: (jnp.minimum((i + 1) * (tm // 8), nb8 - 1), 0)),
        pl.BlockSpec((None, N_MOD, D_MODEL), lambda i, *_: (geo.cond_row(i, tm), 0, 0)),
        pl.BlockSpec((1, D_MODEL), lambda i, *_: (0, 0)),
    ]


def _rkv_kernel(x_ref, xp_ref, xn_ref, m_ref, g_ref, mu_ref, w_ref, o_ref, h_ref, xx_ref, *, geo, tm):
    @pl.when(pl.program_id(1) == 0)
    def _():
        h, xx = _shifted(geo, x_ref, xp_ref, xn_ref, m_ref, g_ref, tm)
        h_ref[...] = h
        xx_ref[...] = xx

    xm = (h_ref[...] + xx_ref[...] * mu_ref[...]).astype(BF16)
    o_ref[...] = _dot(xm, w_ref[pl.program_id(1)])


def _rkv(geo, x, mods, g, mu3, w3):
    tm = TM_RWKV
    return pl.pallas_call(
        functools.partial(_rkv_kernel, geo=geo, tm=tm),
        grid=(geo.n_tok // tm, 3),
        in_specs=_halo_specs(geo, tm) + [
            pl.BlockSpec((None, 1, D_MODEL), lambda i, p: (p, 0, 0)),
            pl.BlockSpec((3, D_MODEL, D_MODEL), lambda i, p: (0, 0, 0), pipeline_mode=pl.Buffered(1)),
        ],
        out_specs=pl.BlockSpec((None, tm, D_MODEL), lambda i, p: (p, i, 0)),
        out_shape=jax.ShapeDtypeStruct((3, geo.n_tok, D_MODEL), F32),
        scratch_shapes=[pltpu.VMEM((tm, D_MODEL), F32), pltpu.VMEM((tm, D_MODEL), F32)],
        compiler_params=_params("parallel", "arbitrary"),
        name="rwkv_rkv",
    )(x, x, x, mods, g, mu3, w3)


def _lora_kernel(x_ref, xp_ref, xn_ref, m_ref, g_ref, mug_ref, mud_ref, g1_ref, g2_ref,
                 wa1_ref, wa2_ref, w0_ref, a0_ref, gate_ref, dec_ref, asig_ref, *, geo, tm):
    h, xx = _shifted(geo, x_ref, xp_ref, xn_ref, m_ref, g_ref, tm)
    mix = lambda mu: (h + xx * mu).astype(BF16)
    zg = _dot(mix(mug_ref[...]), g1_ref[...])
    gate_ref[...] = _dot(_sigmoid(zg).astype(BF16), g2_ref[...])
    for d in range(2):
        zw = _dot(mix(mud_ref[2 * d]), wa1_ref[d])
        wl = w0_ref[d] + _dot(jnp.tanh(zw).astype(BF16), wa2_ref[d])
        dec_ref[d] = -LOG_DECAY_SCALE * _sigmoid(wl)
        za = _dot(mix(mud_ref[2 * d + 1]), wa1_ref[2 + d])
        asig_ref[d] = _sigmoid(a0_ref[d] + _dot(za.astype(BF16), wa2_ref[2 + d]))


def _lora(geo, x, mods, g, mu_g, mu_dir, g1, g2, wa1, wa2, w0, a0):
    tm = TM_RWKV
    full = lambda a: pl.BlockSpec(a.shape, lambda i: (0,) * a.ndim)
    tok2 = pl.BlockSpec((2, tm, D_MODEL), lambda i: (0, i, 0))
    return pl.pallas_call(
        functools.partial(_lora_kernel, geo=geo, tm=tm),
        grid=(geo.n_tok // tm,),
        in_specs=_halo_specs(geo, tm) + [full(a) for a in (mu_g, mu_dir, g1, g2, wa1, wa2, w0, a0)],
        out_specs=[pl.BlockSpec((tm, D_MODEL), lambda i: (i, 0)), tok2, tok2],
        out_shape=[jax.ShapeDtypeStruct((geo.n_tok, D_MODEL), F32),
                   jax.ShapeDtypeStruct((2, geo.n_tok, D_MODEL), F32),
                   jax.ShapeDtypeStruct((2, geo.n_tok, D_MODEL), F32)],
        compiler_params=_params("parallel"),
        name="rwkv_lora",
    )(x, x, x, mods, g, mu_g, mu_dir, g1, g2, wa1, wa2, w0, a0)


def _split2(x):
    hi = x.astype(BF16)
    return hi, (x - hi.astype(F32)).astype(BF16)


def _head_sums(x, ones_blk):
    hi, lo = _split2(x)
    return _dot(hi, ones_blk) + _dot(lo, ones_blk)


def _pair_ones():
    r = lax.broadcasted_iota(jnp.int32, (PAIR, PAIR), 0)
    c = lax.broadcasted_iota(jnp.int32, (PAIR, PAIR), 1)
    same = (r < RWKV_HEAD) == (c < RWKV_HEAD)
    return same, jnp.where(same, 1.0, 0.0).astype(BF16)


def _wkv_kernel(*refs, geo, tb, aliased):
    r_ref, k_ref, v_ref, dec_ref, asig_ref, kk_ref, ka_ref, rk_ref, s0_ref = refs[:9]
    y_ref, bon_ref, sf_fwd_ref, sf_bwd_ref, s_ref = refs[9 + (2 if aliased else 0):]
    d = pl.program_id(0)
    i = pl.program_id(1)
    nblk = pl.num_programs(1)
    blk = i + d * (nblk - 1 - 2 * i)
    start = blk * tb
    seq = jnp.where(start < geo.n_ctx, geo.s_ctx, geo.s_lat)
    rel = jnp.where(start < geo.n_ctx, start, start - geo.n_ctx)
    at_lo = (rel % seq) == 0
    at_hi = ((rel + tb) % seq) == 0
    first = jnp.where(d == 0, at_lo, at_hi)
    last = jnp.where(d == 0, at_hi, at_lo)

    @pl.when(first)
    def _():
        s_ref[...] = s0_ref[...]

    nch = tb // CHUNK
    sgn = 1 - 2 * d
    fwd = (d == 0).astype(F32)
    row = lax.broadcasted_iota(jnp.int32, (CHUNK, PAIR), 0)
    lane = lax.broadcasted_iota(jnp.int32, (CHUNK, PAIR), 1)
    sidx = lane & (RWKV_HEAD - 1)
    delta = (row - sidx) * sgn
    strict = delta > 0
    incl = delta >= 0
    eye = jnp.where(row == sidx, 1.0, 0.0)
    off_masks = []
    m = 1
    while m < CHUNK:
        off_masks.append(strict & ((row // (2 * m)) == (sidx // (2 * m))) & ((row // m) != (sidx // m)))
        m *= 2
    head0 = lane < RWKV_HEAD
    tr = lax.broadcasted_iota(jnp.int32, (CHUNK, 3 * CHUNK), 0)
    ts = lax.broadcasted_iota(jnp.int32, (CHUNK, 3 * CHUNK), 1) & (CHUNK - 1)
    tri3 = jnp.where((tr - ts) * sgn >= 0, 1.0, 0.0).astype(BF16)
    same_head, ones_blk = _pair_ones()

    def expand(x):
        return jnp.concatenate([jnp.where(head0, x, 0.0), jnp.where(head0, 0.0, x)], axis=0)

    def chunk_body(ci, carry):
        cc = ci * sgn + d * (nch - 1)
        rows = pl.ds(pl.multiple_of(cc * CHUNK, CHUNK), CHUNK)

        def group_body(pg, carry2):
            ps = [pg * PAIR_GROUP + q for q in range(PAIR_GROUP)]
            cols = [pl.ds(pl.multiple_of(p * PAIR, PAIR), PAIR) for p in ps]
            each = lambda f, *ls: [f(*xs) for xs in zip(*ls)]
            r = [r_ref[rows, c] for c in cols]
            k = [k_ref[rows, c] for c in cols]
            v = [v_ref[rows, c] for c in cols]
            logw = [dec_ref[rows, c] for c in cols]
            a = [asig_ref[rows, c] for c in cols]
            kk = [x * kk_ref[:, c] for x, c in zip(k, cols)]
            kd = [x * (1.0 + (y - 1.0) * ka_ref[:, c]) for x, y, c in zip(k, a, cols)]
            sums = [_head_sums(jnp.concatenate([x * x, y * z * rk_ref[:, c]], axis=0), ones_blk)
                    for x, y, z, c in zip(kk, r, kd, cols)]
            kk = each(lambda x, s: x / jnp.maximum(jnp.sqrt(s[:CHUNK]), 1e-12), kk, sums)
            b_in = each(lambda x, y: x * y, kk, a)
            for c, s, y in zip(cols, sums, v):
                bon_ref[rows, c] = s[CHUNK:] * y

            l1 =[x.astype(BF16) for x in logw]
            e1 = each(lambda x, y: x - y.astype(F32), logw, l1)
            l2 = [x.astype(BF16) for x in e1]
            l3 = each(lambda x, y: (x - y.astype(F32)).astype(BF16), e1, l2)
            cum = each(lambda x, y, z: _dot(tri3, jnp.concatenate([x, y, z], axis=0)), l1, l2, l3)
            tot = [fwd * x[CHUNK - 1:CHUNK, :] + (1.0 - fwd) * x[0:1, :] for x in cum]
            c_inv = [jnp.exp(-x) for x in cum]
            at = each(lambda x, y, z: -x * jnp.exp(y - z), kk, cum, logw)
            rt = each(lambda x, y: x * jnp.exp(y), r, cum)
            c_end = each(lambda x, y: jnp.exp(x - y), tot, cum)

            lhs = each(lambda x, y: jnp.concatenate([x, y], axis=0).astype(BF16), at, rt)
            rhs = each(lambda x, y, z: jnp.concatenate([expand(x * z), expand(y * z)], axis=0).astype(BF16),
                       b_in, kd, c_inv)
            g = each(_dot_nt, lhs, rhs)
            ab = [jnp.where(strict, x[:CHUNK, :PAIR], 0.0) for x in g]
            ak = [jnp.where(strict, x[:CHUNK, PAIR:], 0.0) for x in g]
            rbk = [jnp.concatenate([jnp.where(incl, x[CHUNK:, :PAIR], 0.0),
                                    jnp.where(incl, x[CHUNK:, PAIR:], 0.0)], axis=1).astype(BF16) for x in g]

            t_inv = [eye + jnp.where(off_masks[0], x, 0.0) for x in ab]
            for off in off_masks[1:]:
                lx = each(lambda x, t: _dot(jnp.where(off, x, 0.0).astype(BF16), expand(t).astype(BF16)), ab, t_inv)
                t_inv = each(lambda t, x: t + _dot(t.astype(BF16), expand(x).astype(BF16)), t_inv, lx)

            s2 = [s_ref[p] for p in ps]
            ars = each(lambda x, s: _dot_nt(x, s.astype(BF16)), lhs, s2)
            vexp = [expand(x).astype(BF16) for x in v]
            rhs_u = each(lambda x, y, z: x[:CHUNK] + _dot(y.astype(BF16), z), ars, ak, vexp)
            u = each(lambda t, x: _dot(t.astype(BF16), expand(x).astype(BF16)), t_inv, rhs_u)
            uv = each(lambda x, y: jnp.concatenate([expand(x).astype(BF16), y], axis=0), u, vexp)
            y_out = each(lambda x, y, z: x[CHUNK:] + _dot(y, z), ars, rbk, uv)
            for c, x in zip(cols, y_out):
                y_ref[rows, c] = x
            uv_t = each(lambda x, y: jnp.concatenate([x, y], axis=0).T.astype(BF16), u, v)
            bk = each(lambda x, y, z: jnp.concatenate([x * z, y * z], axis=0).astype(BF16), b_in, kd, c_end)
            upd = each(_dot, uv_t, bk)
            for p, s, x, t in zip(ps, s2, upd, tot):
                s_ref[p] = s * jnp.exp(t) + jnp.where(same_head, x, 0.0)
            return carry2

        lax.fori_loop(0, N_PAIRS // PAIR_GROUP, group_body, 0)
        return carry

    lax.fori_loop(0, nch, chunk_body, 0)

    def write_states(dst_ref):
        for p in range(N_PAIRS):
            s = s_ref[p]
            dst_ref[2 * p] = s[:RWKV_HEAD, :RWKV_HEAD]
            dst_ref[2 * p + 1] = s[RWKV_HEAD:, RWKV_HEAD:]

    ends_ctx = last & (start < geo.n_ctx)

    @pl.when(ends_ctx & (d == 0))
    def _():
        write_states(sf_fwd_ref)

    @pl.when(ends_ctx & (d == 1))
    def _():
        write_states(sf_bwd_ref)


def _wkv(geo, rkv, dec, asig, kk_c, ka_c, rk_c, s0, j, n_layers, prev_states):
    tb = TB_WKV
    nblk = geo.n_tok // tb
    blk = lambda d, i: i + d * (nblk - 1 - 2 * i)
    tok = lambda which: pl.BlockSpec((None, tb, D_MODEL), lambda d, i: (which, blk(d, i), 0))
    perdir = pl.BlockSpec((None, tb, D_MODEL), lambda d, i: (d, blk(d, i), 0))
    const = pl.BlockSpec((1, D_MODEL), lambda d, i: (0, 0))
    last_seq = geo.b_ctx - 1
    seq = lambda d, i: jnp.minimum(blk(d, i) * tb // geo.s_ctx, last_seq)
    final = lambda which: pl.BlockSpec(
        (None, None, RWKV_HEADS, RWKV_HEAD, RWKV_HEAD),
        lambda d, i: (jnp.where(d == which, seq(d, i), last_seq), j, 0, 0, 0))
    final_shape = jax.ShapeDtypeStruct((geo.b_ctx, n_layers, RWKV_HEADS, RWKV_HEAD, RWKV_HEAD), F32)
    in_specs = [tok(0), tok(1), tok(2), perdir, perdir, const, const, const,
                pl.BlockSpec((None, None, N_PAIRS, PAIR, PAIR), lambda d, i: (d, geo.cond_row(blk(d, i), tb), 0, 0, 0))]
    args = [rkv, rkv, rkv, dec, asig, kk_c, ka_c, rk_c, s0]
    aliases = {}
    if prev_states is not None:
        in_specs += [pl.BlockSpec(memory_space=pl.ANY)] * 2
        aliases = {len(args): 2, len(args) + 1: 3}
        args += list(prev_states)
    return pl.pallas_call(
        functools.partial(_wkv_kernel, geo=geo, tb=tb, aliased=prev_states is not None),
        grid=(2, nblk),
        in_specs=in_specs,
        out_specs=[perdir, perdir, final(0), final(1)],
        out_shape=[jax.ShapeDtypeStruct((2, geo.n_tok, D_MODEL), F32),
                   jax.ShapeDtypeStruct((2, geo.n_tok, D_MODEL), F32), final_shape, final_shape],
        input_output_aliases=aliases,
        scratch_shapes=[pltpu.VMEM((N_PAIRS, PAIR, PAIR), F32)],
        compiler_params=_params("arbitrary", "arbitrary"),
        name="wkv_chunked",
    )(*args)


def _rwkv_out_kernel(y_ref, bon_ref, gate_ref, lw_ref, lb_ref, w_ref, x_ref, m_ref, o_ref, z_ref):
    _, ones_blk = _pair_ones()
    inv_n = 1.0 / RWKV_HEAD
    for s in range(N_PAIRS):
        cols = slice(s * PAIR, (s + 1) * PAIR)
        y = y_ref[0, :, cols] + y_ref[1, :, cols]
        yc = y - _head_sums(y, ones_blk) * inv_n
        var = _head_sums(yc * yc, ones_blk) * inv_n
        z = yc * lax.rsqrt(var + GN_EPS) * lw_ref[:, cols] + lb_ref[:, cols] + bon_ref[0, :, cols] + bon_ref[1, :, cols]
        z_ref[:, cols] = (z * gate_ref[:, cols]).astype(BF16)
    o_ref[...] = x_ref[...] + m_ref[5:6, :] * _dot(z_ref[...], w_ref[...])


def _rwkv_out(geo, y, bon, gate, lw, lb, w, x, mods):
    tm = TM_OUT
    row = pl.BlockSpec((tm, D_MODEL), lambda i: (i, 0))
    both = pl.BlockSpec((2, tm, D_MODEL), lambda i: (0, i, 0))
    const = pl.BlockSpec((1, D_MODEL), lambda i: (0, 0))
    return pl.pallas_call(
        _rwkv_out_kernel,
        grid=(geo.n_tok // tm,),
        in_specs=[both, both, row, const, const, pl.BlockSpec((D_MODEL, D_MODEL), lambda i: (0, 0)), row,
                  pl.BlockSpec((None, N_MOD, D_MODEL), lambda i: (geo.cond_row(i, tm), 0, 0))],
        out_specs=row,
        out_shape=jax.ShapeDtypeStruct((geo.n_tok, D_MODEL), F32),
        scratch_shapes=[pltpu.VMEM((tm, D_MODEL), BF16)],
        compiler_params=_params("parallel"),
        name="rwkv_out",
    )(y, bon, gate, lw, lb, w, x, mods)


def _state_to_pairs(s):
    b = s.shape[0]
    s5 = s.reshape(b, N_PAIRS, 2, RWKV_HEAD, RWKV_HEAD)
    z = jnp.zeros_like(s5[:, :, 0])
    top = jnp.concatenate([s5[:, :, 0], z], axis=-1)
    bot = jnp.concatenate([z, s5[:, :, 1]], axis=-1)
    return jnp.concatenate([top, bot], axis=-2)


def _rope_tables(geo):
    rows = geo.s_lat // GRID_W
    row = jnp.repeat(jnp.arange(rows, dtype=F32), GRID_W)
    col = jnp.tile(jnp.arange(GRID_W, dtype=F32), rows)
    inv = ROPE_THETA ** (-jnp.arange(ROPE_FREQS, dtype=F32) / ROPE_FREQS)
    ang = jnp.stack([row[:, None] * inv, col[:, None] * inv], axis=1)
    ang = jnp.broadcast_to(ang[:, :, None, :], (geo.s_lat, 2, 2, ROPE_FREQS)).reshape(geo.s_lat, QK_ROPE)
    cos = jnp.concatenate([jnp.ones((geo.n_ctx, QK_ROPE), F32), jnp.tile(jnp.cos(ang), (geo.b_lat, 1))], axis=0)
    sin = jnp.concatenate([jnp.zeros((geo.n_ctx, QK_ROPE), F32), jnp.tile(jnp.sin(ang), (geo.b_lat, 1))], axis=0)
    return cos, sin


def _rot_cols(w):
    w4 = w.reshape(w.shape[:-1] + (2, 2, ROPE_FREQS))
    return jnp.stack([-w4[..., 1, :], w4[..., 0, :]], axis=-2).reshape(w.shape)


def _mla_layer(geo, x, mods, g, j, cos, sin, cache_ckv, cache_krope, p):
    w_dkv = p['mla_w_dkv'][j]
    w_cat = jnp.concatenate([p['mla_w_dq'][j], w_dkv, _rot_cols(w_dkv[:, KV_LORA:])], axis=1).astype(BF16)
    cq, ckv, ckvb, kr = _mla_proj(geo, x, mods, g, w_cat, p['mla_q_norm'][j][None], p['mla_kv_norm'][j][None],
                                  cos, sin)
    w_uq = p['mla_w_uq'][j].reshape(Q_LORA, MLA_HEADS, QK_NOPE + QK_ROPE).transpose(1, 0, 2)
    wq = w_uq.astype(BF16)
    wqr = _rot_cols(w_uq[..., QK_NOPE:]).astype(BF16)
    wkv = p['mla_w_ukv'][j].reshape(KV_LORA, MLA_HEADS, QK_NOPE + V_DIM).transpose(1, 0, 2).astype(BF16)
    o_ctx = _attn(geo, cq, ckvb, kr, wq, wqr, wkv, latent=False)
    o = _attn(geo, cq, ckvb, kr, wq, wqr, wkv, latent=True,
              past_ckv=cache_ckv[:, j], past_kr=cache_krope[:, j], cos=cos, sin=sin, o_ctx=o_ctx)
    x = _out_proj(geo, o, p['mla_w_o'][j].astype(BF16), x, mods)
    new_ckv = ckv[:geo.n_ctx].reshape(geo.b_ctx, geo.s_ctx, KV_LORA)
    new_kr = kr[:geo.n_ctx].reshape(geo.b_ctx, geo.s_ctx, QK_ROPE)
    return x, new_ckv, new_kr


def _rwkv_layer(geo, x, mods, g, j, s0_fwd, s0_bwd, prev_states, p):
    mu = p['rwkv_mu'][j]
    mu_dir = p['rwkv_mu_dir'][j].reshape(4, 1, D_MODEL)
    w3 = jnp.stack([p['rwkv_w_r'][j], p['rwkv_w_k'][j], p['rwkv_w_v'][j]]).astype(BF16)
    rkv = _rkv(geo, x, mods, g, mu[:3, None, :], w3)
    pad1 = lambda w: jnp.pad(w, ((0, 0), (0, 0), (0, LORA_PAD - w.shape[-1])))
    pad2 = lambda w: jnp.pad(w, ((0, 0), (0, LORA_PAD - w.shape[-2]), (0, 0)))
    wa1 = jnp.concatenate([pad1(p['rwkv_w1'][j]), pad1(p['rwkv_a1'][j])]).astype(BF16)
    wa2 = jnp.concatenate([pad2(p['rwkv_w2'][j]), pad2(p['rwkv_a2'][j])]).astype(BF16)
    gate, dec, asig = _lora(geo, x, mods, g, mu[3:4], mu_dir, p['rwkv_g1'][j].astype(BF16),
                            p['rwkv_g2'][j].astype(BF16), wa1, wa2,
                            p['rwkv_w0'][j][:, None, :], p['rwkv_a0'][j][:, None, :])
    zero = jnp.zeros((2, 1, N_PAIRS, PAIR, PAIR), F32)
    s0 = jnp.concatenate([zero, jnp.stack([_state_to_pairs(s0_fwd), _state_to_pairs(s0_bwd)])], axis=1)
    y, bon, sf, sb = _wkv(geo, rkv, dec, asig, p['rwkv_k_k'][j][None], p['rwkv_k_a'][j][None],
                          p['rwkv_r_k'][j].reshape(1, D_MODEL), s0, j, p['rwkv_mu'].shape[0], prev_states)
    x = _rwkv_out(geo, y, bon, gate, p['rwkv_ln_w'][j][None], p['rwkv_ln_b'][j][None],
                  p['rwkv_w_o'][j].astype(BF16), x, mods)
    return x, (sf, sb)


def kernel(x_prompt, x_sample, cache_ckv, cache_krope, state_wkv_fwd, state_wkv_bwd, c, c_ctx, w_ada, b_ada, norm_sub, norm_final, w_ffn_in, w_ffn_out, mla_w_dq, mla_q_norm, mla_w_uq, mla_w_dkv, mla_kv_norm, mla_w_ukv, mla_w_o, rwkv_mu, rwkv_mu_dir, rwkv_w_r, rwkv_w_k, rwkv_w_v, rwkv_w0, rwkv_w1, rwkv_w2, rwkv_a0, rwkv_a1, rwkv_a2, rwkv_g1, rwkv_g2, rwkv_k_k, rwkv_k_a, rwkv_r_k, rwkv_ln_w, rwkv_ln_b, rwkv_w_o):
    p = dict(mla_w_dq=mla_w_dq, mla_q_norm=mla_q_norm, mla_w_uq=mla_w_uq, mla_w_dkv=mla_w_dkv,
             mla_kv_norm=mla_kv_norm, mla_w_ukv=mla_w_ukv, mla_w_o=mla_w_o,
             rwkv_mu=rwkv_mu, rwkv_mu_dir=rwkv_mu_dir, rwkv_w_r=rwkv_w_r, rwkv_w_k=rwkv_w_k,
             rwkv_w_v=rwkv_w_v, rwkv_w0=rwkv_w0, rwkv_w1=rwkv_w1, rwkv_w2=rwkv_w2,
             rwkv_a0=rwkv_a0, rwkv_a1=rwkv_a1, rwkv_a2=rwkv_a2, rwkv_g1=rwkv_g1, rwkv_g2=rwkv_g2,
             rwkv_k_k=rwkv_k_k, rwkv_k_a=rwkv_k_a, rwkv_r_k=rwkv_r_k,
             rwkv_ln_w=rwkv_ln_w, rwkv_ln_b=rwkv_ln_b, rwkv_w_o=rwkv_w_o)
    b_ctx, s_ctx, _ = x_prompt.shape
    b_lat, s_lat, _ = x_sample.shape
    geo = _Geom(b_ctx, s_ctx, b_lat, s_lat)
    assert geo.n_ctx % s_lat == 0 and s_ctx % TM_RWKV == 0 and s_lat % TM_FFN == 0
    assert s_ctx == TB_WKV and s_lat % TB_WKV == 0

    x = jnp.concatenate([x_prompt.reshape(geo.n_ctx, D_MODEL), x_sample.reshape(geo.n_lat, D_MODEL)], axis=0)
    cond = jnp.concatenate([c_ctx[None], c, jnp.zeros((COND_ROWS - 1 - b_lat, D_MODEL), F32)], axis=0)
    mods_all = _ada(cond, w_ada, b_ada).reshape(DEPTH, COND_ROWS, N_MOD, D_MODEL)
    cos, sin = _rope_tables(geo)
    nf = norm_final[None]
    w_in = w_ffn_in[0, 0].astype(BF16)
    w_out = w_ffn_out[0, 0].astype(BF16)

    ckv_l, kr_l, states = [], [], None
    for l in range(DEPTH):
        j = l // N_MIXERS
        mods = mods_all[l]
        x, w_in, w_out = _ffn(geo, x, mods, norm_sub[l, 0][None], w_in, w_out, nf, sub=0, final=False,
                              nxt=(w_ffn_in, w_ffn_out, l, 1))
        if l % N_MIXERS == 0:
            x, ckv, kr = _mla_layer(geo, x, mods, norm_sub[l, 1][None], j, cos, sin, cache_ckv, cache_krope, p)
            ckv_l.append(ckv)
            kr_l.append(kr)
        else:
            x, states = _rwkv_layer(geo, x, mods, norm_sub[l, 1][None], j,
                                    state_wkv_fwd[:, j], state_wkv_bwd[:, j], states, p)
        if l < DEPTH - 1:
            x, w_in, w_out = _ffn(geo, x, mods, norm_sub[l, 2][None], w_in, w_out, nf, sub=2, final=False,
                                  nxt=(w_ffn_in, w_ffn_out, l + 1, 0))
        else:
            x, = _ffn(geo, x, mods, norm_sub[l, 2][None], w_in, w_out, nf, sub=2, final=True)

    y_prompt = x[:geo.n_ctx].reshape(b_ctx, s_ctx, D_MODEL)
    y_sample = x[geo.n_ctx:].reshape(b_lat, s_lat, D_MODEL)
    return (y_prompt, y_sample, jnp.stack(ckv_l, axis=1), jnp.stack(kr_l, axis=1),
            states[0], states[1])
```

```python
import functools

import jax
import jax.numpy as jnp
from jax import lax
from jax.experimental import pallas as pl
from jax.experimental.pallas import tpu as pltpu

D_MODEL = 2048
DEPTH = 4
N_MIXERS = 2
D_FF = 5632
N_MOD = 9
RMS_EPS = 1e-6
MLA_HEADS = 16
Q_LORA = 512
KV_LORA = 512
QK_NOPE = 128
QK_ROPE = 64
V_DIM = 128
ROPE_FREQS = QK_ROPE // 4
ROPE_THETA = 10000.0
GRID_W = 64
ATTN_SCALE = (QK_NOPE + QK_ROPE) ** -0.5
RWKV_HEAD = 64
RWKV_HEADS = D_MODEL // RWKV_HEAD
GN_EPS = 64e-5
LOG_DECAY_SCALE = 0.6065306597126334
LORA_PAD = 128

COND_ROWS = 8
VMEM_LIMIT = 56 * 1024 * 1024

TM_FFN = 512
TF_FFN = 512
CAST_TILE = 256
TM_PROJ = 512
TM_RWKV = 256
TN_ADA = 1024
Q_TILE = 256
ATTN_LOCKSTEP = 4
CHUNK = 64
PAIR = 2 * RWKV_HEAD
N_PAIRS = D_MODEL // PAIR
PAIR_GROUP = 16
TB_WKV = 256
TM_OUT = 256
GN_SLAB = 256

BF16 = jnp.bfloat16
F32 = jnp.float32


def _params(*sem):
    return pltpu.CompilerParams(dimension_semantics=sem, vmem_limit_bytes=VMEM_LIMIT)


def _sigmoid(x):
    return 1.0 / (1.0 + jnp.exp(-x))


def _modulate(x, g, shift, scale):
    ms = jnp.mean(x * x, axis=-1, keepdims=True)
    return (x * lax.rsqrt(ms + RMS_EPS) * g) * (1.0 + scale) + shift


def _rms(x, w):
    ms = jnp.mean(x * x, axis=-1, keepdims=True)
    return x * lax.rsqrt(ms + RMS_EPS) * w


def _dot(a, b):
    return jnp.dot(a, b, preferred_element_type=F32)


def _dot_nt(a, b):
    return lax.dot_general(a, b, (((1,), (1,)), ((), ())), preferred_element_type=F32)


class _Geom:
    def __init__(self, b_ctx, s_ctx, b_lat, s_lat):
        self.b_ctx, self.s_ctx, self.b_lat, self.s_lat = b_ctx, s_ctx, b_lat, s_lat
        self.n_ctx = b_ctx * s_ctx
        self.n_lat = b_lat * s_lat
        self.n_tok = self.n_ctx + self.n_lat

    def cond_row(self, i, tm):
        start = i * tm
        return jnp.where(start < self.n_ctx, 0, 1 + (start - self.n_ctx) // self.s_lat)


def _ada_kernel(c_ref, w_ref, b_ref, o_ref):
    c = c_ref[...]
    s = (c * _sigmoid(c)).astype(BF16)
    o_ref[...] = _dot(s, w_ref[...].astype(BF16)) + b_ref[...]


def _ada(cond, w_ada, b_ada):
    n = N_MOD * D_MODEL
    return pl.pallas_call(
        _ada_kernel,
        grid=(DEPTH, n // TN_ADA),
        in_specs=[
            pl.BlockSpec((COND_ROWS, D_MODEL), lambda l, j: (0, 0)),
            pl.BlockSpec((None, D_MODEL, TN_ADA), lambda l, j: (l, 0, j)),
            pl.BlockSpec((None, 1, TN_ADA), lambda l, j: (l, 0, j)),
        ],
        out_specs=pl.BlockSpec((None, COND_ROWS, TN_ADA), lambda l, j: (l, 0, j)),
        out_shape=jax.ShapeDtypeStruct((DEPTH, COND_ROWS, n), F32),
        compiler_params=_params("parallel", "parallel"),
        name="ada",
    )(cond, w_ada, b_ada.reshape(DEPTH, 1, n))


def _ffn_kernel(*refs, sub, final, cast_next):
    if cast_next:
        (x_ref, m_ref, g_ref, wg_ref, wu_ref, wo_ref, nf_ref, ci_ref, co_ref,
         o_ref, cib_ref, cob_ref, h_ref, acc_ref) = refs
    else:
        x_ref, m_ref, g_ref, wg_ref, wu_ref, wo_ref, nf_ref, o_ref, h_ref, acc_ref = refs
    f = pl.program_id(1)

    @pl.when(f == 0)
    def _():
        h = _modulate(x_ref[...], g_ref[...], m_ref[3 * sub:3 * sub + 1, :], m_ref[3 * sub + 1:3 * sub + 2, :])
        h_ref[...] = h.astype(BF16)
        acc_ref[...] = jnp.zeros_like(acc_ref)

    h = h_ref[...]
    gate = _dot(h, wg_ref[...])
    up = _dot(h, wu_ref[...])
    act = (gate * _sigmoid(gate) * up).astype(BF16)
    acc_ref[...] += _dot(act, wo_ref[...])
    if cast_next:
        step = pl.program_id(0) * pl.num_programs(1) + f
        n_in, n_out = cast_next

        @pl.when(step < n_in)
        def _():
            cib_ref[...] = ci_ref[...].astype(BF16)

        @pl.when((step >= n_in) & (step < n_in + n_out))
        def _():
            cob_ref[...] = co_ref[...].astype(BF16)

    @pl.when(f == pl.num_programs(1) - 1)
    def _():
        y = x_ref[...] + 0.5 * m_ref[3 * sub + 2:3 * sub + 3, :] * acc_ref[...]
        if final:
            y = _rms(y, nf_ref[...])
        o_ref[...] = y


def _ffn(geo, x, mods, g, w_in, w_out, nf, *, sub, final, nxt=None):
    tm, tf = TM_FFN, TF_FFN
    nf_blocks = D_FF // tf
    n_in, n_out = 2 * D_FF // CAST_TILE, D_FF // CAST_TILE
    step = lambda i, f: i * nf_blocks + f
    in_blk = lambda i, f: jnp.minimum(step(i, f), n_in - 1)
    out_blk = lambda i, f: jnp.clip(step(i, f) - n_in, 0, n_out - 1)
    in_specs = [
        pl.BlockSpec((tm, D_MODEL), lambda i, f: (i, 0)),
        pl.BlockSpec((None, N_MOD, D_MODEL), lambda i, f: (geo.cond_row(i, tm), 0, 0)),
        pl.BlockSpec((1, D_MODEL), lambda i, f: (0, 0)),
        pl.BlockSpec((D_MODEL, tf), lambda i, f: (0, f)),
        pl.BlockSpec((D_MODEL, tf), lambda i, f: (0, f + nf_blocks)),
        pl.BlockSpec((tf, D_MODEL), lambda i, f: (f, 0)),
        pl.BlockSpec((1, D_MODEL), lambda i, f: (0, 0)),
    ]
    out_specs = [pl.BlockSpec((tm, D_MODEL), lambda i, f: (i, 0))]
    out_shape = [jax.ShapeDtypeStruct((geo.n_tok, D_MODEL), F32)]
    args = [x, mods, g, w_in, w_in, w_out, nf]
    if nxt is not None:
        w_in_all, w_out_all, layer, half = nxt
        assert (geo.n_tok // tm) * nf_blocks >= n_in + n_out
        in_specs += [
            pl.BlockSpec((None, None, D_MODEL, CAST_TILE), lambda i, f: (layer, half, 0, in_blk(i, f))),
            pl.BlockSpec((None, None, CAST_TILE, D_MODEL), lambda i, f: (layer, half, out_blk(i, f), 0)),
        ]
        out_specs += [
            pl.BlockSpec((D_MODEL, CAST_TILE), lambda i, f: (0, in_blk(i, f))),
            pl.BlockSpec((CAST_TILE, D_MODEL), lambda i, f: (out_blk(i, f), 0)),
        ]
        out_shape += [jax.ShapeDtypeStruct((D_MODEL, 2 * D_FF), BF16), jax.ShapeDtypeStruct((D_FF, D_MODEL), BF16)]
        args += [w_in_all, w_out_all]
    return pl.pallas_call(
        functools.partial(_ffn_kernel, sub=sub, final=final, cast_next=(n_in, n_out) if nxt is not None else None),
        grid=(geo.n_tok // tm, nf_blocks),
        in_specs=in_specs,
        out_specs=out_specs,
        out_shape=out_shape,
        scratch_shapes=[pltpu.VMEM((tm, D_MODEL), BF16), pltpu.VMEM((tm, D_MODEL), F32)],
        compiler_params=_params("arbitrary", "arbitrary"),
        name="ffn",
    )(*args)


def _mla_proj_kernel(x_ref, m_ref, g_ref, w_ref, qn_ref, kvn_ref, cos_ref, sin_ref,
                     cq_ref, ckv_ref, ckvb_ref, kr_ref):
    h = _modulate(x_ref[...], g_ref[...], m_ref[3:4, :], m_ref[4:5, :]).astype(BF16)
    z = _dot(h, w_ref[...])
    cq_ref[...] = _rms(z[:, :Q_LORA], qn_ref[...]).astype(BF16)
    ckv = _rms(z[:, Q_LORA:Q_LORA + KV_LORA], kvn_ref[...])
    ckv_ref[...] = ckv
    ckvb_ref[...] = ckv.astype(BF16)
    o = Q_LORA + KV_LORA
    kr_ref[...] = z[:, o:o + QK_ROPE] * cos_ref[...] + z[:, o + QK_ROPE:o + 2 * QK_ROPE] * sin_ref[...]


def _mla_proj(geo, x, mods, g, w_cat, qn, kvn, cos, sin):
    tm = TM_PROJ
    n_out = w_cat.shape[1]
    row = lambda i: (i, 0)
    fix = lambda i: (0, 0)
    return pl.pallas_call(
        _mla_proj_kernel,
        grid=(geo.n_tok // tm,),
        in_specs=[
            pl.BlockSpec((tm, D_MODEL), row),
            pl.BlockSpec((None, N_MOD, D_MODEL), lambda i: (geo.cond_row(i, tm), 0, 0)),
            pl.BlockSpec((1, D_MODEL), fix),
            pl.BlockSpec((D_MODEL, n_out), fix),
            pl.BlockSpec((1, Q_LORA), fix),
            pl.BlockSpec((1, KV_LORA), fix),
            pl.BlockSpec((tm, QK_ROPE), row),
            pl.BlockSpec((tm, QK_ROPE), row),
        ],
        out_specs=[
            pl.BlockSpec((tm, Q_LORA), row),
            pl.BlockSpec((tm, KV_LORA), row),
            pl.BlockSpec((tm, KV_LORA), row),
            pl.BlockSpec((tm, QK_ROPE), row),
        ],
        out_shape=[
            jax.ShapeDtypeStruct((geo.n_tok, Q_LORA), BF16),
            jax.ShapeDtypeStruct((geo.n_tok, KV_LORA), F32),
            jax.ShapeDtypeStruct((geo.n_tok, KV_LORA), BF16),
            jax.ShapeDtypeStruct((geo.n_tok, QK_ROPE), F32),
        ],
        compiler_params=_params("parallel"),
        name="mla_proj",
    )(x, mods, g, w_cat, qn, kvn, cos, sin)


def _attn_kernel(*refs, s_len, t_past, rope):
    if rope:
        (cq_ref, ckvb_ref, kr_ref, pckv_ref, pkr_ref, cos_ref, sin_ref,
         wq_ref, wqr_ref, wkv_ref, _, o_ref, kv_all, kr_all) = refs
    else:
        cq_ref, ckvb_ref, kr_ref, wq_ref, wkv_ref, o_ref, kv_all, kr_all = refs
    if t_past:
        kv_all[0:t_past, :] = pckv_ref[...].astype(BF16)
        kr_all[0:t_past, :] = pkr_ref[...].astype(BF16)
    kv_all[t_past:, :] = ckvb_ref[...]
    kr_all[t_past:, :] = kr_ref[...].astype(BF16)

    n_q = s_len // Q_TILE
    hg = max(1, ATTN_LOCKSTEP // n_q)
    row_slices = [slice(qb * Q_TILE, (qb + 1) * Q_TILE) for qb in range(n_q)]

    def head_group(g, carry):
        heads = [g * hg + i for i in range(hg)]
        kv = kv_all[...]
        krb = kr_all[...]
        kvp = [_dot(kv, wkv_ref[hd]) for hd in heads]
        kn = [x[:, :QK_NOPE].astype(BF16) for x in kvp]
        v = [x[:, QK_NOPE:].astype(BF16) for x in kvp]
        items = [(i, qb) for i in range(hg) for qb in range(n_q)]
        cq = [cq_ref[rows, :] for rows in row_slices]
        q = [_dot(cq[qb], wq_ref[heads[i]]) for i, qb in items]
        qr = [x[:, QK_NOPE:] for x in q]
        if rope:
            rot = [_dot(cq[qb], wqr_ref[heads[i]]) for i, qb in items]
            qr = [x * cos_ref[row_slices[qb], :] + y * sin_ref[row_slices[qb], :]
                  for x, y, (i, qb) in zip(qr, rot, items)]
        s = [(_dot_nt(x[:, :QK_NOPE].astype(BF16), kn[i]) + _dot_nt(y.astype(BF16), krb)) * ATTN_SCALE
             for x, y, (i, qb) in zip(q, qr, items)]
        p = [jnp.exp(x - jnp.max(x, axis=-1, keepdims=True)) for x in s]
        pr = [(x * (1.0 / jnp.sum(x, axis=-1, keepdims=True))).astype(BF16) for x in p]
        o = [_dot(x, v[i]).astype(BF16) for x, (i, qb) in zip(pr, items)]
        for x, (i, qb) in zip(o, items):
            o_ref[row_slices[qb], pl.ds(pl.multiple_of(heads[i] * V_DIM, V_DIM), V_DIM)] = x
        return carry

    lax.fori_loop(0, MLA_HEADS // hg, head_group, 0)


def _attn(geo, cq, ckvb, kr, wq, wqr, wkv, *, latent, past_ckv=None, past_kr=None, cos=None, sin=None, o_ctx=None):
    if latent:
        nb, s_len, off = geo.b_lat, geo.s_lat, geo.n_ctx // geo.s_lat
        t_past = past_ckv.shape[1]
    else:
        nb, s_len, off, t_past = geo.b_ctx, geo.s_ctx, 0, 0
    row = lambda b: (b + off, 0)
    fix3 = lambda b: (0, 0, 0)
    tok = lambda width: pl.BlockSpec((s_len, width), row)
    wspec = lambda w: pl.BlockSpec(w.shape, fix3)
    in_specs = [tok(Q_LORA), tok(KV_LORA), tok(QK_ROPE)]
    args = [cq, ckvb, kr]
    if latent:
        in_specs += [pl.BlockSpec((None, t_past, KV_LORA), lambda b: (b, 0, 0)),
                     pl.BlockSpec((None, t_past, QK_ROPE), lambda b: (b, 0, 0)),
                     tok(QK_ROPE), tok(QK_ROPE), wspec(wq), wspec(wqr)]
        args += [past_ckv, past_kr, cos, sin, wq, wqr]
    else:
        in_specs += [wspec(wq)]
        args += [wq]
    in_specs += [wspec(wkv)]
    args += [wkv]
    aliases = {}
    if latent:
        in_specs += [pl.BlockSpec(memory_space=pl.ANY)]
        args += [o_ctx]
        aliases = {len(args) - 1: 0}
    kern = functools.partial(_attn_kernel, s_len=s_len, t_past=t_past, rope=latent)
    return pl.pallas_call(
        kern,
        grid=(nb,),
        in_specs=in_specs,
        out_specs=pl.BlockSpec((s_len, D_MODEL), row),
        out_shape=jax.ShapeDtypeStruct((geo.n_tok, D_MODEL), BF16),
        input_output_aliases=aliases,
        scratch_shapes=[pltpu.VMEM((t_past + s_len, KV_LORA), BF16),
                        pltpu.VMEM((t_past + s_len, QK_ROPE), BF16)],
        compiler_params=_params("parallel"),
        name="attn_lat" if latent else "attn_ctx",
    )(*args)


def _out_proj_kernel(a_ref, w_ref, x_ref, m_ref, o_ref):
    o_ref[...] = x_ref[...] + m_ref[5:6, :] * _dot(a_ref[...], w_ref[...])


def _out_proj(geo, a, w, x, mods):
    tm = TM_PROJ
    row = pl.BlockSpec((tm, D_MODEL), lambda i: (i, 0))
    return pl.pallas_call(
        _out_proj_kernel,
        grid=(geo.n_tok // tm,),
        in_specs=[row, pl.BlockSpec((D_MODEL, D_MODEL), lambda i: (0, 0)), row,
                  pl.BlockSpec((None, N_MOD, D_MODEL), lambda i: (geo.cond_row(i, tm), 0, 0))],
        out_specs=row,
        out_shape=jax.ShapeDtypeStruct((geo.n_tok, D_MODEL), F32),
        compiler_params=_params("parallel"),
        name="out_proj",
    )(a, w, x, mods)


def _shifted(geo, x_ref, xp_ref, xn_ref, m_ref, g_ref, tm):
    i = pl.program_id(0)
    g, shift, scale = g_ref[...], m_ref[3:4, :], m_ref[4:5, :]
    h = _modulate(x_ref[...], g, shift, scale)
    start = i * tm
    seq = jnp.where(start < geo.n_ctx, geo.s_ctx, geo.s_lat)
    rel = jnp.where(start < geo.n_ctx, start, start - geo.n_ctx)
    has_prev = (rel % seq) != 0
    has_next = ((rel + tm) % seq) != 0
    hp = _modulate(xp_ref[...], g, shift, scale)[7:8, :]
    hn = _modulate(xn_ref[...], g, shift, scale)[0:1, :]
    hp = jnp.where(has_prev, hp, 0.0)
    hn = jnp.where(has_next, hn, 0.0)
    r = lax.broadcasted_iota(jnp.int32, h.shape, 0)
    down = jnp.where(r == 0, hp, pltpu.roll(h, 1, 0))
    up = jnp.where(r == tm - 1, hn, pltpu.roll(h, tm - 1, 0))
    return h, 0.5 * (down + up) - h


def _halo_specs(geo, tm):
    nb8 = geo.n_tok // 8
    return [
        pl.BlockSpec((tm, D_MODEL), lambda i, *_: (i, 0)),
        pl.BlockSpec((8, D_MODEL), lambda i, *_: (jnp.maximum(i * (tm // 8) - 1, 0), 0)),
        pl.BlockSpec((8, D_MODEL), lambda i, *_: (jnp.minimum((i + 1) * (tm // 8), nb8 - 1), 0)),
        pl.BlockSpec((None, N_MOD, D_MODEL), lambda i, *_: (geo.cond_row(i, tm), 0, 0)),
        pl.BlockSpec((1, D_MODEL), lambda i, *_: (0, 0)),
    ]


def _rkv_kernel(x_ref, xp_ref, xn_ref, m_ref, g_ref, mu_ref, w_ref, o_ref, h_ref, xx_ref, *, geo, tm):
    @pl.when(pl.program_id(1) == 0)
    def _():
        h, xx = _shifted(geo, x_ref, xp_ref, xn_ref, m_ref, g_ref, tm)
        h_ref[...] = h
        xx_ref[...] = xx

    xm = (h_ref[...] + xx_ref[...] * mu_ref[...]).astype(BF16)
    o_ref[...] = _dot(xm, w_ref[pl.program_id(1)])


def _rkv(geo, x, mods, g, mu3, w3):
    tm = TM_RWKV
    return pl.pallas_call(
        functools.partial(_rkv_kernel, geo=geo, tm=tm),
        grid=(geo.n_tok // tm, 3),
        in_specs=_halo_specs(geo, tm) + [
            pl.BlockSpec((None, 1, D_MODEL), lambda i, p: (p, 0, 0)),
            pl.BlockSpec((3, D_MODEL, D_MODEL), lambda i, p: (0, 0, 0), pipeline_mode=pl.Buffered(1)),
        ],
        out_specs=pl.BlockSpec((None, tm, D_MODEL), lambda i, p: (p, i, 0)),
        out_shape=jax.ShapeDtypeStruct((3, geo.n_tok, D_MODEL), F32),
        scratch_shapes=[pltpu.VMEM((tm, D_MODEL), F32), pltpu.VMEM((tm, D_MODEL), F32)],
        compiler_params=_params("parallel", "arbitrary"),
        name="rwkv_rkv",
    )(x, x, x, mods, g, mu3, w3)


def _lora_kernel(x_ref, xp_ref, xn_ref, m_ref, g_ref, mug_ref, mud_ref, g1_ref, g2_ref,
                 wa1_ref, wa2_ref, w0_ref, a0_ref, gate_ref, dec_ref, asig_ref, *, geo, tm):
    h, xx = _shifted(geo, x_ref, xp_ref, xn_ref, m_ref, g_ref, tm)
    mix = lambda mu: (h + xx * mu).astype(BF16)
    zg = _dot(mix(mug_ref[...]), g1_ref[...])
    gate_ref[...] = _dot(_sigmoid(zg).astype(BF16), g2_ref[...])
    for d in range(2):
        zw = _dot(mix(mud_ref[2 * d]), wa1_ref[d])
        wl = w0_ref[d] + _dot(jnp.tanh(zw).astype(BF16), wa2_ref[d])
        dec_ref[d] = -LOG_DECAY_SCALE * _sigmoid(wl)
        za = _dot(mix(mud_ref[2 * d + 1]), wa1_ref[2 + d])
        asig_ref[d] = _sigmoid(a0_ref[d] + _dot(za.astype(BF16), wa2_ref[2 + d]))


def _lora(geo, x, mods, g, mu_g, mu_dir, g1, g2, wa1, wa2, w0, a0):
    tm = TM_RWKV
    full = lambda a: pl.BlockSpec(a.shape, lambda i: (0,) * a.ndim)
    tok2 = pl.BlockSpec((2, tm, D_MODEL), lambda i: (0, i, 0))
    return pl.pallas_call(
        functools.partial(_lora_kernel, geo=geo, tm=tm),
        grid=(geo.n_tok // tm,),
        in_specs=_halo_specs(geo, tm) + [full(a) for a in (mu_g, mu_dir, g1, g2, wa1, wa2, w0, a0)],
        out_specs=[pl.BlockSpec((tm, D_MODEL), lambda i: (i, 0)), tok2, tok2],
        out_shape=[jax.ShapeDtypeStruct((geo.n_tok, D_MODEL), F32),
                   jax.ShapeDtypeStruct((2, geo.n_tok, D_MODEL), F32),
                   jax.ShapeDtypeStruct((2, geo.n_tok, D_MODEL), F32)],
        compiler_params=_params("parallel"),
        name="rwkv_lora",
    )(x, x, x, mods, g, mu_g, mu_dir, g1, g2, wa1, wa2, w0, a0)


def _head_sums(x, ones_blk):
    return _dot(x.astype(BF16), ones_blk)


def _head_ones(width):
    r = lax.broadcasted_iota(jnp.int32, (width, width), 0)
    c = lax.broadcasted_iota(jnp.int32, (width, width), 1)
    same = (r // RWKV_HEAD) == (c // RWKV_HEAD)
    return same, jnp.where(same, 1.0, 0.0).astype(BF16)


def _wkv_kernel(*refs, geo, tb, aliased):
    r_ref, k_ref, v_ref, dec_ref, asig_ref, kk_ref, ka_ref, rk_ref, s0_ref = refs[:9]
    y_ref, bon_ref, sf_fwd_ref, sf_bwd_ref, s_ref = refs[9 + (2 if aliased else 0):]
    d = pl.program_id(0)
    i = pl.program_id(1)
    nblk = pl.num_programs(1)
    blk = i + d * (nblk - 1 - 2 * i)
    start = blk * tb
    seq = jnp.where(start < geo.n_ctx, geo.s_ctx, geo.s_lat)
    rel = jnp.where(start < geo.n_ctx, start, start - geo.n_ctx)
    at_lo = (rel % seq) == 0
    at_hi = ((rel + tb) % seq) == 0
    first = jnp.where(d == 0, at_lo, at_hi)
    last = jnp.where(d == 0, at_hi, at_lo)

    @pl.when(first)
    def _():
        s_ref[...] = s0_ref[...]

    nch = tb // CHUNK
    sgn = 1 - 2 * d
    fwd = (d == 0).astype(F32)
    row = lax.broadcasted_iota(jnp.int32, (CHUNK, PAIR), 0)
    lane = lax.broadcasted_iota(jnp.int32, (CHUNK, PAIR), 1)
    sidx = lane & (RWKV_HEAD - 1)
    delta = (row - sidx) * sgn
    strict = delta > 0
    incl = delta >= 0
    eye = jnp.where(row == sidx, 1.0, 0.0)
    off_masks = []
    m = 1
    while m < CHUNK:
        off_masks.append(strict & ((row // (2 * m)) == (sidx // (2 * m))) & ((row // m) != (sidx // m)))
        m *= 2
    head0 = lane < RWKV_HEAD
    tr = lax.broadcasted_iota(jnp.int32, (CHUNK, 3 * CHUNK), 0)
    ts = lax.broadcasted_iota(jnp.int32, (CHUNK, 3 * CHUNK), 1) & (CHUNK - 1)
    tri3 = jnp.where((tr - ts) * sgn >= 0, 1.0, 0.0).astype(BF16)
    same_head, ones_blk = _head_ones(PAIR)

    def expand(x):
        return jnp.concatenate([jnp.where(head0, x, 0.0), jnp.where(head0, 0.0, x)], axis=0)

    def chunk_body(ci, carry):
        cc = ci * sgn + d * (nch - 1)
        rows = pl.ds(pl.multiple_of(cc * CHUNK, CHUNK), CHUNK)

        def group_body(pg, carry2):
            ps = [pg * PAIR_GROUP + q for q in range(PAIR_GROUP)]
            cols = [pl.ds(pl.multiple_of(p * PAIR, PAIR), PAIR) for p in ps]
            each = lambda f, *ls: [f(*xs) for xs in zip(*ls)]
            r = [r_ref[rows, c] for c in cols]
            k = [k_ref[rows, c] for c in cols]
            v = [v_ref[rows, c] for c in cols]
            logw = [dec_ref[rows, c] for c in cols]
            a = [asig_ref[rows, c] for c in cols]
            kk = [x * kk_ref[:, c] for x, c in zip(k, cols)]
            kd = [x * (1.0 + (y - 1.0) * ka_ref[:, c]) for x, y, c in zip(k, a, cols)]
            sums = [_head_sums(jnp.concatenate([x * x, y * z * rk_ref[:, c]], axis=0), ones_blk)
                    for x, y, z, c in zip(kk, r, kd, cols)]
            kk = each(lambda x, s: x / jnp.maximum(jnp.sqrt(s[:CHUNK]), 1e-12), kk, sums)
            b_in = each(lambda x, y: x * y, kk, a)
            for c, s, y in zip(cols, sums, v):
                bon_ref[rows, c] = s[CHUNK:] * y

            l1 =[x.astype(BF16) for x in logw]
            e1 = each(lambda x, y: x - y.astype(F32), logw, l1)
            l2 = [x.astype(BF16) for x in e1]
            l3 = each(lambda x, y: (x - y.astype(F32)).astype(BF16), e1, l2)
            cum = each(lambda x, y, z: _dot(tri3, jnp.concatenate([x, y, z], axis=0)), l1, l2, l3)
            tot = [fwd * x[CHUNK - 1:CHUNK, :] + (1.0 - fwd) * x[0:1, :] for x in cum]
            c_inv = [jnp.exp(-x) for x in cum]
            at = each(lambda x, y, z: -x * jnp.exp(y - z), kk, cum, logw)
            rt = each(lambda x, y: x * jnp.exp(y), r, cum)
            c_end = each(lambda x, y: jnp.exp(x - y), tot, cum)

            lhs = each(lambda x, y: jnp.concatenate([x, y], axis=0).astype(BF16), at, rt)
            rhs = each(lambda x, y, z: jnp.concatenate([expand(x * z), expand(y * z)], axis=0).astype(BF16),
                       b_in, kd, c_inv)
            g = each(_dot_nt, lhs, rhs)
            ab = [jnp.where(strict, x[:CHUNK, :PAIR], 0.0) for x in g]
            ak = [jnp.where(strict, x[:CHUNK, PAIR:], 0.0) for x in g]
            rbk = [jnp.concatenate([jnp.where(incl, x[CHUNK:, :PAIR], 0.0),
                                    jnp.where(incl, x[CHUNK:, PAIR:], 0.0)], axis=1).astype(BF16) for x in g]

            t_inv = [eye + jnp.where(off_masks[0], x, 0.0) for x in ab]
            for off in off_masks[1:]:
                lx = each(lambda x, t: _dot(jnp.where(off, x, 0.0).astype(BF16), expand(t).astype(BF16)), ab, t_inv)
                t_inv = each(lambda t, x: t + _dot(t.astype(BF16), expand(x).astype(BF16)), t_inv, lx)

            s2 = [s_ref[p] for p in ps]
            ars = each(lambda x, s: _dot_nt(x, s.astype(BF16)), lhs, s2)
            vexp = [expand(x).astype(BF16) for x in v]
            rhs_u = each(lambda x, y, z: x[:CHUNK] + _dot(y.astype(BF16), z), ars, ak, vexp)
            u = each(lambda t, x: _dot(t.astype(BF16), expand(x).astype(BF16)), t_inv, rhs_u)
            uv = each(lambda x, y: jnp.concatenate([expand(x).astype(BF16), y], axis=0), u, vexp)
            y_out = each(lambda x, y, z: x[CHUNK:] + _dot(y, z), ars, rbk, uv)
            for c, x in zip(cols, y_out):
                y_ref[rows, c] = x
            uv_t = each(lambda x, y: jnp.concatenate([x, y], axis=0).T.astype(BF16), u, v)
            bk = each(lambda x, y, z: jnp.concatenate([x * z, y * z], axis=0).astype(BF16), b_in, kd, c_end)
            upd = each(_dot, uv_t, bk)
            for p, s, x, t in zip(ps, s2, upd, tot):
                s_ref[p] = s * jnp.exp(t) + jnp.where(same_head, x, 0.0)
            return carry2

        lax.fori_loop(0, N_PAIRS // PAIR_GROUP, group_body, 0)
        return carry

    lax.fori_loop(0, nch, chunk_body, 0)

    def write_states(dst_ref):
        for p in range(N_PAIRS):
            s = s_ref[p]
            dst_ref[2 * p] = s[:RWKV_HEAD, :RWKV_HEAD]
            dst_ref[2 * p + 1] = s[RWKV_HEAD:, RWKV_HEAD:]

    ends_ctx = last & (start < geo.n_ctx)

    @pl.when(ends_ctx & (d == 0))
    def _():
        write_states(sf_fwd_ref)

    @pl.when(ends_ctx & (d == 1))
    def _():
        write_states(sf_bwd_ref)


def _wkv(geo, rkv, dec, asig, kk_c, ka_c, rk_c, s0, j, n_layers, prev_states):
    tb = TB_WKV
    nblk = geo.n_tok // tb
    blk = lambda d, i: i + d * (nblk - 1 - 2 * i)
    tok = lambda which: pl.BlockSpec((None, tb, D_MODEL), lambda d, i: (which, blk(d, i), 0))
    perdir = pl.BlockSpec((None, tb, D_MODEL), lambda d, i: (d, blk(d, i), 0))
    const = pl.BlockSpec((1, D_MODEL), lambda d, i: (0, 0))
    last_seq = geo.b_ctx - 1
    seq = lambda d, i: jnp.minimum(blk(d, i) * tb // geo.s_ctx, last_seq)
    final = lambda which: pl.BlockSpec(
        (None, None, RWKV_HEADS, RWKV_HEAD, RWKV_HEAD),
        lambda d, i: (jnp.where(d == which, seq(d, i), last_seq), j, 0, 0, 0))
    final_shape = jax.ShapeDtypeStruct((geo.b_ctx, n_layers, RWKV_HEADS, RWKV_HEAD, RWKV_HEAD), F32)
    in_specs = [tok(0), tok(1), tok(2), perdir, perdir, const, const, const,
                pl.BlockSpec((None, None, N_PAIRS, PAIR, PAIR), lambda d, i: (d, geo.cond_row(blk(d, i), tb), 0, 0, 0))]
    args = [rkv, rkv, rkv, dec, asig, kk_c, ka_c, rk_c, s0]
    aliases = {}
    if prev_states is not None:
        in_specs += [pl.BlockSpec(memory_space=pl.ANY)] * 2
        aliases = {len(args): 2, len(args) + 1: 3}
        args += list(prev_states)
    return pl.pallas_call(
        functools.partial(_wkv_kernel, geo=geo, tb=tb, aliased=prev_states is not None),
        grid=(2, nblk),
        in_specs=in_specs,
        out_specs=[perdir, perdir, final(0), final(1)],
        out_shape=[jax.ShapeDtypeStruct((2, geo.n_tok, D_MODEL), F32),
                   jax.ShapeDtypeStruct((2, geo.n_tok, D_MODEL), F32), final_shape, final_shape],
        input_output_aliases=aliases,
        scratch_shapes=[pltpu.VMEM((N_PAIRS, PAIR, PAIR), F32)],
        compiler_params=_params("arbitrary", "arbitrary"),
        name="wkv_chunked",
    )(*args)


def _rwkv_out_kernel(y_ref, bon_ref, gate_ref, lw_ref, lb_ref, w_ref, x_ref, m_ref, o_ref, z_ref):
    _, ones_blk = _head_ones(GN_SLAB)
    inv_n = 1.0 / RWKV_HEAD
    for s in range(D_MODEL // GN_SLAB):
        cols = slice(s * GN_SLAB, (s + 1) * GN_SLAB)
        y = y_ref[0, :, cols] + y_ref[1, :, cols]
        yc = y - _head_sums(y, ones_blk) * inv_n
        var = _head_sums(yc * yc, ones_blk) * inv_n
        z = yc * lax.rsqrt(var + GN_EPS) * lw_ref[:, cols] + lb_ref[:, cols] + bon_ref[0, :, cols] + bon_ref[1, :, cols]
        z_ref[:, cols] = (z * gate_ref[:, cols]).astype(BF16)
    o_ref[...] = x_ref[...] + m_ref[5:6, :] * _dot(z_ref[...], w_ref[...])


def _rwkv_out(geo, y, bon, gate, lw, lb, w, x, mods):
    tm = TM_OUT
    row = pl.BlockSpec((tm, D_MODEL), lambda i: (i, 0))
    both = pl.BlockSpec((2, tm, D_MODEL), lambda i: (0, i, 0))
    const = pl.BlockSpec((1, D_MODEL), lambda i: (0, 0))
    return pl.pallas_call(
        _rwkv_out_kernel,
        grid=(geo.n_tok // tm,),
        in_specs=[both, both, row, const, const, pl.BlockSpec((D_MODEL, D_MODEL), lambda i: (0, 0)), row,
                  pl.BlockSpec((None, N_MOD, D_MODEL), lambda i: (geo.cond_row(i, tm), 0, 0))],
        out_specs=row,
        out_shape=jax.ShapeDtypeStruct((geo.n_tok, D_MODEL), F32),
        scratch_shapes=[pltpu.VMEM((tm, D_MODEL), BF16)],
        compiler_params=_params("parallel"),
        name="rwkv_out",
    )(y, bon, gate, lw, lb, w, x, mods)


def _state_to_pairs(s):
    b = s.shape[0]
    s5 = s.reshape(b, N_PAIRS, 2, RWKV_HEAD, RWKV_HEAD)
    z = jnp.zeros_like(s5[:, :, 0])
    top = jnp.concatenate([s5[:, :, 0], z], axis=-1)
    bot = jnp.concatenate([z, s5[:, :, 1]], axis=-1)
    return jnp.concatenate([top, bot], axis=-2)


def _rope_tables(geo):
    rows = geo.s_lat // GRID_W
    row = jnp.repeat(jnp.arange(rows, dtype=F32), GRID_W)
    col = jnp.tile(jnp.arange(GRID_W, dtype=F32), rows)
    inv = ROPE_THETA ** (-jnp.arange(ROPE_FREQS, dtype=F32) / ROPE_FREQS)
    ang = jnp.stack([row[:, None] * inv, col[:, None] * inv], axis=1)
    ang = jnp.broadcast_to(ang[:, :, None, :], (geo.s_lat, 2, 2, ROPE_FREQS)).reshape(geo.s_lat, QK_ROPE)
    cos = jnp.concatenate([jnp.ones((geo.n_ctx, QK_ROPE), F32), jnp.tile(jnp.cos(ang), (geo.b_lat, 1))], axis=0)
    sin = jnp.concatenate([jnp.zeros((geo.n_ctx, QK_ROPE), F32), jnp.tile(jnp.sin(ang), (geo.b_lat, 1))], axis=0)
    return cos, sin


def _rot_cols(w):
    w4 = w.reshape(w.shape[:-1] + (2, 2, ROPE_FREQS))
    return jnp.stack([-w4[..., 1, :], w4[..., 0, :]], axis=-2).reshape(w.shape)


def _mla_layer(geo, x, mods, g, j, cos, sin, cache_ckv, cache_krope, p):
    w_dkv = p['mla_w_dkv'][j]
    w_cat = jnp.concatenate([p['mla_w_dq'][j], w_dkv, _rot_cols(w_dkv[:, KV_LORA:])], axis=1).astype(BF16)
    cq, ckv, ckvb, kr = _mla_proj(geo, x, mods, g, w_cat, p['mla_q_norm'][j][None], p['mla_kv_norm'][j][None],
                                  cos, sin)
    w_uq = p['mla_w_uq'][j].reshape(Q_LORA, MLA_HEADS, QK_NOPE + QK_ROPE).transpose(1, 0, 2)
    wq = w_uq.astype(BF16)
    wqr = _rot_cols(w_uq[..., QK_NOPE:]).astype(BF16)
    wkv = p['mla_w_ukv'][j].reshape(KV_LORA, MLA_HEADS, QK_NOPE + V_DIM).transpose(1, 0, 2).astype(BF16)
    o_ctx = _attn(geo, cq, ckvb, kr, wq, wqr, wkv, latent=False)
    o = _attn(geo, cq, ckvb, kr, wq, wqr, wkv, latent=True,
              past_ckv=cache_ckv[:, j], past_kr=cache_krope[:, j], cos=cos, sin=sin, o_ctx=o_ctx)
    x = _out_proj(geo, o, p['mla_w_o'][j].astype(BF16), x, mods)
    new_ckv = ckv[:geo.n_ctx].reshape(geo.b_ctx, geo.s_ctx, KV_LORA)
    new_kr = kr[:geo.n_ctx].reshape(geo.b_ctx, geo.s_ctx, QK_ROPE)
    return x, new_ckv, new_kr


def _rwkv_layer(geo, x, mods, g, j, s0_fwd, s0_bwd, prev_states, p):
    mu = p['rwkv_mu'][j]
    mu_dir = p['rwkv_mu_dir'][j].reshape(4, 1, D_MODEL)
    w3 = jnp.stack([p['rwkv_w_r'][j], p['rwkv_w_k'][j], p['rwkv_w_v'][j]]).astype(BF16)
    rkv = _rkv(geo, x, mods, g, mu[:3, None, :], w3)
    pad1 = lambda w: jnp.pad(w, ((0, 0), (0, 0), (0, LORA_PAD - w.shape[-1])))
    pad2 = lambda w: jnp.pad(w, ((0, 0), (0, LORA_PAD - w.shape[-2]), (0, 0)))
    wa1 = jnp.concatenate([pad1(p['rwkv_w1'][j]), pad1(p['rwkv_a1'][j])]).astype(BF16)
    wa2 = jnp.concatenate([pad2(p['rwkv_w2'][j]), pad2(p['rwkv_a2'][j])]).astype(BF16)
    gate, dec, asig = _lora(geo, x, mods, g, mu[3:4], mu_dir, p['rwkv_g1'][j].astype(BF16),
                            p['rwkv_g2'][j].astype(BF16), wa1, wa2,
                            p['rwkv_w0'][j][:, None, :], p['rwkv_a0'][j][:, None, :])
    zero = jnp.zeros((2, 1, N_PAIRS, PAIR, PAIR), F32)
    s0 = jnp.concatenate([zero, jnp.stack([_state_to_pairs(s0_fwd), _state_to_pairs(s0_bwd)])], axis=1)
    y, bon, sf, sb = _wkv(geo, rkv, dec, asig, p['rwkv_k_k'][j][None], p['rwkv_k_a'][j][None],
                          p['rwkv_r_k'][j].reshape(1, D_MODEL), s0, j, p['rwkv_mu'].shape[0], prev_states)
    x = _rwkv_out(geo, y, bon, gate, p['rwkv_ln_w'][j][None], p['rwkv_ln_b'][j][None],
                  p['rwkv_w_o'][j].astype(BF16), x, mods)
    return x, (sf, sb)


def kernel(x_prompt, x_sample, cache_ckv, cache_krope, state_wkv_fwd, state_wkv_bwd, c, c_ctx, w_ada, b_ada, norm_sub, norm_final, w_ffn_in, w_ffn_out, mla_w_dq, mla_q_norm, mla_w_uq, mla_w_dkv, mla_kv_norm, mla_w_ukv, mla_w_o, rwkv_mu, rwkv_mu_dir, rwkv_w_r, rwkv_w_k, rwkv_w_v, rwkv_w0, rwkv_w1, rwkv_w2, rwkv_a0, rwkv_a1, rwkv_a2, rwkv_g1, rwkv_g2, rwkv_k_k, rwkv_k_a, rwkv_r_k, rwkv_ln_w, rwkv_ln_b, rwkv_w_o):
    p = dict(mla_w_dq=mla_w_dq, mla_q_norm=mla_q_norm, mla_w_uq=mla_w_uq, mla_w_dkv=mla_w_dkv,
             mla_kv_norm=mla_kv_norm, mla_w_ukv=mla_w_ukv, mla_w_o=mla_w_o,
             rwkv_mu=rwkv_mu, rwkv_mu_dir=rwkv_mu_dir, rwkv_w_r=rwkv_w_r, rwkv_w_k=rwkv_w_k,
             rwkv_w_v=rwkv_w_v, rwkv_w0=rwkv_w0, rwkv_w1=rwkv_w1, rwkv_w2=rwkv_w2,
             rwkv_a0=rwkv_a0, rwkv_a1=rwkv_a1, rwkv_a2=rwkv_a2, rwkv_g1=rwkv_g1, rwkv_g2=rwkv_g2,
             rwkv_k_k=rwkv_k_k, rwkv_k_a=rwkv_k_a, rwkv_r_k=rwkv_r_k,
             rwkv_ln_w=rwkv_ln_w, rwkv_ln_b=rwkv_ln_b, rwkv_w_o=rwkv_w_o)
    b_ctx, s_ctx, _ = x_prompt.shape
    b_lat, s_lat, _ = x_sample.shape
    geo = _Geom(b_ctx, s_ctx, b_lat, s_lat)
    assert geo.n_ctx % s_lat == 0 and s_ctx % TM_RWKV == 0 and s_lat % TM_FFN == 0
    assert s_ctx == TB_WKV and s_lat % TB_WKV == 0

    x = jnp.concatenate([x_prompt.reshape(geo.n_ctx, D_MODEL), x_sample.reshape(geo.n_lat, D_MODEL)], axis=0)
    cond = jnp.concatenate([c_ctx[None], c, jnp.zeros((COND_ROWS - 1 - b_lat, D_MODEL), F32)], axis=0)
    mods_all = _ada(cond, w_ada, b_ada).reshape(DEPTH, COND_ROWS, N_MOD, D_MODEL)
    cos, sin = _rope_tables(geo)
    nf = norm_final[None]
    w_in = w_ffn_in[0, 0].astype(BF16)
    w_out = w_ffn_out[0, 0].astype(BF16)

    ckv_l, kr_l, states = [], [], None
    for l in range(DEPTH):
        j = l // N_MIXERS
        mods = mods_all[l]
        x, w_in, w_out = _ffn(geo, x, mods, norm_sub[l, 0][None], w_in, w_out, nf, sub=0, final=False,
                              nxt=(w_ffn_in, w_ffn_out, l, 1))
        if l % N_MIXERS == 0:
            x, ckv, kr = _mla_layer(geo, x, mods, norm_sub[l, 1][None], j, cos, sin, cache_ckv, cache_krope, p)
            ckv_l.append(ckv)
            kr_l.append(kr)
        else:
            x, states = _rwkv_layer(geo, x, mods, norm_sub[l, 1][None], j,
                                    state_wkv_fwd[:, j], state_wkv_bwd[:, j], states, p)
        if l < DEPTH - 1:
            x, w_in, w_out = _ffn(geo, x, mods, norm_sub[l, 2][None], w_in, w_out, nf, sub=2, final=False,
                                  nxt=(w_ffn_in, w_ffn_out, l + 1, 0))
        else:
            x, = _ffn(geo, x, mods, norm_sub[l, 2][None], w_in, w_out, nf, sub=2, final=True)

    y_prompt = x[:geo.n_ctx].reshape(b_ctx, s_ctx, D_MODEL)
    y_sample = x[geo.n_ctx:].reshape(b_lat, s_lat, D_MODEL)
    return (y_prompt, y_sample, jnp.stack(ckv_l, axis=1), jnp.stack(kr_l, axis=1),
            states[0], states[1])
```

```python
import functools

import jax
import jax.numpy as jnp
from jax import lax
from jax.experimental import pallas as pl
from jax.experimental.pallas import tpu as pltpu

D_MODEL = 2048
DEPTH = 4
N_MIXERS = 2
D_FF = 5632
N_MOD = 9
RMS_EPS = 1e-6
MLA_HEADS = 16
Q_LORA = 512
KV_LORA = 512
QK_NOPE = 128
QK_ROPE = 64
V_DIM = 128
ROPE_FREQS = QK_ROPE // 4
ROPE_THETA = 10000.0
GRID_W = 64
ATTN_SCALE = (QK_NOPE + QK_ROPE) ** -0.5
RWKV_HEAD = 64
RWKV_HEADS = D_MODEL // RWKV_HEAD
GN_EPS = 64e-5
LOG_DECAY_SCALE = 0.6065306597126334
LORA_PAD = 128

COND_ROWS = 8
VMEM_LIMIT = 56 * 1024 * 1024

TM_FFN = 512
TF_FFN = 512
CAST_TILE = 256
TM_PROJ = 512
TM_RWKV = 256
TN_ADA = 1024
Q_TILE = 256
ATTN_LOCKSTEP = 4
CHUNK = 64
PAIR = 2 * RWKV_HEAD
N_PAIRS = D_MODEL // PAIR
PAIR_GROUP = 16
TB_WKV = 256
TM_OUT = 256
GN_SLAB = 256

BF16 = jnp.bfloat16
F32 = jnp.float32


def _params(*sem):
    return pltpu.CompilerParams(dimension_semantics=sem, vmem_limit_bytes=VMEM_LIMIT)


def _sigmoid(x):
    return 1.0 / (1.0 + jnp.exp(-x))


def _modulate(x, g, shift, scale):
    ms = jnp.mean(x * x, axis=-1, keepdims=True)
    return (x * lax.rsqrt(ms + RMS_EPS) * g) * (1.0 + scale) + shift


def _rms(x, w):
    ms = jnp.mean(x * x, axis=-1, keepdims=True)
    return x * lax.rsqrt(ms + RMS_EPS) * w


def _dot(a, b):
    return jnp.dot(a, b, preferred_element_type=F32)


def _dot_nt(a, b):
    return lax.dot_general(a, b, (((1,), (1,)), ((), ())), preferred_element_type=F32)


class _Geom:
    def __init__(self, b_ctx, s_ctx, b_lat, s_lat):
        self.b_ctx, self.s_ctx, self.b_lat, self.s_lat = b_ctx, s_ctx, b_lat, s_lat
        self.n_ctx = b_ctx * s_ctx
        self.n_lat = b_lat * s_lat
        self.n_tok = self.n_ctx + self.n_lat

    def cond_row(self, i, tm):
        start = i * tm
        return jnp.where(start < self.n_ctx, 0, 1 + (start - self.n_ctx) // self.s_lat)


def _ada_kernel(c_ref, w_ref, b_ref, o_ref):
    c = c_ref[...]
    s = (c * _sigmoid(c)).astype(BF16)
    o_ref[...] = _dot(s, w_ref[...].astype(BF16)) + b_ref[...]


def _ada(cond, w_ada, b_ada):
    n = N_MOD * D_MODEL
    return pl.pallas_call(
        _ada_kernel,
        grid=(DEPTH, n // TN_ADA),
        in_specs=[
            pl.BlockSpec((COND_ROWS, D_MODEL), lambda l, j: (0, 0)),
            pl.BlockSpec((None, D_MODEL, TN_ADA), lambda l, j: (l, 0, j)),
            pl.BlockSpec((None, 1, TN_ADA), lambda l, j: (l, 0, j)),
        ],
        out_specs=pl.BlockSpec((None, COND_ROWS, TN_ADA), lambda l, j: (l, 0, j)),
        out_shape=jax.ShapeDtypeStruct((DEPTH, COND_ROWS, n), F32),
        compiler_params=_params("parallel", "parallel"),
        name="ada",
    )(cond, w_ada, b_ada.reshape(DEPTH, 1, n))


def _ffn_kernel(*refs, sub, final, cast_next):
    if cast_next:
        (x_ref, m_ref, g_ref, wg_ref, wu_ref, wo_ref, nf_ref, ci_ref, co_ref,
         o_ref, cib_ref, cob_ref, h_ref, acc_ref) = refs
    else:
        x_ref, m_ref, g_ref, wg_ref, wu_ref, wo_ref, nf_ref, o_ref, h_ref, acc_ref = refs
    f = pl.program_id(1)

    def swiglu_slice(h):
        gate = _dot(h, wg_ref[...])
        up = _dot(h, wu_ref[...])
        act = (gate * _sigmoid(gate) * up).astype(BF16)
        return _dot(act, wo_ref[...])

    @pl.when(f == 0)
    def _():
        h = _modulate(x_ref[...], g_ref[...], m_ref[3 * sub:3 * sub + 1, :], m_ref[3 * sub + 1:3 * sub + 2, :])
        h = h.astype(BF16)
        h_ref[...] = h
        acc_ref[...] = swiglu_slice(h)

    @pl.when(f > 0)
    def _():
        acc_ref[...] += swiglu_slice(h_ref[...])

    if cast_next:
        step = pl.program_id(0) * pl.num_programs(1) + f
        n_in, n_out = cast_next

        @pl.when(step < n_in)
        def _():
            cib_ref[...] = ci_ref[...].astype(BF16)

        @pl.when((step >= n_in) & (step < n_in + n_out))
        def _():
            cob_ref[...] = co_ref[...].astype(BF16)

    @pl.when(f == pl.num_programs(1) - 1)
    def _():
        y = x_ref[...] + 0.5 * m_ref[3 * sub + 2:3 * sub + 3, :] * acc_ref[...]
        if final:
            y = _rms(y, nf_ref[...])
        o_ref[...] = y


def _ffn(geo, x, mods, g, w_in, w_out, nf, *, sub, final, nxt=None):
    tm, tf = TM_FFN, TF_FFN
    nf_blocks = D_FF // tf
    n_in, n_out = 2 * D_FF // CAST_TILE, D_FF // CAST_TILE
    step = lambda i, f: i * nf_blocks + f
    in_blk = lambda i, f: jnp.minimum(step(i, f), n_in - 1)
    out_blk = lambda i, f: jnp.clip(step(i, f) - n_in, 0, n_out - 1)
    in_specs = [
        pl.BlockSpec((tm, D_MODEL), lambda i, f: (i, 0)),
        pl.BlockSpec((None, N_MOD, D_MODEL), lambda i, f: (geo.cond_row(i, tm), 0, 0)),
        pl.BlockSpec((1, D_MODEL), lambda i, f: (0, 0)),
        pl.BlockSpec((D_MODEL, tf), lambda i, f: (0, f)),
        pl.BlockSpec((D_MODEL, tf), lambda i, f: (0, f + nf_blocks)),
        pl.BlockSpec((tf, D_MODEL), lambda i, f: (f, 0)),
        pl.BlockSpec((1, D_MODEL), lambda i, f: (0, 0)),
    ]
    out_specs = [pl.BlockSpec((tm, D_MODEL), lambda i, f: (i, 0))]
    out_shape = [jax.ShapeDtypeStruct((geo.n_tok, D_MODEL), F32)]
    args = [x, mods, g, w_in, w_in, w_out, nf]
    if nxt is not None:
        w_in_all, w_out_all, layer, half = nxt
        assert (geo.n_tok // tm) * nf_blocks >= n_in + n_out
        in_specs += [
            pl.BlockSpec((None, None, D_MODEL, CAST_TILE), lambda i, f: (layer, half, 0, in_blk(i, f))),
            pl.BlockSpec((None, None, CAST_TILE, D_MODEL), lambda i, f: (layer, half, out_blk(i, f), 0)),
        ]
        out_specs += [
            pl.BlockSpec((D_MODEL, CAST_TILE), lambda i, f: (0, in_blk(i, f))),
            pl.BlockSpec((CAST_TILE, D_MODEL), lambda i, f: (out_blk(i, f), 0)),
        ]
        out_shape += [jax.ShapeDtypeStruct((D_MODEL, 2 * D_FF), BF16), jax.ShapeDtypeStruct((D_FF, D_MODEL), BF16)]
        args += [w_in_all, w_out_all]
    return pl.pallas_call(
        functools.partial(_ffn_kernel, sub=sub, final=final, cast_next=(n_in, n_out) if nxt is not None else None),
        grid=(geo.n_tok // tm, nf_blocks),
        in_specs=in_specs,
        out_specs=out_specs,
        out_shape=out_shape,
        scratch_shapes=[pltpu.VMEM((tm, D_MODEL), BF16), pltpu.VMEM((tm, D_MODEL), F32)],
        compiler_params=_params("arbitrary", "arbitrary"),
        name="ffn",
    )(*args)


def _mla_proj_kernel(x_ref, m_ref, g_ref, w_ref, qn_ref, kvn_ref, cos_ref, sin_ref,
                     cq_ref, ckv_ref, ckvb_ref, kr_ref):
    h = _modulate(x_ref[...], g_ref[...], m_ref[3:4, :], m_ref[4:5, :]).astype(BF16)
    z = _dot(h, w_ref[...])
    cq_ref[...] = _rms(z[:, :Q_LORA], qn_ref[...]).astype(BF16)
    ckv = _rms(z[:, Q_LORA:Q_LORA + KV_LORA], kvn_ref[...])
    ckv_ref[...] = ckv
    ckvb_ref[...] = ckv.astype(BF16)
    o = Q_LORA + KV_LORA
    kr_ref[...] = z[:, o:o + QK_ROPE] * cos_ref[...] + z[:, o + QK_ROPE:o + 2 * QK_ROPE] * sin_ref[...]


def _mla_proj(geo, x, mods, g, w_cat, qn, kvn, cos, sin):
    tm = TM_PROJ
    n_out = w_cat.shape[1]
    row = lambda i: (i, 0)
    fix = lambda i: (0, 0)
    return pl.pallas_call(
        _mla_proj_kernel,
        grid=(geo.n_tok // tm,),
        in_specs=[
            pl.BlockSpec((tm, D_MODEL), row),
            pl.BlockSpec((None, N_MOD, D_MODEL), lambda i: (geo.cond_row(i, tm), 0, 0)),
            pl.BlockSpec((1, D_MODEL), fix),
            pl.BlockSpec((D_MODEL, n_out), fix),
            pl.BlockSpec((1, Q_LORA), fix),
            pl.BlockSpec((1, KV_LORA), fix),
            pl.BlockSpec((tm, QK_ROPE), row),
            pl.BlockSpec((tm, QK_ROPE), row),
        ],
        out_specs=[
            pl.BlockSpec((tm, Q_LORA), row),
            pl.BlockSpec((tm, KV_LORA), row),
            pl.BlockSpec((tm, KV_LORA), row),
            pl.BlockSpec((tm, QK_ROPE), row),
        ],
        out_shape=[
            jax.ShapeDtypeStruct((geo.n_tok, Q_LORA), BF16),
            jax.ShapeDtypeStruct((geo.n_tok, KV_LORA), F32),
            jax.ShapeDtypeStruct((geo.n_tok, KV_LORA), BF16),
            jax.ShapeDtypeStruct((geo.n_tok, QK_ROPE), F32),
        ],
        compiler_params=_params("parallel"),
        name="mla_proj",
    )(x, mods, g, w_cat, qn, kvn, cos, sin)


def _attn_kernel(*refs, s_len, t_past, rope):
    if rope:
        (cq_ref, ckvb_ref, kr_ref, pckv_ref, pkr_ref, cos_ref, sin_ref,
         wq_ref, wqr_ref, wkv_ref, _, o_ref, kv_all, kr_all) = refs
    else:
        cq_ref, ckvb_ref, kr_ref, wq_ref, wkv_ref, o_ref, kv_all, kr_all = refs
    if t_past:
        kv_all[0:t_past, :] = pckv_ref[...].astype(BF16)
        kr_all[0:t_past, :] = pkr_ref[...].astype(BF16)
    kv_all[t_past:, :] = ckvb_ref[...]
    kr_all[t_past:, :] = kr_ref[...].astype(BF16)

    n_q = s_len // Q_TILE
    hg = max(1, ATTN_LOCKSTEP // n_q)
    row_slices = [slice(qb * Q_TILE, (qb + 1) * Q_TILE) for qb in range(n_q)]

    def head_group(g, carry):
        heads = [g * hg + i for i in range(hg)]
        kv = kv_all[...]
        krb = kr_all[...]
        kvp = [_dot(kv, wkv_ref[hd]) for hd in heads]
        kn = [x[:, :QK_NOPE].astype(BF16) for x in kvp]
        v = [x[:, QK_NOPE:].astype(BF16) for x in kvp]
        items = [(i, qb) for i in range(hg) for qb in range(n_q)]
        cq = [cq_ref[rows, :] for rows in row_slices]
        q = [_dot(cq[qb], wq_ref[heads[i]]) for i, qb in items]
        qr = [x[:, QK_NOPE:] for x in q]
        if rope:
            rot = [_dot(cq[qb], wqr_ref[heads[i]]) for i, qb in items]
            qr = [x * cos_ref[row_slices[qb], :] + y * sin_ref[row_slices[qb], :]
                  for x, y, (i, qb) in zip(qr, rot, items)]
        s = [(_dot_nt(x[:, :QK_NOPE].astype(BF16), kn[i]) + _dot_nt(y.astype(BF16), krb)) * ATTN_SCALE
             for x, y, (i, qb) in zip(q, qr, items)]
        p = [jnp.exp(x - jnp.max(x, axis=-1, keepdims=True)) for x in s]
        pr = [(x * (1.0 / jnp.sum(x, axis=-1, keepdims=True))).astype(BF16) for x in p]
        o = [_dot(x, v[i]).astype(BF16) for x, (i, qb) in zip(pr, items)]
        for x, (i, qb) in zip(o, items):
            o_ref[row_slices[qb], pl.ds(pl.multiple_of(heads[i] * V_DIM, V_DIM), V_DIM)] = x
        return carry

    lax.fori_loop(0, MLA_HEADS // hg, head_group, 0)


def _attn(geo, cq, ckvb, kr, wq, wqr, wkv, *, latent, past_ckv=None, past_kr=None, cos=None, sin=None, o_ctx=None):
    if latent:
        nb, s_len, off = geo.b_lat, geo.s_lat, geo.n_ctx // geo.s_lat
        t_past = past_ckv.shape[1]
    else:
        nb, s_len, off, t_past = geo.b_ctx, geo.s_ctx, 0, 0
    row = lambda b: (b + off, 0)
    fix3 = lambda b: (0, 0, 0)
    tok = lambda width: pl.BlockSpec((s_len, width), row)
    wspec = lambda w: pl.BlockSpec(w.shape, fix3)
    in_specs = [tok(Q_LORA), tok(KV_LORA), tok(QK_ROPE)]
    args = [cq, ckvb, kr]
    if latent:
        in_specs += [pl.BlockSpec((None, t_past, KV_LORA), lambda b: (b, 0, 0)),
                     pl.BlockSpec((None, t_past, QK_ROPE), lambda b: (b, 0, 0)),
                     tok(QK_ROPE), tok(QK_ROPE), wspec(wq), wspec(wqr)]
        args += [past_ckv, past_kr, cos, sin, wq, wqr]
    else:
        in_specs += [wspec(wq)]
        args += [wq]
    in_specs += [wspec(wkv)]
    args += [wkv]
    aliases = {}
    if latent:
        in_specs += [pl.BlockSpec(memory_space=pl.ANY)]
        args += [o_ctx]
        aliases = {len(args) - 1: 0}
    kern = functools.partial(_attn_kernel, s_len=s_len, t_past=t_past, rope=latent)
    return pl.pallas_call(
        kern,
        grid=(nb,),
        in_specs=in_specs,
        out_specs=pl.BlockSpec((s_len, D_MODEL), row),
        out_shape=jax.ShapeDtypeStruct((geo.n_tok, D_MODEL), BF16),
        input_output_aliases=aliases,
        scratch_shapes=[pltpu.VMEM((t_past + s_len, KV_LORA), BF16),
                        pltpu.VMEM((t_past + s_len, QK_ROPE), BF16)],
        compiler_params=_params("parallel"),
        name="attn_lat" if latent else "attn_ctx",
    )(*args)


def _out_proj_kernel(a_ref, w_ref, x_ref, m_ref, o_ref):
    o_ref[...] = x_ref[...] + m_ref[5:6, :] * _dot(a_ref[...], w_ref[...])


def _out_proj(geo, a, w, x, mods):
    tm = TM_PROJ
    row = pl.BlockSpec((tm, D_MODEL), lambda i: (i, 0))
    return pl.pallas_call(
        _out_proj_kernel,
        grid=(geo.n_tok // tm,),
        in_specs=[row, pl.BlockSpec((D_MODEL, D_MODEL), lambda i: (0, 0)), row,
                  pl.BlockSpec((None, N_MOD, D_MODEL), lambda i: (geo.cond_row(i, tm), 0, 0))],
        out_specs=row,
        out_shape=jax.ShapeDtypeStruct((geo.n_tok, D_MODEL), F32),
        compiler_params=_params("parallel"),
        name="out_proj",
    )(a, w, x, mods)


def _shifted(geo, x_ref, xp_ref, xn_ref, m_ref, g_ref, tm):
    i = pl.program_id(0)
    g, shift, scale = g_ref[...], m_ref[3:4, :], m_ref[4:5, :]
    h = _modulate(x_ref[...], g, shift, scale)
    start = i * tm
    seq = jnp.where(start < geo.n_ctx, geo.s_ctx, geo.s_lat)
    rel = jnp.where(start < geo.n_ctx, start, start - geo.n_ctx)
    has_prev = (rel % seq) != 0
    has_next = ((rel + tm) % seq) != 0
    hp = _modulate(xp_ref[...], g, shift, scale)[7:8, :]
    hn = _modulate(xn_ref[...], g, shift, scale)[0:1, :]
    hp = jnp.where(has_prev, hp, 0.0)
    hn = jnp.where(has_next, hn, 0.0)
    r = lax.broadcasted_iota(jnp.int32, h.shape, 0)
    down = jnp.where(r == 0, hp, pltpu.roll(h, 1, 0))
    up = jnp.where(r == tm - 1, hn, pltpu.roll(h, tm - 1, 0))
    return h, 0.5 * (down + up) - h


def _halo_specs(geo, tm):
    nb8 = geo.n_tok // 8
    return [
        pl.BlockSpec((tm, D_MODEL), lambda i, *_: (i, 0)),
        pl.BlockSpec((8, D_MODEL), lambda i, *_: (jnp.maximum(i * (tm // 8) - 1, 0), 0)),
        pl.BlockSpec((8, D_MODEL), lambda i, *_: (jnp.minimum((i + 1) * (tm // 8), nb8 - 1), 0)),
        pl.BlockSpec((None, N_MOD, D_MODEL), lambda i, *_: (geo.cond_row(i, tm), 0, 0)),
        pl.BlockSpec((1, D_MODEL), lambda i, *_: (0, 0)),
    ]


def _rkv_kernel(x_ref, xp_ref, xn_ref, m_ref, g_ref, mu_ref, w_ref, o_ref, h_ref, xx_ref, *, geo, tm):
    proj = pl.program_id(1)

    def project(h, xx):
        xm = (h + xx * mu_ref[...]).astype(BF16)
        o_ref[...] = _dot(xm, w_ref[proj])

    @pl.when(proj == 0)
    def _():
        h, xx = _shifted(geo, x_ref, xp_ref, xn_ref, m_ref, g_ref, tm)
        h_ref[...] = h
        xx_ref[...] = xx
        project(h, xx)

    @pl.when(proj > 0)
    def _():
        project(h_ref[...], xx_ref[...])


def _rkv(geo, x, mods, g, mu3, w3):
    tm = TM_RWKV
    return pl.pallas_call(
        functools.partial(_rkv_kernel, geo=geo, tm=tm),
        grid=(geo.n_tok // tm, 3),
        in_specs=_halo_specs(geo, tm) + [
            pl.BlockSpec((None, 1, D_MODEL), lambda i, p: (p, 0, 0)),
            pl.BlockSpec((3, D_MODEL, D_MODEL), lambda i, p: (0, 0, 0), pipeline_mode=pl.Buffered(1)),
        ],
        out_specs=pl.BlockSpec((None, tm, D_MODEL), lambda i, p: (p, i, 0)),
        out_shape=jax.ShapeDtypeStruct((3, geo.n_tok, D_MODEL), F32),
        scratch_shapes=[pltpu.VMEM((tm, D_MODEL), F32), pltpu.VMEM((tm, D_MODEL), F32)],
        compiler_params=_params("parallel", "arbitrary"),
        name="rwkv_rkv",
    )(x, x, x, mods, g, mu3, w3)


def _lora_kernel(x_ref, xp_ref, xn_ref, m_ref, g_ref, mug_ref, mud_ref, g1_ref, g2_ref,
                 wa1_ref, wa2_ref, w0_ref, a0_ref, gate_ref, dec_ref, asig_ref, *, geo, tm):
    h, xx = _shifted(geo, x_ref, xp_ref, xn_ref, m_ref, g_ref, tm)
    mix = lambda mu: (h + xx * mu).astype(BF16)
    zg = _dot(mix(mug_ref[...]), g1_ref[...])
    gate_ref[...] = _dot(_sigmoid(zg).astype(BF16), g2_ref[...])
    for d in range(2):
        zw = _dot(mix(mud_ref[2 * d]), wa1_ref[d])
        wl = w0_ref[d] + _dot(jnp.tanh(zw).astype(BF16), wa2_ref[d])
        dec_ref[d] = -LOG_DECAY_SCALE * _sigmoid(wl)
        za = _dot(mix(mud_ref[2 * d + 1]), wa1_ref[2 + d])
        asig_ref[d] = _sigmoid(a0_ref[d] + _dot(za.astype(BF16), wa2_ref[2 + d]))


def _lora(geo, x, mods, g, mu_g, mu_dir, g1, g2, wa1, wa2, w0, a0):
    tm = TM_RWKV
    full = lambda a: pl.BlockSpec(a.shape, lambda i: (0,) * a.ndim)
    tok2 = pl.BlockSpec((2, tm, D_MODEL), lambda i: (0, i, 0))
    return pl.pallas_call(
        functools.partial(_lora_kernel, geo=geo, tm=tm),
        grid=(geo.n_tok // tm,),
        in_specs=_halo_specs(geo, tm) + [full(a) for a in (mu_g, mu_dir, g1, g2, wa1, wa2, w0, a0)],
        out_specs=[pl.BlockSpec((tm, D_MODEL), lambda i: (i, 0)), tok2, tok2],
        out_shape=[jax.ShapeDtypeStruct((geo.n_tok, D_MODEL), F32),
                   jax.ShapeDtypeStruct((2, geo.n_tok, D_MODEL), F32),
                   jax.ShapeDtypeStruct((2, geo.n_tok, D_MODEL), F32)],
        compiler_params=_params("parallel"),
        name="rwkv_lora",
    )(x, x, x, mods, g, mu_g, mu_dir, g1, g2, wa1, wa2, w0, a0)


def _head_sums(x, ones_blk):
    return _dot(x.astype(BF16), ones_blk)


def _head_ones(width):
    r = lax.broadcasted_iota(jnp.int32, (width, width), 0)
    c = lax.broadcasted_iota(jnp.int32, (width, width), 1)
    same = (r // RWKV_HEAD) == (c // RWKV_HEAD)
    return same, jnp.where(same, 1.0, 0.0).astype(BF16)


def _wkv_kernel(*refs, geo, tb, aliased):
    r_ref, k_ref, v_ref, dec_ref, asig_ref, kk_ref, ka_ref, rk_ref, s0_ref = refs[:9]
    y_ref, bon_ref, sf_fwd_ref, sf_bwd_ref, s_ref = refs[9 + (2 if aliased else 0):]
    d = pl.program_id(0)
    i = pl.program_id(1)
    nblk = pl.num_programs(1)
    blk = i + d * (nblk - 1 - 2 * i)
    start = blk * tb
    seq = jnp.where(start < geo.n_ctx, geo.s_ctx, geo.s_lat)
    rel = jnp.where(start < geo.n_ctx, start, start - geo.n_ctx)
    at_lo = (rel % seq) == 0
    at_hi = ((rel + tb) % seq) == 0
    first = jnp.where(d == 0, at_lo, at_hi)
    last = jnp.where(d == 0, at_hi, at_lo)

    @pl.when(first)
    def _():
        s_ref[...] = s0_ref[...]

    nch = tb // CHUNK
    sgn = 1 - 2 * d
    fwd = (d == 0).astype(F32)
    row = lax.broadcasted_iota(jnp.int32, (CHUNK, PAIR), 0)
    lane = lax.broadcasted_iota(jnp.int32, (CHUNK, PAIR), 1)
    sidx = lane & (RWKV_HEAD - 1)
    delta = (row - sidx) * sgn
    strict = delta > 0
    incl = delta >= 0
    eye = jnp.where(row == sidx, 1.0, 0.0)
    off_masks = []
    m = 1
    while m < CHUNK:
        off_masks.append(strict & ((row // (2 * m)) == (sidx // (2 * m))) & ((row // m) != (sidx // m)))
        m *= 2
    head0 = lane < RWKV_HEAD
    tr = lax.broadcasted_iota(jnp.int32, (CHUNK, 3 * CHUNK), 0)
    ts = lax.broadcasted_iota(jnp.int32, (CHUNK, 3 * CHUNK), 1) & (CHUNK - 1)
    tri3 = jnp.where((tr - ts) * sgn >= 0, 1.0, 0.0).astype(BF16)
    same_head, ones_blk = _head_ones(PAIR)

    def expand(x):
        return jnp.concatenate([jnp.where(head0, x, 0.0), jnp.where(head0, 0.0, x)], axis=0)

    def chunk_body(ci, carry):
        cc = ci * sgn + d * (nch - 1)
        rows = pl.ds(pl.multiple_of(cc * CHUNK, CHUNK), CHUNK)

        def group_body(pg, carry2):
            ps = [pg * PAIR_GROUP + q for q in range(PAIR_GROUP)]
            cols = [pl.ds(pl.multiple_of(p * PAIR, PAIR), PAIR) for p in ps]
            each = lambda f, *ls: [f(*xs) for xs in zip(*ls)]
            r = [r_ref[rows, c] for c in cols]
            k = [k_ref[rows, c] for c in cols]
            v = [v_ref[rows, c] for c in cols]
            logw = [dec_ref[rows, c] for c in cols]
            a = [asig_ref[rows, c] for c in cols]
            kk = [x * kk_ref[:, c] for x, c in zip(k, cols)]
            kd = [x * (1.0 + (y - 1.0) * ka_ref[:, c]) for x, y, c in zip(k, a, cols)]
            sums = [_head_sums(jnp.concatenate([x * x, y * z * rk_ref[:, c]], axis=0), ones_blk)
                    for x, y, z, c in zip(kk, r, kd, cols)]
            kk = each(lambda x, s: x / jnp.maximum(jnp.sqrt(s[:CHUNK]), 1e-12), kk, sums)
            b_in = each(lambda x, y: x * y, kk, a)
            for c, s, y in zip(cols, sums, v):
                bon_ref[rows, c] = s[CHUNK:] * y

            l1 =[x.astype(BF16) for x in logw]
            e1 = each(lambda x, y: x - y.astype(F32), logw, l1)
            l2 = [x.astype(BF16) for x in e1]
            l3 = each(lambda x, y: (x - y.astype(F32)).astype(BF16), e1, l2)
            cum = each(lambda x, y, z: _dot(tri3, jnp.concatenate([x, y, z], axis=0)), l1, l2, l3)
            tot = [fwd * x[CHUNK - 1:CHUNK, :] + (1.0 - fwd) * x[0:1, :] for x in cum]
            c_inv = [jnp.exp(-x) for x in cum]
            at = each(lambda x, y, z: -x * jnp.exp(y - z), kk, cum, logw)
            rt = each(lambda x, y: x * jnp.exp(y), r, cum)
            c_end = each(lambda x, y: jnp.exp(x - y), tot, cum)

            lhs = each(lambda x, y: jnp.concatenate([x, y], axis=0).astype(BF16), at, rt)
            rhs = each(lambda x, y, z: jnp.concatenate([expand(x * z), expand(y * z)], axis=0).astype(BF16),
                       b_in, kd, c_inv)
            g = each(_dot_nt, lhs, rhs)
            ab = [jnp.where(strict, x[:CHUNK, :PAIR], 0.0) for x in g]
            ak = [jnp.where(strict, x[:CHUNK, PAIR:], 0.0) for x in g]
            rbk = [jnp.concatenate([jnp.where(incl, x[CHUNK:, :PAIR], 0.0),
                                    jnp.where(incl, x[CHUNK:, PAIR:], 0.0)], axis=1).astype(BF16) for x in g]

            t_inv = [eye + jnp.where(off_masks[0], x, 0.0) for x in ab]
            for off in off_masks[1:]:
                lx = each(lambda x, t: _dot(jnp.where(off, x, 0.0).astype(BF16), expand(t).astype(BF16)), ab, t_inv)
                t_inv = each(lambda t, x: t + _dot(t.astype(BF16), expand(x).astype(BF16)), t_inv, lx)

            s2 = [s_ref[p] for p in ps]
            ars = each(lambda x, s: _dot_nt(x, s.astype(BF16)), lhs, s2)
            vexp = [expand(x).astype(BF16) for x in v]
            rhs_u = each(lambda x, y, z: x[:CHUNK] + _dot(y.astype(BF16), z), ars, ak, vexp)
            u = each(lambda t, x: _dot(t.astype(BF16), expand(x).astype(BF16)), t_inv, rhs_u)
            uv = each(lambda x, y: jnp.concatenate([expand(x).astype(BF16), y], axis=0), u, vexp)
            y_out = each(lambda x, y, z: x[CHUNK:] + _dot(y, z), ars, rbk, uv)
            for c, x in zip(cols, y_out):
                y_ref[rows, c] = x
            uv_t = each(lambda x, y: jnp.concatenate([x, y], axis=0).T.astype(BF16), u, v)
            bk = each(lambda x, y, z: jnp.concatenate([x * z, y * z], axis=0).astype(BF16), b_in, kd, c_end)
            upd = each(_dot, uv_t, bk)
            for p, s, x, t in zip(ps, s2, upd, tot):
                s_ref[p] = s * jnp.exp(t) + jnp.where(same_head, x, 0.0)
            return carry2

        lax.fori_loop(0, N_PAIRS // PAIR_GROUP, group_body, 0)
        return carry

    lax.fori_loop(0, nch, chunk_body, 0)

    def write_states(dst_ref):
        for p in range(N_PAIRS):
            s = s_ref[p]
            dst_ref[2 * p] = s[:RWKV_HEAD, :RWKV_HEAD]
            dst_ref[2 * p + 1] = s[RWKV_HEAD:, RWKV_HEAD:]

    ends_ctx = last & (start < geo.n_ctx)

    @pl.when(ends_ctx & (d == 0))
    def _():
        write_states(sf_fwd_ref)

    @pl.when(ends_ctx & (d == 1))
    def _():
        write_states(sf_bwd_ref)


def _wkv(geo, rkv, dec, asig, kk_c, ka_c, rk_c, s0, j, n_layers, prev_states):
    tb = TB_WKV
    nblk = geo.n_tok // tb
    blk = lambda d, i: i + d * (nblk - 1 - 2 * i)
    tok = lambda which: pl.BlockSpec((None, tb, D_MODEL), lambda d, i: (which, blk(d, i), 0))
    perdir = pl.BlockSpec((None, tb, D_MODEL), lambda d, i: (d, blk(d, i), 0))
    const = pl.BlockSpec((1, D_MODEL), lambda d, i: (0, 0))
    last_seq = geo.b_ctx - 1
    seq = lambda d, i: jnp.minimum(blk(d, i) * tb // geo.s_ctx, last_seq)
    final = lambda which: pl.BlockSpec(
        (None, None, RWKV_HEADS, RWKV_HEAD, RWKV_HEAD),
        lambda d, i: (jnp.where(d == which, seq(d, i), last_seq), j, 0, 0, 0))
    final_shape = jax.ShapeDtypeStruct((geo.b_ctx, n_layers, RWKV_HEADS, RWKV_HEAD, RWKV_HEAD), F32)
    in_specs = [tok(0), tok(1), tok(2), perdir, perdir, const, const, const,
                pl.BlockSpec((None, None, N_PAIRS, PAIR, PAIR), lambda d, i: (d, geo.cond_row(blk(d, i), tb), 0, 0, 0))]
    args = [rkv, rkv, rkv, dec, asig, kk_c, ka_c, rk_c, s0]
    aliases = {}
    if prev_states is not None:
        in_specs += [pl.BlockSpec(memory_space=pl.ANY)] * 2
        aliases = {len(args): 2, len(args) + 1: 3}
        args += list(prev_states)
    return pl.pallas_call(
        functools.partial(_wkv_kernel, geo=geo, tb=tb, aliased=prev_states is not None),
        grid=(2, nblk),
        in_specs=in_specs,
        out_specs=[perdir, perdir, final(0), final(1)],
        out_shape=[jax.ShapeDtypeStruct((2, geo.n_tok, D_MODEL), F32),
                   jax.ShapeDtypeStruct((2, geo.n_tok, D_MODEL), F32), final_shape, final_shape],
        input_output_aliases=aliases,
        scratch_shapes=[pltpu.VMEM((N_PAIRS, PAIR, PAIR), F32)],
        compiler_params=_params("arbitrary", "arbitrary"),
        name="wkv_chunked",
    )(*args)


def _rwkv_out_kernel(y_ref, bon_ref, gate_ref, lw_ref, lb_ref, w_ref, x_ref, m_ref, o_ref, z_ref):
    _, ones_blk = _head_ones(GN_SLAB)
    inv_n = 1.0 / RWKV_HEAD
    for s in range(D_MODEL // GN_SLAB):
        cols = slice(s * GN_SLAB, (s + 1) * GN_SLAB)
        y = y_ref[0, :, cols] + y_ref[1, :, cols]
        yc = y - _head_sums(y, ones_blk) * inv_n
        var = _head_sums(yc * yc, ones_blk) * inv_n
        z = yc * lax.rsqrt(var + GN_EPS) * lw_ref[:, cols] + lb_ref[:, cols] + bon_ref[0, :, cols] + bon_ref[1, :, cols]
        z_ref[:, cols] = (z * gate_ref[:, cols]).astype(BF16)
    o_ref[...] = x_ref[...] + m_ref[5:6, :] * _dot(z_ref[...], w_ref[...])


def _rwkv_out(geo, y, bon, gate, lw, lb, w, x, mods):
    tm = TM_OUT
    row = pl.BlockSpec((tm, D_MODEL), lambda i: (i, 0))
    both = pl.BlockSpec((2, tm, D_MODEL), lambda i: (0, i, 0))
    const = pl.BlockSpec((1, D_MODEL), lambda i: (0, 0))
    return pl.pallas_call(
        _rwkv_out_kernel,
        grid=(geo.n_tok // tm,),
        in_specs=[both, both, row, const, const, pl.BlockSpec((D_MODEL, D_MODEL), lambda i: (0, 0)), row,
                  pl.BlockSpec((None, N_MOD, D_MODEL), lambda i: (geo.cond_row(i, tm), 0, 0))],
        out_specs=row,
        out_shape=jax.ShapeDtypeStruct((geo.n_tok, D_MODEL), F32),
        scratch_shapes=[pltpu.VMEM((tm, D_MODEL), BF16)],
        compiler_params=_params("parallel"),
        name="rwkv_out",
    )(y, bon, gate, lw, lb, w, x, mods)


def _state_to_pairs(s):
    b = s.shape[0]
    s5 = s.reshape(b, N_PAIRS, 2, RWKV_HEAD, RWKV_HEAD)
    z = jnp.zeros_like(s5[:, :, 0])
    top = jnp.concatenate([s5[:, :, 0], z], axis=-1)
    bot = jnp.concatenate([z, s5[:, :, 1]], axis=-1)
    return jnp.concatenate([top, bot], axis=-2)


def _rope_tables(geo):
    rows = geo.s_lat // GRID_W
    row = jnp.repeat(jnp.arange(rows, dtype=F32), GRID_W)
    col = jnp.tile(jnp.arange(GRID_W, dtype=F32), rows)
    inv = ROPE_THETA ** (-jnp.arange(ROPE_FREQS, dtype=F32) / ROPE_FREQS)
    ang = jnp.stack([row[:, None] * inv, col[:, None] * inv], axis=1)
    ang = jnp.broadcast_to(ang[:, :, None, :], (geo.s_lat, 2, 2, ROPE_FREQS)).reshape(geo.s_lat, QK_ROPE)
    cos = jnp.concatenate([jnp.ones((geo.n_ctx, QK_ROPE), F32), jnp.tile(jnp.cos(ang), (geo.b_lat, 1))], axis=0)
    sin = jnp.concatenate([jnp.zeros((geo.n_ctx, QK_ROPE), F32), jnp.tile(jnp.sin(ang), (geo.b_lat, 1))], axis=0)
    return cos, sin


def _rot_cols(w):
    w4 = w.reshape(w.shape[:-1] + (2, 2, ROPE_FREQS))
    return jnp.stack([-w4[..., 1, :], w4[..., 0, :]], axis=-2).reshape(w.shape)


def _mla_layer(geo, x, mods, g, j, cos, sin, cache_ckv, cache_krope, p):
    w_dkv = p['mla_w_dkv'][j]
    w_cat = jnp.concatenate([p['mla_w_dq'][j], w_dkv, _rot_cols(w_dkv[:, KV_LORA:])], axis=1).astype(BF16)
    cq, ckv, ckvb, kr = _mla_proj(geo, x, mods, g, w_cat, p['mla_q_norm'][j][None], p['mla_kv_norm'][j][None],
                                  cos, sin)
    w_uq = p['mla_w_uq'][j].reshape(Q_LORA, MLA_HEADS, QK_NOPE + QK_ROPE).transpose(1, 0, 2)
    wq = w_uq.astype(BF16)
    wqr = _rot_cols(w_uq[..., QK_NOPE:]).astype(BF16)
    wkv = p['mla_w_ukv'][j].reshape(KV_LORA, MLA_HEADS, QK_NOPE + V_DIM).transpose(1, 0, 2).astype(BF16)
    o_ctx = _attn(geo, cq, ckvb, kr, wq, wqr, wkv, latent=False)
    o = _attn(geo, cq, ckvb, kr, wq, wqr, wkv, latent=True,
              past_ckv=cache_ckv[:, j], past_kr=cache_krope[:, j], cos=cos, sin=sin, o_ctx=o_ctx)
    x = _out_proj(geo, o, p['mla_w_o'][j].astype(BF16), x, mods)
    new_ckv = ckv[:geo.n_ctx].reshape(geo.b_ctx, geo.s_ctx, KV_LORA)
    new_kr = kr[:geo.n_ctx].reshape(geo.b_ctx, geo.s_ctx, QK_ROPE)
    return x, new_ckv, new_kr


def _rwkv_layer(geo, x, mods, g, j, s0_fwd, s0_bwd, prev_states, p):
    mu = p['rwkv_mu'][j]
    mu_dir = p['rwkv_mu_dir'][j].reshape(4, 1, D_MODEL)
    w3 = jnp.stack([p['rwkv_w_r'][j], p['rwkv_w_k'][j], p['rwkv_w_v'][j]]).astype(BF16)
    rkv = _rkv(geo, x, mods, g, mu[:3, None, :], w3)
    pad1 = lambda w: jnp.pad(w, ((0, 0), (0, 0), (0, LORA_PAD - w.shape[-1])))
    pad2 = lambda w: jnp.pad(w, ((0, 0), (0, LORA_PAD - w.shape[-2]), (0, 0)))
    wa1 = jnp.concatenate([pad1(p['rwkv_w1'][j]), pad1(p['rwkv_a1'][j])]).astype(BF16)
    wa2 = jnp.concatenate([pad2(p['rwkv_w2'][j]), pad2(p['rwkv_a2'][j])]).astype(BF16)
    gate, dec, asig = _lora(geo, x, mods, g, mu[3:4], mu_dir, p['rwkv_g1'][j].astype(BF16),
                            p['rwkv_g2'][j].astype(BF16), wa1, wa2,
                            p['rwkv_w0'][j][:, None, :], p['rwkv_a0'][j][:, None, :])
    zero = jnp.zeros((2, 1, N_PAIRS, PAIR, PAIR), F32)
    s0 = jnp.concatenate([zero, jnp.stack([_state_to_pairs(s0_fwd), _state_to_pairs(s0_bwd)])], axis=1)
    y, bon, sf, sb = _wkv(geo, rkv, dec, asig, p['rwkv_k_k'][j][None], p['rwkv_k_a'][j][None],
                          p['rwkv_r_k'][j].reshape(1, D_MODEL), s0, j, p['rwkv_mu'].shape[0], prev_states)
    x = _rwkv_out(geo, y, bon, gate, p['rwkv_ln_w'][j][None], p['rwkv_ln_b'][j][None],
                  p['rwkv_w_o'][j].astype(BF16), x, mods)
    return x, (sf, sb)


def kernel(x_prompt, x_sample, cache_ckv, cache_krope, state_wkv_fwd, state_wkv_bwd, c, c_ctx, w_ada, b_ada, norm_sub, norm_final, w_ffn_in, w_ffn_out, mla_w_dq, mla_q_norm, mla_w_uq, mla_w_dkv, mla_kv_norm, mla_w_ukv, mla_w_o, rwkv_mu, rwkv_mu_dir, rwkv_w_r, rwkv_w_k, rwkv_w_v, rwkv_w0, rwkv_w1, rwkv_w2, rwkv_a0, rwkv_a1, rwkv_a2, rwkv_g1, rwkv_g2, rwkv_k_k, rwkv_k_a, rwkv_r_k, rwkv_ln_w, rwkv_ln_b, rwkv_w_o):
    p = dict(mla_w_dq=mla_w_dq, mla_q_norm=mla_q_norm, mla_w_uq=mla_w_uq, mla_w_dkv=mla_w_dkv,
             mla_kv_norm=mla_kv_norm, mla_w_ukv=mla_w_ukv, mla_w_o=mla_w_o,
             rwkv_mu=rwkv_mu, rwkv_mu_dir=rwkv_mu_dir, rwkv_w_r=rwkv_w_r, rwkv_w_k=rwkv_w_k,
             rwkv_w_v=rwkv_w_v, rwkv_w0=rwkv_w0, rwkv_w1=rwkv_w1, rwkv_w2=rwkv_w2,
             rwkv_a0=rwkv_a0, rwkv_a1=rwkv_a1, rwkv_a2=rwkv_a2, rwkv_g1=rwkv_g1, rwkv_g2=rwkv_g2,
             rwkv_k_k=rwkv_k_k, rwkv_k_a=rwkv_k_a, rwkv_r_k=rwkv_r_k,
             rwkv_ln_w=rwkv_ln_w, rwkv_ln_b=rwkv_ln_b, rwkv_w_o=rwkv_w_o)
    b_ctx, s_ctx, _ = x_prompt.shape
    b_lat, s_lat, _ = x_sample.shape
    geo = _Geom(b_ctx, s_ctx, b_lat, s_lat)
    assert geo.n_ctx % s_lat == 0 and s_ctx % TM_RWKV == 0 and s_lat % TM_FFN == 0
    assert s_ctx == TB_WKV and s_lat % TB_WKV == 0

    x = jnp.concatenate([x_prompt.reshape(geo.n_ctx, D_MODEL), x_sample.reshape(geo.n_lat, D_MODEL)], axis=0)
    cond = jnp.concatenate([c_ctx[None], c, jnp.zeros((COND_ROWS - 1 - b_lat, D_MODEL), F32)], axis=0)
    mods_all = _ada(cond, w_ada, b_ada).reshape(DEPTH, COND_ROWS, N_MOD, D_MODEL)
    cos, sin = _rope_tables(geo)
    nf = norm_final[None]
    w_in = w_ffn_in[0, 0].astype(BF16)
    w_out = w_ffn_out[0, 0].astype(BF16)

    ckv_l, kr_l, states = [], [], None
    for l in range(DEPTH):
        j = l // N_MIXERS
        mods = mods_all[l]
        x, w_in, w_out = _ffn(geo, x, mods, norm_sub[l, 0][None], w_in, w_out, nf, sub=0, final=False,
                              nxt=(w_ffn_in, w_ffn_out, l, 1))
        if l % N_MIXERS == 0:
            x, ckv, kr = _mla_layer(geo, x, mods, norm_sub[l, 1][None], j, cos, sin, cache_ckv, cache_krope, p)
            ckv_l.append(ckv)
            kr_l.append(kr)
        else:
            x, states = _rwkv_layer(geo, x, mods, norm_sub[l, 1][None], j,
                                    state_wkv_fwd[:, j], state_wkv_bwd[:, j], states, p)
        if l < DEPTH - 1:
            x, w_in, w_out = _ffn(geo, x, mods, norm_sub[l, 2][None], w_in, w_out, nf, sub=2, final=False,
                                  nxt=(w_ffn_in, w_ffn_out, l + 1, 0))
        else:
            x, = _ffn(geo, x, mods, norm_sub[l, 2][None], w_in, w_out, nf, sub=2, final=True)

    y_prompt = x[:geo.n_ctx].reshape(b_ctx, s_ctx, D_MODEL)
    y_sample = x[geo.n_ctx:].reshape(b_lat, s_lat, D_MODEL)
    return (y_prompt, y_sample, jnp.stack(ckv_l, axis=1), jnp.stack(kr_l, axis=1),
            states[0], states[1])
```

```python
import functools

import jax
import jax.numpy as jnp
from jax import lax
from jax.experimental import pallas as pl
from jax.experimental.pallas import tpu as pltpu

D_MODEL = 2048
DEPTH = 4
N_MIXERS = 2
D_FF = 5632
N_MOD = 9
RMS_EPS = 1e-6
MLA_HEADS = 16
Q_LORA = 512
KV_LORA = 512
QK_NOPE = 128
QK_ROPE = 64
V_DIM = 128
ROPE_FREQS = QK_ROPE // 4
ROPE_THETA = 10000.0
GRID_W = 64
ATTN_SCALE = (QK_NOPE + QK_ROPE) ** -0.5
RWKV_HEAD = 64
RWKV_HEADS = D_MODEL // RWKV_HEAD
GN_EPS = 64e-5
LOG_DECAY_SCALE = 0.6065306597126334
LORA_PAD = 128

COND_ROWS = 8
VMEM_LIMIT = 56 * 1024 * 1024

TM_FFN = 512
TF_FFN = 512
CAST_TILE = 256
TM_PROJ = 512
TM_RWKV = 256
TN_ADA = 1024
Q_TILE = 256
ATTN_LOCKSTEP = 8
CHUNK = 64
PAIR = 2 * RWKV_HEAD
N_PAIRS = D_MODEL // PAIR
PAIR_GROUP = 16
TB_WKV = 256
TM_OUT = 256
GN_SLAB = 256

BF16 = jnp.bfloat16
F32 = jnp.float32


def _params(*sem):
    return pltpu.CompilerParams(dimension_semantics=sem, vmem_limit_bytes=VMEM_LIMIT)


def _sigmoid(x):
    return 1.0 / (1.0 + jnp.exp(-x))


def _modulate(x, g, shift, scale):
    ms = jnp.mean(x * x, axis=-1, keepdims=True)
    return (x * lax.rsqrt(ms + RMS_EPS) * g) * (1.0 + scale) + shift


def _rms(x, w):
    ms = jnp.mean(x * x, axis=-1, keepdims=True)
    return x * lax.rsqrt(ms + RMS_EPS) * w


def _dot(a, b):
    return jnp.dot(a, b, preferred_element_type=F32)


def _dot_nt(a, b):
    return lax.dot_general(a, b, (((1,), (1,)), ((), ())), preferred_element_type=F32)


class _Geom:
    def __init__(self, b_ctx, s_ctx, b_lat, s_lat):
        self.b_ctx, self.s_ctx, self.b_lat, self.s_lat = b_ctx, s_ctx, b_lat, s_lat
        self.n_ctx = b_ctx * s_ctx
        self.n_lat = b_lat * s_lat
        self.n_tok = self.n_ctx + self.n_lat

    def cond_row(self, i, tm):
        start = i * tm
        return jnp.where(start < self.n_ctx, 0, 1 + (start - self.n_ctx) // self.s_lat)


def _ada_kernel(c_ref, w_ref, b_ref, o_ref):
    c = c_ref[...]
    s = (c * _sigmoid(c)).astype(BF16)
    o_ref[...] = _dot(s, w_ref[...].astype(BF16)) + b_ref[...]


def _ada(cond, w_ada, b_ada):
    n = N_MOD * D_MODEL
    return pl.pallas_call(
        _ada_kernel,
        grid=(DEPTH, n // TN_ADA),
        in_specs=[
            pl.BlockSpec((COND_ROWS, D_MODEL), lambda l, j: (0, 0)),
            pl.BlockSpec((None, D_MODEL, TN_ADA), lambda l, j: (l, 0, j)),
            pl.BlockSpec((None, 1, TN_ADA), lambda l, j: (l, 0, j)),
        ],
        out_specs=pl.BlockSpec((None, COND_ROWS, TN_ADA), lambda l, j: (l, 0, j)),
        out_shape=jax.ShapeDtypeStruct((DEPTH, COND_ROWS, n), F32),
        compiler_params=_params("parallel", "parallel"),
        name="ada",
    )(cond, w_ada, b_ada.reshape(DEPTH, 1, n))


def _ffn_kernel(*refs, sub, final, cast_next):
    if cast_next:
        (x_ref, m_ref, g_ref, wg_ref, wu_ref, wo_ref, nf_ref, ci_ref, co_ref,
         o_ref, cib_ref, cob_ref, h_ref, acc_ref) = refs
    else:
        x_ref, m_ref, g_ref, wg_ref, wu_ref, wo_ref, nf_ref, o_ref, h_ref, acc_ref = refs
    f = pl.program_id(1)

    def swiglu_slice(h):
        gate = _dot(h, wg_ref[...])
        up = _dot(h, wu_ref[...])
        act = (gate * _sigmoid(gate) * up).astype(BF16)
        return _dot(act, wo_ref[...])

    @pl.when(f == 0)
    def _():
        h = _modulate(x_ref[...], g_ref[...], m_ref[3 * sub:3 * sub + 1, :], m_ref[3 * sub + 1:3 * sub + 2, :])
        h = h.astype(BF16)
        h_ref[...] = h
        acc_ref[...] = swiglu_slice(h)

    @pl.when(f > 0)
    def _():
        acc_ref[...] += swiglu_slice(h_ref[...])

    if cast_next:
        step = pl.program_id(0) * pl.num_programs(1) + f
        n_in, n_out = cast_next

        @pl.when(step < n_in)
        def _():
            cib_ref[...] = ci_ref[...].astype(BF16)

        @pl.when((step >= n_in) & (step < n_in + n_out))
        def _():
            cob_ref[...] = co_ref[...].astype(BF16)

    @pl.when(f == pl.num_programs(1) - 1)
    def _():
        y = x_ref[...] + 0.5 * m_ref[3 * sub + 2:3 * sub + 3, :] * acc_ref[...]
        if final:
            y = _rms(y, nf_ref[...])
        o_ref[...] = y


def _ffn(geo, x, mods, g, w_in, w_out, nf, *, sub, final, nxt=None):
    tm, tf = TM_FFN, TF_FFN
    nf_blocks = D_FF // tf
    n_in, n_out = 2 * D_FF // CAST_TILE, D_FF // CAST_TILE
    step = lambda i, f: i * nf_blocks + f
    in_blk = lambda i, f: jnp.minimum(step(i, f), n_in - 1)
    out_blk = lambda i, f: jnp.clip(step(i, f) - n_in, 0, n_out - 1)
    in_specs = [
        pl.BlockSpec((tm, D_MODEL), lambda i, f: (i, 0)),
        pl.BlockSpec((None, N_MOD, D_MODEL), lambda i, f: (geo.cond_row(i, tm), 0, 0)),
        pl.BlockSpec((1, D_MODEL), lambda i, f: (0, 0)),
        pl.BlockSpec((D_MODEL, tf), lambda i, f: (0, f)),
        pl.BlockSpec((D_MODEL, tf), lambda i, f: (0, f + nf_blocks)),
        pl.BlockSpec((tf, D_MODEL), lambda i, f: (f, 0)),
        pl.BlockSpec((1, D_MODEL), lambda i, f: (0, 0)),
    ]
    out_specs = [pl.BlockSpec((tm, D_MODEL), lambda i, f: (i, 0))]
    out_shape = [jax.ShapeDtypeStruct((geo.n_tok, D_MODEL), F32)]
    args = [x, mods, g, w_in, w_in, w_out, nf]
    if nxt is not None:
        w_in_all, w_out_all, layer, half = nxt
        assert (geo.n_tok // tm) * nf_blocks >= n_in + n_out
        in_specs += [
            pl.BlockSpec((None, None, D_MODEL, CAST_TILE), lambda i, f: (layer, half, 0, in_blk(i, f))),
            pl.BlockSpec((None, None, CAST_TILE, D_MODEL), lambda i, f: (layer, half, out_blk(i, f), 0)),
        ]
        out_specs += [
            pl.BlockSpec((D_MODEL, CAST_TILE), lambda i, f: (0, in_blk(i, f))),
            pl.BlockSpec((CAST_TILE, D_MODEL), lambda i, f: (out_blk(i, f), 0)),
        ]
        out_shape += [jax.ShapeDtypeStruct((D_MODEL, 2 * D_FF), BF16), jax.ShapeDtypeStruct((D_FF, D_MODEL), BF16)]
        args += [w_in_all, w_out_all]
    return pl.pallas_call(
        functools.partial(_ffn_kernel, sub=sub, final=final, cast_next=(n_in, n_out) if nxt is not None else None),
        grid=(geo.n_tok // tm, nf_blocks),
        in_specs=in_specs,
        out_specs=out_specs,
        out_shape=out_shape,
        scratch_shapes=[pltpu.VMEM((tm, D_MODEL), BF16), pltpu.VMEM((tm, D_MODEL), F32)],
        compiler_params=_params("arbitrary", "arbitrary"),
        name="ffn",
    )(*args)


def _mla_proj_kernel(x_ref, m_ref, g_ref, w_ref, qn_ref, kvn_ref, cos_ref, sin_ref,
                     cq_ref, ckv_ref, ckvb_ref, kr_ref):
    h = _modulate(x_ref[...], g_ref[...], m_ref[3:4, :], m_ref[4:5, :]).astype(BF16)
    z = _dot(h, w_ref[...])
    cq_ref[...] = _rms(z[:, :Q_LORA], qn_ref[...]).astype(BF16)
    ckv = _rms(z[:, Q_LORA:Q_LORA + KV_LORA], kvn_ref[...])
    ckv_ref[...] = ckv
    ckvb_ref[...] = ckv.astype(BF16)
    o = Q_LORA + KV_LORA
    kr_ref[...] = z[:, o:o + QK_ROPE] * cos_ref[...] + z[:, o + QK_ROPE:o + 2 * QK_ROPE] * sin_ref[...]


def _mla_proj(geo, x, mods, g, w_cat, qn, kvn, cos, sin):
    tm = TM_PROJ
    n_out = w_cat.shape[1]
    row = lambda i: (i, 0)
    fix = lambda i: (0, 0)
    return pl.pallas_call(
        _mla_proj_kernel,
        grid=(geo.n_tok // tm,),
        in_specs=[
            pl.BlockSpec((tm, D_MODEL), row),
            pl.BlockSpec((None, N_MOD, D_MODEL), lambda i: (geo.cond_row(i, tm), 0, 0)),
            pl.BlockSpec((1, D_MODEL), fix),
            pl.BlockSpec((D_MODEL, n_out), fix),
            pl.BlockSpec((1, Q_LORA), fix),
            pl.BlockSpec((1, KV_LORA), fix),
            pl.BlockSpec((tm, QK_ROPE), row),
            pl.BlockSpec((tm, QK_ROPE), row),
        ],
        out_specs=[
            pl.BlockSpec((tm, Q_LORA), row),
            pl.BlockSpec((tm, KV_LORA), row),
            pl.BlockSpec((tm, KV_LORA), row),
            pl.BlockSpec((tm, QK_ROPE), row),
        ],
        out_shape=[
            jax.ShapeDtypeStruct((geo.n_tok, Q_LORA), BF16),
            jax.ShapeDtypeStruct((geo.n_tok, KV_LORA), F32),
            jax.ShapeDtypeStruct((geo.n_tok, KV_LORA), BF16),
            jax.ShapeDtypeStruct((geo.n_tok, QK_ROPE), F32),
        ],
        compiler_params=_params("parallel"),
        name="mla_proj",
    )(x, mods, g, w_cat, qn, kvn, cos, sin)


def _attn_kernel(*refs, s_len, t_past, rope):
    if rope:
        (cq_ref, ckvb_ref, kr_ref, pckv_ref, pkr_ref, cos_ref, sin_ref,
         wq_ref, wqr_ref, wkv_ref, _, o_ref, kv_all, kr_all) = refs
    else:
        cq_ref, ckvb_ref, kr_ref, wq_ref, wkv_ref, o_ref, kv_all, kr_all = refs
    if t_past:
        kv_all[0:t_past, :] = pckv_ref[...].astype(BF16)
        kr_all[0:t_past, :] = pkr_ref[...].astype(BF16)
    kv_all[t_past:, :] = ckvb_ref[...]
    kr_all[t_past:, :] = kr_ref[...].astype(BF16)

    n_q = s_len // Q_TILE
    hg = max(1, ATTN_LOCKSTEP // n_q)
    row_slices = [slice(qb * Q_TILE, (qb + 1) * Q_TILE) for qb in range(n_q)]

    def head_group(g, carry):
        heads = [g * hg + i for i in range(hg)]
        kv = kv_all[...]
        krb = kr_all[...]
        kvp = [_dot(kv, wkv_ref[hd]) for hd in heads]
        kn = [x[:, :QK_NOPE].astype(BF16) for x in kvp]
        v = [x[:, QK_NOPE:].astype(BF16) for x in kvp]
        items = [(i, qb) for i in range(hg) for qb in range(n_q)]
        cq = [cq_ref[rows, :] for rows in row_slices]
        q = [_dot(cq[qb], wq_ref[heads[i]]) for i, qb in items]
        qr = [x[:, QK_NOPE:] for x in q]
        if rope:
            rot = [_dot(cq[qb], wqr_ref[heads[i]]) for i, qb in items]
            qr = [x * cos_ref[row_slices[qb], :] + y * sin_ref[row_slices[qb], :]
                  for x, y, (i, qb) in zip(qr, rot, items)]
        s = [(_dot_nt(x[:, :QK_NOPE].astype(BF16), kn[i]) + _dot_nt(y.astype(BF16), krb)) * ATTN_SCALE
             for x, y, (i, qb) in zip(q, qr, items)]
        p = [jnp.exp(x - jnp.max(x, axis=-1, keepdims=True)) for x in s]
        pr = [(x * (1.0 / jnp.sum(x, axis=-1, keepdims=True))).astype(BF16) for x in p]
        o = [_dot(x, v[i]).astype(BF16) for x, (i, qb) in zip(pr, items)]
        for x, (i, qb) in zip(o, items):
            o_ref[row_slices[qb], pl.ds(pl.multiple_of(heads[i] * V_DIM, V_DIM), V_DIM)] = x
        return carry

    lax.fori_loop(0, MLA_HEADS // hg, head_group, 0)


def _attn(geo, cq, ckvb, kr, wq, wqr, wkv, *, latent, past_ckv=None, past_kr=None, cos=None, sin=None, o_ctx=None):
    if latent:
        nb, s_len, off = geo.b_lat, geo.s_lat, geo.n_ctx // geo.s_lat
        t_past = past_ckv.shape[1]
    else:
        nb, s_len, off, t_past = geo.b_ctx, geo.s_ctx, 0, 0
    row = lambda b: (b + off, 0)
    fix3 = lambda b: (0, 0, 0)
    tok = lambda width: pl.BlockSpec((s_len, width), row)
    wspec = lambda w: pl.BlockSpec(w.shape, fix3)
    in_specs = [tok(Q_LORA), tok(KV_LORA), tok(QK_ROPE)]
    args = [cq, ckvb, kr]
    if latent:
        in_specs += [pl.BlockSpec((None, t_past, KV_LORA), lambda b: (b, 0, 0)),
                     pl.BlockSpec((None, t_past, QK_ROPE), lambda b: (b, 0, 0)),
                     tok(QK_ROPE), tok(QK_ROPE), wspec(wq), wspec(wqr)]
        args += [past_ckv, past_kr, cos, sin, wq, wqr]
    else:
        in_specs += [wspec(wq)]
        args += [wq]
    in_specs += [wspec(wkv)]
    args += [wkv]
    aliases = {}
    if latent:
        in_specs += [pl.BlockSpec(memory_space=pl.ANY)]
        args += [o_ctx]
        aliases = {len(args) - 1: 0}
    kern = functools.partial(_attn_kernel, s_len=s_len, t_past=t_past, rope=latent)
    return pl.pallas_call(
        kern,
        grid=(nb,),
        in_specs=in_specs,
        out_specs=pl.BlockSpec((s_len, D_MODEL), row),
        out_shape=jax.ShapeDtypeStruct((geo.n_tok, D_MODEL), BF16),
        input_output_aliases=aliases,
        scratch_shapes=[pltpu.VMEM((t_past + s_len, KV_LORA), BF16),
                        pltpu.VMEM((t_past + s_len, QK_ROPE), BF16)],
        compiler_params=_params("parallel"),
        name="attn_lat" if latent else "attn_ctx",
    )(*args)


def _out_proj_kernel(a_ref, w_ref, x_ref, m_ref, o_ref):
    o_ref[...] = x_ref[...] + m_ref[5:6, :] * _dot(a_ref[...], w_ref[...])


def _out_proj(geo, a, w, x, mods):
    tm = TM_PROJ
    row = pl.BlockSpec((tm, D_MODEL), lambda i: (i, 0))
    return pl.pallas_call(
        _out_proj_kernel,
        grid=(geo.n_tok // tm,),
        in_specs=[row, pl.BlockSpec((D_MODEL, D_MODEL), lambda i: (0, 0)), row,
                  pl.BlockSpec((None, N_MOD, D_MODEL), lambda i: (geo.cond_row(i, tm), 0, 0))],
        out_specs=row,
        out_shape=jax.ShapeDtypeStruct((geo.n_tok, D_MODEL), F32),
        compiler_params=_params("parallel"),
        name="out_proj",
    )(a, w, x, mods)


def _shifted(geo, x_ref, xp_ref, xn_ref, m_ref, g_ref, tm):
    i = pl.program_id(0)
    g, shift, scale = g_ref[...], m_ref[3:4, :], m_ref[4:5, :]
    h = _modulate(x_ref[...], g, shift, scale)
    start = i * tm
    seq = jnp.where(start < geo.n_ctx, geo.s_ctx, geo.s_lat)
    rel = jnp.where(start < geo.n_ctx, start, start - geo.n_ctx)
    has_prev = (rel % seq) != 0
    has_next = ((rel + tm) % seq) != 0
    hp = _modulate(xp_ref[...], g, shift, scale)[7:8, :]
    hn = _modulate(xn_ref[...], g, shift, scale)[0:1, :]
    hp = jnp.where(has_prev, hp, 0.0)
    hn = jnp.where(has_next, hn, 0.0)
    r = lax.broadcasted_iota(jnp.int32, h.shape, 0)
    down = jnp.where(r == 0, hp, pltpu.roll(h, 1, 0))
    up = jnp.where(r == tm - 1, hn, pltpu.roll(h, tm - 1, 0))
    return h, 0.5 * (down + up) - h


def _halo_specs(geo, tm):
    nb8 = geo.n_tok // 8
    return [
        pl.BlockSpec((tm, D_MODEL), lambda i, *_: (i, 0)),
        pl.BlockSpec((8, D_MODEL), lambda i, *_: (jnp.maximum(i * (tm // 8) - 1, 0), 0)),
        pl.BlockSpec((8, D_MODEL), lambda i, *_: (jnp.minimum((i + 1) * (tm // 8), nb8 - 1), 0)),
        pl.BlockSpec((None, N_MOD, D_MODEL), lambda i, *_: (geo.cond_row(i, tm), 0, 0)),
        pl.BlockSpec((1, D_MODEL), lambda i, *_: (0, 0)),
    ]


def _rkv_kernel(x_ref, xp_ref, xn_ref, m_ref, g_ref, mu_ref, w_ref, o_ref, h_ref, xx_ref, *, geo, tm):
    proj = pl.program_id(1)

    def project(h, xx):
        xm = (h + xx * mu_ref[...]).astype(BF16)
        o_ref[...] = _dot(xm, w_ref[proj])

    @pl.when(proj == 0)
    def _():
        h, xx = _shifted(geo, x_ref, xp_ref, xn_ref, m_ref, g_ref, tm)
        h_ref[...] = h
        xx_ref[...] = xx
        project(h, xx)

    @pl.when(proj > 0)
    def _():
        project(h_ref[...], xx_ref[...])


def _rkv(geo, x, mods, g, mu3, w3):
    tm = TM_RWKV
    return pl.pallas_call(
        functools.partial(_rkv_kernel, geo=geo, tm=tm),
        grid=(geo.n_tok // tm, 3),
        in_specs=_halo_specs(geo, tm) + [
            pl.BlockSpec((None, 1, D_MODEL), lambda i, p: (p, 0, 0)),
            pl.BlockSpec((3, D_MODEL, D_MODEL), lambda i, p: (0, 0, 0), pipeline_mode=pl.Buffered(1)),
        ],
        out_specs=pl.BlockSpec((None, tm, D_MODEL), lambda i, p: (p, i, 0)),
        out_shape=jax.ShapeDtypeStruct((3, geo.n_tok, D_MODEL), F32),
        scratch_shapes=[pltpu.VMEM((tm, D_MODEL), F32), pltpu.VMEM((tm, D_MODEL), F32)],
        compiler_params=_params("parallel", "arbitrary"),
        name="rwkv_rkv",
    )(x, x, x, mods, g, mu3, w3)


def _lora_kernel(x_ref, xp_ref, xn_ref, m_ref, g_ref, mug_ref, mud_ref, g1_ref, g2_ref,
                 wa1_ref, wa2_ref, w0_ref, a0_ref, gate_ref, dec_ref, asig_ref, *, geo, tm):
    h, xx = _shifted(geo, x_ref, xp_ref, xn_ref, m_ref, g_ref, tm)
    mix = lambda mu: (h + xx * mu).astype(BF16)
    zg = _dot(mix(mug_ref[...]), g1_ref[...])
    gate_ref[...] = _dot(_sigmoid(zg).astype(BF16), g2_ref[...])
    for d in range(2):
        zw = _dot(mix(mud_ref[2 * d]), wa1_ref[d])
        wl = w0_ref[d] + _dot(jnp.tanh(zw).astype(BF16), wa2_ref[d])
        dec_ref[d] = -LOG_DECAY_SCALE * _sigmoid(wl)
        za = _dot(mix(mud_ref[2 * d + 1]), wa1_ref[2 + d])
        asig_ref[d] = _sigmoid(a0_ref[d] + _dot(za.astype(BF16), wa2_ref[2 + d]))


def _lora(geo, x, mods, g, mu_g, mu_dir, g1, g2, wa1, wa2, w0, a0):
    tm = TM_RWKV
    full = lambda a: pl.BlockSpec(a.shape, lambda i: (0,) * a.ndim)
    tok2 = pl.BlockSpec((2, tm, D_MODEL), lambda i: (0, i, 0))
    return pl.pallas_call(
        functools.partial(_lora_kernel, geo=geo, tm=tm),
        grid=(geo.n_tok // tm,),
        in_specs=_halo_specs(geo, tm) + [full(a) for a in (mu_g, mu_dir, g1, g2, wa1, wa2, w0, a0)],
        out_specs=[pl.BlockSpec((tm, D_MODEL), lambda i: (i, 0)), tok2, tok2],
        out_shape=[jax.ShapeDtypeStruct((geo.n_tok, D_MODEL), F32),
                   jax.ShapeDtypeStruct((2, geo.n_tok, D_MODEL), F32),
                   jax.ShapeDtypeStruct((2, geo.n_tok, D_MODEL), F32)],
        compiler_params=_params("parallel"),
        name="rwkv_lora",
    )(x, x, x, mods, g, mu_g, mu_dir, g1, g2, wa1, wa2, w0, a0)


def _head_sums(x, ones_blk):
    return _dot(x.astype(BF16), ones_blk)


def _head_ones(width):
    r = lax.broadcasted_iota(jnp.int32, (width, width), 0)
    c = lax.broadcasted_iota(jnp.int32, (width, width), 1)
    same = (r // RWKV_HEAD) == (c // RWKV_HEAD)
    return same, jnp.where(same, 1.0, 0.0).astype(BF16)


def _wkv_kernel(*refs, geo, tb, aliased):
    r_ref, k_ref, v_ref, dec_ref, asig_ref, kk_ref, ka_ref, rk_ref, s0_ref = refs[:9]
    y_ref, bon_ref, sf_fwd_ref, sf_bwd_ref, s_ref = refs[9 + (2 if aliased else 0):]
    d = pl.program_id(0)
    i = pl.program_id(1)
    nblk = pl.num_programs(1)
    blk = i + d * (nblk - 1 - 2 * i)
    start = blk * tb
    seq = jnp.where(start < geo.n_ctx, geo.s_ctx, geo.s_lat)
    rel = jnp.where(start < geo.n_ctx, start, start - geo.n_ctx)
    at_lo = (rel % seq) == 0
    at_hi = ((rel + tb) % seq) == 0
    first = jnp.where(d == 0, at_lo, at_hi)
    last = jnp.where(d == 0, at_hi, at_lo)

    @pl.when(first)
    def _():
        s_ref[...] = s0_ref[...]

    nch = tb // CHUNK
    sgn = 1 - 2 * d
    fwd = (d == 0).astype(F32)
    row = lax.broadcasted_iota(jnp.int32, (CHUNK, PAIR), 0)
    lane = lax.broadcasted_iota(jnp.int32, (CHUNK, PAIR), 1)
    sidx = lane & (RWKV_HEAD - 1)
    delta = (row - sidx) * sgn
    strict = delta > 0
    incl = delta >= 0
    eye = jnp.where(row == sidx, 1.0, 0.0)
    off_masks = []
    m = 1
    while m < CHUNK:
        off_masks.append(strict & ((row // (2 * m)) == (sidx // (2 * m))) & ((row // m) != (sidx // m)))
        m *= 2
    head0 = lane < RWKV_HEAD
    tr = lax.broadcasted_iota(jnp.int32, (CHUNK, 3 * CHUNK), 0)
    ts = lax.broadcasted_iota(jnp.int32, (CHUNK, 3 * CHUNK), 1) & (CHUNK - 1)
    tri3 = jnp.where((tr - ts) * sgn >= 0, 1.0, 0.0).astype(BF16)
    same_head, ones_blk = _head_ones(PAIR)

    def expand(x):
        return jnp.concatenate([jnp.where(head0, x, 0.0), jnp.where(head0, 0.0, x)], axis=0)

    def chunk_body(ci, carry):
        cc = ci * sgn + d * (nch - 1)
        rows = pl.ds(pl.multiple_of(cc * CHUNK, CHUNK), CHUNK)

        def group_body(pg, carry2):
            ps = [pg * PAIR_GROUP + q for q in range(PAIR_GROUP)]
            cols = [pl.ds(pl.multiple_of(p * PAIR, PAIR), PAIR) for p in ps]
            each = lambda f, *ls: [f(*xs) for xs in zip(*ls)]
            r = [r_ref[rows, c] for c in cols]
            k = [k_ref[rows, c] for c in cols]
            v = [v_ref[rows, c] for c in cols]
            logw = [dec_ref[rows, c] for c in cols]
            a = [asig_ref[rows, c] for c in cols]
            kk = [x * kk_ref[:, c] for x, c in zip(k, cols)]
            kd = [x * (1.0 + (y - 1.0) * ka_ref[:, c]) for x, y, c in zip(k, a, cols)]
            sums = [_head_sums(jnp.concatenate([x * x, y * z * rk_ref[:, c]], axis=0), ones_blk)
                    for x, y, z, c in zip(kk, r, kd, cols)]
            kk = each(lambda x, s: x / jnp.maximum(jnp.sqrt(s[:CHUNK]), 1e-12), kk, sums)
            b_in = each(lambda x, y: x * y, kk, a)
            for c, s, y in zip(cols, sums, v):
                bon_ref[rows, c] = s[CHUNK:] * y

            l1 =[x.astype(BF16) for x in logw]
            e1 = each(lambda x, y: x - y.astype(F32), logw, l1)
            l2 = [x.astype(BF16) for x in e1]
            l3 = each(lambda x, y: (x - y.astype(F32)).astype(BF16), e1, l2)
            cum = each(lambda x, y, z: _dot(tri3, jnp.concatenate([x, y, z], axis=0)), l1, l2, l3)
            tot = [fwd * x[CHUNK - 1:CHUNK, :] + (1.0 - fwd) * x[0:1, :] for x in cum]
            c_inv = [jnp.exp(-x) for x in cum]
            at = each(lambda x, y, z: -x * jnp.exp(y - z), kk, cum, logw)
            rt = each(lambda x, y: x * jnp.exp(y), r, cum)
            c_end = each(lambda x, y: jnp.exp(x - y), tot, cum)

            lhs = each(lambda x, y: jnp.concatenate([x, y], axis=0).astype(BF16), at, rt)
            rhs = each(lambda x, y, z: jnp.concatenate([expand(x * z), expand(y * z)], axis=0).astype(BF16),
                       b_in, kd, c_inv)
            g = each(_dot_nt, lhs, rhs)
            ab = [jnp.where(strict, x[:CHUNK, :PAIR], 0.0) for x in g]
            ak = [jnp.where(strict, x[:CHUNK, PAIR:], 0.0) for x in g]
            rbk = [jnp.concatenate([jnp.where(incl, x[CHUNK:, :PAIR], 0.0),
                                    jnp.where(incl, x[CHUNK:, PAIR:], 0.0)], axis=1).astype(BF16) for x in g]

            t_inv = [eye + jnp.where(off_masks[0], x, 0.0) for x in ab]
            for off in off_masks[1:]:
                lx = each(lambda x, t: _dot(jnp.where(off, x, 0.0).astype(BF16), expand(t).astype(BF16)), ab, t_inv)
                t_inv = each(lambda t, x: t + _dot(t.astype(BF16), expand(x).astype(BF16)), t_inv, lx)

            s2 = [s_ref[p] for p in ps]
            ars = each(lambda x, s: _dot_nt(x, s.astype(BF16)), lhs, s2)
            vexp = [expand(x).astype(BF16) for x in v]
            rhs_u = each(lambda x, y, z: x[:CHUNK] + _dot(y.astype(BF16), z), ars, ak, vexp)
            u = each(lambda t, x: _dot(t.astype(BF16), expand(x).astype(BF16)), t_inv, rhs_u)
            uv = each(lambda x, y: jnp.concatenate([expand(x).astype(BF16), y], axis=0), u, vexp)
            y_out = each(lambda x, y, z: x[CHUNK:] + _dot(y, z), ars, rbk, uv)
            for c, x in zip(cols, y_out):
                y_ref[rows, c] = x
            uv_t = each(lambda x, y: jnp.concatenate([x, y], axis=0).T.astype(BF16), u, v)
            bk = each(lambda x, y, z: jnp.concatenate([x * z, y * z], axis=0).astype(BF16), b_in, kd, c_end)
            upd = each(_dot, uv_t, bk)
            for p, s, x, t in zip(ps, s2, upd, tot):
                s_ref[p] = s * jnp.exp(t) + jnp.where(same_head, x, 0.0)
            return carry2

        lax.fori_loop(0, N_PAIRS // PAIR_GROUP, group_body, 0)
        return carry

    lax.fori_loop(0, nch, chunk_body, 0)

    def write_states(dst_ref):
        for p in range(N_PAIRS):
            s = s_ref[p]
            dst_ref[2 * p] = s[:RWKV_HEAD, :RWKV_HEAD]
            dst_ref[2 * p + 1] = s[RWKV_HEAD:, RWKV_HEAD:]

    ends_ctx = last & (start < geo.n_ctx)

    @pl.when(ends_ctx & (d == 0))
    def _():
        write_states(sf_fwd_ref)

    @pl.when(ends_ctx & (d == 1))
    def _():
        write_states(sf_bwd_ref)


def _wkv(geo, rkv, dec, asig, kk_c, ka_c, rk_c, s0, j, n_layers, prev_states):
    tb = TB_WKV
    nblk = geo.n_tok // tb
    blk = lambda d, i: i + d * (nblk - 1 - 2 * i)
    tok = lambda which: pl.BlockSpec((None, tb, D_MODEL), lambda d, i: (which, blk(d, i), 0))
    perdir = pl.BlockSpec((None, tb, D_MODEL), lambda d, i: (d, blk(d, i), 0))
    const = pl.BlockSpec((1, D_MODEL), lambda d, i: (0, 0))
    last_seq = geo.b_ctx - 1
    seq = lambda d, i: jnp.minimum(blk(d, i) * tb // geo.s_ctx, last_seq)
    final = lambda which: pl.BlockSpec(
        (None, None, RWKV_HEADS, RWKV_HEAD, RWKV_HEAD),
        lambda d, i: (jnp.where(d == which, seq(d, i), last_seq), j, 0, 0, 0))
    final_shape = jax.ShapeDtypeStruct((geo.b_ctx, n_layers, RWKV_HEADS, RWKV_HEAD, RWKV_HEAD), F32)
    in_specs = [tok(0), tok(1), tok(2), perdir, perdir, const, const, const,
                pl.BlockSpec((None, None, N_PAIRS, PAIR, PAIR), lambda d, i: (d, geo.cond_row(blk(d, i), tb), 0, 0, 0))]
    args = [rkv, rkv, rkv, dec, asig, kk_c, ka_c, rk_c, s0]
    aliases = {}
    if prev_states is not None:
        in_specs += [pl.BlockSpec(memory_space=pl.ANY)] * 2
        aliases = {len(args): 2, len(args) + 1: 3}
        args += list(prev_states)
    return pl.pallas_call(
        functools.partial(_wkv_kernel, geo=geo, tb=tb, aliased=prev_states is not None),
        grid=(2, nblk),
        in_specs=in_specs,
        out_specs=[perdir, perdir, final(0), final(1)],
        out_shape=[jax.ShapeDtypeStruct((2, geo.n_tok, D_MODEL), F32),
                   jax.ShapeDtypeStruct((2, geo.n_tok, D_MODEL), F32), final_shape, final_shape],
        input_output_aliases=aliases,
        scratch_shapes=[pltpu.VMEM((N_PAIRS, PAIR, PAIR), F32)],
        compiler_params=_params("arbitrary", "arbitrary"),
        name="wkv_chunked",
    )(*args)


def _rwkv_out_kernel(y_ref, bon_ref, gate_ref, lw_ref, lb_ref, w_ref, x_ref, m_ref, o_ref, z_ref):
    _, ones_blk = _head_ones(GN_SLAB)
    inv_n = 1.0 / RWKV_HEAD
    for s in range(D_MODEL // GN_SLAB):
        cols = slice(s * GN_SLAB, (s + 1) * GN_SLAB)
        y = y_ref[0, :, cols] + y_ref[1, :, cols]
        yc = y - _head_sums(y, ones_blk) * inv_n
        var = _head_sums(yc * yc, ones_blk) * inv_n
        z = yc * lax.rsqrt(var + GN_EPS) * lw_ref[:, cols] + lb_ref[:, cols] + bon_ref[0, :, cols] + bon_ref[1, :, cols]
        z_ref[:, cols] = (z * gate_ref[:, cols]).astype(BF16)
    o_ref[...] = x_ref[...] + m_ref[5:6, :] * _dot(z_ref[...], w_ref[...])


def _rwkv_out(geo, y, bon, gate, lw, lb, w, x, mods):
    tm = TM_OUT
    row = pl.BlockSpec((tm, D_MODEL), lambda i: (i, 0))
    both = pl.BlockSpec((2, tm, D_MODEL), lambda i: (0, i, 0))
    const = pl.BlockSpec((1, D_MODEL), lambda i: (0, 0))
    return pl.pallas_call(
        _rwkv_out_kernel,
        grid=(geo.n_tok // tm,),
        in_specs=[both, both, row, const, const, pl.BlockSpec((D_MODEL, D_MODEL), lambda i: (0, 0)), row,
                  pl.BlockSpec((None, N_MOD, D_MODEL), lambda i: (geo.cond_row(i, tm), 0, 0))],
        out_specs=row,
        out_shape=jax.ShapeDtypeStruct((geo.n_tok, D_MODEL), F32),
        scratch_shapes=[pltpu.VMEM((tm, D_MODEL), BF16)],
        compiler_params=_params("parallel"),
        name="rwkv_out",
    )(y, bon, gate, lw, lb, w, x, mods)


def _state_to_pairs(s):
    b = s.shape[0]
    s5 = s.reshape(b, N_PAIRS, 2, RWKV_HEAD, RWKV_HEAD)
    z = jnp.zeros_like(s5[:, :, 0])
    top = jnp.concatenate([s5[:, :, 0], z], axis=-1)
    bot = jnp.concatenate([z, s5[:, :, 1]], axis=-1)
    return jnp.concatenate([top, bot], axis=-2)


def _rope_tables(geo):
    rows = geo.s_lat // GRID_W
    row = jnp.repeat(jnp.arange(rows, dtype=F32), GRID_W)
    col = jnp.tile(jnp.arange(GRID_W, dtype=F32), rows)
    inv = ROPE_THETA ** (-jnp.arange(ROPE_FREQS, dtype=F32) / ROPE_FREQS)
    ang = jnp.stack([row[:, None] * inv, col[:, None] * inv], axis=1)
    ang = jnp.broadcast_to(ang[:, :, None, :], (geo.s_lat, 2, 2, ROPE_FREQS)).reshape(geo.s_lat, QK_ROPE)
    cos = jnp.concatenate([jnp.ones((geo.n_ctx, QK_ROPE), F32), jnp.tile(jnp.cos(ang), (geo.b_lat, 1))], axis=0)
    sin = jnp.concatenate([jnp.zeros((geo.n_ctx, QK_ROPE), F32), jnp.tile(jnp.sin(ang), (geo.b_lat, 1))], axis=0)
    return cos, sin


def _rot_cols(w):
    w4 = w.reshape(w.shape[:-1] + (2, 2, ROPE_FREQS))
    return jnp.stack([-w4[..., 1, :], w4[..., 0, :]], axis=-2).reshape(w.shape)


def _mla_layer(geo, x, mods, g, j, cos, sin, cache_ckv, cache_krope, p):
    w_dkv = p['mla_w_dkv'][j]
    w_cat = jnp.concatenate([p['mla_w_dq'][j], w_dkv, _rot_cols(w_dkv[:, KV_LORA:])], axis=1).astype(BF16)
    cq, ckv, ckvb, kr = _mla_proj(geo, x, mods, g, w_cat, p['mla_q_norm'][j][None], p['mla_kv_norm'][j][None],
                                  cos, sin)
    w_uq = p['mla_w_uq'][j].reshape(Q_LORA, MLA_HEADS, QK_NOPE + QK_ROPE).transpose(1, 0, 2)
    wq = w_uq.astype(BF16)
    wqr = _rot_cols(w_uq[..., QK_NOPE:]).astype(BF16)
    wkv = p['mla_w_ukv'][j].reshape(KV_LORA, MLA_HEADS, QK_NOPE + V_DIM).transpose(1, 0, 2).astype(BF16)
    o_ctx = _attn(geo, cq, ckvb, kr, wq, wqr, wkv, latent=False)
    o = _attn(geo, cq, ckvb, kr, wq, wqr, wkv, latent=True,
              past_ckv=cache_ckv[:, j], past_kr=cache_krope[:, j], cos=cos, sin=sin, o_ctx=o_ctx)
    x = _out_proj(geo, o, p['mla_w_o'][j].astype(BF16), x, mods)
    new_ckv = ckv[:geo.n_ctx].reshape(geo.b_ctx, geo.s_ctx, KV_LORA)
    new_kr = kr[:geo.n_ctx].reshape(geo.b_ctx, geo.s_ctx, QK_ROPE)
    return x, new_ckv, new_kr


def _rwkv_layer(geo, x, mods, g, j, s0_fwd, s0_bwd, prev_states, p):
    mu = p['rwkv_mu'][j]
    mu_dir = p['rwkv_mu_dir'][j].reshape(4, 1, D_MODEL)
    w3 = jnp.stack([p['rwkv_w_r'][j], p['rwkv_w_k'][j], p['rwkv_w_v'][j]]).astype(BF16)
    rkv = _rkv(geo, x, mods, g, mu[:3, None, :], w3)
    pad1 = lambda w: jnp.pad(w, ((0, 0), (0, 0), (0, LORA_PAD - w.shape[-1])))
    pad2 = lambda w: jnp.pad(w, ((0, 0), (0, LORA_PAD - w.shape[-2]), (0, 0)))
    wa1 = jnp.concatenate([pad1(p['rwkv_w1'][j]), pad1(p['rwkv_a1'][j])]).astype(BF16)
    wa2 = jnp.concatenate([pad2(p['rwkv_w2'][j]), pad2(p['rwkv_a2'][j])]).astype(BF16)
    gate, dec, asig = _lora(geo, x, mods, g, mu[3:4], mu_dir, p['rwkv_g1'][j].astype(BF16),
                            p['rwkv_g2'][j].astype(BF16), wa1, wa2,
                            p['rwkv_w0'][j][:, None, :], p['rwkv_a0'][j][:, None, :])
    zero = jnp.zeros((2, 1, N_PAIRS, PAIR, PAIR), F32)
    s0 = jnp.concatenate([zero, jnp.stack([_state_to_pairs(s0_fwd), _state_to_pairs(s0_bwd)])], axis=1)
    y, bon, sf, sb = _wkv(geo, rkv, dec, asig, p['rwkv_k_k'][j][None], p['rwkv_k_a'][j][None],
                          p['rwkv_r_k'][j].reshape(1, D_MODEL), s0, j, p['rwkv_mu'].shape[0], prev_states)
    x = _rwkv_out(geo, y, bon, gate, p['rwkv_ln_w'][j][None], p['rwkv_ln_b'][j][None],
                  p['rwkv_w_o'][j].astype(BF16), x, mods)
    return x, (sf, sb)


def kernel(x_prompt, x_sample, cache_ckv, cache_krope, state_wkv_fwd, state_wkv_bwd, c, c_ctx, w_ada, b_ada, norm_sub, norm_final, w_ffn_in, w_ffn_out, mla_w_dq, mla_q_norm, mla_w_uq, mla_w_dkv, mla_kv_norm, mla_w_ukv, mla_w_o, rwkv_mu, rwkv_mu_dir, rwkv_w_r, rwkv_w_k, rwkv_w_v, rwkv_w0, rwkv_w1, rwkv_w2, rwkv_a0, rwkv_a1, rwkv_a2, rwkv_g1, rwkv_g2, rwkv_k_k, rwkv_k_a, rwkv_r_k, rwkv_ln_w, rwkv_ln_b, rwkv_w_o):
    p = dict(mla_w_dq=mla_w_dq, mla_q_norm=mla_q_norm, mla_w_uq=mla_w_uq, mla_w_dkv=mla_w_dkv,
             mla_kv_norm=mla_kv_norm, mla_w_ukv=mla_w_ukv, mla_w_o=mla_w_o,
             rwkv_mu=rwkv_mu, rwkv_mu_dir=rwkv_mu_dir, rwkv_w_r=rwkv_w_r, rwkv_w_k=rwkv_w_k,
             rwkv_w_v=rwkv_w_v, rwkv_w0=rwkv_w0, rwkv_w1=rwkv_w1, rwkv_w2=rwkv_w2,
             rwkv_a0=rwkv_a0, rwkv_a1=rwkv_a1, rwkv_a2=rwkv_a2, rwkv_g1=rwkv_g1, rwkv_g2=rwkv_g2,
             rwkv_k_k=rwkv_k_k, rwkv_k_a=rwkv_k_a, rwkv_r_k=rwkv_r_k,
             rwkv_ln_w=rwkv_ln_w, rwkv_ln_b=rwkv_ln_b, rwkv_w_o=rwkv_w_o)
    b_ctx, s_ctx, _ = x_prompt.shape
    b_lat, s_lat, _ = x_sample.shape
    geo = _Geom(b_ctx, s_ctx, b_lat, s_lat)
    assert geo.n_ctx % s_lat == 0 and s_ctx % TM_RWKV == 0 and s_lat % TM_FFN == 0
    assert s_ctx == TB_WKV and s_lat % TB_WKV == 0

    x = jnp.concatenate([x_prompt.reshape(geo.n_ctx, D_MODEL), x_sample.reshape(geo.n_lat, D_MODEL)], axis=0)
    cond = jnp.concatenate([c_ctx[None], c, jnp.zeros((COND_ROWS - 1 - b_lat, D_MODEL), F32)], axis=0)
    mods_all = _ada(cond, w_ada, b_ada).reshape(DEPTH, COND_ROWS, N_MOD, D_MODEL)
    cos, sin = _rope_tables(geo)
    nf = norm_final[None]
    w_in = w_ffn_in[0, 0].astype(BF16)
    w_out = w_ffn_out[0, 0].astype(BF16)

    ckv_l, kr_l, states = [], [], None
    for l in range(DEPTH):
        j = l // N_MIXERS
        mods = mods_all[l]
        x, w_in, w_out = _ffn(geo, x, mods, norm_sub[l, 0][None], w_in, w_out, nf, sub=0, final=False,
                              nxt=(w_ffn_in, w_ffn_out, l, 1))
        if l % N_MIXERS == 0:
            x, ckv, kr = _mla_layer(geo, x, mods, norm_sub[l, 1][None], j, cos, sin, cache_ckv, cache_krope, p)
            ckv_l.append(ckv)
            kr_l.append(kr)
        else:
            x, states = _rwkv_layer(geo, x, mods, norm_sub[l, 1][None], j,
                                    state_wkv_fwd[:, j], state_wkv_bwd[:, j], states, p)
        if l < DEPTH - 1:
            x, w_in, w_out = _ffn(geo, x, mods, norm_sub[l, 2][None], w_in, w_out, nf, sub=2, final=False,
                                  nxt=(w_ffn_in, w_ffn_out, l + 1, 0))
        else:
            x, = _ffn(geo, x, mods, norm_sub[l, 2][None], w_in, w_out, nf, sub=2, final=True)

    y_prompt = x[:geo.n_ctx].reshape(b_ctx, s_ctx, D_MODEL)
    y_sample = x[geo.n_ctx:].reshape(b_lat, s_lat, D_MODEL)
    return (y_prompt, y_sample, jnp.stack(ckv_l, axis=1), jnp.stack(kr_l, axis=1),
            states[0], states[1])
```

```python
import functools

import jax
import jax.numpy as jnp
from jax import lax
from jax.experimental import pallas as pl
from jax.experimental.pallas import tpu as pltpu

D_MODEL = 2048
DEPTH = 4
N_MIXERS = 2
D_FF = 5632
N_MOD = 9
RMS_EPS = 1e-6
MLA_HEADS = 16
Q_LORA = 512
KV_LORA = 512
QK_NOPE = 128
QK_ROPE = 64
V_DIM = 128
ROPE_FREQS = QK_ROPE // 4
ROPE_THETA = 10000.0
GRID_W = 64
ATTN_SCALE = (QK_NOPE + QK_ROPE) ** -0.5
RWKV_HEAD = 64
RWKV_HEADS = D_MODEL // RWKV_HEAD
GN_EPS = 64e-5
LOG_DECAY_SCALE = 0.6065306597126334
LORA_PAD = 128

COND_ROWS = 8
VMEM_LIMIT = 56 * 1024 * 1024

TM_FFN = 512
TF_FFN = 512
CAST_TILE = 128
TM_PROJ = 512
TM_RWKV = 256
TN_ADA = 1024
Q_TILE = 256
ATTN_LOCKSTEP = 8
CHUNK = 64
PAIR = 2 * RWKV_HEAD
N_PAIRS = D_MODEL // PAIR
PAIR_GROUP = 16
TB_WKV = 256
TM_OUT = 256
GN_SLAB = 256

BF16 = jnp.bfloat16
F32 = jnp.float32


def _params(*sem):
    return pltpu.CompilerParams(dimension_semantics=sem, vmem_limit_bytes=VMEM_LIMIT)


def _sigmoid(x):
    return 1.0 / (1.0 + jnp.exp(-x))


def _modulate(x, g, shift, scale):
    ms = jnp.mean(x * x, axis=-1, keepdims=True)
    return (x * lax.rsqrt(ms + RMS_EPS) * g) * (1.0 + scale) + shift


def _rms(x, w):
    ms = jnp.mean(x * x, axis=-1, keepdims=True)
    return x * lax.rsqrt(ms + RMS_EPS) * w


def _dot(a, b):
    return jnp.dot(a, b, preferred_element_type=F32)


def _dot_nt(a, b):
    return lax.dot_general(a, b, (((1,), (1,)), ((), ())), preferred_element_type=F32)


class _Geom:
    def __init__(self, b_ctx, s_ctx, b_lat, s_lat):
        self.b_ctx, self.s_ctx, self.b_lat, self.s_lat = b_ctx, s_ctx, b_lat, s_lat
        self.n_ctx = b_ctx * s_ctx
        self.n_lat = b_lat * s_lat
        self.n_tok = self.n_ctx + self.n_lat

    def cond_row(self, i, tm):
        start = i * tm
        return jnp.where(start < self.n_ctx, 0, 1 + (start - self.n_ctx) // self.s_lat)


def _ada_kernel(c_ref, w_ref, b_ref, o_ref):
    c = c_ref[...]
    s = (c * _sigmoid(c)).astype(BF16)
    o_ref[...] = _dot(s, w_ref[...].astype(BF16)) + b_ref[...]


def _ada(cond, w_ada, b_ada):
    n = N_MOD * D_MODEL
    return pl.pallas_call(
        _ada_kernel,
        grid=(DEPTH, n // TN_ADA),
        in_specs=[
            pl.BlockSpec((COND_ROWS, D_MODEL), lambda l, j: (0, 0)),
            pl.BlockSpec((None, D_MODEL, TN_ADA), lambda l, j: (l, 0, j)),
            pl.BlockSpec((None, 1, TN_ADA), lambda l, j: (l, 0, j)),
        ],
        out_specs=pl.BlockSpec((None, COND_ROWS, TN_ADA), lambda l, j: (l, 0, j)),
        out_shape=jax.ShapeDtypeStruct((DEPTH, COND_ROWS, n), F32),
        compiler_params=_params("parallel", "parallel"),
        name="ada",
    )(cond, w_ada, b_ada.reshape(DEPTH, 1, n))


def _ffn_kernel(*refs, sub, final, cast_next):
    if cast_next:
        (x_ref, m_ref, g_ref, wg_ref, wu_ref, wo_ref, nf_ref, ci_ref, co_ref,
         o_ref, cib_ref, cob_ref, h_ref, acc_ref) = refs
    else:
        x_ref, m_ref, g_ref, wg_ref, wu_ref, wo_ref, nf_ref, o_ref, h_ref, acc_ref = refs
    f = pl.program_id(1)

    def swiglu_slice(h):
        gate = _dot(h, wg_ref[...])
        up = _dot(h, wu_ref[...])
        act = (gate * _sigmoid(gate) * up).astype(BF16)
        return _dot(act, wo_ref[...])

    @pl.when(f == 0)
    def _():
        h = _modulate(x_ref[...], g_ref[...], m_ref[3 * sub:3 * sub + 1, :], m_ref[3 * sub + 1:3 * sub + 2, :])
        h = h.astype(BF16)
        h_ref[...] = h
        acc_ref[...] = swiglu_slice(h)

    @pl.when(f > 0)
    def _():
        acc_ref[...] += swiglu_slice(h_ref[...])

    if cast_next:
        step = pl.program_id(0) * pl.num_programs(1) + f
        n_in, n_out = cast_next

        @pl.when(step < n_in)
        def _():
            cib_ref[...] = ci_ref[...].astype(BF16)

        @pl.when((step >= n_in) & (step < n_in + n_out))
        def _():
            cob_ref[...] = co_ref[...].astype(BF16)

    @pl.when(f == pl.num_programs(1) - 1)
    def _():
        y = x_ref[...] + 0.5 * m_ref[3 * sub + 2:3 * sub + 3, :] * acc_ref[...]
        if final:
            y = _rms(y, nf_ref[...])
        o_ref[...] = y


def _ffn(geo, x, mods, g, w_in, w_out, nf, *, sub, final, nxt=None):
    tm, tf = TM_FFN, TF_FFN
    nf_blocks = D_FF // tf
    n_in, n_out = 2 * D_FF // CAST_TILE, D_FF // CAST_TILE
    step = lambda i, f: i * nf_blocks + f
    in_blk = lambda i, f: jnp.minimum(step(i, f), n_in - 1)
    out_blk = lambda i, f: jnp.clip(step(i, f) - n_in, 0, n_out - 1)
    in_specs = [
        pl.BlockSpec((tm, D_MODEL), lambda i, f: (i, 0)),
        pl.BlockSpec((None, N_MOD, D_MODEL), lambda i, f: (geo.cond_row(i, tm), 0, 0)),
        pl.BlockSpec((1, D_MODEL), lambda i, f: (0, 0)),
        pl.BlockSpec((D_MODEL, tf), lambda i, f: (0, f)),
        pl.BlockSpec((D_MODEL, tf), lambda i, f: (0, f + nf_blocks)),
        pl.BlockSpec((tf, D_MODEL), lambda i, f: (f, 0)),
        pl.BlockSpec((1, D_MODEL), lambda i, f: (0, 0)),
    ]
    out_specs = [pl.BlockSpec((tm, D_MODEL), lambda i, f: (i, 0))]
    out_shape = [jax.ShapeDtypeStruct((geo.n_tok, D_MODEL), F32)]
    args = [x, mods, g, w_in, w_in, w_out, nf]
    if nxt is not None:
        w_in_all, w_out_all, layer, half = nxt
        assert (geo.n_tok // tm) * nf_blocks >= n_in + n_out
        in_specs += [
            pl.BlockSpec((None, None, D_MODEL, CAST_TILE), lambda i, f: (layer, half, 0, in_blk(i, f))),
            pl.BlockSpec((None, None, CAST_TILE, D_MODEL), lambda i, f: (layer, half, out_blk(i, f), 0)),
        ]
        out_specs += [
            pl.BlockSpec((D_MODEL, CAST_TILE), lambda i, f: (0, in_blk(i, f))),
            pl.BlockSpec((CAST_TILE, D_MODEL), lambda i, f: (out_blk(i, f), 0)),
        ]
        out_shape += [jax.ShapeDtypeStruct((D_MODEL, 2 * D_FF), BF16), jax.ShapeDtypeStruct((D_FF, D_MODEL), BF16)]
        args += [w_in_all, w_out_all]
    return pl.pallas_call(
        functools.partial(_ffn_kernel, sub=sub, final=final, cast_next=(n_in, n_out) if nxt is not None else None),
        grid=(geo.n_tok // tm, nf_blocks),
        in_specs=in_specs,
        out_specs=out_specs,
        out_shape=out_shape,
        scratch_shapes=[pltpu.VMEM((tm, D_MODEL), BF16), pltpu.VMEM((tm, D_MODEL), F32)],
        compiler_params=_params("arbitrary", "arbitrary"),
        name="ffn",
    )(*args)


def _mla_proj_kernel(x_ref, m_ref, g_ref, w_ref, qn_ref, kvn_ref, cos_ref, sin_ref,
                     cq_ref, ckv_ref, ckvb_ref, kr_ref):
    h = _modulate(x_ref[...], g_ref[...], m_ref[3:4, :], m_ref[4:5, :]).astype(BF16)
    z = _dot(h, w_ref[...])
    cq_ref[...] = _rms(z[:, :Q_LORA], qn_ref[...]).astype(BF16)
    ckv = _rms(z[:, Q_LORA:Q_LORA + KV_LORA], kvn_ref[...])
    ckv_ref[...] = ckv
    ckvb_ref[...] = ckv.astype(BF16)
    o = Q_LORA + KV_LORA
    kr_ref[...] = z[:, o:o + QK_ROPE] * cos_ref[...] + z[:, o + QK_ROPE:o + 2 * QK_ROPE] * sin_ref[...]


def _mla_proj(geo, x, mods, g, w_cat, qn, kvn, cos, sin):
    tm = TM_PROJ
    n_out = w_cat.shape[1]
    row = lambda i: (i, 0)
    fix = lambda i: (0, 0)
    return pl.pallas_call(
        _mla_proj_kernel,
        grid=(geo.n_tok // tm,),
        in_specs=[
            pl.BlockSpec((tm, D_MODEL), row),
            pl.BlockSpec((None, N_MOD, D_MODEL), lambda i: (geo.cond_row(i, tm), 0, 0)),
            pl.BlockSpec((1, D_MODEL), fix),
            pl.BlockSpec((D_MODEL, n_out), fix),
            pl.BlockSpec((1, Q_LORA), fix),
            pl.BlockSpec((1, KV_LORA), fix),
            pl.BlockSpec((tm, QK_ROPE), row),
            pl.BlockSpec((tm, QK_ROPE), row),
        ],
        out_specs=[
            pl.BlockSpec((tm, Q_LORA), row),
            pl.BlockSpec((tm, KV_LORA), row),
            pl.BlockSpec((tm, KV_LORA), row),
            pl.BlockSpec((tm, QK_ROPE), row),
        ],
        out_shape=[
            jax.ShapeDtypeStruct((geo.n_tok, Q_LORA), BF16),
            jax.ShapeDtypeStruct((geo.n_tok, KV_LORA), F32),
            jax.ShapeDtypeStruct((geo.n_tok, KV_LORA), BF16),
            jax.ShapeDtypeStruct((geo.n_tok, QK_ROPE), F32),
        ],
        compiler_params=_params("parallel"),
        name="mla_proj",
    )(x, mods, g, w_cat, qn, kvn, cos, sin)


def _attn_kernel(*refs, s_len, t_past, rope):
    if rope:
        (cq_ref, ckvb_ref, kr_ref, pckv_ref, pkr_ref, cos_ref, sin_ref,
         wq_ref, wqr_ref, wkv_ref, _, o_ref, kv_all, kr_all) = refs
    else:
        cq_ref, ckvb_ref, kr_ref, wq_ref, wkv_ref, o_ref, kv_all, kr_all = refs
    if t_past:
        kv_all[0:t_past, :] = pckv_ref[...].astype(BF16)
        kr_all[0:t_past, :] = pkr_ref[...].astype(BF16)
    kv_all[t_past:, :] = ckvb_ref[...]
    kr_all[t_past:, :] = kr_ref[...].astype(BF16)

    n_q = s_len // Q_TILE
    hg = max(1, ATTN_LOCKSTEP // n_q)
    row_slices = [slice(qb * Q_TILE, (qb + 1) * Q_TILE) for qb in range(n_q)]

    def head_group(g, carry):
        heads = [g * hg + i for i in range(hg)]
        kv = kv_all[...]
        krb = kr_all[...]
        kvp = [_dot(kv, wkv_ref[hd]) for hd in heads]
        kn = [x[:, :QK_NOPE].astype(BF16) for x in kvp]
        v = [x[:, QK_NOPE:].astype(BF16) for x in kvp]
        items = [(i, qb) for i in range(hg) for qb in range(n_q)]
        cq = [cq_ref[rows, :] for rows in row_slices]
        q = [_dot(cq[qb], wq_ref[heads[i]]) for i, qb in items]
        qr = [x[:, QK_NOPE:] for x in q]
        if rope:
            rot = [_dot(cq[qb], wqr_ref[heads[i]]) for i, qb in items]
            qr = [x * cos_ref[row_slices[qb], :] + y * sin_ref[row_slices[qb], :]
                  for x, y, (i, qb) in zip(qr, rot, items)]
        s = [(_dot_nt(x[:, :QK_NOPE].astype(BF16), kn[i]) + _dot_nt(y.astype(BF16), krb)) * ATTN_SCALE
             for x, y, (i, qb) in zip(q, qr, items)]
        p = [jnp.exp(x - jnp.max(x, axis=-1, keepdims=True)) for x in s]
        pr = [(x * (1.0 / jnp.sum(x, axis=-1, keepdims=True))).astype(BF16) for x in p]
        o = [_dot(x, v[i]).astype(BF16) for x, (i, qb) in zip(pr, items)]
        for x, (i, qb) in zip(o, items):
            o_ref[row_slices[qb], pl.ds(pl.multiple_of(heads[i] * V_DIM, V_DIM), V_DIM)] = x
        return carry

    lax.fori_loop(0, MLA_HEADS // hg, head_group, 0)


def _attn(geo, cq, ckvb, kr, wq, wqr, wkv, *, latent, past_ckv=None, past_kr=None, cos=None, sin=None, o_ctx=None):
    if latent:
        nb, s_len, off = geo.b_lat, geo.s_lat, geo.n_ctx // geo.s_lat
        t_past = past_ckv.shape[1]
    else:
        nb, s_len, off, t_past = geo.b_ctx, geo.s_ctx, 0, 0
    row = lambda b: (b + off, 0)
    fix3 = lambda b: (0, 0, 0)
    tok = lambda width: pl.BlockSpec((s_len, width), row)
    wspec = lambda w: pl.BlockSpec(w.shape, fix3)
    in_specs = [tok(Q_LORA), tok(KV_LORA), tok(QK_ROPE)]
    args = [cq, ckvb, kr]
    if latent:
        in_specs += [pl.BlockSpec((None, t_past, KV_LORA), lambda b: (b, 0, 0)),
                     pl.BlockSpec((None, t_past, QK_ROPE), lambda b: (b, 0, 0)),
                     tok(QK_ROPE), tok(QK_ROPE), wspec(wq), wspec(wqr)]
        args += [past_ckv, past_kr, cos, sin, wq, wqr]
    else:
        in_specs += [wspec(wq)]
        args += [wq]
    in_specs += [wspec(wkv)]
    args += [wkv]
    aliases = {}
    if latent:
        in_specs += [pl.BlockSpec(memory_space=pl.ANY)]
        args += [o_ctx]
        aliases = {len(args) - 1: 0}
    kern = functools.partial(_attn_kernel, s_len=s_len, t_past=t_past, rope=latent)
    return pl.pallas_call(
        kern,
        grid=(nb,),
        in_specs=in_specs,
        out_specs=pl.BlockSpec((s_len, D_MODEL), row),
        out_shape=jax.ShapeDtypeStruct((geo.n_tok, D_MODEL), BF16),
        input_output_aliases=aliases,
        scratch_shapes=[pltpu.VMEM((t_past + s_len, KV_LORA), BF16),
                        pltpu.VMEM((t_past + s_len, QK_ROPE), BF16)],
        compiler_params=_params("parallel"),
        name="attn_lat" if latent else "attn_ctx",
    )(*args)


def _out_proj_kernel(a_ref, w_ref, x_ref, m_ref, o_ref):
    o_ref[...] = x_ref[...] + m_ref[5:6, :] * _dot(a_ref[...], w_ref[...])


def _out_proj(geo, a, w, x, mods):
    tm = TM_PROJ
    row = pl.BlockSpec((tm, D_MODEL), lambda i: (i, 0))
    return pl.pallas_call(
        _out_proj_kernel,
        grid=(geo.n_tok // tm,),
        in_specs=[row, pl.BlockSpec((D_MODEL, D_MODEL), lambda i: (0, 0)), row,
                  pl.BlockSpec((None, N_MOD, D_MODEL), lambda i: (geo.cond_row(i, tm), 0, 0))],
        out_specs=row,
        out_shape=jax.ShapeDtypeStruct((geo.n_tok, D_MODEL), F32),
        compiler_params=_params("parallel"),
        name="out_proj",
    )(a, w, x, mods)


def _shifted(geo, x_ref, xp_ref, xn_ref, m_ref, g_ref, tm):
    i = pl.program_id(0)
    g, shift, scale = g_ref[...], m_ref[3:4, :], m_ref[4:5, :]
    h = _modulate(x_ref[...], g, shift, scale)
    start = i * tm
    seq = jnp.where(start < geo.n_ctx, geo.s_ctx, geo.s_lat)
    rel = jnp.where(start < geo.n_ctx, start, start - geo.n_ctx)
    has_prev = (rel % seq) != 0
    has_next = ((rel + tm) % seq) != 0
    hp = _modulate(xp_ref[...], g, shift, scale)[7:8, :]
    hn = _modulate(xn_ref[...], g, shift, scale)[0:1, :]
    hp = jnp.where(has_prev, hp, 0.0)
    hn = jnp.where(has_next, hn, 0.0)
    r = lax.broadcasted_iota(jnp.int32, h.shape, 0)
    down = jnp.where(r == 0, hp, pltpu.roll(h, 1, 0))
    up = jnp.where(r == tm - 1, hn, pltpu.roll(h, tm - 1, 0))
    return h, 0.5 * (down + up) - h


def _halo_specs(geo, tm):
    nb8 = geo.n_tok // 8
    return [
        pl.BlockSpec((tm, D_MODEL), lambda i, *_: (i, 0)),
        pl.BlockSpec((8, D_MODEL), lambda i, *_: (jnp.maximum(i * (tm // 8) - 1, 0), 0)),
        pl.BlockSpec((8, D_MODEL), lambda i, *_: (jnp.minimum((i + 1) * (tm // 8), nb8 - 1), 0)),
        pl.BlockSpec((None, N_MOD, D_MODEL), lambda i, *_: (geo.cond_row(i, tm), 0, 0)),
        pl.BlockSpec((1, D_MODEL), lambda i, *_: (0, 0)),
    ]


def _rkv_kernel(x_ref, xp_ref, xn_ref, m_ref, g_ref, mu_ref, w_ref, o_ref, h_ref, xx_ref, *, geo, tm):
    proj = pl.program_id(1)

    def project(h, xx):
        xm = (h + xx * mu_ref[...]).astype(BF16)
        o_ref[...] = _dot(xm, w_ref[proj])

    @pl.when(proj == 0)
    def _():
        h, xx = _shifted(geo, x_ref, xp_ref, xn_ref, m_ref, g_ref, tm)
        h_ref[...] = h
        xx_ref[...] = xx
        project(h, xx)

    @pl.when(proj > 0)
    def _():
        project(h_ref[...], xx_ref[...])


def _rkv(geo, x, mods, g, mu3, w3):
    tm = TM_RWKV
    return pl.pallas_call(
        functools.partial(_rkv_kernel, geo=geo, tm=tm),
        grid=(geo.n_tok // tm, 3),
        in_specs=_halo_specs(geo, tm) + [
            pl.BlockSpec((None, 1, D_MODEL), lambda i, p: (p, 0, 0)),
            pl.BlockSpec((3, D_MODEL, D_MODEL), lambda i, p: (0, 0, 0), pipeline_mode=pl.Buffered(1)),
        ],
        out_specs=pl.BlockSpec((None, tm, D_MODEL), lambda i, p: (p, i, 0)),
        out_shape=jax.ShapeDtypeStruct((3, geo.n_tok, D_MODEL), F32),
        scratch_shapes=[pltpu.VMEM((tm, D_MODEL), F32), pltpu.VMEM((tm, D_MODEL), F32)],
        compiler_params=_params("parallel", "arbitrary"),
        name="rwkv_rkv",
    )(x, x, x, mods, g, mu3, w3)


def _lora_kernel(x_ref, xp_ref, xn_ref, m_ref, g_ref, mug_ref, mud_ref, g1_ref, g2_ref,
                 wa1_ref, wa2_ref, w0_ref, a0_ref, gate_ref, dec_ref, asig_ref, *, geo, tm):
    h, xx = _shifted(geo, x_ref, xp_ref, xn_ref, m_ref, g_ref, tm)
    mix = lambda mu: (h + xx * mu).astype(BF16)
    zg = _dot(mix(mug_ref[...]), g1_ref[...])
    gate_ref[...] = _dot(_sigmoid(zg).astype(BF16), g2_ref[...])
    for d in range(2):
        zw = _dot(mix(mud_ref[2 * d]), wa1_ref[d])
        wl = w0_ref[d] + _dot(jnp.tanh(zw).astype(BF16), wa2_ref[d])
        dec_ref[d] = -LOG_DECAY_SCALE * _sigmoid(wl)
        za = _dot(mix(mud_ref[2 * d + 1]), wa1_ref[2 + d])
        asig_ref[d] = _sigmoid(a0_ref[d] + _dot(za.astype(BF16), wa2_ref[2 + d]))


def _lora(geo, x, mods, g, mu_g, mu_dir, g1, g2, wa1, wa2, w0, a0):
    tm = TM_RWKV
    full = lambda a: pl.BlockSpec(a.shape, lambda i: (0,) * a.ndim)
    tok2 = pl.BlockSpec((2, tm, D_MODEL), lambda i: (0, i, 0))
    return pl.pallas_call(
        functools.partial(_lora_kernel, geo=geo, tm=tm),
        grid=(geo.n_tok // tm,),
        in_specs=_halo_specs(geo, tm) + [full(a) for a in (mu_g, mu_dir, g1, g2, wa1, wa2, w0, a0)],
        out_specs=[pl.BlockSpec((tm, D_MODEL), lambda i: (i, 0)), tok2, tok2],
        out_shape=[jax.ShapeDtypeStruct((geo.n_tok, D_MODEL), F32),
                   jax.ShapeDtypeStruct((2, geo.n_tok, D_MODEL), F32),
                   jax.ShapeDtypeStruct((2, geo.n_tok, D_MODEL), F32)],
        compiler_params=_params("parallel"),
        name="rwkv_lora",
    )(x, x, x, mods, g, mu_g, mu_dir, g1, g2, wa1, wa2, w0, a0)


def _head_sums(x, ones_blk):
    return _dot(x.astype(BF16), ones_blk)


def _head_ones(width):
    r = lax.broadcasted_iota(jnp.int32, (width, width), 0)
    c = lax.broadcasted_iota(jnp.int32, (width, width), 1)
    same = (r // RWKV_HEAD) == (c // RWKV_HEAD)
    return same, jnp.where(same, 1.0, 0.0).astype(BF16)


def _wkv_kernel(*refs, geo, tb, aliased):
    r_ref, k_ref, v_ref, dec_ref, asig_ref, kk_ref, ka_ref, rk_ref, s0_ref = refs[:9]
    y_ref, bon_ref, sf_fwd_ref, sf_bwd_ref, s_ref = refs[9 + (2 if aliased else 0):]
    d = pl.program_id(0)
    i = pl.program_id(1)
    nblk = pl.num_programs(1)
    blk = i + d * (nblk - 1 - 2 * i)
    start = blk * tb
    seq = jnp.where(start < geo.n_ctx, geo.s_ctx, geo.s_lat)
    rel = jnp.where(start < geo.n_ctx, start, start - geo.n_ctx)
    at_lo = (rel % seq) == 0
    at_hi = ((rel + tb) % seq) == 0
    first = jnp.where(d == 0, at_lo, at_hi)
    last = jnp.where(d == 0, at_hi, at_lo)

    @pl.when(first)
    def _():
        s_ref[...] = s0_ref[...]

    nch = tb // CHUNK
    sgn = 1 - 2 * d
    fwd = (d == 0).astype(F32)
    row = lax.broadcasted_iota(jnp.int32, (CHUNK, PAIR), 0)
    lane = lax.broadcasted_iota(jnp.int32, (CHUNK, PAIR), 1)
    sidx = lane & (RWKV_HEAD - 1)
    delta = (row - sidx) * sgn
    strict = delta > 0
    incl = delta >= 0
    eye = jnp.where(row == sidx, 1.0, 0.0)
    off_masks = []
    m = 1
    while m < CHUNK:
        off_masks.append(strict & ((row // (2 * m)) == (sidx // (2 * m))) & ((row // m) != (sidx // m)))
        m *= 2
    head0 = lane < RWKV_HEAD
    tr = lax.broadcasted_iota(jnp.int32, (CHUNK, 3 * CHUNK), 0)
    ts = lax.broadcasted_iota(jnp.int32, (CHUNK, 3 * CHUNK), 1) & (CHUNK - 1)
    tri3 = jnp.where((tr - ts) * sgn >= 0, 1.0, 0.0).astype(BF16)
    same_head, ones_blk = _head_ones(PAIR)

    def expand(x):
        return jnp.concatenate([jnp.where(head0, x, 0.0), jnp.where(head0, 0.0, x)], axis=0)

    def chunk_body(ci, carry):
        cc = ci * sgn + d * (nch - 1)
        rows = pl.ds(pl.multiple_of(cc * CHUNK, CHUNK), CHUNK)

        def group_body(pg, carry2):
            ps = [pg * PAIR_GROUP + q for q in range(PAIR_GROUP)]
            cols = [pl.ds(pl.multiple_of(p * PAIR, PAIR), PAIR) for p in ps]
            each = lambda f, *ls: [f(*xs) for xs in zip(*ls)]
            r = [r_ref[rows, c] for c in cols]
            k = [k_ref[rows, c] for c in cols]
            v = [v_ref[rows, c] for c in cols]
            logw = [dec_ref[rows, c] for c in cols]
            a = [asig_ref[rows, c] for c in cols]
            kk = [x * kk_ref[:, c] for x, c in zip(k, cols)]
            kd = [x * (1.0 + (y - 1.0) * ka_ref[:, c]) for x, y, c in zip(k, a, cols)]
            sums = [_head_sums(jnp.concatenate([x * x, y * z * rk_ref[:, c]], axis=0), ones_blk)
                    for x, y, z, c in zip(kk, r, kd, cols)]
            kk = each(lambda x, s: x / jnp.maximum(jnp.sqrt(s[:CHUNK]), 1e-12), kk, sums)
            b_in = each(lambda x, y: x * y, kk, a)
            for c, s, y in zip(cols, sums, v):
                bon_ref[rows, c] = s[CHUNK:] * y

            l1 =[x.astype(BF16) for x in logw]
            e1 = each(lambda x, y: x - y.astype(F32), logw, l1)
            l2 = [x.astype(BF16) for x in e1]
            l3 = each(lambda x, y: (x - y.astype(F32)).astype(BF16), e1, l2)
            cum = each(lambda x, y, z: _dot(tri3, jnp.concatenate([x, y, z], axis=0)), l1, l2, l3)
            tot = [fwd * x[CHUNK - 1:CHUNK, :] + (1.0 - fwd) * x[0:1, :] for x in cum]
            c_inv = [jnp.exp(-x) for x in cum]
            at = each(lambda x, y, z: -x * jnp.exp(y - z), kk, cum, logw)
            rt = each(lambda x, y: x * jnp.exp(y), r, cum)
            c_end = each(lambda x, y: jnp.exp(x - y), tot, cum)

            lhs = each(lambda x, y: jnp.concatenate([x, y], axis=0).astype(BF16), at, rt)
            rhs = each(lambda x, y, z: jnp.concatenate([expand(x * z), expand(y * z)], axis=0).astype(BF16),
                       b_in, kd, c_inv)
            g = each(_dot_nt, lhs, rhs)
            ab = [jnp.where(strict, x[:CHUNK, :PAIR], 0.0) for x in g]
            ak = [jnp.where(strict, x[:CHUNK, PAIR:], 0.0) for x in g]
            rbk = [jnp.concatenate([jnp.where(incl, x[CHUNK:, :PAIR], 0.0),
                                    jnp.where(incl, x[CHUNK:, PAIR:], 0.0)], axis=1).astype(BF16) for x in g]

            t_inv = [eye + jnp.where(off_masks[0], x, 0.0) for x in ab]
            for off in off_masks[1:]:
                lx = each(lambda x, t: _dot(jnp.where(off, x, 0.0).astype(BF16), expand(t).astype(BF16)), ab, t_inv)
                t_inv = each(lambda t, x: t + _dot(t.astype(BF16), expand(x).astype(BF16)), t_inv, lx)

            s2 = [s_ref[p] for p in ps]
            ars = each(lambda x, s: _dot_nt(x, s.astype(BF16)), lhs, s2)
            vexp = [expand(x).astype(BF16) for x in v]
            rhs_u = each(lambda x, y, z: x[:CHUNK] + _dot(y.astype(BF16), z), ars, ak, vexp)
            u = each(lambda t, x: _dot(t.astype(BF16), expand(x).astype(BF16)), t_inv, rhs_u)
            uv = each(lambda x, y: jnp.concatenate([expand(x).astype(BF16), y], axis=0), u, vexp)
            y_out = each(lambda x, y, z: x[CHUNK:] + _dot(y, z), ars, rbk, uv)
            for c, x in zip(cols, y_out):
                y_ref[rows, c] = x
            uv_t = each(lambda x, y: jnp.concatenate([x, y], axis=0).T.astype(BF16), u, v)
            bk = each(lambda x, y, z: jnp.concatenate([x * z, y * z], axis=0).astype(BF16), b_in, kd, c_end)
            upd = each(_dot, uv_t, bk)
            for p, s, x, t in zip(ps, s2, upd, tot):
                s_ref[p] = s * jnp.exp(t) + jnp.where(same_head, x, 0.0)
            return carry2

        lax.fori_loop(0, N_PAIRS // PAIR_GROUP, group_body, 0)
        return carry

    lax.fori_loop(0, nch, chunk_body, 0)

    def write_states(dst_ref):
        for p in range(N_PAIRS):
            s = s_ref[p]
            dst_ref[2 * p] = s[:RWKV_HEAD, :RWKV_HEAD]
            dst_ref[2 * p + 1] = s[RWKV_HEAD:, RWKV_HEAD:]

    ends_ctx = last & (start < geo.n_ctx)

    @pl.when(ends_ctx & (d == 0))
    def _():
        write_states(sf_fwd_ref)

    @pl.when(ends_ctx & (d == 1))
    def _():
        write_states(sf_bwd_ref)


def _wkv(geo, rkv, dec, asig, kk_c, ka_c, rk_c, s0, j, n_layers, prev_states):
    tb = TB_WKV
    nblk = geo.n_tok // tb
    blk = lambda d, i: i + d * (nblk - 1 - 2 * i)
    tok = lambda which: pl.BlockSpec((None, tb, D_MODEL), lambda d, i: (which, blk(d, i), 0))
    perdir = pl.BlockSpec((None, tb, D_MODEL), lambda d, i: (d, blk(d, i), 0))
    const = pl.BlockSpec((1, D_MODEL), lambda d, i: (0, 0))
    last_seq = geo.b_ctx - 1
    seq = lambda d, i: jnp.minimum(blk(d, i) * tb // geo.s_ctx, last_seq)
    final = lambda which: pl.BlockSpec(
        (None, None, RWKV_HEADS, RWKV_HEAD, RWKV_HEAD),
        lambda d, i: (jnp.where(d == which, seq(d, i), last_seq), j, 0, 0, 0))
    final_shape = jax.ShapeDtypeStruct((geo.b_ctx, n_layers, RWKV_HEADS, RWKV_HEAD, RWKV_HEAD), F32)
    in_specs = [tok(0), tok(1), tok(2), perdir, perdir, const, const, const,
                pl.BlockSpec((None, None, N_PAIRS, PAIR, PAIR), lambda d, i: (d, geo.cond_row(blk(d, i), tb), 0, 0, 0))]
    args = [rkv, rkv, rkv, dec, asig, kk_c, ka_c, rk_c, s0]
    aliases = {}
    if prev_states is not None:
        in_specs += [pl.BlockSpec(memory_space=pl.ANY)] * 2
        aliases = {len(args): 2, len(args) + 1: 3}
        args += list(prev_states)
    return pl.pallas_call(
        functools.partial(_wkv_kernel, geo=geo, tb=tb, aliased=prev_states is not None),
        grid=(2, nblk),
        in_specs=in_specs,
        out_specs=[perdir, perdir, final(0), final(1)],
        out_shape=[jax.ShapeDtypeStruct((2, geo.n_tok, D_MODEL), F32),
                   jax.ShapeDtypeStruct((2, geo.n_tok, D_MODEL), F32), final_shape, final_shape],
        input_output_aliases=aliases,
        scratch_shapes=[pltpu.VMEM((N_PAIRS, PAIR, PAIR), F32)],
        compiler_params=_params("arbitrary", "arbitrary"),
        name="wkv_chunked",
    )(*args)


def _rwkv_out_kernel(y_ref, bon_ref, gate_ref, lw_ref, lb_ref, w_ref, x_ref, m_ref, o_ref, z_ref):
    _, ones_blk = _head_ones(GN_SLAB)
    inv_n = 1.0 / RWKV_HEAD
    for s in range(D_MODEL // GN_SLAB):
        cols = slice(s * GN_SLAB, (s + 1) * GN_SLAB)
        y = y_ref[0, :, cols] + y_ref[1, :, cols]
        yc = y - _head_sums(y, ones_blk) * inv_n
        var = _head_sums(yc * yc, ones_blk) * inv_n
        z = yc * lax.rsqrt(var + GN_EPS) * lw_ref[:, cols] + lb_ref[:, cols] + bon_ref[0, :, cols] + bon_ref[1, :, cols]
        z_ref[:, cols] = (z * gate_ref[:, cols]).astype(BF16)
    o_ref[...] = x_ref[...] + m_ref[5:6, :] * _dot(z_ref[...], w_ref[...])


def _rwkv_out(geo, y, bon, gate, lw, lb, w, x, mods):
    tm = TM_OUT
    row = pl.BlockSpec((tm, D_MODEL), lambda i: (i, 0))
    both = pl.BlockSpec((2, tm, D_MODEL), lambda i: (0, i, 0))
    const = pl.BlockSpec((1, D_MODEL), lambda i: (0, 0))
    return pl.pallas_call(
        _rwkv_out_kernel,
        grid=(geo.n_tok // tm,),
        in_specs=[both, both, row, const, const, pl.BlockSpec((D_MODEL, D_MODEL), lambda i: (0, 0)), row,
                  pl.BlockSpec((None, N_MOD, D_MODEL), lambda i: (geo.cond_row(i, tm), 0, 0))],
        out_specs=row,
        out_shape=jax.ShapeDtypeStruct((geo.n_tok, D_MODEL), F32),
        scratch_shapes=[pltpu.VMEM((tm, D_MODEL), BF16)],
        compiler_params=_params("parallel"),
        name="rwkv_out",
    )(y, bon, gate, lw, lb, w, x, mods)


def _state_to_pairs(s):
    b = s.shape[0]
    s5 = s.reshape(b, N_PAIRS, 2, RWKV_HEAD, RWKV_HEAD)
    z = jnp.zeros_like(s5[:, :, 0])
    top = jnp.concatenate([s5[:, :, 0], z], axis=-1)
    bot = jnp.concatenate([z, s5[:, :, 1]], axis=-1)
    return jnp.concatenate([top, bot], axis=-2)


def _rope_tables(geo):
    rows = geo.s_lat // GRID_W
    row = jnp.repeat(jnp.arange(rows, dtype=F32), GRID_W)
    col = jnp.tile(jnp.arange(GRID_W, dtype=F32), rows)
    inv = ROPE_THETA ** (-jnp.arange(ROPE_FREQS, dtype=F32) / ROPE_FREQS)
    ang = jnp.stack([row[:, None] * inv, col[:, None] * inv], axis=1)
    ang = jnp.broadcast_to(ang[:, :, None, :], (geo.s_lat, 2, 2, ROPE_FREQS)).reshape(geo.s_lat, QK_ROPE)
    cos = jnp.concatenate([jnp.ones((geo.n_ctx, QK_ROPE), F32), jnp.tile(jnp.cos(ang), (geo.b_lat, 1))], axis=0)
    sin = jnp.concatenate([jnp.zeros((geo.n_ctx, QK_ROPE), F32), jnp.tile(jnp.sin(ang), (geo.b_lat, 1))], axis=0)
    return cos, sin


def _rot_cols(w):
    w4 = w.reshape(w.shape[:-1] + (2, 2, ROPE_FREQS))
    return jnp.stack([-w4[..., 1, :], w4[..., 0, :]], axis=-2).reshape(w.shape)


def _mla_layer(geo, x, mods, g, j, cos, sin, cache_ckv, cache_krope, p):
    w_dkv = p['mla_w_dkv'][j]
    w_cat = jnp.concatenate([p['mla_w_dq'][j], w_dkv, _rot_cols(w_dkv[:, KV_LORA:])], axis=1).astype(BF16)
    cq, ckv, ckvb, kr = _mla_proj(geo, x, mods, g, w_cat, p['mla_q_norm'][j][None], p['mla_kv_norm'][j][None],
                                  cos, sin)
    w_uq = p['mla_w_uq'][j].reshape(Q_LORA, MLA_HEADS, QK_NOPE + QK_ROPE).transpose(1, 0, 2)
    wq = w_uq.astype(BF16)
    wqr = _rot_cols(w_uq[..., QK_NOPE:]).astype(BF16)
    wkv = p['mla_w_ukv'][j].reshape(KV_LORA, MLA_HEADS, QK_NOPE + V_DIM).transpose(1, 0, 2).astype(BF16)
    o_ctx = _attn(geo, cq, ckvb, kr, wq, wqr, wkv, latent=False)
    o = _attn(geo, cq, ckvb, kr, wq, wqr, wkv, latent=True,
              past_ckv=cache_ckv[:, j], past_kr=cache_krope[:, j], cos=cos, sin=sin, o_ctx=o_ctx)
    x = _out_proj(geo, o, p['mla_w_o'][j].astype(BF16), x, mods)
    new_ckv = ckv[:geo.n_ctx].reshape(geo.b_ctx, geo.s_ctx, KV_LORA)
    new_kr = kr[:geo.n_ctx].reshape(geo.b_ctx, geo.s_ctx, QK_ROPE)
    return x, new_ckv, new_kr


def _rwkv_layer(geo, x, mods, g, j, s0_fwd, s0_bwd, prev_states, p):
    mu = p['rwkv_mu'][j]
    mu_dir = p['rwkv_mu_dir'][j].reshape(4, 1, D_MODEL)
    w3 = jnp.stack([p['rwkv_w_r'][j], p['rwkv_w_k'][j], p['rwkv_w_v'][j]]).astype(BF16)
    rkv = _rkv(geo, x, mods, g, mu[:3, None, :], w3)
    pad1 = lambda w: jnp.pad(w, ((0, 0), (0, 0), (0, LORA_PAD - w.shape[-1])))
    pad2 = lambda w: jnp.pad(w, ((0, 0), (0, LORA_PAD - w.shape[-2]), (0, 0)))
    wa1 = jnp.concatenate([pad1(p['rwkv_w1'][j]), pad1(p['rwkv_a1'][j])]).astype(BF16)
    wa2 = jnp.concatenate([pad2(p['rwkv_w2'][j]), pad2(p['rwkv_a2'][j])]).astype(BF16)
    gate, dec, asig = _lora(geo, x, mods, g, mu[3:4], mu_dir, p['rwkv_g1'][j].astype(BF16),
                            p['rwkv_g2'][j].astype(BF16), wa1, wa2,
                            p['rwkv_w0'][j][:, None, :], p['rwkv_a0'][j][:, None, :])
    zero = jnp.zeros((2, 1, N_PAIRS, PAIR, PAIR), F32)
    s0 = jnp.concatenate([zero, jnp.stack([_state_to_pairs(s0_fwd), _state_to_pairs(s0_bwd)])], axis=1)
    y, bon, sf, sb = _wkv(geo, rkv, dec, asig, p['rwkv_k_k'][j][None], p['rwkv_k_a'][j][None],
                          p['rwkv_r_k'][j].reshape(1, D_MODEL), s0, j, p['rwkv_mu'].shape[0], prev_states)
    x = _rwkv_out(geo, y, bon, gate, p['rwkv_ln_w'][j][None], p['rwkv_ln_b'][j][None],
                  p['rwkv_w_o'][j].astype(BF16), x, mods)
    return x, (sf, sb)


def kernel(x_prompt, x_sample, cache_ckv, cache_krope, state_wkv_fwd, state_wkv_bwd, c, c_ctx, w_ada, b_ada, norm_sub, norm_final, w_ffn_in, w_ffn_out, mla_w_dq, mla_q_norm, mla_w_uq, mla_w_dkv, mla_kv_norm, mla_w_ukv, mla_w_o, rwkv_mu, rwkv_mu_dir, rwkv_w_r, rwkv_w_k, rwkv_w_v, rwkv_w0, rwkv_w1, rwkv_w2, rwkv_a0, rwkv_a1, rwkv_a2, rwkv_g1, rwkv_g2, rwkv_k_k, rwkv_k_a, rwkv_r_k, rwkv_ln_w, rwkv_ln_b, rwkv_w_o):
    p = dict(mla_w_dq=mla_w_dq, mla_q_norm=mla_q_norm, mla_w_uq=mla_w_uq, mla_w_dkv=mla_w_dkv,
             mla_kv_norm=mla_kv_norm, mla_w_ukv=mla_w_ukv, mla_w_o=mla_w_o,
             rwkv_mu=rwkv_mu, rwkv_mu_dir=rwkv_mu_dir, rwkv_w_r=rwkv_w_r, rwkv_w_k=rwkv_w_k,
             rwkv_w_v=rwkv_w_v, rwkv_w0=rwkv_w0, rwkv_w1=rwkv_w1, rwkv_w2=rwkv_w2,
             rwkv_a0=rwkv_a0, rwkv_a1=rwkv_a1, rwkv_a2=rwkv_a2, rwkv_g1=rwkv_g1, rwkv_g2=rwkv_g2,
             rwkv_k_k=rwkv_k_k, rwkv_k_a=rwkv_k_a, rwkv_r_k=rwkv_r_k,
             rwkv_ln_w=rwkv_ln_w, rwkv_ln_b=rwkv_ln_b, rwkv_w_o=rwkv_w_o)
    b_ctx, s_ctx, _ = x_prompt.shape
    b_lat, s_lat, _ = x_sample.shape
    geo = _Geom(b_ctx, s_ctx, b_lat, s_lat)
    assert geo.n_ctx % s_lat == 0 and s_ctx % TM_RWKV == 0 and s_lat % TM_FFN == 0
    assert s_ctx == TB_WKV and s_lat % TB_WKV == 0

    x = jnp.concatenate([x_prompt.reshape(geo.n_ctx, D_MODEL), x_sample.reshape(geo.n_lat, D_MODEL)], axis=0)
    cond = jnp.concatenate([c_ctx[None], c, jnp.zeros((COND_ROWS - 1 - b_lat, D_MODEL), F32)], axis=0)
    mods_all = _ada(cond, w_ada, b_ada).reshape(DEPTH, COND_ROWS, N_MOD, D_MODEL)
    cos, sin = _rope_tables(geo)
    nf = norm_final[None]
    w_in = w_ffn_in[0, 0].astype(BF16)
    w_out = w_ffn_out[0, 0].astype(BF16)

    ckv_l, kr_l, states = [], [], None
    for l in range(DEPTH):
        j = l // N_MIXERS
        mods = mods_all[l]
        x, w_in, w_out = _ffn(geo, x, mods, norm_sub[l, 0][None], w_in, w_out, nf, sub=0, final=False,
                              nxt=(w_ffn_in, w_ffn_out, l, 1))
        if l % N_MIXERS == 0:
            x, ckv, kr = _mla_layer(geo, x, mods, norm_sub[l, 1][None], j, cos, sin, cache_ckv, cache_krope, p)
            ckv_l.append(ckv)
            kr_l.append(kr)
        else:
            x, states = _rwkv_layer(geo, x, mods, norm_sub[l, 1][None], j,
                                    state_wkv_fwd[:, j], state_wkv_bwd[:, j], states, p)
        if l < DEPTH - 1:
            x, w_in, w_out = _ffn(geo, x, mods, norm_sub[l, 2][None], w_in, w_out, nf, sub=2, final=False,
                                  nxt=(w_ffn_in, w_ffn_out, l + 1, 0))
        else:
            x, = _ffn(geo, x, mods, norm_sub[l, 2][None], w_in, w_out, nf, sub=2, final=True)

    y_prompt = x[:geo.n_ctx].reshape(b_ctx, s_ctx, D_MODEL)
    y_sample = x[geo.n_ctx:].reshape(b_lat, s_lat, D_MODEL)
    return (y_prompt, y_sample, jnp.stack(ckv_l, axis=1), jnp.stack(kr_l, axis=1),
            states[0], states[1])
```

```python
import functools

import jax
import jax.numpy as jnp
from jax import lax
from jax.experimental import pallas as pl
from jax.experimental.pallas import tpu as pltpu

D_MODEL = 2048
DEPTH = 4
N_MIXERS = 2
D_FF = 5632
N_MOD = 9
RMS_EPS = 1e-6
MLA_HEADS = 16
Q_LORA = 512
KV_LORA = 512
QK_NOPE = 128
QK_ROPE = 64
V_DIM = 128
ROPE_FREQS = QK_ROPE // 4
ROPE_THETA = 10000.0
GRID_W = 64
ATTN_SCALE = (QK_NOPE + QK_ROPE) ** -0.5
RWKV_HEAD = 64
RWKV_HEADS = D_MODEL // RWKV_HEAD
GN_EPS = 64e-5
LOG_DECAY_SCALE = 0.6065306597126334
LORA_PAD = 128

COND_ROWS = 8
VMEM_LIMIT = 56 * 1024 * 1024

TM_FFN = 512
TF_FFN = 512
CAST_TILE = 256
TM_PROJ = 512
TM_RWKV = 256
TN_ADA = 1024
Q_TILE = 256
ATTN_LOCKSTEP = 8
CHUNK = 64
PAIR = 2 * RWKV_HEAD
N_PAIRS = D_MODEL // PAIR
PAIR_GROUP = 16
TB_WKV = 256
TM_OUT = 256
GN_SLAB = 256

BF16 = jnp.bfloat16
F32 = jnp.float32


def _params(*sem):
    return pltpu.CompilerParams(dimension_semantics=sem, vmem_limit_bytes=VMEM_LIMIT)


def _sigmoid(x):
    return 1.0 / (1.0 + jnp.exp(-x))


def _modulate(x, g, shift, scale):
    ms = jnp.mean(x * x, axis=-1, keepdims=True)
    return (x * lax.rsqrt(ms + RMS_EPS) * g) * (1.0 + scale) + shift


def _rms(x, w):
    ms = jnp.mean(x * x, axis=-1, keepdims=True)
    return x * lax.rsqrt(ms + RMS_EPS) * w


def _dot(a, b):
    return jnp.dot(a, b, preferred_element_type=F32)


def _dot_nt(a, b):
    return lax.dot_general(a, b, (((1,), (1,)), ((), ())), preferred_element_type=F32)


class _Geom:
    def __init__(self, b_ctx, s_ctx, b_lat, s_lat):
        self.b_ctx, self.s_ctx, self.b_lat, self.s_lat = b_ctx, s_ctx, b_lat, s_lat
        self.n_ctx = b_ctx * s_ctx
        self.n_lat = b_lat * s_lat
        self.n_tok = self.n_ctx + self.n_lat

    def cond_row(self, i, tm):
        start = i * tm
        return jnp.where(start < self.n_ctx, 0, 1 + (start - self.n_ctx) // self.s_lat)


def _ada_kernel(c_ref, w_ref, b_ref, o_ref):
    c = c_ref[...]
    s = (c * _sigmoid(c)).astype(BF16)
    o_ref[...] = _dot(s, w_ref[...].astype(BF16)) + b_ref[...]


def _ada(cond, w_ada, b_ada):
    n = N_MOD * D_MODEL
    return pl.pallas_call(
        _ada_kernel,
        grid=(DEPTH, n // TN_ADA),
        in_specs=[
            pl.BlockSpec((COND_ROWS, D_MODEL), lambda l, j: (0, 0)),
            pl.BlockSpec((None, D_MODEL, TN_ADA), lambda l, j: (l, 0, j)),
            pl.BlockSpec((None, 1, TN_ADA), lambda l, j: (l, 0, j)),
        ],
        out_specs=pl.BlockSpec((None, COND_ROWS, TN_ADA), lambda l, j: (l, 0, j)),
        out_shape=jax.ShapeDtypeStruct((DEPTH, COND_ROWS, n), F32),
        compiler_params=_params("parallel", "parallel"),
        name="ada",
    )(cond, w_ada, b_ada.reshape(DEPTH, 1, n))


def _ffn_kernel(*refs, sub, final, cast_next, ctx_blocks):
    refs = list(refs)
    take = lambda n: [refs.pop(0) for _ in range(n)]
    x_ref, m_ref, g_ref, wg_ref, wu_ref, wo_ref, nf_ref = take(7)
    if cast_next:
        ci_ref, co_ref = take(2)
    o_refs = take(2 if final else 1)
    if cast_next:
        cib_ref, cob_ref = take(2)
    h_ref, acc_ref = refs
    f = pl.program_id(1)

    def swiglu_slice(h):
        gate = _dot(h, wg_ref[...])
        up = _dot(h, wu_ref[...])
        act = (gate * _sigmoid(gate) * up).astype(BF16)
        return _dot(act, wo_ref[...])

    @pl.when(f == 0)
    def _():
        h = _modulate(x_ref[...], g_ref[...], m_ref[3 * sub:3 * sub + 1, :], m_ref[3 * sub + 1:3 * sub + 2, :])
        h = h.astype(BF16)
        h_ref[...] = h
        acc_ref[...] = swiglu_slice(h)

    @pl.when(f > 0)
    def _():
        acc_ref[...] += swiglu_slice(h_ref[...])

    if cast_next:
        step = pl.program_id(0) * pl.num_programs(1) + f
        n_in, n_out = cast_next

        @pl.when(step < n_in)
        def _():
            cib_ref[...] = ci_ref[...].astype(BF16)

        @pl.when((step >= n_in) & (step < n_in + n_out))
        def _():
            cob_ref[...] = co_ref[...].astype(BF16)

    @pl.when(f == pl.num_programs(1) - 1)
    def _():
        y = x_ref[...] + 0.5 * m_ref[3 * sub + 2:3 * sub + 3, :] * acc_ref[...]
        if not final:
            o_refs[0][...] = y
        else:
            y = _rms(y, nf_ref[...])
            is_ctx = pl.program_id(0) < ctx_blocks

            @pl.when(is_ctx)
            def _():
                o_refs[0][...] = y

            @pl.when(jnp.logical_not(is_ctx))
            def _():
                o_refs[1][...] = y


def _ffn(geo, x, mods, g, w_in, w_out, nf, *, sub, final, nxt=None):
    tm, tf = TM_FFN, TF_FFN
    nf_blocks = D_FF // tf
    n_in, n_out = 2 * D_FF // CAST_TILE, D_FF // CAST_TILE
    step = lambda i, f: i * nf_blocks + f
    in_blk = lambda i, f: jnp.minimum(step(i, f), n_in - 1)
    out_blk = lambda i, f: jnp.clip(step(i, f) - n_in, 0, n_out - 1)
    ctx_blocks = geo.n_ctx // tm
    whole = pl.BlockSpec((tm, D_MODEL), lambda i, f: (i, 0))
    in_specs = [
        whole,
        pl.BlockSpec((None, N_MOD, D_MODEL), lambda i, f: (geo.cond_row(i, tm), 0, 0)),
        pl.BlockSpec((1, D_MODEL), lambda i, f: (0, 0)),
        pl.BlockSpec((D_MODEL, tf), lambda i, f: (0, f)),
        pl.BlockSpec((D_MODEL, tf), lambda i, f: (0, f + nf_blocks)),
        pl.BlockSpec((tf, D_MODEL), lambda i, f: (f, 0)),
        pl.BlockSpec((1, D_MODEL), lambda i, f: (0, 0)),
    ]
    if final:
        out_specs = [pl.BlockSpec((tm, D_MODEL), lambda i, f: (jnp.minimum(i, ctx_blocks - 1), 0)),
                     pl.BlockSpec((tm, D_MODEL), lambda i, f: (jnp.maximum(i - ctx_blocks, 0), 0))]
        out_shape = [jax.ShapeDtypeStruct((geo.n_ctx, D_MODEL), F32), jax.ShapeDtypeStruct((geo.n_lat, D_MODEL), F32)]
    else:
        out_specs = [whole]
        out_shape = [jax.ShapeDtypeStruct((geo.n_tok, D_MODEL), F32)]
    args = [x, mods, g, w_in, w_in, w_out, nf]
    if nxt is not None:
        w_in_all, w_out_all, layer, half = nxt
        assert (geo.n_tok // tm) * nf_blocks >= n_in + n_out
        in_specs += [
            pl.BlockSpec((None, None, D_MODEL, CAST_TILE), lambda i, f: (layer, half, 0, in_blk(i, f))),
            pl.BlockSpec((None, None, CAST_TILE, D_MODEL), lambda i, f: (layer, half, out_blk(i, f), 0)),
        ]
        out_specs += [
            pl.BlockSpec((D_MODEL, CAST_TILE), lambda i, f: (0, in_blk(i, f))),
            pl.BlockSpec((CAST_TILE, D_MODEL), lambda i, f: (out_blk(i, f), 0)),
        ]
        out_shape += [jax.ShapeDtypeStruct((D_MODEL, 2 * D_FF), BF16), jax.ShapeDtypeStruct((D_FF, D_MODEL), BF16)]
        args += [w_in_all, w_out_all]
    return pl.pallas_call(
        functools.partial(_ffn_kernel, sub=sub, final=final, cast_next=(n_in, n_out) if nxt is not None else None,
                          ctx_blocks=ctx_blocks),
        grid=(geo.n_tok // tm, nf_blocks),
        in_specs=in_specs,
        out_specs=out_specs,
        out_shape=out_shape,
        scratch_shapes=[pltpu.VMEM((tm, D_MODEL), BF16), pltpu.VMEM((tm, D_MODEL), F32)],
        compiler_params=_params("arbitrary", "arbitrary"),
        name="ffn",
    )(*args)


def _mla_proj_kernel(*refs, aliased, ctx_blocks):
    x_ref, m_ref, g_ref, w_ref, qn_ref, kvn_ref, cos_ref, sin_ref = refs[:8]
    cq_ref, ckvb_ref, kr_ref, ckv_ctx_ref, kr_ctx_ref = refs[8 + (2 if aliased else 0):]
    h = _modulate(x_ref[...], g_ref[...], m_ref[3:4, :], m_ref[4:5, :]).astype(BF16)
    z = _dot(h, w_ref[...])
    cq_ref[...] = _rms(z[:, :Q_LORA], qn_ref[...]).astype(BF16)
    ckv = _rms(z[:, Q_LORA:Q_LORA + KV_LORA], kvn_ref[...])
    ckvb_ref[...] = ckv.astype(BF16)
    o = Q_LORA + KV_LORA
    kr = z[:, o:o + QK_ROPE] * cos_ref[...] + z[:, o + QK_ROPE:o + 2 * QK_ROPE] * sin_ref[...]
    kr_ref[...] = kr

    @pl.when(pl.program_id(0) < ctx_blocks)
    def _():
        ckv_ctx_ref[...] = ckv.reshape(ckv_ctx_ref.shape)
        kr_ctx_ref[...] = kr.reshape(kr_ctx_ref.shape)


def _mla_proj(geo, x, mods, g, w_cat, qn, kvn, cos, sin, j, n_layers, prev_cache):
    tm = TM_PROJ
    n_out = w_cat.shape[1]
    seqs = tm // geo.s_ctx
    ctx_blocks = geo.n_ctx // tm
    row = lambda i: (i, 0)
    fix = lambda i: (0, 0)
    cache = lambda i: (jnp.minimum(i, ctx_blocks - 1), j, 0, 0)
    in_specs = [
        pl.BlockSpec((tm, D_MODEL), row),
        pl.BlockSpec((None, N_MOD, D_MODEL), lambda i: (geo.cond_row(i, tm), 0, 0)),
        pl.BlockSpec((1, D_MODEL), fix),
        pl.BlockSpec((D_MODEL, n_out), fix),
        pl.BlockSpec((1, Q_LORA), fix),
        pl.BlockSpec((1, KV_LORA), fix),
        pl.BlockSpec((tm, QK_ROPE), row),
        pl.BlockSpec((tm, QK_ROPE), row),
    ]
    args = [x, mods, g, w_cat, qn, kvn, cos, sin]
    aliases = {}
    if prev_cache is not None:
        in_specs += [pl.BlockSpec(memory_space=pl.ANY)] * 2
        aliases = {len(args): 3, len(args) + 1: 4}
        args += list(prev_cache)
    return pl.pallas_call(
        functools.partial(_mla_proj_kernel, aliased=prev_cache is not None, ctx_blocks=ctx_blocks),
        grid=(geo.n_tok // tm,),
        in_specs=in_specs,
        out_specs=[
            pl.BlockSpec((tm, Q_LORA), row),
            pl.BlockSpec((tm, KV_LORA), row),
            pl.BlockSpec((tm, QK_ROPE), row),
            pl.BlockSpec((seqs, None, geo.s_ctx, KV_LORA), cache),
            pl.BlockSpec((seqs, None, geo.s_ctx, QK_ROPE), cache),
        ],
        out_shape=[
            jax.ShapeDtypeStruct((geo.n_tok, Q_LORA), BF16),
            jax.ShapeDtypeStruct((geo.n_tok, KV_LORA), BF16),
            jax.ShapeDtypeStruct((geo.n_tok, QK_ROPE), F32),
            jax.ShapeDtypeStruct((geo.b_ctx, n_layers, geo.s_ctx, KV_LORA), F32),
            jax.ShapeDtypeStruct((geo.b_ctx, n_layers, geo.s_ctx, QK_ROPE), F32),
        ],
        input_output_aliases=aliases,
        compiler_params=_params("arbitrary"),
        name="mla_proj",
    )(*args)


def _attn_kernel(*refs, s_len, t_past, rope):
    if rope:
        (cq_ref, ckvb_ref, kr_ref, pckv_ref, pkr_ref, cos_ref, sin_ref,
         wq_ref, wqr_ref, wkv_ref, _, o_ref, kv_all, kr_all) = refs
    else:
        cq_ref, ckvb_ref, kr_ref, wq_ref, wkv_ref, o_ref, kv_all, kr_all = refs
    if t_past:
        kv_all[0:t_past, :] = pckv_ref[...].astype(BF16)
        kr_all[0:t_past, :] = pkr_ref[...].astype(BF16)
    kv_all[t_past:, :] = ckvb_ref[...]
    kr_all[t_past:, :] = kr_ref[...].astype(BF16)

    n_q = s_len // Q_TILE
    hg = max(1, ATTN_LOCKSTEP // n_q)
    row_slices = [slice(qb * Q_TILE, (qb + 1) * Q_TILE) for qb in range(n_q)]

    def head_group(g, carry):
        heads = [g * hg + i for i in range(hg)]
        kv = kv_all[...]
        krb = kr_all[...]
        kvp = [_dot(kv, wkv_ref[hd]) for hd in heads]
        kn = [x[:, :QK_NOPE].astype(BF16) for x in kvp]
        v = [x[:, QK_NOPE:].astype(BF16) for x in kvp]
        items = [(i, qb) for i in range(hg) for qb in range(n_q)]
        cq = [cq_ref[rows, :] for rows in row_slices]
        q = [_dot(cq[qb], wq_ref[heads[i]]) for i, qb in items]
        qr = [x[:, QK_NOPE:] for x in q]
        if rope:
            rot = [_dot(cq[qb], wqr_ref[heads[i]]) for i, qb in items]
            qr = [x * cos_ref[row_slices[qb], :] + y * sin_ref[row_slices[qb], :]
                  for x, y, (i, qb) in zip(qr, rot, items)]
        s = [(_dot_nt(x[:, :QK_NOPE].astype(BF16), kn[i]) + _dot_nt(y.astype(BF16), krb)) * ATTN_SCALE
             for x, y, (i, qb) in zip(q, qr, items)]
        p = [jnp.exp(x - jnp.max(x, axis=-1, keepdims=True)) for x in s]
        pr = [(x * (1.0 / jnp.sum(x, axis=-1, keepdims=True))).astype(BF16) for x in p]
        o = [_dot(x, v[i]).astype(BF16) for x, (i, qb) in zip(pr, items)]
        for x, (i, qb) in zip(o, items):
            o_ref[row_slices[qb], pl.ds(pl.multiple_of(heads[i] * V_DIM, V_DIM), V_DIM)] = x
        return carry

    lax.fori_loop(0, MLA_HEADS // hg, head_group, 0)


def _attn(geo, cq, ckvb, kr, wq, wqr, wkv, *, latent, past_ckv=None, past_kr=None, cos=None, sin=None, o_ctx=None):
    if latent:
        nb, s_len, off = geo.b_lat, geo.s_lat, geo.n_ctx // geo.s_lat
        t_past = past_ckv.shape[1]
    else:
        nb, s_len, off, t_past = geo.b_ctx, geo.s_ctx, 0, 0
    row = lambda b: (b + off, 0)
    fix3 = lambda b: (0, 0, 0)
    tok = lambda width: pl.BlockSpec((s_len, width), row)
    wspec = lambda w: pl.BlockSpec(w.shape, fix3)
    in_specs = [tok(Q_LORA), tok(KV_LORA), tok(QK_ROPE)]
    args = [cq, ckvb, kr]
    if latent:
        in_specs += [pl.BlockSpec((None, t_past, KV_LORA), lambda b: (b, 0, 0)),
                     pl.BlockSpec((None, t_past, QK_ROPE), lambda b: (b, 0, 0)),
                     tok(QK_ROPE), tok(QK_ROPE), wspec(wq), wspec(wqr)]
        args += [past_ckv, past_kr, cos, sin, wq, wqr]
    else:
        in_specs += [wspec(wq)]
        args += [wq]
    in_specs += [wspec(wkv)]
    args += [wkv]
    aliases = {}
    if latent:
        in_specs += [pl.BlockSpec(memory_space=pl.ANY)]
        args += [o_ctx]
        aliases = {len(args) - 1: 0}
    kern = functools.partial(_attn_kernel, s_len=s_len, t_past=t_past, rope=latent)
    return pl.pallas_call(
        kern,
        grid=(nb,),
        in_specs=in_specs,
        out_specs=pl.BlockSpec((s_len, D_MODEL), row),
        out_shape=jax.ShapeDtypeStruct((geo.n_tok, D_MODEL), BF16),
        input_output_aliases=aliases,
        scratch_shapes=[pltpu.VMEM((t_past + s_len, KV_LORA), BF16),
                        pltpu.VMEM((t_past + s_len, QK_ROPE), BF16)],
        compiler_params=_params("parallel"),
        name="attn_lat" if latent else "attn_ctx",
    )(*args)


def _out_proj_kernel(a_ref, w_ref, x_ref, m_ref, o_ref):
    o_ref[...] = x_ref[...] + m_ref[5:6, :] * _dot(a_ref[...], w_ref[...])


def _out_proj(geo, a, w, x, mods):
    tm = TM_PROJ
    row = pl.BlockSpec((tm, D_MODEL), lambda i: (i, 0))
    return pl.pallas_call(
        _out_proj_kernel,
        grid=(geo.n_tok // tm,),
        in_specs=[row, pl.BlockSpec((D_MODEL, D_MODEL), lambda i: (0, 0)), row,
                  pl.BlockSpec((None, N_MOD, D_MODEL), lambda i: (geo.cond_row(i, tm), 0, 0))],
        out_specs=row,
        out_shape=jax.ShapeDtypeStruct((geo.n_tok, D_MODEL), F32),
        compiler_params=_params("parallel"),
        name="out_proj",
    )(a, w, x, mods)


def _shifted(geo, x_ref, xp_ref, xn_ref, m_ref, g_ref, tm):
    i = pl.program_id(0)
    g, shift, scale = g_ref[...], m_ref[3:4, :], m_ref[4:5, :]
    h = _modulate(x_ref[...], g, shift, scale)
    start = i * tm
    seq = jnp.where(start < geo.n_ctx, geo.s_ctx, geo.s_lat)
    rel = jnp.where(start < geo.n_ctx, start, start - geo.n_ctx)
    has_prev = (rel % seq) != 0
    has_next = ((rel + tm) % seq) != 0
    hp = _modulate(xp_ref[...], g, shift, scale)[7:8, :]
    hn = _modulate(xn_ref[...], g, shift, scale)[0:1, :]
    hp = jnp.where(has_prev, hp, 0.0)
    hn = jnp.where(has_next, hn, 0.0)
    r = lax.broadcasted_iota(jnp.int32, h.shape, 0)
    down = jnp.where(r == 0, hp, pltpu.roll(h, 1, 0))
    up = jnp.where(r == tm - 1, hn, pltpu.roll(h, tm - 1, 0))
    return h, 0.5 * (down + up) - h


def _halo_specs(geo, tm):
    nb8 = geo.n_tok // 8
    return [
        pl.BlockSpec((tm, D_MODEL), lambda i, *_: (i, 0)),
        pl.BlockSpec((8, D_MODEL), lambda i, *_: (jnp.maximum(i * (tm // 8) - 1, 0), 0)),
        pl.BlockSpec((8, D_MODEL), lambda i, *_: (jnp.minimum((i + 1) * (tm // 8), nb8 - 1), 0)),
        pl.BlockSpec((None, N_MOD, D_MODEL), lambda i, *_: (geo.cond_row(i, tm), 0, 0)),
        pl.BlockSpec((1, D_MODEL), lambda i, *_: (0, 0)),
    ]


def _rkv_kernel(x_ref, xp_ref, xn_ref, m_ref, g_ref, mu_ref, w_ref, o_ref, h_ref, xx_ref, *, geo, tm):
    proj = pl.program_id(1)

    def project(h, xx):
        xm = (h + xx * mu_ref[...]).astype(BF16)
        o_ref[...] = _dot(xm, w_ref[proj])

    @pl.when(proj == 0)
    def _():
        h, xx = _shifted(geo, x_ref, xp_ref, xn_ref, m_ref, g_ref, tm)
        h_ref[...] = h
        xx_ref[...] = xx
        project(h, xx)

    @pl.when(proj > 0)
    def _():
        project(h_ref[...], xx_ref[...])


def _rkv(geo, x, mods, g, mu3, w3):
    tm = TM_RWKV
    return pl.pallas_call(
        functools.partial(_rkv_kernel, geo=geo, tm=tm),
        grid=(geo.n_tok // tm, 3),
        in_specs=_halo_specs(geo, tm) + [
            pl.BlockSpec((None, 1, D_MODEL), lambda i, p: (p, 0, 0)),
            pl.BlockSpec((3, D_MODEL, D_MODEL), lambda i, p: (0, 0, 0), pipeline_mode=pl.Buffered(1)),
        ],
        out_specs=pl.BlockSpec((None, tm, D_MODEL), lambda i, p: (p, i, 0)),
        out_shape=jax.ShapeDtypeStruct((3, geo.n_tok, D_MODEL), F32),
        scratch_shapes=[pltpu.VMEM((tm, D_MODEL), F32), pltpu.VMEM((tm, D_MODEL), F32)],
        compiler_params=_params("parallel", "arbitrary"),
        name="rwkv_rkv",
    )(x, x, x, mods, g, mu3, w3)


def _lora_kernel(x_ref, xp_ref, xn_ref, m_ref, g_ref, mug_ref, mud_ref, g1_ref, g2_ref,
                 wa1_ref, wa2_ref, w0_ref, a0_ref, gate_ref, dec_ref, asig_ref, *, geo, tm):
    h, xx = _shifted(geo, x_ref, xp_ref, xn_ref, m_ref, g_ref, tm)
    mix = lambda mu: (h + xx * mu).astype(BF16)
    zg = _dot(mix(mug_ref[...]), g1_ref[...])
    gate_ref[...] = _dot(_sigmoid(zg).astype(BF16), g2_ref[...])
    for d in range(2):
        zw = _dot(mix(mud_ref[2 * d]), wa1_ref[d])
        wl = w0_ref[d] + _dot(jnp.tanh(zw).astype(BF16), wa2_ref[d])
        dec_ref[d] = -LOG_DECAY_SCALE * _sigmoid(wl)
        za = _dot(mix(mud_ref[2 * d + 1]), wa1_ref[2 + d])
        asig_ref[d] = _sigmoid(a0_ref[d] + _dot(za.astype(BF16), wa2_ref[2 + d]))


def _lora(geo, x, mods, g, mu_g, mu_dir, g1, g2, wa1, wa2, w0, a0):
    tm = TM_RWKV
    full = lambda a: pl.BlockSpec(a.shape, lambda i: (0,) * a.ndim)
    tok2 = pl.BlockSpec((2, tm, D_MODEL), lambda i: (0, i, 0))
    return pl.pallas_call(
        functools.partial(_lora_kernel, geo=geo, tm=tm),
        grid=(geo.n_tok // tm,),
        in_specs=_halo_specs(geo, tm) + [full(a) for a in (mu_g, mu_dir, g1, g2, wa1, wa2, w0, a0)],
        out_specs=[pl.BlockSpec((tm, D_MODEL), lambda i: (i, 0)), tok2, tok2],
        out_shape=[jax.ShapeDtypeStruct((geo.n_tok, D_MODEL), F32),
                   jax.ShapeDtypeStruct((2, geo.n_tok, D_MODEL), F32),
                   jax.ShapeDtypeStruct((2, geo.n_tok, D_MODEL), F32)],
        compiler_params=_params("parallel"),
        name="rwkv_lora",
    )(x, x, x, mods, g, mu_g, mu_dir, g1, g2, wa1, wa2, w0, a0)


def _head_sums(x, ones_blk):
    return _dot(x.astype(BF16), ones_blk)


def _head_ones(width):
    r = lax.broadcasted_iota(jnp.int32, (width, width), 0)
    c = lax.broadcasted_iota(jnp.int32, (width, width), 1)
    same = (r // RWKV_HEAD) == (c // RWKV_HEAD)
    return same, jnp.where(same, 1.0, 0.0).astype(BF16)


def _wkv_kernel(*refs, geo, tb, aliased):
    r_ref, k_ref, v_ref, dec_ref, asig_ref, kk_ref, ka_ref, rk_ref, s0_ref = refs[:9]
    y_ref, bon_ref, sf_fwd_ref, sf_bwd_ref, s_ref = refs[9 + (2 if aliased else 0):]
    d = pl.program_id(0)
    i = pl.program_id(1)
    nblk = pl.num_programs(1)
    blk = i + d * (nblk - 1 - 2 * i)
    start = blk * tb
    seq = jnp.where(start < geo.n_ctx, geo.s_ctx, geo.s_lat)
    rel = jnp.where(start < geo.n_ctx, start, start - geo.n_ctx)
    at_lo = (rel % seq) == 0
    at_hi = ((rel + tb) % seq) == 0
    first = jnp.where(d == 0, at_lo, at_hi)
    last = jnp.where(d == 0, at_hi, at_lo)

    @pl.when(first)
    def _():
        s_ref[...] = s0_ref[...]

    nch = tb // CHUNK
    sgn = 1 - 2 * d
    fwd = (d == 0).astype(F32)
    row = lax.broadcasted_iota(jnp.int32, (CHUNK, PAIR), 0)
    lane = lax.broadcasted_iota(jnp.int32, (CHUNK, PAIR), 1)
    sidx = lane & (RWKV_HEAD - 1)
    delta = (row - sidx) * sgn
    strict = delta > 0
    incl = delta >= 0
    eye = jnp.where(row == sidx, 1.0, 0.0)
    off_masks = []
    m = 1
    while m < CHUNK:
        off_masks.append(strict & ((row // (2 * m)) == (sidx // (2 * m))) & ((row // m) != (sidx // m)))
        m *= 2
    head0 = lane < RWKV_HEAD
    tr = lax.broadcasted_iota(jnp.int32, (CHUNK, 3 * CHUNK), 0)
    ts = lax.broadcasted_iota(jnp.int32, (CHUNK, 3 * CHUNK), 1) & (CHUNK - 1)
    tri3 = jnp.where((tr - ts) * sgn >= 0, 1.0, 0.0).astype(BF16)
    same_head, ones_blk = _head_ones(PAIR)

    def expand(x):
        return jnp.concatenate([jnp.where(head0, x, 0.0), jnp.where(head0, 0.0, x)], axis=0)

    def chunk_body(ci, carry):
        cc = ci * sgn + d * (nch - 1)
        rows = pl.ds(pl.multiple_of(cc * CHUNK, CHUNK), CHUNK)

        def group_body(pg, carry2):
            ps = [pg * PAIR_GROUP + q for q in range(PAIR_GROUP)]
            cols = [pl.ds(pl.multiple_of(p * PAIR, PAIR), PAIR) for p in ps]
            each = lambda f, *ls: [f(*xs) for xs in zip(*ls)]
            r = [r_ref[rows, c] for c in cols]
            k = [k_ref[rows, c] for c in cols]
            v = [v_ref[rows, c] for c in cols]
            logw = [dec_ref[rows, c] for c in cols]
            a = [asig_ref[rows, c] for c in cols]
            kk = [x * kk_ref[:, c] for x, c in zip(k, cols)]
            kd = [x * (1.0 + (y - 1.0) * ka_ref[:, c]) for x, y, c in zip(k, a, cols)]
            sums = [_head_sums(jnp.concatenate([x * x, y * z * rk_ref[:, c]], axis=0), ones_blk)
                    for x, y, z, c in zip(kk, r, kd, cols)]
            kk = each(lambda x, s: x / jnp.maximum(jnp.sqrt(s[:CHUNK]), 1e-12), kk, sums)
            b_in = each(lambda x, y: x * y, kk, a)
            for c, s, y in zip(cols, sums, v):
                bon_ref[rows, c] = s[CHUNK:] * y

            l1 =[x.astype(BF16) for x in logw]
            e1 = each(lambda x, y: x - y.astype(F32), logw, l1)
            l2 = [x.astype(BF16) for x in e1]
            l3 = each(lambda x, y: (x - y.astype(F32)).astype(BF16), e1, l2)
            cum = each(lambda x, y, z: _dot(tri3, jnp.concatenate([x, y, z], axis=0)), l1, l2, l3)
            tot = [fwd * x[CHUNK - 1:CHUNK, :] + (1.0 - fwd) * x[0:1, :] for x in cum]
            c_inv = [jnp.exp(-x) for x in cum]
            at = each(lambda x, y, z: -x * jnp.exp(y - z), kk, cum, logw)
            rt = each(lambda x, y: x * jnp.exp(y), r, cum)
            c_end = each(lambda x, y: jnp.exp(x - y), tot, cum)

            lhs = each(lambda x, y: jnp.concatenate([x, y], axis=0).astype(BF16), at, rt)
            rhs = each(lambda x, y, z: jnp.concatenate([expand(x * z), expand(y * z)], axis=0).astype(BF16),
                       b_in, kd, c_inv)
            g = each(_dot_nt, lhs, rhs)
            ab = [jnp.where(strict, x[:CHUNK, :PAIR], 0.0) for x in g]
            ak = [jnp.where(strict, x[:CHUNK, PAIR:], 0.0) for x in g]
            rbk = [jnp.concatenate([jnp.where(incl, x[CHUNK:, :PAIR], 0.0),
                                    jnp.where(incl, x[CHUNK:, PAIR:], 0.0)], axis=1).astype(BF16) for x in g]

            t_inv = [eye + jnp.where(off_masks[0], x, 0.0) for x in ab]
            for off in off_masks[1:]:
                lx = each(lambda x, t: _dot(jnp.where(off, x, 0.0).astype(BF16), expand(t).astype(BF16)), ab, t_inv)
                t_inv = each(lambda t, x: t + _dot(t.astype(BF16), expand(x).astype(BF16)), t_inv, lx)

            s2 = [s_ref[p] for p in ps]
            ars = each(lambda x, s: _dot_nt(x, s.astype(BF16)), lhs, s2)
            vexp = [expand(x).astype(BF16) for x in v]
            rhs_u = each(lambda x, y, z: x[:CHUNK] + _dot(y.astype(BF16), z), ars, ak, vexp)
            u = each(lambda t, x: _dot(t.astype(BF16), expand(x).astype(BF16)), t_inv, rhs_u)
            uv = each(lambda x, y: jnp.concatenate([expand(x).astype(BF16), y], axis=0), u, vexp)
            y_out = each(lambda x, y, z: x[CHUNK:] + _dot(y, z), ars, rbk, uv)
            for c, x in zip(cols, y_out):
                y_ref[rows, c] = x
            uv_t = each(lambda x, y: jnp.concatenate([x, y], axis=0).T.astype(BF16), u, v)
            bk = each(lambda x, y, z: jnp.concatenate([x * z, y * z], axis=0).astype(BF16), b_in, kd, c_end)
            upd = each(_dot, uv_t, bk)
            for p, s, x, t in zip(ps, s2, upd, tot):
                s_ref[p] = s * jnp.exp(t) + jnp.where(same_head, x, 0.0)
            return carry2

        lax.fori_loop(0, N_PAIRS // PAIR_GROUP, group_body, 0)
        return carry

    lax.fori_loop(0, nch, chunk_body, 0)

    def write_states(dst_ref):
        for p in range(N_PAIRS):
            s = s_ref[p]
            dst_ref[2 * p] = s[:RWKV_HEAD, :RWKV_HEAD]
            dst_ref[2 * p + 1] = s[RWKV_HEAD:, RWKV_HEAD:]

    ends_ctx = last & (start < geo.n_ctx)

    @pl.when(ends_ctx & (d == 0))
    def _():
        write_states(sf_fwd_ref)

    @pl.when(ends_ctx & (d == 1))
    def _():
        write_states(sf_bwd_ref)


def _wkv(geo, rkv, dec, asig, kk_c, ka_c, rk_c, s0, j, n_layers, prev_states):
    tb = TB_WKV
    nblk = geo.n_tok // tb
    blk = lambda d, i: i + d * (nblk - 1 - 2 * i)
    tok = lambda which: pl.BlockSpec((None, tb, D_MODEL), lambda d, i: (which, blk(d, i), 0))
    perdir = pl.BlockSpec((None, tb, D_MODEL), lambda d, i: (d, blk(d, i), 0))
    const = pl.BlockSpec((1, D_MODEL), lambda d, i: (0, 0))
    last_seq = geo.b_ctx - 1
    seq = lambda d, i: jnp.minimum(blk(d, i) * tb // geo.s_ctx, last_seq)
    final = lambda which: pl.BlockSpec(
        (None, None, RWKV_HEADS, RWKV_HEAD, RWKV_HEAD),
        lambda d, i: (jnp.where(d == which, seq(d, i), last_seq), j, 0, 0, 0))
    final_shape = jax.ShapeDtypeStruct((geo.b_ctx, n_layers, RWKV_HEADS, RWKV_HEAD, RWKV_HEAD), F32)
    in_specs = [tok(0), tok(1), tok(2), perdir, perdir, const, const, const,
                pl.BlockSpec((None, None, N_PAIRS, PAIR, PAIR), lambda d, i: (d, geo.cond_row(blk(d, i), tb), 0, 0, 0))]
    args = [rkv, rkv, rkv, dec, asig, kk_c, ka_c, rk_c, s0]
    aliases = {}
    if prev_states is not None:
        in_specs += [pl.BlockSpec(memory_space=pl.ANY)] * 2
        aliases = {len(args): 2, len(args) + 1: 3}
        args += list(prev_states)
    return pl.pallas_call(
        functools.partial(_wkv_kernel, geo=geo, tb=tb, aliased=prev_states is not None),
        grid=(2, nblk),
        in_specs=in_specs,
        out_specs=[perdir, perdir, final(0), final(1)],
        out_shape=[jax.ShapeDtypeStruct((2, geo.n_tok, D_MODEL), F32),
                   jax.ShapeDtypeStruct((2, geo.n_tok, D_MODEL), F32), final_shape, final_shape],
        input_output_aliases=aliases,
        scratch_shapes=[pltpu.VMEM((N_PAIRS, PAIR, PAIR), F32)],
        compiler_params=_params("arbitrary", "arbitrary"),
        name="wkv_chunked",
    )(*args)


def _rwkv_out_kernel(y_ref, bon_ref, gate_ref, lw_ref, lb_ref, w_ref, x_ref, m_ref, o_ref, z_ref):
    _, ones_blk = _head_ones(GN_SLAB)
    inv_n = 1.0 / RWKV_HEAD
    for s in range(D_MODEL // GN_SLAB):
        cols = slice(s * GN_SLAB, (s + 1) * GN_SLAB)
        y = y_ref[0, :, cols] + y_ref[1, :, cols]
        yc = y - _head_sums(y, ones_blk) * inv_n
        var = _head_sums(yc * yc, ones_blk) * inv_n
        z = yc * lax.rsqrt(var + GN_EPS) * lw_ref[:, cols] + lb_ref[:, cols] + bon_ref[0, :, cols] + bon_ref[1, :, cols]
        z_ref[:, cols] = (z * gate_ref[:, cols]).astype(BF16)
    o_ref[...] = x_ref[...] + m_ref[5:6, :] * _dot(z_ref[...], w_ref[...])


def _rwkv_out(geo, y, bon, gate, lw, lb, w, x, mods):
    tm = TM_OUT
    row = pl.BlockSpec((tm, D_MODEL), lambda i: (i, 0))
    both = pl.BlockSpec((2, tm, D_MODEL), lambda i: (0, i, 0))
    const = pl.BlockSpec((1, D_MODEL), lambda i: (0, 0))
    return pl.pallas_call(
        _rwkv_out_kernel,
        grid=(geo.n_tok // tm,),
        in_specs=[both, both, row, const, const, pl.BlockSpec((D_MODEL, D_MODEL), lambda i: (0, 0)), row,
                  pl.BlockSpec((None, N_MOD, D_MODEL), lambda i: (geo.cond_row(i, tm), 0, 0))],
        out_specs=row,
        out_shape=jax.ShapeDtypeStruct((geo.n_tok, D_MODEL), F32),
        scratch_shapes=[pltpu.VMEM((tm, D_MODEL), BF16)],
        compiler_params=_params("parallel"),
        name="rwkv_out",
    )(y, bon, gate, lw, lb, w, x, mods)


def _state_to_pairs(s):
    b = s.shape[0]
    s5 = s.reshape(b, N_PAIRS, 2, RWKV_HEAD, RWKV_HEAD)
    z = jnp.zeros_like(s5[:, :, 0])
    top = jnp.concatenate([s5[:, :, 0], z], axis=-1)
    bot = jnp.concatenate([z, s5[:, :, 1]], axis=-1)
    return jnp.concatenate([top, bot], axis=-2)


def _rope_tables(geo):
    rows = geo.s_lat // GRID_W
    row = jnp.repeat(jnp.arange(rows, dtype=F32), GRID_W)
    col = jnp.tile(jnp.arange(GRID_W, dtype=F32), rows)
    inv = ROPE_THETA ** (-jnp.arange(ROPE_FREQS, dtype=F32) / ROPE_FREQS)
    ang = jnp.stack([row[:, None] * inv, col[:, None] * inv], axis=1)
    ang = jnp.broadcast_to(ang[:, :, None, :], (geo.s_lat, 2, 2, ROPE_FREQS)).reshape(geo.s_lat, QK_ROPE)
    cos = jnp.concatenate([jnp.ones((geo.n_ctx, QK_ROPE), F32), jnp.tile(jnp.cos(ang), (geo.b_lat, 1))], axis=0)
    sin = jnp.concatenate([jnp.zeros((geo.n_ctx, QK_ROPE), F32), jnp.tile(jnp.sin(ang), (geo.b_lat, 1))], axis=0)
    return cos, sin


def _rot_cols(w):
    w4 = w.reshape(w.shape[:-1] + (2, 2, ROPE_FREQS))
    return jnp.stack([-w4[..., 1, :], w4[..., 0, :]], axis=-2).reshape(w.shape)


def _mla_layer(geo, x, mods, g, j, cos, sin, cache_ckv, cache_krope, prev_cache, p):
    w_dkv = p['mla_w_dkv'][j]
    w_cat = jnp.concatenate([p['mla_w_dq'][j], w_dkv, _rot_cols(w_dkv[:, KV_LORA:])], axis=1).astype(BF16)
    cq, ckvb, kr, new_ckv, new_kr = _mla_proj(geo, x, mods, g, w_cat, p['mla_q_norm'][j][None],
                                              p['mla_kv_norm'][j][None], cos, sin, j, p['mla_w_dq'].shape[0],
                                              prev_cache)
    w_uq = p['mla_w_uq'][j].reshape(Q_LORA, MLA_HEADS, QK_NOPE + QK_ROPE).transpose(1, 0, 2)
    wq = w_uq.astype(BF16)
    wqr = _rot_cols(w_uq[..., QK_NOPE:]).astype(BF16)
    wkv = p['mla_w_ukv'][j].reshape(KV_LORA, MLA_HEADS, QK_NOPE + V_DIM).transpose(1, 0, 2).astype(BF16)
    o_ctx = _attn(geo, cq, ckvb, kr, wq, wqr, wkv, latent=False)
    o = _attn(geo, cq, ckvb, kr, wq, wqr, wkv, latent=True,
              past_ckv=cache_ckv[:, j], past_kr=cache_krope[:, j], cos=cos, sin=sin, o_ctx=o_ctx)
    x = _out_proj(geo, o, p['mla_w_o'][j].astype(BF16), x, mods)
    return x, (new_ckv, new_kr)


def _rwkv_layer(geo, x, mods, g, j, s0_fwd, s0_bwd, prev_states, p):
    mu = p['rwkv_mu'][j]
    mu_dir = p['rwkv_mu_dir'][j].reshape(4, 1, D_MODEL)
    w3 = jnp.stack([p['rwkv_w_r'][j], p['rwkv_w_k'][j], p['rwkv_w_v'][j]]).astype(BF16)
    rkv = _rkv(geo, x, mods, g, mu[:3, None, :], w3)
    pad1 = lambda w: jnp.pad(w, ((0, 0), (0, 0), (0, LORA_PAD - w.shape[-1])))
    pad2 = lambda w: jnp.pad(w, ((0, 0), (0, LORA_PAD - w.shape[-2]), (0, 0)))
    wa1 = jnp.concatenate([pad1(p['rwkv_w1'][j]), pad1(p['rwkv_a1'][j])]).astype(BF16)
    wa2 = jnp.concatenate([pad2(p['rwkv_w2'][j]), pad2(p['rwkv_a2'][j])]).astype(BF16)
    gate, dec, asig = _lora(geo, x, mods, g, mu[3:4], mu_dir, p['rwkv_g1'][j].astype(BF16),
                            p['rwkv_g2'][j].astype(BF16), wa1, wa2,
                            p['rwkv_w0'][j][:, None, :], p['rwkv_a0'][j][:, None, :])
    zero = jnp.zeros((2, 1, N_PAIRS, PAIR, PAIR), F32)
    s0 = jnp.concatenate([zero, jnp.stack([_state_to_pairs(s0_fwd), _state_to_pairs(s0_bwd)])], axis=1)
    y, bon, sf, sb = _wkv(geo, rkv, dec, asig, p['rwkv_k_k'][j][None], p['rwkv_k_a'][j][None],
                          p['rwkv_r_k'][j].reshape(1, D_MODEL), s0, j, p['rwkv_mu'].shape[0], prev_states)
    x = _rwkv_out(geo, y, bon, gate, p['rwkv_ln_w'][j][None], p['rwkv_ln_b'][j][None],
                  p['rwkv_w_o'][j].astype(BF16), x, mods)
    return x, (sf, sb)


def kernel(x_prompt, x_sample, cache_ckv, cache_krope, state_wkv_fwd, state_wkv_bwd, c, c_ctx, w_ada, b_ada, norm_sub, norm_final, w_ffn_in, w_ffn_out, mla_w_dq, mla_q_norm, mla_w_uq, mla_w_dkv, mla_kv_norm, mla_w_ukv, mla_w_o, rwkv_mu, rwkv_mu_dir, rwkv_w_r, rwkv_w_k, rwkv_w_v, rwkv_w0, rwkv_w1, rwkv_w2, rwkv_a0, rwkv_a1, rwkv_a2, rwkv_g1, rwkv_g2, rwkv_k_k, rwkv_k_a, rwkv_r_k, rwkv_ln_w, rwkv_ln_b, rwkv_w_o):
    p = dict(mla_w_dq=mla_w_dq, mla_q_norm=mla_q_norm, mla_w_uq=mla_w_uq, mla_w_dkv=mla_w_dkv,
             mla_kv_norm=mla_kv_norm, mla_w_ukv=mla_w_ukv, mla_w_o=mla_w_o,
             rwkv_mu=rwkv_mu, rwkv_mu_dir=rwkv_mu_dir, rwkv_w_r=rwkv_w_r, rwkv_w_k=rwkv_w_k,
             rwkv_w_v=rwkv_w_v, rwkv_w0=rwkv_w0, rwkv_w1=rwkv_w1, rwkv_w2=rwkv_w2,
             rwkv_a0=rwkv_a0, rwkv_a1=rwkv_a1, rwkv_a2=rwkv_a2, rwkv_g1=rwkv_g1, rwkv_g2=rwkv_g2,
             rwkv_k_k=rwkv_k_k, rwkv_k_a=rwkv_k_a, rwkv_r_k=rwkv_r_k,
             rwkv_ln_w=rwkv_ln_w, rwkv_ln_b=rwkv_ln_b, rwkv_w_o=rwkv_w_o)
    b_ctx, s_ctx, _ = x_prompt.shape
    b_lat, s_lat, _ = x_sample.shape
    geo = _Geom(b_ctx, s_ctx, b_lat, s_lat)
    assert geo.n_ctx % s_lat == 0 and s_ctx % TM_RWKV == 0 and s_lat % TM_FFN == 0
    assert s_ctx == TB_WKV and s_lat % TB_WKV == 0

    x = jnp.concatenate([x_prompt.reshape(geo.n_ctx, D_MODEL), x_sample.reshape(geo.n_lat, D_MODEL)], axis=0)
    cond =jnp.concatenate([c_ctx[None], c, jnp.zeros((COND_ROWS - 1 - b_lat, D_MODEL), F32)], axis=0)
    mods_all = _ada(cond, w_ada, b_ada).reshape(DEPTH, COND_ROWS, N_MOD, D_MODEL)
    cos, sin = _rope_tables(geo)
    nf = norm_final[None]
    w_in = w_ffn_in[0, 0].astype(BF16)
    w_out = w_ffn_out[0, 0].astype(BF16)

    cache, states = None, None
    for l in range(DEPTH):
        j = l // N_MIXERS
        mods = mods_all[l]
        x, w_in, w_out = _ffn(geo, x, mods, norm_sub[l, 0][None], w_in, w_out, nf, sub=0, final=False,
                              nxt=(w_ffn_in, w_ffn_out, l, 1))
        if l % N_MIXERS == 0:
            x, cache = _mla_layer(geo, x, mods, norm_sub[l, 1][None], j, cos, sin, cache_ckv, cache_krope, cache, p)
        else:
            x, states = _rwkv_layer(geo, x, mods, norm_sub[l, 1][None], j,
                                    state_wkv_fwd[:, j], state_wkv_bwd[:, j], states, p)
        if l < DEPTH - 1:
            x, w_in, w_out = _ffn(geo, x, mods, norm_sub[l, 2][None], w_in, w_out, nf, sub=2, final=False,
                                  nxt=(w_ffn_in, w_ffn_out, l + 1, 0))
        else:
            y_ctx, y_lat = _ffn(geo, x, mods, norm_sub[l, 2][None], w_in, w_out, nf, sub=2, final=True)

    return (y_ctx.reshape(b_ctx, s_ctx, D_MODEL), y_lat.reshape(b_lat, s_lat, D_MODEL),
            cache[0], cache[1], states[0], states[1])
```

```python
import functools

import jax
import jax.numpy as jnp
from jax import lax
from jax.experimental import pallas as pl
from jax.experimental.pallas import tpu as pltpu

D_MODEL = 2048
DEPTH = 4
N_MIXERS = 2
D_FF = 5632
N_MOD = 9
RMS_EPS = 1e-6
MLA_HEADS = 16
Q_LORA = 512
KV_LORA = 512
QK_NOPE = 128
QK_ROPE = 64
V_DIM = 128
ROPE_FREQS = QK_ROPE // 4
ROPE_THETA = 10000.0
GRID_W = 64
ATTN_SCALE = (QK_NOPE + QK_ROPE) ** -0.5
RWKV_HEAD = 64
RWKV_HEADS = D_MODEL // RWKV_HEAD
GN_EPS = 64e-5
LOG_DECAY_SCALE = 0.6065306597126334
LORA_PAD = 128

COND_ROWS = 8
HALO_ROWS = 8
VMEM_LIMIT = 56 * 1024 * 1024

TM_FFN = 512
TF_FFN = 512
CAST_TILE = 256
TM_PROJ = 512
TM_RWKV = 256
TN_ADA = 1024
Q_TILE = 256
ATTN_LOCKSTEP = 8
CHUNK = 64
PAIR = 2 * RWKV_HEAD
N_PAIRS = D_MODEL // PAIR
PAIR_GROUP = 16
TB_WKV = 256
TM_OUT = 256
GN_SLAB = 256

BF16 = jnp.bfloat16
F32 = jnp.float32


def _params(*sem):
    return pltpu.CompilerParams(dimension_semantics=sem, vmem_limit_bytes=VMEM_LIMIT)


def _sigmoid(x):
    return 1.0 / (1.0 + jnp.exp(-x))


def _modulate(x, g, shift, scale):
    ms = jnp.mean(x * x, axis=-1, keepdims=True)
    return (x * lax.rsqrt(ms + RMS_EPS) * g) * (1.0 + scale) + shift


def _rms(x, w):
    ms = jnp.mean(x * x, axis=-1, keepdims=True)
    return x * lax.rsqrt(ms + RMS_EPS) * w


def _dot(a, b):
    return jnp.dot(a, b, preferred_element_type=F32)


def _dot_nt(a, b):
    return lax.dot_general(a, b, (((1,), (1,)), ((), ())), preferred_element_type=F32)


class _Geom:
    def __init__(self, b_ctx, s_ctx, b_lat, s_lat):
        self.b_ctx, self.s_ctx, self.b_lat, self.s_lat = b_ctx, s_ctx, b_lat, s_lat
        self.n_ctx = b_ctx * s_ctx
        self.n_lat = b_lat * s_lat
        self.n_tok = self.n_ctx + self.n_lat

    def cond_row(self, i, tm):
        start = i * tm
        return jnp.where(start < self.n_ctx, 0, 1 + (start - self.n_ctx) // self.s_lat)


def _ada_kernel(c_ref, w_ref, b_ref, o_ref):
    c = c_ref[...]
    s = (c * _sigmoid(c)).astype(BF16)
    o_ref[...] = _dot(s, w_ref[...].astype(BF16)) + b_ref[...]


def _ada(cond, w_ada, b_ada):
    n = N_MOD * D_MODEL
    return pl.pallas_call(
        _ada_kernel,
        grid=(DEPTH, n // TN_ADA),
        in_specs=[
            pl.BlockSpec((COND_ROWS, D_MODEL), lambda l, j: (0, 0)),
            pl.BlockSpec((None, D_MODEL, TN_ADA), lambda l, j: (l, 0, j)),
            pl.BlockSpec((None, 1, TN_ADA), lambda l, j: (l, 0, j)),
        ],
        out_specs=pl.BlockSpec((None, COND_ROWS, TN_ADA), lambda l, j: (l, 0, j)),
        out_shape=jax.ShapeDtypeStruct((DEPTH, COND_ROWS, n), F32),
        compiler_params=_params("parallel", "parallel"),
        name="ada",
    )(cond, w_ada, b_ada.reshape(DEPTH, 1, n))


def _ffn_kernel(*refs, sub, final, cast_next, ctx_blocks):
    refs = list(refs)
    take = lambda n: [refs.pop(0) for _ in range(n)]
    x_ref, m_ref, g_ref, wg_ref, wu_ref, wo_ref, nf_ref = take(7)
    if cast_next:
        ci_ref, co_ref = take(2)
    o_refs = take(2 if final else 1)
    if cast_next:
        cib_ref, cob_ref = take(2)
    h_ref, acc_ref = refs
    f = pl.program_id(1)

    def swiglu_slice(h):
        gate = _dot(h, wg_ref[...])
        up = _dot(h, wu_ref[...])
        act = (gate * _sigmoid(gate) * up).astype(BF16)
        return _dot(act, wo_ref[...])

    @pl.when(f == 0)
    def _():
        h = _modulate(x_ref[...], g_ref[...], m_ref[3 * sub:3 * sub + 1, :], m_ref[3 * sub + 1:3 * sub + 2, :])
        h = h.astype(BF16)
        h_ref[...] = h
        acc_ref[...] = swiglu_slice(h)

    @pl.when(f > 0)
    def _():
        acc_ref[...] += swiglu_slice(h_ref[...])

    if cast_next:
        step = pl.program_id(0) * pl.num_programs(1) + f
        n_in, n_out = cast_next

        @pl.when(step < n_in)
        def _():
            cib_ref[...] = ci_ref[...].astype(BF16)

        @pl.when((step >= n_in) & (step < n_in + n_out))
        def _():
            cob_ref[...] = co_ref[...].astype(BF16)

    @pl.when(f == pl.num_programs(1) - 1)
    def _():
        y = x_ref[...] + 0.5 * m_ref[3 * sub + 2:3 * sub + 3, :] * acc_ref[...]
        if not final:
            o_refs[0][...] = y
        else:
            y = _rms(y, nf_ref[...])
            is_ctx = pl.program_id(0) < ctx_blocks

            @pl.when(is_ctx)
            def _():
                o_refs[0][...] = y

            @pl.when(jnp.logical_not(is_ctx))
            def _():
                o_refs[1][...] = y


def _ffn(geo, x, mods, g, w_in, w_out, nf, *, sub, final, nxt=None):
    tm, tf = TM_FFN, TF_FFN
    nf_blocks = D_FF // tf
    n_in, n_out = 2 * D_FF // CAST_TILE, D_FF // CAST_TILE
    step = lambda i, f: i * nf_blocks + f
    in_blk = lambda i, f: jnp.minimum(step(i, f), n_in - 1)
    out_blk = lambda i, f: jnp.clip(step(i, f) - n_in, 0, n_out - 1)
    ctx_blocks = geo.n_ctx // tm
    whole = pl.BlockSpec((tm, D_MODEL), lambda i, f: (i, 0))
    in_specs = [
        whole,
        pl.BlockSpec((None, N_MOD, D_MODEL), lambda i, f: (geo.cond_row(i, tm), 0, 0)),
        pl.BlockSpec((1, D_MODEL), lambda i, f: (0, 0)),
        pl.BlockSpec((D_MODEL, tf), lambda i, f: (0, f)),
        pl.BlockSpec((D_MODEL, tf), lambda i, f: (0, f + nf_blocks)),
        pl.BlockSpec((tf, D_MODEL), lambda i, f: (f, 0)),
        pl.BlockSpec((1, D_MODEL), lambda i, f: (0, 0)),
    ]
    if final:
        out_specs = [pl.BlockSpec((tm, D_MODEL), lambda i, f: (jnp.minimum(i, ctx_blocks - 1), 0)),
                     pl.BlockSpec((tm, D_MODEL), lambda i, f: (jnp.maximum(i - ctx_blocks, 0), 0))]
        out_shape = [jax.ShapeDtypeStruct((geo.n_ctx, D_MODEL), F32), jax.ShapeDtypeStruct((geo.n_lat, D_MODEL), F32)]
    else:
        out_specs = [whole]
        out_shape = [jax.ShapeDtypeStruct((geo.n_tok, D_MODEL), F32)]
    args = [x, mods, g, w_in, w_in, w_out, nf]
    if nxt is not None:
        w_in_all, w_out_all, layer, half = nxt
        assert (geo.n_tok // tm) * nf_blocks >= n_in + n_out
        in_specs += [
            pl.BlockSpec((None, None, D_MODEL, CAST_TILE), lambda i, f: (layer, half, 0, in_blk(i, f))),
            pl.BlockSpec((None, None, CAST_TILE, D_MODEL), lambda i, f: (layer, half, out_blk(i, f), 0)),
        ]
        out_specs += [
            pl.BlockSpec((D_MODEL, CAST_TILE), lambda i, f: (0, in_blk(i, f))),
            pl.BlockSpec((CAST_TILE, D_MODEL), lambda i, f: (out_blk(i, f), 0)),
        ]
        out_shape += [jax.ShapeDtypeStruct((D_MODEL, 2 * D_FF), BF16), jax.ShapeDtypeStruct((D_FF, D_MODEL), BF16)]
        args += [w_in_all, w_out_all]
    return pl.pallas_call(
        functools.partial(_ffn_kernel, sub=sub, final=final, cast_next=(n_in, n_out) if nxt is not None else None,
                          ctx_blocks=ctx_blocks),
        grid=(geo.n_tok // tm, nf_blocks),
        in_specs=in_specs,
        out_specs=out_specs,
        out_shape=out_shape,
        scratch_shapes=[pltpu.VMEM((tm, D_MODEL), BF16), pltpu.VMEM((tm, D_MODEL), F32)],
        compiler_params=_params("arbitrary", "arbitrary"),
        name="ffn",
    )(*args)


def _mla_proj_kernel(*refs, aliased, ctx_blocks):
    x_ref, m_ref, g_ref, w_ref, qn_ref, kvn_ref, cos_ref, sin_ref = refs[:8]
    cq_ref, ckvb_ref, kr_ref, ckv_ctx_ref, kr_ctx_ref = refs[8 + (2 if aliased else 0):]
    h = _modulate(x_ref[...], g_ref[...], m_ref[3:4, :], m_ref[4:5, :]).astype(BF16)
    z = _dot(h, w_ref[...])
    cq_ref[...] = _rms(z[:, :Q_LORA], qn_ref[...]).astype(BF16)
    ckv = _rms(z[:, Q_LORA:Q_LORA + KV_LORA], kvn_ref[...])
    ckvb_ref[...] = ckv.astype(BF16)
    o = Q_LORA + KV_LORA
    kr = z[:, o:o + QK_ROPE] * cos_ref[...] + z[:, o + QK_ROPE:o + 2 * QK_ROPE] * sin_ref[...]
    kr_ref[...] = kr

    @pl.when(pl.program_id(0) < ctx_blocks)
    def _():
        ckv_ctx_ref[...] = ckv.reshape(ckv_ctx_ref.shape)
        kr_ctx_ref[...] = kr.reshape(kr_ctx_ref.shape)


def _mla_proj(geo, x, mods, g, w_cat, qn, kvn, cos, sin, j, n_layers, prev_cache):
    tm = TM_PROJ
    n_out = w_cat.shape[1]
    seqs = tm // geo.s_ctx
    ctx_blocks = geo.n_ctx // tm
    row = lambda i: (i, 0)
    fix = lambda i: (0, 0)
    cache = lambda i: (jnp.minimum(i, ctx_blocks - 1), j, 0, 0)
    in_specs = [
        pl.BlockSpec((tm, D_MODEL), row),
        pl.BlockSpec((None, N_MOD, D_MODEL), lambda i: (geo.cond_row(i, tm), 0, 0)),
        pl.BlockSpec((1, D_MODEL), fix),
        pl.BlockSpec((D_MODEL, n_out), fix),
        pl.BlockSpec((1, Q_LORA), fix),
        pl.BlockSpec((1, KV_LORA), fix),
        pl.BlockSpec((tm, QK_ROPE), row),
        pl.BlockSpec((tm, QK_ROPE), row),
    ]
    args = [x, mods, g, w_cat, qn, kvn, cos, sin]
    aliases = {}
    if prev_cache is not None:
        in_specs += [pl.BlockSpec(memory_space=pl.ANY)] * 2
        aliases = {len(args): 3, len(args) + 1: 4}
        args += list(prev_cache)
    return pl.pallas_call(
        functools.partial(_mla_proj_kernel, aliased=prev_cache is not None, ctx_blocks=ctx_blocks),
        grid=(geo.n_tok // tm,),
        in_specs=in_specs,
        out_specs=[
            pl.BlockSpec((tm, Q_LORA), row),
            pl.BlockSpec((tm, KV_LORA), row),
            pl.BlockSpec((tm, QK_ROPE), row),
            pl.BlockSpec((seqs, None, geo.s_ctx, KV_LORA), cache),
            pl.BlockSpec((seqs, None, geo.s_ctx, QK_ROPE), cache),
        ],
        out_shape=[
            jax.ShapeDtypeStruct((geo.n_tok, Q_LORA), BF16),
            jax.ShapeDtypeStruct((geo.n_tok, KV_LORA), BF16),
            jax.ShapeDtypeStruct((geo.n_tok, QK_ROPE), F32),
            jax.ShapeDtypeStruct((geo.b_ctx, n_layers, geo.s_ctx, KV_LORA), F32),
            jax.ShapeDtypeStruct((geo.b_ctx, n_layers, geo.s_ctx, QK_ROPE), F32),
        ],
        input_output_aliases=aliases,
        compiler_params=_params("arbitrary"),
        name="mla_proj",
    )(*args)


def _attn_kernel(*refs, s_len, t_past, rope):
    if rope:
        (cq_ref, ckvb_ref, kr_ref, pckv_ref, pkr_ref, cos_ref, sin_ref,
         wq_ref, wqr_ref, wkv_ref, _, o_ref, kv_all, kr_all) = refs
    else:
        cq_ref, ckvb_ref, kr_ref, wq_ref, wkv_ref, o_ref, kv_all, kr_all = refs
    if t_past:
        kv_all[0:t_past, :] = pckv_ref[...].astype(BF16)
        kr_all[0:t_past, :] = pkr_ref[...].astype(BF16)
    kv_all[t_past:, :] = ckvb_ref[...]
    kr_all[t_past:, :] = kr_ref[...].astype(BF16)

    n_q = s_len // Q_TILE
    hg = max(1, ATTN_LOCKSTEP // n_q)
    row_slices = [slice(qb * Q_TILE, (qb + 1) * Q_TILE) for qb in range(n_q)]

    def head_group(g, carry):
        heads = [g * hg + i for i in range(hg)]
        kv = kv_all[...]
        krb = kr_all[...]
        kvp = [_dot(kv, wkv_ref[hd]) for hd in heads]
        kn = [x[:, :QK_NOPE].astype(BF16) for x in kvp]
        v = [x[:, QK_NOPE:].astype(BF16) for x in kvp]
        items = [(i, qb) for i in range(hg) for qb in range(n_q)]
        cq = [cq_ref[rows, :] for rows in row_slices]
        q = [_dot(cq[qb], wq_ref[heads[i]]) for i, qb in items]
        qr = [x[:, QK_NOPE:] for x in q]
        if rope:
            rot = [_dot(cq[qb], wqr_ref[heads[i]]) for i, qb in items]
            qr = [x * cos_ref[row_slices[qb], :] + y * sin_ref[row_slices[qb], :]
                  for x, y, (i, qb) in zip(qr, rot, items)]
        s = [(_dot_nt(x[:, :QK_NOPE].astype(BF16), kn[i]) + _dot_nt(y.astype(BF16), krb)) * ATTN_SCALE
             for x, y, (i, qb) in zip(q, qr, items)]
        p = [jnp.exp(x - jnp.max(x, axis=-1, keepdims=True)) for x in s]
        pr = [(x * (1.0 / jnp.sum(x, axis=-1, keepdims=True))).astype(BF16) for x in p]
        o = [_dot(x, v[i]).astype(BF16) for x, (i, qb) in zip(pr, items)]
        for x, (i, qb) in zip(o, items):
            o_ref[row_slices[qb], pl.ds(pl.multiple_of(heads[i] * V_DIM, V_DIM), V_DIM)] = x
        return carry

    lax.fori_loop(0, MLA_HEADS // hg, head_group, 0)


def _attn(geo, cq, ckvb, kr, wq, wqr, wkv, *, latent, past_ckv=None, past_kr=None, cos=None, sin=None, o_ctx=None):
    if latent:
        nb, s_len, off = geo.b_lat, geo.s_lat, geo.n_ctx // geo.s_lat
        t_past = past_ckv.shape[1]
    else:
        nb, s_len, off, t_past = geo.b_ctx, geo.s_ctx, 0, 0
    row = lambda b: (b + off, 0)
    fix3 = lambda b: (0, 0, 0)
    tok = lambda width: pl.BlockSpec((s_len, width), row)
    wspec = lambda w: pl.BlockSpec(w.shape, fix3)
    in_specs = [tok(Q_LORA), tok(KV_LORA), tok(QK_ROPE)]
    args = [cq, ckvb, kr]
    if latent:
        in_specs += [pl.BlockSpec((None, t_past, KV_LORA), lambda b: (b, 0, 0)),
                     pl.BlockSpec((None, t_past, QK_ROPE), lambda b: (b, 0, 0)),
                     tok(QK_ROPE), tok(QK_ROPE), wspec(wq), wspec(wqr)]
        args += [past_ckv, past_kr, cos, sin, wq, wqr]
    else:
        in_specs += [wspec(wq)]
        args += [wq]
    in_specs += [wspec(wkv)]
    args += [wkv]
    aliases = {}
    if latent:
        in_specs += [pl.BlockSpec(memory_space=pl.ANY)]
        args += [o_ctx]
        aliases = {len(args) - 1: 0}
    kern = functools.partial(_attn_kernel, s_len=s_len, t_past=t_past, rope=latent)
    return pl.pallas_call(
        kern,
        grid=(nb,),
        in_specs=in_specs,
        out_specs=pl.BlockSpec((s_len, D_MODEL), row),
        out_shape=jax.ShapeDtypeStruct((geo.n_tok, D_MODEL), BF16),
        input_output_aliases=aliases,
        scratch_shapes=[pltpu.VMEM((t_past + s_len, KV_LORA), BF16),
                        pltpu.VMEM((t_past + s_len, QK_ROPE), BF16)],
        compiler_params=_params("parallel"),
        name="attn_lat" if latent else "attn_ctx",
    )(*args)


def _out_proj_kernel(a_ref, w_ref, x_ref, m_ref, o_ref):
    o_ref[...] = x_ref[...] + m_ref[5:6, :] * _dot(a_ref[...], w_ref[...])


def _out_proj(geo, a, w, x, mods):
    tm = TM_PROJ
    row = pl.BlockSpec((tm, D_MODEL), lambda i: (i, 0))
    return pl.pallas_call(
        _out_proj_kernel,
        grid=(geo.n_tok // tm,),
        in_specs=[row, pl.BlockSpec((D_MODEL, D_MODEL), lambda i: (0, 0)), row,
                  pl.BlockSpec((None, N_MOD, D_MODEL), lambda i: (geo.cond_row(i, tm), 0, 0))],
        out_specs=row,
        out_shape=jax.ShapeDtypeStruct((geo.n_tok, D_MODEL), F32),
        compiler_params=_params("parallel"),
        name="out_proj",
    )(a, w, x, mods)


def _shifted(geo, x_ref, xp_ref, xn_ref, m_ref, g_ref, tm):
    i = pl.program_id(0)
    g, shift, scale = g_ref[...], m_ref[3:4, :], m_ref[4:5, :]
    h = _modulate(x_ref[...], g, shift, scale)
    start = i * tm
    seq = jnp.where(start < geo.n_ctx, geo.s_ctx, geo.s_lat)
    rel = jnp.where(start < geo.n_ctx, start, start - geo.n_ctx)
    has_prev = (rel % seq) != 0
    has_next = ((rel + tm) % seq) != 0
    hp = _modulate(xp_ref[...], g, shift, scale)[HALO_ROWS - 1:HALO_ROWS, :]
    hn = _modulate(xn_ref[...], g, shift, scale)[0:1, :]
    hp = jnp.where(has_prev, hp, 0.0)
    hn = jnp.where(has_next, hn, 0.0)
    r = lax.broadcasted_iota(jnp.int32, h.shape, 0)
    down = jnp.where(r == 0, hp, pltpu.roll(h, 1, 0))
    up = jnp.where(r == tm - 1, hn, pltpu.roll(h, tm - 1, 0))
    return h, 0.5 * (down + up) - h


def _halo_specs(geo, tm):
    per_block = tm // HALO_ROWS
    last = geo.n_tok // HALO_ROWS - 1
    return [
        pl.BlockSpec((tm, D_MODEL), lambda i, *_: (i, 0)),
        pl.BlockSpec((HALO_ROWS, D_MODEL), lambda i, *_: (jnp.maximum(i * per_block - 1, 0), 0)),
        pl.BlockSpec((HALO_ROWS, D_MODEL), lambda i, *_: (jnp.minimum((i + 1) * per_block, last), 0)),
        pl.BlockSpec((None, N_MOD, D_MODEL), lambda i, *_: (geo.cond_row(i, tm), 0, 0)),
        pl.BlockSpec((1, D_MODEL), lambda i, *_: (0, 0)),
    ]


def _rkv_kernel(x_ref, xp_ref, xn_ref, m_ref, g_ref, mu_ref, w_ref, o_ref, h_ref, xx_ref, *, geo, tm):
    proj = pl.program_id(1)

    def project(h, xx):
        xm = (h + xx * mu_ref[...]).astype(BF16)
        o_ref[...] = _dot(xm, w_ref[proj])

    @pl.when(proj == 0)
    def _():
        h, xx = _shifted(geo, x_ref, xp_ref, xn_ref, m_ref, g_ref, tm)
        h_ref[...] = h
        xx_ref[...] = xx
        project(h, xx)

    @pl.when(proj > 0)
    def _():
        project(h_ref[...], xx_ref[...])


def _rkv(geo, x, mods, g, mu3, w3):
    tm = TM_RWKV
    return pl.pallas_call(
        functools.partial(_rkv_kernel, geo=geo, tm=tm),
        grid=(geo.n_tok // tm, 3),
        in_specs=_halo_specs(geo, tm) + [
            pl.BlockSpec((None, 1, D_MODEL), lambda i, p: (p, 0, 0)),
            pl.BlockSpec((3, D_MODEL, D_MODEL), lambda i, p: (0, 0, 0), pipeline_mode=pl.Buffered(1)),
        ],
        out_specs=pl.BlockSpec((None, tm, D_MODEL), lambda i, p: (p, i, 0)),
        out_shape=jax.ShapeDtypeStruct((3, geo.n_tok, D_MODEL), F32),
        scratch_shapes=[pltpu.VMEM((tm, D_MODEL), F32), pltpu.VMEM((tm, D_MODEL), F32)],
        compiler_params=_params("parallel", "arbitrary"),
        name="rwkv_rkv",
    )(x, x, x, mods, g, mu3, w3)


def _lora_kernel(x_ref, xp_ref, xn_ref, m_ref, g_ref, mug_ref, mud_ref, g1_ref, g2_ref,
                 wa1_ref, wa2_ref, w0_ref, a0_ref, gate_ref, dec_ref, asig_ref, *, geo, tm):
    h, xx = _shifted(geo, x_ref, xp_ref, xn_ref, m_ref, g_ref, tm)
    mix = lambda mu: (h + xx * mu).astype(BF16)
    zg = _dot(mix(mug_ref[...]), g1_ref[...])
    gate_ref[...] = _dot(_sigmoid(zg).astype(BF16), g2_ref[...])
    for d in range(2):
        zw = _dot(mix(mud_ref[2 * d]), wa1_ref[d])
        wl = w0_ref[d] + _dot(jnp.tanh(zw).astype(BF16), wa2_ref[d])
        dec_ref[d] = -LOG_DECAY_SCALE * _sigmoid(wl)
        za = _dot(mix(mud_ref[2 * d + 1]), wa1_ref[2 + d])
        asig_ref[d] = _sigmoid(a0_ref[d] + _dot(za.astype(BF16), wa2_ref[2 + d]))


def _lora(geo, x, mods, g, mu_g, mu_dir, g1, g2, wa1, wa2, w0, a0):
    tm = TM_RWKV
    full = lambda a: pl.BlockSpec(a.shape, lambda i: (0,) * a.ndim)
    tok2 = pl.BlockSpec((2, tm, D_MODEL), lambda i: (0, i, 0))
    return pl.pallas_call(
        functools.partial(_lora_kernel, geo=geo, tm=tm),
        grid=(geo.n_tok // tm,),
        in_specs=_halo_specs(geo, tm) + [full(a) for a in (mu_g, mu_dir, g1, g2, wa1, wa2, w0, a0)],
        out_specs=[pl.BlockSpec((tm, D_MODEL), lambda i: (i, 0)), tok2, tok2],
        out_shape=[jax.ShapeDtypeStruct((geo.n_tok, D_MODEL), F32),
                   jax.ShapeDtypeStruct((2, geo.n_tok, D_MODEL), F32),
                   jax.ShapeDtypeStruct((2, geo.n_tok, D_MODEL), F32)],
        compiler_params=_params("parallel"),
        name="rwkv_lora",
    )(x, x, x, mods, g, mu_g, mu_dir, g1, g2, wa1, wa2, w0, a0)


def _head_sums(x, ones_blk):
    return _dot(x.astype(BF16), ones_blk)


def _head_ones(width):
    r = lax.broadcasted_iota(jnp.int32, (width, width), 0)
    c = lax.broadcasted_iota(jnp.int32, (width, width), 1)
    same = (r // RWKV_HEAD) == (c // RWKV_HEAD)
    return same, jnp.where(same, 1.0, 0.0).astype(BF16)


def _wkv_kernel(*refs, geo, tb, aliased):
    r_ref, k_ref, v_ref, dec_ref, asig_ref, kk_ref, ka_ref, rk_ref, s0_ref = refs[:9]
    y_ref, bon_ref, sf_fwd_ref, sf_bwd_ref, s_ref = refs[9 + (2 if aliased else 0):]
    d = pl.program_id(0)
    i = pl.program_id(1)
    nblk = pl.num_programs(1)
    blk = i + d * (nblk - 1 - 2 * i)
    start = blk * tb
    seq = jnp.where(start < geo.n_ctx, geo.s_ctx, geo.s_lat)
    rel = jnp.where(start < geo.n_ctx, start, start - geo.n_ctx)
    at_lo = (rel % seq) == 0
    at_hi = ((rel + tb) % seq) == 0
    first = jnp.where(d == 0, at_lo, at_hi)
    last = jnp.where(d == 0, at_hi, at_lo)

    @pl.when(first)
    def _():
        s_ref[...] = s0_ref[...]

    nch = tb // CHUNK
    sgn = 1 - 2 * d
    fwd = (d == 0).astype(F32)
    row = lax.broadcasted_iota(jnp.int32, (CHUNK, PAIR), 0)
    lane = lax.broadcasted_iota(jnp.int32, (CHUNK, PAIR), 1)
    sidx = lane & (RWKV_HEAD - 1)
    delta = (row - sidx) * sgn
    strict = delta > 0
    incl = delta >= 0
    eye = jnp.where(row == sidx, 1.0, 0.0)
    off_masks = []
    m = 1
    while m < CHUNK:
        off_masks.append(strict & ((row // (2 * m)) == (sidx // (2 * m))) & ((row // m) != (sidx // m)))
        m *= 2
    head0 = lane < RWKV_HEAD
    tr = lax.broadcasted_iota(jnp.int32, (CHUNK, 3 * CHUNK), 0)
    ts = lax.broadcasted_iota(jnp.int32, (CHUNK, 3 * CHUNK), 1) & (CHUNK - 1)
    tri3 = jnp.where((tr - ts) * sgn >= 0, 1.0, 0.0).astype(BF16)
    same_head, ones_blk = _head_ones(PAIR)

    def expand(x):
        return jnp.concatenate([jnp.where(head0, x, 0.0), jnp.where(head0, 0.0, x)], axis=0)

    def chunk_body(ci, carry):
        cc = ci * sgn + d * (nch - 1)
        rows = pl.ds(pl.multiple_of(cc * CHUNK, CHUNK), CHUNK)

        def group_body(pg, carry2):
            ps = [pg * PAIR_GROUP + q for q in range(PAIR_GROUP)]
            cols = [pl.ds(pl.multiple_of(p * PAIR, PAIR), PAIR) for p in ps]
            each = lambda f, *ls: [f(*xs) for xs in zip(*ls)]
            r = [r_ref[rows, c] for c in cols]
            k = [k_ref[rows, c] for c in cols]
            v = [v_ref[rows, c] for c in cols]
            logw = [dec_ref[rows, c] for c in cols]
            a = [asig_ref[rows, c] for c in cols]
            kk = [x * kk_ref[:, c] for x, c in zip(k, cols)]
            kd = [x * (1.0 + (y - 1.0) * ka_ref[:, c]) for x, y, c in zip(k, a, cols)]
            sums = [_head_sums(jnp.concatenate([x * x, y * z * rk_ref[:, c]], axis=0), ones_blk)
                    for x, y, z, c in zip(kk, r, kd, cols)]
            kk = each(lambda x, s: x / jnp.maximum(jnp.sqrt(s[:CHUNK]), 1e-12), kk, sums)
            b_in = each(lambda x, y: x * y, kk, a)
            for c, s, y in zip(cols, sums, v):
                bon_ref[rows, c] = s[CHUNK:] * y

            l1 =[x.astype(BF16) for x in logw]
            e1 = each(lambda x, y: x - y.astype(F32), logw, l1)
            l2 = [x.astype(BF16) for x in e1]
            l3 = each(lambda x, y: (x - y.astype(F32)).astype(BF16), e1, l2)
            cum = each(lambda x, y, z: _dot(tri3, jnp.concatenate([x, y, z], axis=0)), l1, l2, l3)
            tot = [fwd * x[CHUNK - 1:CHUNK, :] + (1.0 - fwd) * x[0:1, :] for x in cum]
            c_inv = [jnp.exp(-x) for x in cum]
            at = each(lambda x, y, z: -x * jnp.exp(y - z), kk, cum, logw)
            rt = each(lambda x, y: x * jnp.exp(y), r, cum)
            c_end = each(lambda x, y: jnp.exp(x - y), tot, cum)

            lhs = each(lambda x, y: jnp.concatenate([x, y], axis=0).astype(BF16), at, rt)
            rhs = each(lambda x, y, z: jnp.concatenate([expand(x * z), expand(y * z)], axis=0).astype(BF16),
                       b_in, kd, c_inv)
            g = each(_dot_nt, lhs, rhs)
            ab = [jnp.where(strict, x[:CHUNK, :PAIR], 0.0) for x in g]
            ak = [jnp.where(strict, x[:CHUNK, PAIR:], 0.0) for x in g]
            rbk = [jnp.concatenate([jnp.where(incl, x[CHUNK:, :PAIR], 0.0),
                                    jnp.where(incl, x[CHUNK:, PAIR:], 0.0)], axis=1).astype(BF16) for x in g]

            t_inv = [eye + jnp.where(off_masks[0], x, 0.0) for x in ab]
            for off in off_masks[1:]:
                lx = each(lambda x, t: _dot(jnp.where(off, x, 0.0).astype(BF16), expand(t).astype(BF16)), ab, t_inv)
                t_inv = each(lambda t, x: t + _dot(t.astype(BF16), expand(x).astype(BF16)), t_inv, lx)

            s2 = [s_ref[p] for p in ps]
            ars = each(lambda x, s: _dot_nt(x, s.astype(BF16)), lhs, s2)
            vexp = [expand(x).astype(BF16) for x in v]
            rhs_u = each(lambda x, y, z: x[:CHUNK] + _dot(y.astype(BF16), z), ars, ak, vexp)
            u = each(lambda t, x: _dot(t.astype(BF16), expand(x).astype(BF16)), t_inv, rhs_u)
            uv = each(lambda x, y: jnp.concatenate([expand(x).astype(BF16), y], axis=0), u, vexp)
            y_out = each(lambda x, y, z: x[CHUNK:] + _dot(y, z), ars, rbk, uv)
            for c, x in zip(cols, y_out):
                y_ref[rows, c] = x
            uv_t = each(lambda x, y: jnp.concatenate([x, y], axis=0).T.astype(BF16), u, v)
            bk = each(lambda x, y, z: jnp.concatenate([x * z, y * z], axis=0).astype(BF16), b_in, kd, c_end)
            upd = each(_dot, uv_t, bk)
            for p, s, x, t in zip(ps, s2, upd, tot):
                s_ref[p] = s * jnp.exp(t) + jnp.where(same_head, x, 0.0)
            return carry2

        lax.fori_loop(0, N_PAIRS // PAIR_GROUP, group_body, 0)
        return carry

    lax.fori_loop(0, nch, chunk_body, 0)

    def write_states(dst_ref):
        for p in range(N_PAIRS):
            s = s_ref[p]
            dst_ref[2 * p] = s[:RWKV_HEAD, :RWKV_HEAD]
            dst_ref[2 * p + 1] = s[RWKV_HEAD:, RWKV_HEAD:]

    ends_ctx = last & (start < geo.n_ctx)

    @pl.when(ends_ctx & (d == 0))
    def _():
        write_states(sf_fwd_ref)

    @pl.when(ends_ctx & (d == 1))
    def _():
        write_states(sf_bwd_ref)


def _wkv(geo, rkv, dec, asig, kk_c, ka_c, rk_c, s0, j, n_layers, prev_states):
    tb = TB_WKV
    nblk = geo.n_tok // tb
    blk = lambda d, i: i + d * (nblk - 1 - 2 * i)
    tok = lambda which: pl.BlockSpec((None, tb, D_MODEL), lambda d, i: (which, blk(d, i), 0))
    perdir = pl.BlockSpec((None, tb, D_MODEL), lambda d, i: (d, blk(d, i), 0))
    const = pl.BlockSpec((1, D_MODEL), lambda d, i: (0, 0))
    last_seq = geo.b_ctx - 1
    seq = lambda d, i: jnp.minimum(blk(d, i) * tb // geo.s_ctx, last_seq)
    final = lambda which: pl.BlockSpec(
        (None, None, RWKV_HEADS, RWKV_HEAD, RWKV_HEAD),
        lambda d, i: (jnp.where(d == which, seq(d, i), last_seq), j, 0, 0, 0))
    final_shape = jax.ShapeDtypeStruct((geo.b_ctx, n_layers, RWKV_HEADS, RWKV_HEAD, RWKV_HEAD), F32)
    in_specs = [tok(0), tok(1), tok(2), perdir, perdir, const, const, const,
                pl.BlockSpec((None, None, N_PAIRS, PAIR, PAIR), lambda d, i: (d, geo.cond_row(blk(d, i), tb), 0, 0, 0))]
    args = [rkv, rkv, rkv, dec, asig, kk_c, ka_c, rk_c, s0]
    aliases = {}
    if prev_states is not None:
        in_specs += [pl.BlockSpec(memory_space=pl.ANY)] * 2
        aliases = {len(args): 2, len(args) + 1: 3}
        args += list(prev_states)
    return pl.pallas_call(
        functools.partial(_wkv_kernel, geo=geo, tb=tb, aliased=prev_states is not None),
        grid=(2, nblk),
        in_specs=in_specs,
        out_specs=[perdir, perdir, final(0), final(1)],
        out_shape=[jax.ShapeDtypeStruct((2, geo.n_tok, D_MODEL), F32),
                   jax.ShapeDtypeStruct((2, geo.n_tok, D_MODEL), F32), final_shape, final_shape],
        input_output_aliases=aliases,
        scratch_shapes=[pltpu.VMEM((N_PAIRS, PAIR, PAIR), F32)],
        compiler_params=_params("arbitrary", "arbitrary"),
        name="wkv_chunked",
    )(*args)


def _rwkv_out_kernel(y_ref, bon_ref, gate_ref, lw_ref, lb_ref, w_ref, x_ref, m_ref, o_ref, z_ref):
    _, ones_blk = _head_ones(GN_SLAB)
    inv_n = 1.0 / RWKV_HEAD
    for s in range(D_MODEL // GN_SLAB):
        cols = slice(s * GN_SLAB, (s + 1) * GN_SLAB)
        y = y_ref[0, :, cols] + y_ref[1, :, cols]
        yc = y - _head_sums(y, ones_blk) * inv_n
        var = _head_sums(yc * yc, ones_blk) * inv_n
        z = yc * lax.rsqrt(var + GN_EPS) * lw_ref[:, cols] + lb_ref[:, cols] + bon_ref[0, :, cols] + bon_ref[1, :, cols]
        z_ref[:, cols] = (z * gate_ref[:, cols]).astype(BF16)
    o_ref[...] = x_ref[...] + m_ref[5:6, :] * _dot(z_ref[...], w_ref[...])


def _rwkv_out(geo, y, bon, gate, lw, lb, w, x, mods):
    tm = TM_OUT
    row = pl.BlockSpec((tm, D_MODEL), lambda i: (i, 0))
    both = pl.BlockSpec((2, tm, D_MODEL), lambda i: (0, i, 0))
    const = pl.BlockSpec((1, D_MODEL), lambda i: (0, 0))
    return pl.pallas_call(
        _rwkv_out_kernel,
        grid=(geo.n_tok // tm,),
        in_specs=[both, both, row, const, const, pl.BlockSpec((D_MODEL, D_MODEL), lambda i: (0, 0)), row,
                  pl.BlockSpec((None, N_MOD, D_MODEL), lambda i: (geo.cond_row(i, tm), 0, 0))],
        out_specs=row,
        out_shape=jax.ShapeDtypeStruct((geo.n_tok, D_MODEL), F32),
        scratch_shapes=[pltpu.VMEM((tm, D_MODEL), BF16)],
        compiler_params=_params("parallel"),
        name="rwkv_out",
    )(y, bon, gate, lw, lb, w, x, mods)


def _state_to_pairs(s):
    b = s.shape[0]
    s5 = s.reshape(b, N_PAIRS, 2, RWKV_HEAD, RWKV_HEAD)
    z = jnp.zeros_like(s5[:, :, 0])
    top = jnp.concatenate([s5[:, :, 0], z], axis=-1)
    bot = jnp.concatenate([z, s5[:, :, 1]], axis=-1)
    return jnp.concatenate([top, bot], axis=-2)


def _rope_tables(geo):
    rows = geo.s_lat // GRID_W
    row = jnp.repeat(jnp.arange(rows, dtype=F32), GRID_W)
    col = jnp.tile(jnp.arange(GRID_W, dtype=F32), rows)
    inv = ROPE_THETA ** (-jnp.arange(ROPE_FREQS, dtype=F32) / ROPE_FREQS)
    ang = jnp.stack([row[:, None] * inv, col[:, None] * inv], axis=1)
    ang = jnp.broadcast_to(ang[:, :, None, :], (geo.s_lat, 2, 2, ROPE_FREQS)).reshape(geo.s_lat, QK_ROPE)
    cos = jnp.concatenate([jnp.ones((geo.n_ctx, QK_ROPE), F32), jnp.tile(jnp.cos(ang), (geo.b_lat, 1))], axis=0)
    sin = jnp.concatenate([jnp.zeros((geo.n_ctx, QK_ROPE), F32), jnp.tile(jnp.sin(ang), (geo.b_lat, 1))], axis=0)
    return cos, sin


def _rot_cols(w):
    w4 = w.reshape(w.shape[:-1] + (2, 2, ROPE_FREQS))
    return jnp.stack([-w4[..., 1, :], w4[..., 0, :]], axis=-2).reshape(w.shape)


def _mla_layer(geo, x, mods, g, j, cos, sin, cache_ckv, cache_krope, prev_cache, p):
    w_dkv = p['mla_w_dkv'][j]
    w_cat = jnp.concatenate([p['mla_w_dq'][j], w_dkv, _rot_cols(w_dkv[:, KV_LORA:])], axis=1).astype(BF16)
    cq, ckvb, kr, new_ckv, new_kr = _mla_proj(geo, x, mods, g, w_cat, p['mla_q_norm'][j][None],
                                              p['mla_kv_norm'][j][None], cos, sin, j, p['mla_w_dq'].shape[0],
                                              prev_cache)
    w_uq = p['mla_w_uq'][j].reshape(Q_LORA, MLA_HEADS, QK_NOPE + QK_ROPE).transpose(1, 0, 2)
    wq = w_uq.astype(BF16)
    wqr = _rot_cols(w_uq[..., QK_NOPE:]).astype(BF16)
    wkv = p['mla_w_ukv'][j].reshape(KV_LORA, MLA_HEADS, QK_NOPE + V_DIM).transpose(1, 0, 2).astype(BF16)
    o_ctx = _attn(geo, cq, ckvb, kr, wq, wqr, wkv, latent=False)
    o = _attn(geo, cq, ckvb, kr, wq, wqr, wkv, latent=True,
              past_ckv=cache_ckv[:, j], past_kr=cache_krope[:, j], cos=cos, sin=sin, o_ctx=o_ctx)
    x = _out_proj(geo, o, p['mla_w_o'][j].astype(BF16), x, mods)
    return x, (new_ckv, new_kr)


def _rwkv_layer(geo, x, mods, g, j, s0_fwd, s0_bwd, prev_states, p):
    mu = p['rwkv_mu'][j]
    mu_dir = p['rwkv_mu_dir'][j].reshape(4, 1, D_MODEL)
    w3 = jnp.stack([p['rwkv_w_r'][j], p['rwkv_w_k'][j], p['rwkv_w_v'][j]]).astype(BF16)
    rkv = _rkv(geo, x, mods, g, mu[:3, None, :], w3)
    pad1 = lambda w: jnp.pad(w, ((0, 0), (0, 0), (0, LORA_PAD - w.shape[-1])))
    pad2 = lambda w: jnp.pad(w, ((0, 0), (0, LORA_PAD - w.shape[-2]), (0, 0)))
    wa1 = jnp.concatenate([pad1(p['rwkv_w1'][j]), pad1(p['rwkv_a1'][j])]).astype(BF16)
    wa2 = jnp.concatenate([pad2(p['rwkv_w2'][j]), pad2(p['rwkv_a2'][j])]).astype(BF16)
    gate, dec, asig = _lora(geo, x, mods, g, mu[3:4], mu_dir, p['rwkv_g1'][j].astype(BF16),
                            p['rwkv_g2'][j].astype(BF16), wa1, wa2,
                            p['rwkv_w0'][j][:, None, :], p['rwkv_a0'][j][:, None, :])
    zero = jnp.zeros((2, 1, N_PAIRS, PAIR, PAIR), F32)
    s0 = jnp.concatenate([zero, jnp.stack([_state_to_pairs(s0_fwd), _state_to_pairs(s0_bwd)])], axis=1)
    y, bon, sf, sb = _wkv(geo, rkv, dec, asig, p['rwkv_k_k'][j][None], p['rwkv_k_a'][j][None],
                          p['rwkv_r_k'][j].reshape(1, D_MODEL), s0, j, p['rwkv_mu'].shape[0], prev_states)
    x = _rwkv_out(geo, y, bon, gate, p['rwkv_ln_w'][j][None], p['rwkv_ln_b'][j][None],
                  p['rwkv_w_o'][j].astype(BF16), x, mods)
    return x, (sf, sb)


def kernel(x_prompt, x_sample, cache_ckv, cache_krope, state_wkv_fwd, state_wkv_bwd, c, c_ctx, w_ada, b_ada, norm_sub, norm_final, w_ffn_in, w_ffn_out, mla_w_dq, mla_q_norm, mla_w_uq, mla_w_dkv, mla_kv_norm, mla_w_ukv, mla_w_o, rwkv_mu, rwkv_mu_dir, rwkv_w_r, rwkv_w_k, rwkv_w_v, rwkv_w0, rwkv_w1, rwkv_w2, rwkv_a0, rwkv_a1, rwkv_a2, rwkv_g1, rwkv_g2, rwkv_k_k, rwkv_k_a, rwkv_r_k, rwkv_ln_w, rwkv_ln_b, rwkv_w_o):
    p = dict(mla_w_dq=mla_w_dq, mla_q_norm=mla_q_norm, mla_w_uq=mla_w_uq, mla_w_dkv=mla_w_dkv,
             mla_kv_norm=mla_kv_norm, mla_w_ukv=mla_w_ukv, mla_w_o=mla_w_o,
             rwkv_mu=rwkv_mu, rwkv_mu_dir=rwkv_mu_dir, rwkv_w_r=rwkv_w_r, rwkv_w_k=rwkv_w_k,
             rwkv_w_v=rwkv_w_v, rwkv_w0=rwkv_w0, rwkv_w1=rwkv_w1, rwkv_w2=rwkv_w2,
             rwkv_a0=rwkv_a0, rwkv_a1=rwkv_a1, rwkv_a2=rwkv_a2, rwkv_g1=rwkv_g1, rwkv_g2=rwkv_g2,
             rwkv_k_k=rwkv_k_k, rwkv_k_a=rwkv_k_a, rwkv_r_k=rwkv_r_k,
             rwkv_ln_w=rwkv_ln_w, rwkv_ln_b=rwkv_ln_b, rwkv_w_o=rwkv_w_o)
    b_ctx, s_ctx, _ = x_prompt.shape
    b_lat, s_lat, _ = x_sample.shape
    geo = _Geom(b_ctx, s_ctx, b_lat, s_lat)
    assert geo.n_ctx % s_lat == 0 and s_ctx % TM_RWKV == 0 and s_lat % TM_FFN == 0 and geo.n_ctx % TM_FFN == 0
    assert TM_PROJ % s_ctx == 0 and geo.n_ctx % TM_PROJ == 0 and s_lat % TM_PROJ == 0
    assert s_ctx == TB_WKV and s_lat % TB_WKV == 0 and s_ctx % TM_OUT == 0

    x = jnp.concatenate([x_prompt.reshape(geo.n_ctx, D_MODEL), x_sample.reshape(geo.n_lat, D_MODEL)], axis=0)
    cond =jnp.concatenate([c_ctx[None], c, jnp.zeros((COND_ROWS - 1 - b_lat, D_MODEL), F32)], axis=0)
    mods_all = _ada(cond, w_ada, b_ada).reshape(DEPTH, COND_ROWS, N_MOD, D_MODEL)
    cos, sin = _rope_tables(geo)
    nf = norm_final[None]
    w_in = w_ffn_in[0, 0].astype(BF16)
    w_out = w_ffn_out[0, 0].astype(BF16)

    cache, states = None, None
    for l in range(DEPTH):
        j = l // N_MIXERS
        mods = mods_all[l]
        x, w_in, w_out = _ffn(geo, x, mods, norm_sub[l, 0][None], w_in, w_out, nf, sub=0, final=False,
                              nxt=(w_ffn_in, w_ffn_out, l, 1))
        if l % N_MIXERS == 0:
            x, cache = _mla_layer(geo, x, mods, norm_sub[l, 1][None], j, cos, sin, cache_ckv, cache_krope, cache, p)
        else:
            x, states = _rwkv_layer(geo, x, mods, norm_sub[l, 1][None], j,
                                    state_wkv_fwd[:, j], state_wkv_bwd[:, j], states, p)
        if l < DEPTH - 1:
            x, w_in, w_out = _ffn(geo, x, mods, norm_sub[l, 2][None], w_in, w_out, nf, sub=2, final=False,
                                  nxt=(w_ffn_in, w_ffn_out, l + 1, 0))
        else:
            y_ctx, y_lat = _ffn(geo, x, mods, norm_sub[l, 2][None], w_in, w_out, nf, sub=2, final=True)

    return (y_ctx.reshape(b_ctx, s_ctx, D_MODEL), y_lat.reshape(b_lat, s_lat, D_MODEL),
            cache[0], cache[1], states[0], states[1])
```

```python
import functools

import jax
import jax.numpy as jnp
from jax import lax
from jax.experimental import pallas as pl
from jax.experimental.pallas import tpu as pltpu

D_MODEL = 2048
DEPTH = 4
N_MIXERS = 2
D_FF = 5632
N_MOD = 9
RMS_EPS = 1e-6
MLA_HEADS = 16
Q_LORA = 512
KV_LORA = 512
QK_NOPE = 128
QK_ROPE = 64
V_DIM = 128
ROPE_FREQS = QK_ROPE // 4
ROPE_THETA = 10000.0
GRID_W = 64
ATTN_SCALE = (QK_NOPE + QK_ROPE) ** -0.5
RWKV_HEAD = 64
RWKV_HEADS = D_MODEL // RWKV_HEAD
GN_EPS = 64e-5
LOG_DECAY_SCALE = 0.6065306597126334
LORA_PAD = 128

COND_ROWS = 8
HALO_ROWS = 8
VMEM_LIMIT = 56 * 1024 * 1024

TM_FFN = 512
TF_FFN = 512
CAST_TILE = 256
TM_PROJ = 512
TM_RWKV = 256
TN_ADA = 1024
Q_TILE = 256
ATTN_LOCKSTEP = 8
CHUNK = 64
PAIR = 2 * RWKV_HEAD
N_PAIRS = D_MODEL // PAIR
PAIR_GROUP = 16
TB_WKV = 256
TM_OUT = 256
GN_SLAB = 256

BF16 = jnp.bfloat16
F32 = jnp.float32


def _params(*sem):
    return pltpu.CompilerParams(dimension_semantics=sem, vmem_limit_bytes=VMEM_LIMIT)


def _sigmoid(x):
    return 1.0 / (1.0 + jnp.exp(-x))


def _modulate(x, g, shift, scale):
    ms = jnp.mean(x * x, axis=-1, keepdims=True)
    return (x * lax.rsqrt(ms + RMS_EPS) * g) * (1.0 + scale) + shift


def _rms(x, w):
    ms = jnp.mean(x * x, axis=-1, keepdims=True)
    return x * lax.rsqrt(ms + RMS_EPS) * w


def _dot(a, b):
    return jnp.dot(a, b, preferred_element_type=F32)


def _dot_nt(a, b):
    return lax.dot_general(a, b, (((1,), (1,)), ((), ())), preferred_element_type=F32)


class _Geom:
    def __init__(self, b_ctx, s_ctx, b_lat, s_lat):
        self.b_ctx, self.s_ctx, self.b_lat, self.s_lat = b_ctx, s_ctx, b_lat, s_lat
        self.n_ctx = b_ctx * s_ctx
        self.n_lat = b_lat * s_lat
        self.n_tok = self.n_ctx + self.n_lat

    def cond_row(self, i, tm):
        start = i * tm
        return jnp.where(start < self.n_ctx, 0, 1 + (start - self.n_ctx) // self.s_lat)


def _ada_kernel(c_ref, w_ref, b_ref, o_ref):
    c = c_ref[...]
    s = (c * _sigmoid(c)).astype(BF16)
    o_ref[...] = _dot(s, w_ref[...].astype(BF16)) + b_ref[...]


def _ada(cond, w_ada, b_ada):
    n = N_MOD * D_MODEL
    return pl.pallas_call(
        _ada_kernel,
        grid=(DEPTH, n // TN_ADA),
        in_specs=[
            pl.BlockSpec((COND_ROWS, D_MODEL), lambda l, j: (0, 0)),
            pl.BlockSpec((None, D_MODEL, TN_ADA), lambda l, j: (l, 0, j)),
            pl.BlockSpec((None, 1, TN_ADA), lambda l, j: (l, 0, j)),
        ],
        out_specs=pl.BlockSpec((None, COND_ROWS, TN_ADA), lambda l, j: (l, 0, j)),
        out_shape=jax.ShapeDtypeStruct((DEPTH, COND_ROWS, n), F32),
        compiler_params=_params("parallel", "parallel"),
        name="ada",
    )(cond, w_ada, b_ada.reshape(DEPTH, 1, n))


def _ffn_kernel(*refs, sub, final, cast_next, ctx_blocks):
    refs = list(refs)
    take = lambda n: [refs.pop(0) for _ in range(n)]
    x_ref, m_ref, g_ref, wg_ref, wu_ref, wo_ref, nf_ref = take(7)
    if cast_next:
        ci_ref, co_ref = take(2)
    o_refs = take(2 if final else 1)
    if cast_next:
        cib_ref, cob_ref = take(2)
    h_ref, acc_ref = refs
    f = pl.program_id(1)

    def swiglu_slice(h):
        gate = _dot(h, wg_ref[...])
        up = _dot(h, wu_ref[...])
        act = (gate * _sigmoid(gate) * up).astype(BF16)
        return _dot(act, wo_ref[...])

    @pl.when(f == 0)
    def _():
        h = _modulate(x_ref[...], g_ref[...], m_ref[3 * sub:3 * sub + 1, :], m_ref[3 * sub + 1:3 * sub + 2, :])
        h = h.astype(BF16)
        h_ref[...] = h
        acc_ref[...] = swiglu_slice(h)

    @pl.when(f > 0)
    def _():
        acc_ref[...] += swiglu_slice(h_ref[...])

    if cast_next:
        step = pl.program_id(0) * pl.num_programs(1) + f
        n_in, n_out = cast_next

        @pl.when(step < n_in)
        def _():
            cib_ref[...] = ci_ref[...].astype(BF16)

        @pl.when((step >= n_in) & (step < n_in + n_out))
        def _():
            cob_ref[...] = co_ref[...].astype(BF16)

    @pl.when(f == pl.num_programs(1) - 1)
    def _():
        y = x_ref[...] + 0.5 * m_ref[3 * sub + 2:3 * sub + 3, :] * acc_ref[...]
        if not final:
            o_refs[0][...] = y
        else:
            y = _rms(y, nf_ref[...])
            is_ctx = pl.program_id(0) < ctx_blocks

            @pl.when(is_ctx)
            def _():
                o_refs[0][...] = y

            @pl.when(jnp.logical_not(is_ctx))
            def _():
                o_refs[1][...] = y


def _ffn(geo, x, mods, g, w_in, w_out, nf, *, sub, final, nxt=None):
    tm, tf = TM_FFN, TF_FFN
    nf_blocks = D_FF // tf
    n_in, n_out = 2 * D_FF // CAST_TILE, D_FF // CAST_TILE
    step = lambda i, f: i * nf_blocks + f
    in_blk = lambda i, f: jnp.minimum(step(i, f), n_in - 1)
    out_blk = lambda i, f: jnp.clip(step(i, f) - n_in, 0, n_out - 1)
    ctx_blocks = geo.n_ctx // tm
    whole = pl.BlockSpec((tm, D_MODEL), lambda i, f: (i, 0))
    in_specs = [
        whole,
        pl.BlockSpec((None, N_MOD, D_MODEL), lambda i, f: (geo.cond_row(i, tm), 0, 0)),
        pl.BlockSpec((1, D_MODEL), lambda i, f: (0, 0)),
        pl.BlockSpec((D_MODEL, tf), lambda i, f: (0, f)),
        pl.BlockSpec((D_MODEL, tf), lambda i, f: (0, f + nf_blocks)),
        pl.BlockSpec((tf, D_MODEL), lambda i, f: (f, 0)),
        pl.BlockSpec((1, D_MODEL), lambda i, f: (0, 0)),
    ]
    if final:
        out_specs = [pl.BlockSpec((tm, D_MODEL), lambda i, f: (jnp.minimum(i, ctx_blocks - 1), 0)),
                     pl.BlockSpec((tm, D_MODEL), lambda i, f: (jnp.maximum(i - ctx_blocks, 0), 0))]
        out_shape = [jax.ShapeDtypeStruct((geo.n_ctx, D_MODEL), F32), jax.ShapeDtypeStruct((geo.n_lat, D_MODEL), F32)]
    else:
        out_specs = [whole]
        out_shape = [jax.ShapeDtypeStruct((geo.n_tok, D_MODEL), F32)]
    args = [x, mods, g, w_in, w_in, w_out, nf]
    if nxt is not None:
        w_in_all, w_out_all, layer, half = nxt
        assert (geo.n_tok // tm) * nf_blocks >= n_in + n_out
        in_specs += [
            pl.BlockSpec((None, None, D_MODEL, CAST_TILE), lambda i, f: (layer, half, 0, in_blk(i, f))),
            pl.BlockSpec((None, None, CAST_TILE, D_MODEL), lambda i, f: (layer, half, out_blk(i, f), 0)),
        ]
        out_specs += [
            pl.BlockSpec((D_MODEL, CAST_TILE), lambda i, f: (0, in_blk(i, f))),
            pl.BlockSpec((CAST_TILE, D_MODEL), lambda i, f: (out_blk(i, f), 0)),
        ]
        out_shape += [jax.ShapeDtypeStruct((D_MODEL, 2 * D_FF), BF16), jax.ShapeDtypeStruct((D_FF, D_MODEL), BF16)]
        args += [w_in_all, w_out_all]
    return pl.pallas_call(
        functools.partial(_ffn_kernel, sub=sub, final=final, cast_next=(n_in, n_out) if nxt is not None else None,
                          ctx_blocks=ctx_blocks),
        grid=(geo.n_tok // tm, nf_blocks),
        in_specs=in_specs,
        out_specs=out_specs,
        out_shape=out_shape,
        scratch_shapes=[pltpu.VMEM((tm, D_MODEL), BF16), pltpu.VMEM((tm, D_MODEL), F32)],
        compiler_params=_params("arbitrary", "arbitrary"),
        name="ffn",
    )(*args)


def _mla_proj_kernel(*refs, aliased, ctx_blocks):
    x_ref, m_ref, g_ref, w_ref, qn_ref, kvn_ref, cos_ref, sin_ref = refs[:8]
    cq_ref, ckvb_ref, kr_ref, ckv_ctx_ref, kr_ctx_ref = refs[8 + (2 if aliased else 0):]
    h = _modulate(x_ref[...], g_ref[...], m_ref[3:4, :], m_ref[4:5, :]).astype(BF16)
    z = _dot(h, w_ref[...])
    cq_ref[...] = _rms(z[:, :Q_LORA], qn_ref[...]).astype(BF16)
    ckv = _rms(z[:, Q_LORA:Q_LORA + KV_LORA], kvn_ref[...])
    ckvb_ref[...] = ckv.astype(BF16)
    o = Q_LORA + KV_LORA
    kr = z[:, o:o + QK_ROPE] * cos_ref[...] + z[:, o + QK_ROPE:o + 2 * QK_ROPE] * sin_ref[...]
    kr_ref[...] = kr

    @pl.when(pl.program_id(0) < ctx_blocks)
    def _():
        ckv_ctx_ref[...] = ckv.reshape(ckv_ctx_ref.shape)
        kr_ctx_ref[...] = kr.reshape(kr_ctx_ref.shape)


def _mla_proj(geo, x, mods, g, w_cat, qn, kvn, cos, sin, j, n_layers, prev_cache):
    tm = TM_PROJ
    n_out = w_cat.shape[1]
    seqs = tm // geo.s_ctx
    ctx_blocks = geo.n_ctx // tm
    row = lambda i: (i, 0)
    fix = lambda i: (0, 0)
    cache = lambda i: (jnp.minimum(i, ctx_blocks - 1), j, 0, 0)
    in_specs = [
        pl.BlockSpec((tm, D_MODEL), row),
        pl.BlockSpec((None, N_MOD, D_MODEL), lambda i: (geo.cond_row(i, tm), 0, 0)),
        pl.BlockSpec((1, D_MODEL), fix),
        pl.BlockSpec((D_MODEL, n_out), fix),
        pl.BlockSpec((1, Q_LORA), fix),
        pl.BlockSpec((1, KV_LORA), fix),
        pl.BlockSpec((tm, QK_ROPE), row),
        pl.BlockSpec((tm, QK_ROPE), row),
    ]
    args = [x, mods, g, w_cat, qn, kvn, cos, sin]
    aliases = {}
    if prev_cache is not None:
        in_specs += [pl.BlockSpec(memory_space=pl.ANY)] * 2
        aliases = {len(args): 3, len(args) + 1: 4}
        args += list(prev_cache)
    return pl.pallas_call(
        functools.partial(_mla_proj_kernel, aliased=prev_cache is not None, ctx_blocks=ctx_blocks),
        grid=(geo.n_tok // tm,),
        in_specs=in_specs,
        out_specs=[
            pl.BlockSpec((tm, Q_LORA), row),
            pl.BlockSpec((tm, KV_LORA), row),
            pl.BlockSpec((tm, QK_ROPE), row),
            pl.BlockSpec((seqs, None, geo.s_ctx, KV_LORA), cache),
            pl.BlockSpec((seqs, None, geo.s_ctx, QK_ROPE), cache),
        ],
        out_shape=[
            jax.ShapeDtypeStruct((geo.n_tok, Q_LORA), BF16),
            jax.ShapeDtypeStruct((geo.n_tok, KV_LORA), BF16),
            jax.ShapeDtypeStruct((geo.n_tok, QK_ROPE), F32),
            jax.ShapeDtypeStruct((geo.b_ctx, n_layers, geo.s_ctx, KV_LORA), F32),
            jax.ShapeDtypeStruct((geo.b_ctx, n_layers, geo.s_ctx, QK_ROPE), F32),
        ],
        input_output_aliases=aliases,
        compiler_params=_params("arbitrary"),
        name="mla_proj",
    )(*args)


def _attn_kernel(*refs, s_len, t_past, rope):
    if rope:
        (cq_ref, ckvb_ref, kr_ref, pckv_ref, pkr_ref, cos_ref, sin_ref,
         wq_ref, wqr_ref, wkv_ref, _, o_ref, kv_all, kr_all) = refs
    else:
        cq_ref, ckvb_ref, kr_ref, wq_ref, wkv_ref, o_ref, kv_all, kr_all = refs
    if t_past:
        kv_all[0:t_past, :] = pckv_ref[...].astype(BF16)
        kr_all[0:t_past, :] = pkr_ref[...].astype(BF16)
    kv_all[t_past:, :] = ckvb_ref[...]
    kr_all[t_past:, :] = kr_ref[...].astype(BF16)

    n_q = s_len // Q_TILE
    hg = max(1, ATTN_LOCKSTEP // n_q)
    row_slices = [slice(qb * Q_TILE, (qb + 1) * Q_TILE) for qb in range(n_q)]

    def head_group(g, carry):
        heads = [g * hg + i for i in range(hg)]
        kv = kv_all[...]
        krb = kr_all[...]
        kvp = [_dot(kv, wkv_ref[hd]) for hd in heads]
        kn = [x[:, :QK_NOPE].astype(BF16) for x in kvp]
        v = [x[:, QK_NOPE:].astype(BF16) for x in kvp]
        items = [(i, qb) for i in range(hg) for qb in range(n_q)]
        cq = [cq_ref[rows, :] for rows in row_slices]
        q = [_dot(cq[qb], wq_ref[heads[i]]) for i, qb in items]
        qr = [x[:, QK_NOPE:] for x in q]
        if rope:
            rot = [_dot(cq[qb], wqr_ref[heads[i]]) for i, qb in items]
            qr = [x * cos_ref[row_slices[qb], :] + y * sin_ref[row_slices[qb], :]
                  for x, y, (i, qb) in zip(qr, rot, items)]
        s = [(_dot_nt(x[:, :QK_NOPE].astype(BF16), kn[i]) + _dot_nt(y.astype(BF16), krb)) * ATTN_SCALE
             for x, y, (i, qb) in zip(q, qr, items)]
        p = [jnp.exp(x - jnp.max(x, axis=-1, keepdims=True)) for x in s]
        pr = [(x * (1.0 / jnp.sum(x, axis=-1, keepdims=True))).astype(BF16) for x in p]
        o = [_dot(x, v[i]).astype(BF16) for x, (i, qb) in zip(pr, items)]
        for x, (i, qb) in zip(o, items):
            o_ref[row_slices[qb], pl.ds(pl.multiple_of(heads[i] * V_DIM, V_DIM), V_DIM)] = x
        return carry

    lax.fori_loop(0, MLA_HEADS // hg, head_group, 0)


def _attn(geo, cq, ckvb, kr, wq, wqr, wkv, *, latent, past_ckv=None, past_kr=None, cos=None, sin=None, o_ctx=None):
    if latent:
        nb, s_len, off = geo.b_lat, geo.s_lat, geo.n_ctx // geo.s_lat
        t_past = past_ckv.shape[1]
    else:
        nb, s_len, off, t_past = geo.b_ctx, geo.s_ctx, 0, 0
    row = lambda b: (b + off, 0)
    fix3 = lambda b: (0, 0, 0)
    tok = lambda width: pl.BlockSpec((s_len, width), row)
    wspec = lambda w: pl.BlockSpec(w.shape, fix3)
    in_specs = [tok(Q_LORA), tok(KV_LORA), tok(QK_ROPE)]
    args = [cq, ckvb, kr]
    if latent:
        in_specs += [pl.BlockSpec((None, t_past, KV_LORA), lambda b: (b, 0, 0)),
                     pl.BlockSpec((None, t_past, QK_ROPE), lambda b: (b, 0, 0)),
                     tok(QK_ROPE), tok(QK_ROPE), wspec(wq), wspec(wqr)]
        args += [past_ckv, past_kr, cos, sin, wq, wqr]
    else:
        in_specs += [wspec(wq)]
        args += [wq]
    in_specs += [wspec(wkv)]
    args += [wkv]
    aliases = {}
    if latent:
        in_specs += [pl.BlockSpec(memory_space=pl.ANY)]
        args += [o_ctx]
        aliases = {len(args) - 1: 0}
    kern = functools.partial(_attn_kernel, s_len=s_len, t_past=t_past, rope=latent)
    return pl.pallas_call(
        kern,
        grid=(nb,),
        in_specs=in_specs,
        out_specs=pl.BlockSpec((s_len, D_MODEL), row),
        out_shape=jax.ShapeDtypeStruct((geo.n_tok, D_MODEL), BF16),
        input_output_aliases=aliases,
        scratch_shapes=[pltpu.VMEM((t_past + s_len, KV_LORA), BF16),
                        pltpu.VMEM((t_past + s_len, QK_ROPE), BF16)],
        compiler_params=_params("parallel"),
        name="attn_lat" if latent else "attn_ctx",
    )(*args)


def _out_proj_kernel(a_ref, w_ref, x_ref, m_ref, o_ref):
    o_ref[...] = x_ref[...] + m_ref[5:6, :] * _dot(a_ref[...], w_ref[...])


def _out_proj(geo, a, w, x, mods):
    tm = TM_PROJ
    row = pl.BlockSpec((tm, D_MODEL), lambda i: (i, 0))
    return pl.pallas_call(
        _out_proj_kernel,
        grid=(geo.n_tok // tm,),
        in_specs=[row, pl.BlockSpec((D_MODEL, D_MODEL), lambda i: (0, 0)), row,
                  pl.BlockSpec((None, N_MOD, D_MODEL), lambda i: (geo.cond_row(i, tm), 0, 0))],
        out_specs=row,
        out_shape=jax.ShapeDtypeStruct((geo.n_tok, D_MODEL), F32),
        compiler_params=_params("parallel"),
        name="out_proj",
    )(a, w, x, mods)


def _shifted(geo, x_ref, xp_ref, xn_ref, m_ref, g_ref, tm):
    i = pl.program_id(0)
    g, shift, scale = g_ref[...], m_ref[3:4, :], m_ref[4:5, :]
    h = _modulate(x_ref[...], g, shift, scale)
    start = i * tm
    seq = jnp.where(start < geo.n_ctx, geo.s_ctx, geo.s_lat)
    rel = jnp.where(start < geo.n_ctx, start, start - geo.n_ctx)
    has_prev = (rel % seq) != 0
    has_next = ((rel + tm) % seq) != 0
    hp = _modulate(xp_ref[...], g, shift, scale)[HALO_ROWS - 1:HALO_ROWS, :]
    hn = _modulate(xn_ref[...], g, shift, scale)[0:1, :]
    hp = jnp.where(has_prev, hp, 0.0)
    hn = jnp.where(has_next, hn, 0.0)
    r = lax.broadcasted_iota(jnp.int32, h.shape, 0)
    down = jnp.where(r == 0, hp, pltpu.roll(h, 1, 0))
    up = jnp.where(r == tm - 1, hn, pltpu.roll(h, tm - 1, 0))
    return h, 0.5 * (down + up) - h


def _halo_specs(geo, tm):
    per_block = tm // HALO_ROWS
    last = geo.n_tok // HALO_ROWS - 1
    return [
        pl.BlockSpec((tm, D_MODEL), lambda i, *_: (i, 0)),
        pl.BlockSpec((HALO_ROWS, D_MODEL), lambda i, *_: (jnp.maximum(i * per_block - 1, 0), 0)),
        pl.BlockSpec((HALO_ROWS, D_MODEL), lambda i, *_: (jnp.minimum((i + 1) * per_block, last), 0)),
        pl.BlockSpec((None, N_MOD, D_MODEL), lambda i, *_: (geo.cond_row(i, tm), 0, 0)),
        pl.BlockSpec((1, D_MODEL), lambda i, *_: (0, 0)),
    ]


def _rkv_kernel(x_ref, xp_ref, xn_ref, m_ref, g_ref, mu_ref, w_ref, o_ref, h_ref, xx_ref, *, geo, tm):
    proj = pl.program_id(1)

    def project(h, xx):
        xm = (h + xx * mu_ref[...]).astype(BF16)
        o_ref[...] = _dot(xm, w_ref[proj])

    @pl.when(proj == 0)
    def _():
        h, xx = _shifted(geo, x_ref, xp_ref, xn_ref, m_ref, g_ref, tm)
        h_ref[...] = h
        xx_ref[...] = xx
        project(h, xx)

    @pl.when(proj > 0)
    def _():
        project(h_ref[...], xx_ref[...])


def _rkv(geo, x, mods, g, mu3, w3):
    tm = TM_RWKV
    return pl.pallas_call(
        functools.partial(_rkv_kernel, geo=geo, tm=tm),
        grid=(geo.n_tok // tm, 3),
        in_specs=_halo_specs(geo, tm) + [
            pl.BlockSpec((None, 1, D_MODEL), lambda i, p: (p, 0, 0)),
            pl.BlockSpec((3, D_MODEL, D_MODEL), lambda i, p: (0, 0, 0), pipeline_mode=pl.Buffered(1)),
        ],
        out_specs=pl.BlockSpec((None, tm, D_MODEL), lambda i, p: (p, i, 0)),
        out_shape=jax.ShapeDtypeStruct((3, geo.n_tok, D_MODEL), F32),
        scratch_shapes=[pltpu.VMEM((tm, D_MODEL), F32), pltpu.VMEM((tm, D_MODEL), F32)],
        compiler_params=_params("parallel", "arbitrary"),
        name="rwkv_rkv",
    )(x, x, x, mods, g, mu3, w3)


def _lora_kernel(x_ref, xp_ref, xn_ref, m_ref, g_ref, w1h_ref, w1x_ref, g2_ref,
                 wa2_ref, w0_ref, a0_ref, gate_ref, dec_ref, asig_ref, *, geo, tm):
    h, xx = _shifted(geo, x_ref, xp_ref, xn_ref, m_ref, g_ref, tm)
    z = _dot(h.astype(BF16), w1h_ref[...]) + _dot(xx.astype(BF16), w1x_ref[...])
    gw = g2_ref.shape[0]
    gate_ref[...] = _dot(_sigmoid(z[:, :gw]).astype(BF16), g2_ref[...])
    for d in range(2):
        zw = z[:, gw + LORA_PAD * d:gw + LORA_PAD * (d + 1)]
        wl = w0_ref[d] + _dot(jnp.tanh(zw).astype(BF16), wa2_ref[d])
        dec_ref[d] = -LOG_DECAY_SCALE * _sigmoid(wl)
        za = z[:, gw + LORA_PAD * (2 + d):gw + LORA_PAD * (3 + d)]
        asig_ref[d] = _sigmoid(a0_ref[d] + _dot(za.astype(BF16), wa2_ref[2 + d]))


def _lora(geo, x, mods, g, w1h, w1x, g2, wa2, w0, a0):
    tm = TM_RWKV
    full = lambda a: pl.BlockSpec(a.shape, lambda i: (0,) * a.ndim)
    tok2 = pl.BlockSpec((2, tm, D_MODEL), lambda i: (0, i, 0))
    return pl.pallas_call(
        functools.partial(_lora_kernel, geo=geo, tm=tm),
        grid=(geo.n_tok // tm,),
        in_specs=_halo_specs(geo, tm) + [full(a) for a in (w1h, w1x, g2, wa2, w0, a0)],
        out_specs=[pl.BlockSpec((tm, D_MODEL), lambda i: (i, 0)), tok2, tok2],
        out_shape=[jax.ShapeDtypeStruct((geo.n_tok, D_MODEL), F32),
                   jax.ShapeDtypeStruct((2, geo.n_tok, D_MODEL), F32),
                   jax.ShapeDtypeStruct((2, geo.n_tok, D_MODEL), F32)],
        compiler_params=_params("parallel"),
        name="rwkv_lora",
    )(x, x, x, mods, g, w1h, w1x, g2, wa2, w0, a0)


def _head_sums(x, ones_blk):
    return _dot(x.astype(BF16), ones_blk)


def _head_ones(width):
    r = lax.broadcasted_iota(jnp.int32, (width, width), 0)
    c = lax.broadcasted_iota(jnp.int32, (width, width), 1)
    same = (r // RWKV_HEAD) == (c // RWKV_HEAD)
    return same, jnp.where(same, 1.0, 0.0).astype(BF16)


def _wkv_kernel(*refs, geo, tb, aliased):
    r_ref, k_ref, v_ref, dec_ref, asig_ref, kk_ref, ka_ref, rk_ref, s0_ref = refs[:9]
    y_ref, bon_ref, sf_fwd_ref, sf_bwd_ref, s_ref = refs[9 + (2 if aliased else 0):]
    d = pl.program_id(0)
    i = pl.program_id(1)
    nblk = pl.num_programs(1)
    blk = i + d * (nblk - 1 - 2 * i)
    start = blk * tb
    seq = jnp.where(start < geo.n_ctx, geo.s_ctx, geo.s_lat)
    rel = jnp.where(start < geo.n_ctx, start, start - geo.n_ctx)
    at_lo = (rel % seq) == 0
    at_hi = ((rel + tb) % seq) == 0
    first = jnp.where(d == 0, at_lo, at_hi)
    last = jnp.where(d == 0, at_hi, at_lo)

    @pl.when(first)
    def _():
        s_ref[...] = s0_ref[...]

    nch = tb // CHUNK
    sgn = 1 - 2 * d
    fwd = (d == 0).astype(F32)
    row = lax.broadcasted_iota(jnp.int32, (CHUNK, PAIR), 0)
    lane = lax.broadcasted_iota(jnp.int32, (CHUNK, PAIR), 1)
    sidx = lane & (RWKV_HEAD - 1)
    delta = (row - sidx) * sgn
    strict = delta > 0
    incl = delta >= 0
    eye = jnp.where(row == sidx, 1.0, 0.0)
    off_masks = []
    m = 1
    while m < CHUNK:
        off_masks.append(strict & ((row // (2 * m)) == (sidx // (2 * m))) & ((row // m) != (sidx // m)))
        m *= 2
    head0 = lane < RWKV_HEAD
    tr = lax.broadcasted_iota(jnp.int32, (CHUNK, 3 * CHUNK), 0)
    ts = lax.broadcasted_iota(jnp.int32, (CHUNK, 3 * CHUNK), 1) & (CHUNK - 1)
    tri3 = jnp.where((tr - ts) * sgn >= 0, 1.0, 0.0).astype(BF16)
    same_head, ones_blk = _head_ones(PAIR)

    def expand(x):
        return jnp.concatenate([jnp.where(head0, x, 0.0), jnp.where(head0, 0.0, x)], axis=0)

    def chunk_body(ci, carry):
        cc = ci * sgn + d * (nch - 1)
        rows = pl.ds(pl.multiple_of(cc * CHUNK, CHUNK), CHUNK)

        def group_body(pg, carry2):
            ps = [pg * PAIR_GROUP + q for q in range(PAIR_GROUP)]
            cols = [pl.ds(pl.multiple_of(p * PAIR, PAIR), PAIR) for p in ps]
            each = lambda f, *ls: [f(*xs) for xs in zip(*ls)]
            r = [r_ref[rows, c] for c in cols]
            k = [k_ref[rows, c] for c in cols]
            v = [v_ref[rows, c] for c in cols]
            logw = [dec_ref[rows, c] for c in cols]
            a = [asig_ref[rows, c] for c in cols]
            kk = [x * kk_ref[:, c] for x, c in zip(k, cols)]
            kd = [x * (1.0 + (y - 1.0) * ka_ref[:, c]) for x, y, c in zip(k, a, cols)]
            sums = [_head_sums(jnp.concatenate([x * x, y * z * rk_ref[:, c]], axis=0), ones_blk)
                    for x, y, z, c in zip(kk, r, kd, cols)]
            kk = each(lambda x, s: x / jnp.maximum(jnp.sqrt(s[:CHUNK]), 1e-12), kk, sums)
            b_in = each(lambda x, y: x * y, kk, a)
            for c, s, y in zip(cols, sums, v):
                bon_ref[rows, c] = s[CHUNK:] * y

            l1 =[x.astype(BF16) for x in logw]
            e1 = each(lambda x, y: x - y.astype(F32), logw, l1)
            l2 = [x.astype(BF16) for x in e1]
            l3 = each(lambda x, y: (x - y.astype(F32)).astype(BF16), e1, l2)
            cum = each(lambda x, y, z: _dot(tri3, jnp.concatenate([x, y, z], axis=0)), l1, l2, l3)
            tot = [fwd * x[CHUNK - 1:CHUNK, :] + (1.0 - fwd) * x[0:1, :] for x in cum]
            c_inv = [jnp.exp(-x) for x in cum]
            at = each(lambda x, y, z: -x * jnp.exp(y - z), kk, cum, logw)
            rt = each(lambda x, y: x * jnp.exp(y), r, cum)
            c_end = each(lambda x, y: jnp.exp(x - y), tot, cum)

            lhs = each(lambda x, y: jnp.concatenate([x, y], axis=0).astype(BF16), at, rt)
            rhs = each(lambda x, y, z: jnp.concatenate([expand(x * z), expand(y * z)], axis=0).astype(BF16),
                       b_in, kd, c_inv)
            g = each(_dot_nt, lhs, rhs)
            ab = [jnp.where(strict, x[:CHUNK, :PAIR], 0.0) for x in g]
            ak = [jnp.where(strict, x[:CHUNK, PAIR:], 0.0) for x in g]
            rbk = [jnp.concatenate([jnp.where(incl, x[CHUNK:, :PAIR], 0.0),
                                    jnp.where(incl, x[CHUNK:, PAIR:], 0.0)], axis=1).astype(BF16) for x in g]

            t_inv = [eye + jnp.where(off_masks[0], x, 0.0) for x in ab]
            for off in off_masks[1:]:
                lx = each(lambda x, t: _dot(jnp.where(off, x, 0.0).astype(BF16), expand(t).astype(BF16)), ab, t_inv)
                t_inv = each(lambda t, x: t + _dot(t.astype(BF16), expand(x).astype(BF16)), t_inv, lx)

            s2 = [s_ref[p] for p in ps]
            ars = each(lambda x, s: _dot_nt(x, s.astype(BF16)), lhs, s2)
            vexp = [expand(x).astype(BF16) for x in v]
            rhs_u = each(lambda x, y, z: x[:CHUNK] + _dot(y.astype(BF16), z), ars, ak, vexp)
            u = each(lambda t, x: _dot(t.astype(BF16), expand(x).astype(BF16)), t_inv, rhs_u)
            uv = each(lambda x, y: jnp.concatenate([expand(x).astype(BF16), y], axis=0), u, vexp)
            y_out = each(lambda x, y, z: x[CHUNK:] + _dot(y, z), ars, rbk, uv)
            for c, x in zip(cols, y_out):
                y_ref[rows, c] = x
            uv_t = each(lambda x, y: jnp.concatenate([x, y], axis=0).T.astype(BF16), u, v)
            bk = each(lambda x, y, z: jnp.concatenate([x * z, y * z], axis=0).astype(BF16), b_in, kd, c_end)
            upd = each(_dot, uv_t, bk)
            for p, s, x, t in zip(ps, s2, upd, tot):
                s_ref[p] = s * jnp.exp(t) + jnp.where(same_head, x, 0.0)
            return carry2

        lax.fori_loop(0, N_PAIRS // PAIR_GROUP, group_body, 0)
        return carry

    lax.fori_loop(0, nch, chunk_body, 0)

    def write_states(dst_ref):
        for p in range(N_PAIRS):
            s = s_ref[p]
            dst_ref[2 * p] = s[:RWKV_HEAD, :RWKV_HEAD]
            dst_ref[2 * p + 1] = s[RWKV_HEAD:, RWKV_HEAD:]

    ends_ctx = last & (start < geo.n_ctx)

    @pl.when(ends_ctx & (d == 0))
    def _():
        write_states(sf_fwd_ref)

    @pl.when(ends_ctx & (d == 1))
    def _():
        write_states(sf_bwd_ref)


def _wkv(geo, rkv, dec, asig, kk_c, ka_c, rk_c, s0, j, n_layers, prev_states):
    tb = TB_WKV
    nblk = geo.n_tok // tb
    blk = lambda d, i: i + d * (nblk - 1 - 2 * i)
    tok = lambda which: pl.BlockSpec((None, tb, D_MODEL), lambda d, i: (which, blk(d, i), 0))
    perdir = pl.BlockSpec((None, tb, D_MODEL), lambda d, i: (d, blk(d, i), 0))
    const = pl.BlockSpec((1, D_MODEL), lambda d, i: (0, 0))
    last_seq = geo.b_ctx - 1
    seq = lambda d, i: jnp.minimum(blk(d, i) * tb // geo.s_ctx, last_seq)
    final = lambda which: pl.BlockSpec(
        (None, None, RWKV_HEADS, RWKV_HEAD, RWKV_HEAD),
        lambda d, i: (jnp.where(d == which, seq(d, i), last_seq), j, 0, 0, 0))
    final_shape = jax.ShapeDtypeStruct((geo.b_ctx, n_layers, RWKV_HEADS, RWKV_HEAD, RWKV_HEAD), F32)
    in_specs = [tok(0), tok(1), tok(2), perdir, perdir, const, const, const,
                pl.BlockSpec((None, None, N_PAIRS, PAIR, PAIR), lambda d, i: (d, geo.cond_row(blk(d, i), tb), 0, 0, 0))]
    args = [rkv, rkv, rkv, dec, asig, kk_c, ka_c, rk_c, s0]
    aliases = {}
    if prev_states is not None:
        in_specs += [pl.BlockSpec(memory_space=pl.ANY)] * 2
        aliases = {len(args): 2, len(args) + 1: 3}
        args += list(prev_states)
    return pl.pallas_call(
        functools.partial(_wkv_kernel, geo=geo, tb=tb, aliased=prev_states is not None),
        grid=(2, nblk),
        in_specs=in_specs,
        out_specs=[perdir, perdir, final(0), final(1)],
        out_shape=[jax.ShapeDtypeStruct((2, geo.n_tok, D_MODEL), F32),
                   jax.ShapeDtypeStruct((2, geo.n_tok, D_MODEL), F32), final_shape, final_shape],
        input_output_aliases=aliases,
        scratch_shapes=[pltpu.VMEM((N_PAIRS, PAIR, PAIR), F32)],
        compiler_params=_params("arbitrary", "arbitrary"),
        name="wkv_chunked",
    )(*args)


def _rwkv_out_kernel(y_ref, bon_ref, gate_ref, lw_ref, lb_ref, w_ref, x_ref, m_ref, o_ref, z_ref):
    _, ones_blk = _head_ones(GN_SLAB)
    inv_n = 1.0 / RWKV_HEAD
    for s in range(D_MODEL // GN_SLAB):
        cols = slice(s * GN_SLAB, (s + 1) * GN_SLAB)
        y = y_ref[0, :, cols] + y_ref[1, :, cols]
        yc = y - _head_sums(y, ones_blk) * inv_n
        var = _head_sums(yc * yc, ones_blk) * inv_n
        z = yc * lax.rsqrt(var + GN_EPS) * lw_ref[:, cols] + lb_ref[:, cols] + bon_ref[0, :, cols] + bon_ref[1, :, cols]
        z_ref[:, cols] = (z * gate_ref[:, cols]).astype(BF16)
    o_ref[...] = x_ref[...] + m_ref[5:6, :] * _dot(z_ref[...], w_ref[...])


def _rwkv_out(geo, y, bon, gate, lw, lb, w, x, mods):
    tm = TM_OUT
    row = pl.BlockSpec((tm, D_MODEL), lambda i: (i, 0))
    both = pl.BlockSpec((2, tm, D_MODEL), lambda i: (0, i, 0))
    const = pl.BlockSpec((1, D_MODEL), lambda i: (0, 0))
    return pl.pallas_call(
        _rwkv_out_kernel,
        grid=(geo.n_tok // tm,),
        in_specs=[both, both, row, const, const, pl.BlockSpec((D_MODEL, D_MODEL), lambda i: (0, 0)), row,
                  pl.BlockSpec((None, N_MOD, D_MODEL), lambda i: (geo.cond_row(i, tm), 0, 0))],
        out_specs=row,
        out_shape=jax.ShapeDtypeStruct((geo.n_tok, D_MODEL), F32),
        scratch_shapes=[pltpu.VMEM((tm, D_MODEL), BF16)],
        compiler_params=_params("parallel"),
        name="rwkv_out",
    )(y, bon, gate, lw, lb, w, x, mods)


def _state_to_pairs(s):
    b = s.shape[0]
    s5 = s.reshape(b, N_PAIRS, 2, RWKV_HEAD, RWKV_HEAD)
    z = jnp.zeros_like(s5[:, :, 0])
    top = jnp.concatenate([s5[:, :, 0], z], axis=-1)
    bot = jnp.concatenate([z, s5[:, :, 1]], axis=-1)
    return jnp.concatenate([top, bot], axis=-2)


def _rope_tables(geo):
    rows = geo.s_lat // GRID_W
    row = jnp.repeat(jnp.arange(rows, dtype=F32), GRID_W)
    col = jnp.tile(jnp.arange(GRID_W, dtype=F32), rows)
    inv = ROPE_THETA ** (-jnp.arange(ROPE_FREQS, dtype=F32) / ROPE_FREQS)
    ang = jnp.stack([row[:, None] * inv, col[:, None] * inv], axis=1)
    ang = jnp.broadcast_to(ang[:, :, None, :], (geo.s_lat, 2, 2, ROPE_FREQS)).reshape(geo.s_lat, QK_ROPE)
    cos = jnp.concatenate([jnp.ones((geo.n_ctx, QK_ROPE), F32), jnp.tile(jnp.cos(ang), (geo.b_lat, 1))], axis=0)
    sin = jnp.concatenate([jnp.zeros((geo.n_ctx, QK_ROPE), F32), jnp.tile(jnp.sin(ang), (geo.b_lat, 1))], axis=0)
    return cos, sin


def _rot_cols(w):
    w4 = w.reshape(w.shape[:-1] + (2, 2, ROPE_FREQS))
    return jnp.stack([-w4[..., 1, :], w4[..., 0, :]], axis=-2).reshape(w.shape)


def _mla_layer(geo, x, mods, g, j, cos, sin, cache_ckv, cache_krope, prev_cache, p):
    w_dkv = p['mla_w_dkv'][j]
    w_cat = jnp.concatenate([p['mla_w_dq'][j], w_dkv, _rot_cols(w_dkv[:, KV_LORA:])], axis=1).astype(BF16)
    cq, ckvb, kr, new_ckv, new_kr = _mla_proj(geo, x, mods, g, w_cat, p['mla_q_norm'][j][None],
                                              p['mla_kv_norm'][j][None], cos, sin, j, p['mla_w_dq'].shape[0],
                                              prev_cache)
    w_uq = p['mla_w_uq'][j].reshape(Q_LORA, MLA_HEADS, QK_NOPE + QK_ROPE).transpose(1, 0, 2)
    wq = w_uq.astype(BF16)
    wqr = _rot_cols(w_uq[..., QK_NOPE:]).astype(BF16)
    wkv = p['mla_w_ukv'][j].reshape(KV_LORA, MLA_HEADS, QK_NOPE + V_DIM).transpose(1, 0, 2).astype(BF16)
    o_ctx = _attn(geo, cq, ckvb, kr, wq, wqr, wkv, latent=False)
    o = _attn(geo, cq, ckvb, kr, wq, wqr, wkv, latent=True,
              past_ckv=cache_ckv[:, j], past_kr=cache_krope[:, j], cos=cos, sin=sin, o_ctx=o_ctx)
    x = _out_proj(geo, o, p['mla_w_o'][j].astype(BF16), x, mods)
    return x, (new_ckv, new_kr)


def _rwkv_layer(geo, x, mods, g, j, s0_fwd, s0_bwd, prev_states, p):
    mu = p['rwkv_mu'][j]
    w3 =jnp.stack([p['rwkv_w_r'][j], p['rwkv_w_k'][j], p['rwkv_w_v'][j]]).astype(BF16)
    rkv = _rkv(geo, x, mods, g, mu[:3, None, :], w3)
    pad1 = lambda w: jnp.pad(w, ((0, 0), (0, LORA_PAD - w.shape[-1])))
    pad2 = lambda w: jnp.pad(w, ((0, 0), (0, LORA_PAD - w.shape[-2]), (0, 0)))
    mu_dir = p['rwkv_mu_dir'][j]
    downs = [p['rwkv_g1'][j]] + [pad1(w) for w in (*p['rwkv_w1'][j], *p['rwkv_a1'][j])]
    mixes = [mu[3], mu_dir[0, 0], mu_dir[1, 0], mu_dir[0, 1], mu_dir[1, 1]]
    w1h = jnp.concatenate(downs, axis=1).astype(BF16)
    w1x = jnp.concatenate([m[:, None] * w for m, w in zip(mixes, downs)], axis=1).astype(BF16)
    wa2 = jnp.concatenate([pad2(p['rwkv_w2'][j]), pad2(p['rwkv_a2'][j])]).astype(BF16)
    gate, dec, asig = _lora(geo, x, mods, g, w1h, w1x, p['rwkv_g2'][j].astype(BF16), wa2,
                            p['rwkv_w0'][j][:, None, :], p['rwkv_a0'][j][:, None, :])
    zero = jnp.zeros((2, 1, N_PAIRS, PAIR, PAIR), F32)
    s0 = jnp.concatenate([zero, jnp.stack([_state_to_pairs(s0_fwd), _state_to_pairs(s0_bwd)])], axis=1)
    y, bon, sf, sb = _wkv(geo, rkv, dec, asig, p['rwkv_k_k'][j][None], p['rwkv_k_a'][j][None],
                          p['rwkv_r_k'][j].reshape(1, D_MODEL), s0, j, p['rwkv_mu'].shape[0], prev_states)
    x = _rwkv_out(geo, y, bon, gate, p['rwkv_ln_w'][j][None], p['rwkv_ln_b'][j][None],
                  p['rwkv_w_o'][j].astype(BF16), x, mods)
    return x, (sf, sb)


def kernel(x_prompt, x_sample, cache_ckv, cache_krope, state_wkv_fwd, state_wkv_bwd, c, c_ctx, w_ada, b_ada, norm_sub, norm_final, w_ffn_in, w_ffn_out, mla_w_dq, mla_q_norm, mla_w_uq, mla_w_dkv, mla_kv_norm, mla_w_ukv, mla_w_o, rwkv_mu, rwkv_mu_dir, rwkv_w_r, rwkv_w_k, rwkv_w_v, rwkv_w0, rwkv_w1, rwkv_w2, rwkv_a0, rwkv_a1, rwkv_a2, rwkv_g1, rwkv_g2, rwkv_k_k, rwkv_k_a, rwkv_r_k, rwkv_ln_w, rwkv_ln_b, rwkv_w_o):
    p = dict(mla_w_dq=mla_w_dq, mla_q_norm=mla_q_norm, mla_w_uq=mla_w_uq, mla_w_dkv=mla_w_dkv,
             mla_kv_norm=mla_kv_norm, mla_w_ukv=mla_w_ukv, mla_w_o=mla_w_o,
             rwkv_mu=rwkv_mu, rwkv_mu_dir=rwkv_mu_dir, rwkv_w_r=rwkv_w_r, rwkv_w_k=rwkv_w_k,
             rwkv_w_v=rwkv_w_v, rwkv_w0=rwkv_w0, rwkv_w1=rwkv_w1, rwkv_w2=rwkv_w2,
             rwkv_a0=rwkv_a0, rwkv_a1=rwkv_a1, rwkv_a2=rwkv_a2, rwkv_g1=rwkv_g1, rwkv_g2=rwkv_g2,
             rwkv_k_k=rwkv_k_k, rwkv_k_a=rwkv_k_a, rwkv_r_k=rwkv_r_k,
             rwkv_ln_w=rwkv_ln_w, rwkv_ln_b=rwkv_ln_b, rwkv_w_o=rwkv_w_o)
    b_ctx, s_ctx, _ = x_prompt.shape
    b_lat, s_lat, _ = x_sample.shape
    geo = _Geom(b_ctx, s_ctx, b_lat, s_lat)
    assert geo.n_ctx % s_lat == 0 and s_ctx % TM_RWKV == 0 and s_lat % TM_FFN == 0 and geo.n_ctx % TM_FFN == 0
    assert TM_PROJ % s_ctx == 0 and geo.n_ctx % TM_PROJ == 0 and s_lat % TM_PROJ == 0
    assert s_ctx == TB_WKV and s_lat % TB_WKV == 0 and s_ctx % TM_OUT == 0

    x = jnp.concatenate([x_prompt.reshape(geo.n_ctx, D_MODEL), x_sample.reshape(geo.n_lat, D_MODEL)], axis=0)
    cond =jnp.concatenate([c_ctx[None], c, jnp.zeros((COND_ROWS - 1 - b_lat, D_MODEL), F32)], axis=0)
    mods_all = _ada(cond, w_ada, b_ada).reshape(DEPTH, COND_ROWS, N_MOD, D_MODEL)
    cos, sin = _rope_tables(geo)
    nf = norm_final[None]
    w_in = w_ffn_in[0, 0].astype(BF16)
    w_out = w_ffn_out[0, 0].astype(BF16)

    cache, states = None, None
    for l in range(DEPTH):
        j = l // N_MIXERS
        mods = mods_all[l]
        x, w_in, w_out = _ffn(geo, x, mods, norm_sub[l, 0][None], w_in, w_out, nf, sub=0, final=False,
                              nxt=(w_ffn_in, w_ffn_out, l, 1))
        if l % N_MIXERS == 0:
            x, cache = _mla_layer(geo, x, mods, norm_sub[l, 1][None], j, cos, sin, cache_ckv, cache_krope, cache, p)
        else:
            x, states = _rwkv_layer(geo, x, mods, norm_sub[l, 1][None], j,
                                    state_wkv_fwd[:, j], state_wkv_bwd[:, j], states, p)
        if l < DEPTH - 1:
            x, w_in, w_out = _ffn(geo, x, mods, norm_sub[l, 2][None], w_in, w_out, nf, sub=2, final=False,
                                  nxt=(w_ffn_in, w_ffn_out, l + 1, 0))
        else:
            y_ctx, y_lat = _ffn(geo, x, mods, norm_sub[l, 2][None], w_in, w_out, nf, sub=2, final=True)

    return (y_ctx.reshape(b_ctx, s_ctx, D_MODEL), y_lat.reshape(b_lat, s_lat, D_MODEL),
            cache[0], cache[1], states[0], states[1])
```
